```python
import jax, jax.numpy as jnp
from jax import lax
import numpy as np

D_MODEL = 1024
BATCH = 2
SEQ = 8192
DEPTH = 1

ATTN_HEADS = 8
HEAD_DIM = 64
ATTN_WIDTH = ATTN_HEADS * HEAD_DIM
DILATED_PATTERNS = ((128, 1), (512, 4), (2048, 16))
POOL_WINDOWS = (2, 4, 8, 16)
POOL_GROUPS = len(POOL_WINDOWS)
POOL_WIDTH = D_MODEL // 2
POOL_GROUP_DIM = POOL_WIDTH // POOL_GROUPS
MIX_WIDTH = ATTN_WIDTH + POOL_WIDTH
IN_WIDTH = 3 * ATTN_WIDTH + POOL_WIDTH
MEM_LEN = 256
MEM_HEADS = 4
MEM_HEAD_DIM = 128
MEM_WIDTH = MEM_HEADS * MEM_HEAD_DIM
N_EXPERTS = 64
N_EXPERT_GROUPS = 8
TOPK_GROUPS = 4
TOP_K = 8
EXPERT_HIDDEN = 256
SHARED_HIDDEN = 256
ROUTED_SCALE = 2.5
MOE_BLOCK = 128
LN_EPS = 1e-5
NEG_INF = -1e30

kernel_name = 'hybrid_dilated_pool_moe_block'


def layer_norm(x, g, b):
    xf = x.astype(jnp.float32)
    mu = xf.mean(-1, keepdims=True)
    var = jnp.square(xf - mu).mean(-1, keepdims=True)
    y = (xf - mu) * lax.rsqrt(var + LN_EPS) * g.astype(jnp.float32) + b.astype(jnp.float32)
    return y.astype(x.dtype)


def alibi_slopes(n_heads):
    return 2.0 ** (-8.0 * jnp.arange(1, n_heads + 1, dtype=jnp.float32) / n_heads)


def _to_strided_blocks(a, dil, blk):
    B, T, H, hd = a.shape
    L = T // dil
    Lp = -(-L // blk) * blk
    a = a.reshape(B, L, dil, H, hd).transpose(0, 3, 2, 1, 4)
    a = jnp.pad(a, ((0, 0), (0, 0), (0, 0), (0, Lp - L), (0, 0)))
    return a.reshape(B, H, dil, Lp // blk, blk, hd)


def dilated_branch(q, k, v, slopes, window, dil):
    B, T, H, hd = q.shape
    steps = window // dil
    blk = steps
    L = T // dil
    qb, kb, vb = (_to_strided_blocks(a, dil, blk) for a in (q, k, v))
    nb = qb.shape[3]

    def with_prev(a):
        prev = jnp.pad(a[:, :, :, :-1], ((0, 0), (0, 0), (0, 0), (1, 0), (0, 0), (0, 0)))
        return jnp.concatenate([prev, a], axis=4)

    kk, vv = with_prev(kb), with_prev(vb)
    dist = jnp.arange(blk)[:, None] + blk - jnp.arange(2 * blk)[None, :]
    first = (jnp.arange(nb) == 0)[:, None, None] & (jnp.arange(2 * blk) < blk)[None, None, :]
    valid = (dist >= 0)[None] & (dist <= steps)[None] & ~first
    s = jnp.einsum('bhrnqd,bhrnkd->bhrnqk', qb, kk, preferred_element_type=jnp.float32) * (hd ** -0.5)
    s = s - slopes[None, :, None, None, None, None] * (dist * dil).astype(jnp.float32)
    s = jnp.where(valid, s, NEG_INF)
    lse = jax.nn.logsumexp(s, axis=-1)
    p = jnp.exp(s - lse[..., None])
    o = jnp.einsum('bhrnqk,bhrnkd->bhrnqd', p.astype(v.dtype), vv, preferred_element_type=jnp.float32)
    o = o.reshape(B, H, dil, nb * blk, hd)[:, :, :, :L].transpose(0, 3, 2, 1, 4).reshape(B, T, H, hd)
    lse = lse.reshape(B, H, dil, nb * blk)[:, :, :, :L].transpose(0, 3, 2, 1).reshape(B, T, H)
    return o, lse


def dilated_attention(q, k, v):
    slopes = alibi_slopes(q.shape[2])
    outs, lses = [], []
    for window, dil in DILATED_PATTERNS:
        o, lse = dilated_branch(q, k, v, slopes, window, dil)
        outs.append(o)
        lses.append(lse)
    wts = jax.nn.softmax(jnp.stack(lses), axis=0)
    return jnp.einsum('gbth,gbthd->bthd', wts, jnp.stack(outs))


def multiscale_pool(p, pool_w, pool_scale):
    B, T, _ = p.shape
    grp = p.astype(jnp.float32).reshape(B, T, POOL_GROUPS, POOL_GROUP_DIM)
    cs0 = jnp.pad(jnp.cumsum(grp, axis=1), ((0, 0), (1, 0), (0, 0), (0, 0)))
    pos = jnp.arange(T, dtype=jnp.float32)
    outs = []
    for g, w in enumerate(POOL_WINDOWS):
        upper = cs0[:, 1:, g]
        lower = jnp.pad(cs0[:, :T + 1 - w, g], ((0, 0), (w - 1, 0), (0, 0)))
        count = jnp.minimum(pos + 1.0, float(w))[None, :, None]
        outs.append((upper - lower) / count - grp[:, :, g])
    pooled = jnp.stack(outs, axis=2)
    mixed = jnp.einsum('btgc,gcd->btgd', pooled, pool_w.astype(jnp.float32))
    return (mixed.reshape(B, T, POOL_WIDTH) * pool_scale.astype(jnp.float32)).astype(p.dtype)


def hybrid_mixer(h, w_in, pool_w, pool_scale, w_out):
    B, T, _ = h.shape
    proj = h @ w_in
    q, k, v, p = jnp.split(proj, [ATTN_WIDTH, 2 * ATTN_WIDTH, 3 * ATTN_WIDTH], axis=-1)
    heads = lambda a: a.reshape(B, T, ATTN_HEADS, HEAD_DIM)
    attn = dilated_attention(heads(q), heads(k), heads(v)).reshape(B, T, ATTN_WIDTH).astype(h.dtype)
    pool = multiscale_pool(p, pool_w, pool_scale)
    return jnp.concatenate([attn, pool], axis=-1) @ w_out


def memory_cross_attention(h, mem, w_cq, w_ck, w_cv, w_co):
    B, T, _ = h.shape
    M = mem.shape[1]
    q = (h @ w_cq).reshape(B, T, MEM_HEADS, MEM_HEAD_DIM)
    k = (mem @ w_ck).reshape(B, M, MEM_HEADS, MEM_HEAD_DIM)
    v = (mem @ w_cv).reshape(B, M, MEM_HEADS, MEM_HEAD_DIM)
    s = jnp.einsum('bthd,bmhd->bhtm', q, k, preferred_element_type=jnp.float32) * (MEM_HEAD_DIM ** -0.5)
    p = jax.nn.softmax(s, axis=-1)
    o = jnp.einsum('bhtm,bmhd->bthd', p.astype(v.dtype), v).reshape(B, T, MEM_WIDTH)
    return o @ w_co


def moe_ffn(h, w_router, router_bias, exp_gate, exp_up, exp_down, sh_gate, sh_up, sh_down):
    B, T, D = h.shape
    N = B * T
    hf = h.reshape(N, D)
    scores = jax.nn.sigmoid(jnp.matmul(hf, w_router, preferred_element_type=jnp.float32))
    biased = scores + router_bias.astype(jnp.float32)
    per_group = N_EXPERTS // N_EXPERT_GROUPS
    group_score = lax.top_k(biased.reshape(N, N_EXPERT_GROUPS, per_group), 2)[0].sum(-1)
    _, top_groups = lax.top_k(group_score, TOPK_GROUPS)
    group_mask = jax.nn.one_hot(top_groups, N_EXPERT_GROUPS, dtype=jnp.float32).sum(1) > 0
    expert_mask = jnp.repeat(group_mask, per_group, axis=1)
    _, idx = lax.top_k(jnp.where(expert_mask, biased, NEG_INF), TOP_K)
    gates = jnp.take_along_axis(scores, idx, axis=1)
    gates = gates / gates.sum(-1, keepdims=True) * ROUTED_SCALE

    A = N * TOP_K
    flat_e = idx.reshape(A)
    order = jnp.argsort(flat_e)
    sorted_e = flat_e[order]
    tok = (order // TOP_K).astype(jnp.int32)
    counts = jnp.bincount(flat_e, length=N_EXPERTS)
    padded = (counts + MOE_BLOCK - 1) // MOE_BLOCK * MOE_BLOCK
    pad_ends = jnp.cumsum(padded)
    rank = jnp.arange(A) - (jnp.cumsum(counts) - counts)[sorted_e]
    dest = (pad_ends - padded)[sorted_e] + rank
    n_blocks = -(-A // MOE_BLOCK) + N_EXPERTS
    P = n_blocks * MOE_BLOCK
    buf_tok = jnp.zeros((P,), jnp.int32).at[dest].set(tok)
    buf_w = jnp.zeros((P,), jnp.float32).at[dest].set(gates.reshape(A)[order])
    block_expert = jnp.minimum(
        jnp.searchsorted(pad_ends, jnp.arange(n_blocks) * MOE_BLOCK, side='right'), N_EXPERTS - 1)

    def expert_block(args):
        tok_blk, e = args
        xb = hf[tok_blk]
        return (jax.nn.silu(xb @ exp_gate[e]) * (xb @ exp_up[e])) @ exp_down[e]

    y = lax.map(expert_block, (buf_tok.reshape(n_blocks, MOE_BLOCK), block_expert))
    routed = jnp.zeros((N, D), jnp.float32).at[buf_tok].add(
        y.reshape(P, D).astype(jnp.float32) * buf_w[:, None])
    shared = (jax.nn.silu(hf @ sh_gate) * (hf @ sh_up)) @ sh_down
    return (routed + shared.astype(jnp.float32)).astype(h.dtype).reshape(B, T, D)


def setup_inputs(seed: int = 0) -> dict:
    key = jax.random.key(seed)
    ks = jax.random.split(key, 26)
    f32 = jnp.float32
    beta = (8.0 * DEPTH) ** -0.25

    def nrm(k, shape, fan_in, scale=1.0):
        return jax.random.normal(k, shape, f32) * (scale * fan_in ** -0.5)

    def gain(k):
        return 1.0 + 0.02 * jax.random.normal(k, (DEPTH, D_MODEL), f32)

    def small(k, shape):
        return 0.02 * jax.random.normal(k, shape, f32)

    col_scale = jnp.concatenate([jnp.ones((2 * ATTN_WIDTH,), f32), jnp.full((ATTN_WIDTH,), beta, f32),
                                 jnp.ones((POOL_WIDTH,), f32)])
    return {
        'x': jax.random.normal(ks[0], (BATCH, SEQ, D_MODEL), f32),
        'mem': jax.random.normal(ks[1], (BATCH, MEM_LEN, D_MODEL), f32),
        'w_in': nrm(ks[2], (DEPTH, D_MODEL, IN_WIDTH), D_MODEL) * col_scale,
        'pool_w': nrm(ks[3], (DEPTH, POOL_GROUPS, POOL_GROUP_DIM, POOL_GROUP_DIM), POOL_GROUP_DIM),
        'pool_scale': 1.0 + 0.02 * jax.random.normal(ks[4], (DEPTH, POOL_WIDTH), f32),
        'w_out': nrm(ks[5], (DEPTH, MIX_WIDTH, D_MODEL), MIX_WIDTH, beta),
        'ln1_g': gain(ks[6]),
        'ln1_b': small(ks[7], (DEPTH, D_MODEL)),
        'w_cq': nrm(ks[8], (DEPTH, D_MODEL, MEM_WIDTH), D_MODEL),
        'w_ck': nrm(ks[9], (DEPTH, D_MODEL, MEM_WIDTH), D_MODEL),
        'w_cv': nrm(ks[10], (DEPTH, D_MODEL, MEM_WIDTH), D_MODEL, beta),
        'w_co': nrm(ks[11], (DEPTH, MEM_WIDTH, D_MODEL), MEM_WIDTH, beta),
        'ln2_g': gain(ks[12]),
        'ln2_b': small(ks[13], (DEPTH, D_MODEL)),
        'w_router': nrm(ks[14], (DEPTH, D_MODEL, N_EXPERTS), D_MODEL),
        'router_bias': 0.01 * jax.random.normal(ks[15], (DEPTH, N_EXPERTS), f32),
        'exp_gate': nrm(ks[16], (DEPTH, N_EXPERTS, D_MODEL, EXPERT_HIDDEN), D_MODEL),
        'exp_up': nrm(ks[17], (DEPTH, N_EXPERTS, D_MODEL, EXPERT_HIDDEN), D_MODEL),
        'exp_down': nrm(ks[18], (DEPTH, N_EXPERTS, EXPERT_HIDDEN, D_MODEL), EXPERT_HIDDEN, beta),
        'sh_gate': nrm(ks[19], (DEPTH, D_MODEL, SHARED_HIDDEN), D_MODEL),
        'sh_up': nrm(ks[20], (DEPTH, D_MODEL, SHARED_HIDDEN), D_MODEL),
        'sh_down': nrm(ks[21], (DEPTH, SHARED_HIDDEN, D_MODEL), SHARED_HIDDEN, beta),
        'ln3_g': gain(ks[22]),
        'ln3_b': small(ks[23], (DEPTH, D_MODEL)),
    }


def reference(x, mem, w_in, pool_w, pool_scale, w_out, ln1_g, ln1_b, w_cq, w_ck, w_cv, w_co,
              ln2_g, ln2_b, w_router, router_bias, exp_gate, exp_up, exp_down, sh_gate, sh_up,
              sh_down, ln3_g, ln3_b):
    alpha = (2.0 * DEPTH) ** 0.25
    for l in range(DEPTH):
        x = layer_norm(alpha * x + hybrid_mixer(x, w_in[l], pool_w[l], pool_scale[l], w_out[l]),
                       ln1_g[l], ln1_b[l])
        x = layer_norm(alpha * x + memory_cross_attention(x, mem, w_cq[l], w_ck[l], w_cv[l], w_co[l]),
                       ln2_g[l], ln2_b[l])
        x = layer_norm(alpha * x + moe_ffn(x, w_router[l], router_bias[l], exp_gate[l], exp_up[l],
                                           exp_down[l], sh_gate[l], sh_up[l], sh_down[l]),
                       ln3_g[l], ln3_b[l])
    return x
```

```python
import functools

import jax
import jax.numpy as jnp
from jax import lax
from jax.experimental import pallas as pl
from jax.experimental.pallas import tpu as pltpu

F32 = jnp.float32
BF16 = jnp.bfloat16
I32 = jnp.int32

ATTN_HEADS = 8
HEAD_DIM = 64
ATTN_WIDTH = ATTN_HEADS * HEAD_DIM
DILATED_PATTERNS = ((128, 1), (512, 4), (2048, 16))
POOL_WINDOWS = (2, 4, 8, 16)
MEM_HEADS = 4
N_EXPERT_GROUPS = 8
TOPK_GROUPS = 4
TOP_K = 8
ROUTED_SCALE = 2.5
LN_EPS = 1e-5
NEG_INF = -1e30

LANES = 128
SUBLANES = 8
VMEM_LIMIT = 56 * 1024 * 1024

ATTN_BLOCK = 128
ROW_TILE = 256
IN_TILE = 512
ROUTE_TILE = 256
EXPERT_BLOCK = 256
MOVE_TILE = 128


def _cparams(sem, vmem=VMEM_LIMIT):
    return pltpu.CompilerParams(dimension_semantics=sem, vmem_limit_bytes=vmem)


def _layer_norm(y, g, b):
    mu = jnp.mean(y, axis=-1, keepdims=True)
    d = y - mu
    var = jnp.mean(d * d, axis=-1, keepdims=True)
    return d * lax.rsqrt(var + LN_EPS) * g + b


def _dot(a, b):
    return jnp.dot(a, b, preferred_element_type=F32)


def _dot_nt(a, b):
    return lax.dot_general(a, b, (((1,), (1,)), ((), ())), preferred_element_type=F32)


def _split_bf16(x):
    hi = x.astype(BF16)
    lo = (x - hi.astype(F32)).astype(BF16)
    return hi, lo


def _in_proj_kernel(x_ref, w_ref, q_ref, k_ref, v_ref, p_ref):
    x = x_ref[...].astype(BF16)
    aw = ATTN_WIDTH
    q_ref[...] = (_dot(x, w_ref[:, 0:aw]) * (HEAD_DIM ** -0.5)).astype(BF16)
    k_ref[...] = _dot(x, w_ref[:, aw:2 * aw]).astype(BF16)
    v_ref[...] = _dot(x, w_ref[:, 2 * aw:3 * aw]).astype(BF16)
    p_ref[...] = _dot(x, w_ref[:, 3 * aw:])


def _in_proj(x2d, w_in_bf):
    n, d = x2d.shape
    pw = w_in_bf.shape[1] - 3 * ATTN_WIDTH
    tm = IN_TILE
    row = lambda w: pl.BlockSpec((tm, w), lambda i: (i, 0))
    return pl.pallas_call(
        _in_proj_kernel,
        grid=(n // tm,),
        in_specs=[row(d), pl.BlockSpec(w_in_bf.shape, lambda i: (0, 0))],
        out_specs=[row(ATTN_WIDTH), row(ATTN_WIDTH), row(ATTN_WIDTH), row(pw)],
        out_shape=[jax.ShapeDtypeStruct((n, ATTN_WIDTH), BF16)] * 3 + [jax.ShapeDtypeStruct((n, pw), F32)],
        compiler_params=_cparams(("arbitrary",)),
        name="in_proj",
    )(x2d, w_in_bf)


def _dil_attn_kernel(q_ref, kc_ref, kp_ref, vc_ref, vp_ref, o_ref, lse_ref, *, dil):
    blk = ATTN_BLOCK
    i = pl.program_id(1)
    qi = lax.broadcasted_iota(I32, (blk, 2 * blk), 0)
    kj = lax.broadcasted_iota(I32, (blk, 2 * blk), 1)
    dist = qi + blk - kj
    valid = (dist >= 0) & (dist <= blk) & ((kj >= blk) | (i > 0))
    penalty = (dist * dil).astype(F32)
    lane = lax.broadcasted_iota(I32, (blk, LANES), 1)
    low_half = lane < HEAD_DIM
    lse_tile = jnp.zeros((blk, LANES), F32)
    for hp in range(ATTN_WIDTH // LANES):
        sl = slice(hp * LANES, (hp + 1) * LANES)
        q2 = q_ref[:, sl]
        kcat = jnp.concatenate([kp_ref[:, sl], kc_ref[:, sl]], axis=0)
        vcat = jnp.concatenate([vp_ref[:, sl], vc_ref[:, sl]], axis=0)
        outs = []
        for e in range(LANES // HEAD_DIM):
            h = hp * (LANES // HEAD_DIM) + e
            slope = 2.0 ** (-8.0 * (h + 1) / ATTN_HEADS)
            keep = low_half if e == 0 else jnp.logical_not(low_half)
            qe = jnp.where(keep, q2, jnp.zeros_like(q2))
            s = _dot_nt(qe, kcat) - slope * penalty
            s = jnp.where(valid, s, NEG_INF)
            m = jnp.max(s, axis=1, keepdims=True)
            p = jnp.exp(s - m)
            l = jnp.sum(p, axis=1, keepdims=True)
            outs.append(_dot(p.astype(BF16), vcat) / l)
            lse_tile = jnp.where(lane == h, m + jnp.log(l), lse_tile)
        o_ref[:, sl] = jnp.where(low_half, outs[0], outs[1]).astype(BF16)
    lse_ref[...] = lse_tile


def _dil_attn(q, k, v, batch, seq, dil):
    blk = ATTN_BLOCK
    sub = seq // dil
    nb = sub // blk
    view = lambda a: a.reshape(batch, sub, dil * ATTN_WIDTH)
    cur = pl.BlockSpec((None, blk, ATTN_WIDTH), lambda b, i, r: (b, i, r))
    prev = pl.BlockSpec((None, blk, ATTN_WIDTH), lambda b, i, r: (b, jnp.maximum(i - 1, 0), r))
    o, lse = pl.pallas_call(
        functools.partial(_dil_attn_kernel, dil=dil),
        grid=(batch, nb, dil),
        in_specs=[cur, cur, prev, cur, prev],
        out_specs=[cur, pl.BlockSpec((None, blk, LANES), lambda b, i, r: (b, i, r))],
        out_shape=[jax.ShapeDtypeStruct((batch, sub, dil * ATTN_WIDTH), BF16),
                   jax.ShapeDtypeStruct((batch, sub, dil * LANES), F32)],
        compiler_params=_cparams(("arbitrary",) * 3),
        name=f"dil_attn_d{dil}",
    )(view(q), view(k), view(k), view(v), view(v))
    return o.reshape(batch * seq, ATTN_WIDTH), lse.reshape(batch * seq, LANES)


def _mem_kv_kernel(mem_ref, wk_ref, wv_ref, k_ref, v_ref):
    m = mem_ref[...].astype(BF16)
    k_ref[...] = _dot(m, wk_ref[...]).astype(BF16)
    v_ref[...] = _dot(m, wv_ref[...]).astype(BF16)


def _mem_kv(mem2d, w_ck_bf, w_cv_bf):
    rows = mem2d.shape[0]
    width = w_ck_bf.shape[1]
    return pl.pallas_call(
        _mem_kv_kernel,
        out_shape=[jax.ShapeDtypeStruct((rows, width), BF16)] * 2,
        compiler_params=_cparams(None),
        name="mem_kv",
    )(mem2d, w_ck_bf, w_cv_bf)


def _row_kernel(x_ref, o1_ref, o4_ref, o16_ref, l1_ref, l4_ref, l16_ref, p_ref, halo_ref,
                poolw_ref, pscale_ref, wout_ref, g1_ref, b1_ref,
                wcq_ref, km_ref, vm_ref, wco_ref, g2_ref, b2_ref,
                wrh_ref, wrl_ref, shg_ref, shu_ref, shd_ref,
                x2_ref, base_ref, logit_ref, *, alpha, seq):
    tm = x_ref.shape[0]
    i = pl.program_id(0)
    tile_pos = (i * tm) % seq

    l1, l4, l16 = l1_ref[...], l4_ref[...], l16_ref[...]
    mx = jnp.maximum(jnp.maximum(l1, l4), l16)
    e1, e4, e16 = jnp.exp(l1 - mx), jnp.exp(l4 - mx), jnp.exp(l16 - mx)
    inv = 1.0 / (e1 + e4 + e16)
    hrow = lax.broadcasted_iota(I32, (LANES, ATTN_WIDTH), 0)
    hcol = lax.broadcasted_iota(I32, (LANES, ATTN_WIDTH), 1) // HEAD_DIM
    spread = jnp.where(hrow == hcol, 1.0, 0.0).astype(BF16)

    def widen(w):
        hi, lo = _split_bf16(w)
        return _dot(hi, spread) + _dot(lo, spread)

    attn = (widen(e1 * inv) * o1_ref[...].astype(F32)
            + widen(e4 * inv) * o4_ref[...].astype(F32)
            + widen(e16 * inv) * o16_ref[...].astype(F32))

    halo_rows = halo_ref.shape[0]
    halo = jnp.where(tile_pos > 0, halo_ref[...], 0.0)
    ext = jnp.concatenate([halo, p_ref[...]], axis=0)
    pos = (tile_pos + lax.broadcasted_iota(I32, (tm, 1), 0)).astype(F32)
    gd = ext.shape[1] // len(POOL_WINDOWS)
    mixed = []
    for g, w in enumerate(POOL_WINDOWS):
        eg = ext[:, g * gd:(g + 1) * gd]
        acc, span = eg, 1
        while span < w:
            acc = acc + pltpu.roll(acc, span, 0)
            span *= 2
        count = jnp.minimum(pos + 1.0, float(w))
        pooled = acc[halo_rows:, :] / count - eg[halo_rows:, :]
        mixed.append(_dot(pooled.astype(BF16), poolw_ref[g]) * pscale_ref[:, g * gd:(g + 1) * gd])
    cat = jnp.concatenate([attn.astype(BF16)] + [m.astype(BF16) for m in mixed], axis=1)
    x1 = _layer_norm(alpha * x_ref[...] + _dot(cat, wout_ref[...]), g1_ref[...], b1_ref[...])

    qc = _dot(x1.astype(BF16), wcq_ref[...])
    mhd = qc.shape[1] // MEM_HEADS
    heads = []
    for h in range(MEM_HEADS):
        sl = slice(h * mhd, (h + 1) * mhd)
        s = _dot_nt(qc[:, sl].astype(BF16), km_ref[:, sl]) * (mhd ** -0.5)
        m = jnp.max(s, axis=1, keepdims=True)
        p = jnp.exp(s - m)
        l = jnp.sum(p, axis=1, keepdims=True)
        heads.append((_dot(p.astype(BF16), vm_ref[:, sl]) / l).astype(BF16))
    oc = jnp.concatenate(heads, axis=1)
    x2 = _layer_norm(alpha * x1 + _dot(oc, wco_ref[...]), g2_ref[...], b2_ref[...])
    x2_ref[...] = x2

    xh, xl = _split_bf16(x2)
    logit_ref[...] = _dot_nt(wrh_ref[...], xh) + _dot_nt(wrh_ref[...], xl) + _dot_nt(wrl_ref[...], xh)

    gate = _dot(xh, shg_ref[...])
    up = _dot(xh, shu_ref[...])
    hid = gate / (1.0 + jnp.exp(-gate)) * up
    base_ref[...] = alpha * x2 + _dot(hid.astype(BF16), shd_ref[...])


def _row_block(x2d, o1, o4, o16, l1, l4, l16, p, pool_w_bf, pool_scale, w_out_bf, g1, b1,
               w_cq_bf, kmem, vmem, w_co_bf, g2, b2, wr_hi, wr_lo, sh_g, sh_u, sh_d,
               *, alpha, batch, seq, mem_len):
    n, d = x2d.shape
    tm = ROW_TILE
    halo_rows = max(POOL_WINDOWS)
    steps_per_seq = seq // tm
    row = lambda w: pl.BlockSpec((tm, w), lambda i: (i, 0))
    full = lambda a: pl.BlockSpec(a.shape, lambda i: (0,) * a.ndim)
    halo = pl.BlockSpec((halo_rows, p.shape[1]), lambda i: (jnp.maximum(i * (tm // halo_rows) - 1, 0), 0))
    memspec = pl.BlockSpec((mem_len, kmem.shape[1]), lambda i: (i // steps_per_seq, 0))
    n_exp = wr_hi.shape[0]
    return pl.pallas_call(
        functools.partial(_row_kernel, alpha=alpha, seq=seq),
        grid=(n // tm,),
        in_specs=[row(d), row(ATTN_WIDTH), row(ATTN_WIDTH), row(ATTN_WIDTH), row(LANES), row(LANES), row(LANES),
                  row(p.shape[1]), halo,
                  full(pool_w_bf), full(pool_scale), full(w_out_bf), full(g1), full(b1),
                  full(w_cq_bf), memspec, memspec, full(w_co_bf), full(g2), full(b2),
                  full(wr_hi), full(wr_lo), full(sh_g), full(sh_u), full(sh_d)],
        out_specs=[row(d), row(d), pl.BlockSpec((n_exp, tm), lambda i: (0, i))],
        out_shape=[jax.ShapeDtypeStruct((n, d), F32), jax.ShapeDtypeStruct((n, d), F32),
                   jax.ShapeDtypeStruct((n_exp, n), F32)],
        compiler_params=_cparams(("arbitrary",)),
        name="row_block",
    )(x2d, o1, o4, o16, l1, l4, l16, p, p, pool_w_bf, pool_scale, w_out_bf, g1, b1,
      w_cq_bf, kmem, vmem, w_co_bf, g2, b2, wr_hi, wr_lo, sh_g, sh_u, sh_d)


def _beat_counts(vals, n_rows):
    tn = vals.shape[1]
    n_tiles = n_rows // SUBLANES
    tiles = [vals[t * SUBLANES:(t + 1) * SUBLANES, :] for t in range(n_tiles)]
    sub = lax.broadcasted_iota(I32, (SUBLANES, tn), 0)
    counts = [jnp.zeros((SUBLANES, tn), F32) for _ in range(n_tiles)]
    for e in range(n_rows):
        te, je = divmod(e, SUBLANES)
        row = jnp.broadcast_to(vals[e:e + 1, :], (SUBLANES, tn))
        for t in range(n_tiles):
            strict = jnp.where(row > tiles[t], 1.0, 0.0)
            loose = jnp.where(row >= tiles[t], 1.0, 0.0)
            if t < te:
                beat = strict
            elif t > te:
                beat = loose
            else:
                beat = jnp.where(sub > je, loose, strict)
            counts[t] = counts[t] + beat
    return jnp.concatenate(counts, axis=0)


def _route_kernel(logit_ref, bias_ref, eidx_ref, rank_ref, gate_ref, count_ref, carry_ref, *, per_group):
    n_exp, tn = logit_ref.shape
    step = pl.program_id(0)

    @pl.when(step == 0)
    def _():
        carry_ref[...] = jnp.zeros_like(carry_ref)

    scores = 1.0 / (1.0 + jnp.exp(-logit_ref[...]))
    biased = scores + bias_ref[:, 0:1]
    n_groups = n_exp // per_group

    gscore = []
    for g in range(n_groups):
        tile = biased[g * per_group:(g + 1) * per_group, :]
        inner = _beat_counts(tile, per_group)
        gscore.append(jnp.sum(jnp.where(inner < 2.0, tile, 0.0), axis=0, keepdims=True))
    gscore = jnp.concatenate(gscore, axis=0)
    grank = _beat_counts(gscore, n_groups)
    grank = jnp.concatenate(
        [jnp.broadcast_to(grank[g:g + 1, :], (per_group, tn)) for g in range(n_groups)], axis=0)
    masked = jnp.where(grank < float(TOPK_GROUPS), biased, NEG_INF)
    self32 = jnp.where(_beat_counts(masked, n_exp) < float(TOP_K), 1.0, 0.0)
    selbf = self32.astype(BF16)

    er = lax.broadcasted_iota(I32, (n_exp, n_exp), 0)
    ec = lax.broadcasted_iota(I32, (n_exp, n_exp), 1)
    below = jnp.where(ec < er, 1.0, 0.0).astype(BF16)
    slot = _dot(below, selbf)
    tr = lax.broadcasted_iota(I32, (tn, tn), 0)
    tc = lax.broadcasted_iota(I32, (tn, tn), 1)
    before = jnp.where(tr < tc, 1.0, 0.0).astype(BF16)
    rank = _dot(selbf, before) + carry_ref[:, 0:1]
    carry_ref[...] = carry_ref[...] + jnp.sum(self32, axis=1, keepdims=True)
    count_ref[...] = carry_ref[...]

    ridx = lax.broadcasted_iota(I32, (n_exp, tn), 0).astype(F32)
    eidx, ranks, gates = [], [], []
    for k in range(TOP_K):
        pick = jnp.where(slot == float(k), self32, 0.0)
        eidx.append(jnp.sum(pick * ridx, axis=0, keepdims=True))
        ranks.append(jnp.sum(pick * rank, axis=0, keepdims=True))
        gates.append(jnp.sum(pick * scores, axis=0, keepdims=True))
    gates = jnp.concatenate(gates, axis=0)
    gates = gates / jnp.sum(gates, axis=0, keepdims=True) * ROUTED_SCALE
    eidx_ref[...] = jnp.concatenate(eidx, axis=0).astype(I32)
    rank_ref[...] = jnp.concatenate(ranks, axis=0).astype(I32)
    gate_ref[...] = gates


def _route(logits_t, bias_col):
    n_exp, n = logits_t.shape
    tn = ROUTE_TILE
    tok = lambda dt: jax.ShapeDtypeStruct((TOP_K, n), dt)
    tokspec = pl.BlockSpec((TOP_K, tn), lambda i: (0, i))
    return pl.pallas_call(
        functools.partial(_route_kernel, per_group=n_exp // N_EXPERT_GROUPS),
        grid=(n // tn,),
        in_specs=[pl.BlockSpec((n_exp, tn), lambda i: (0, i)), pl.BlockSpec(bias_col.shape, lambda i: (0, 0))],
        out_specs=[tokspec, tokspec, tokspec, pl.BlockSpec((n_exp, LANES), lambda i: (0, 0))],
        out_shape=[tok(I32), tok(I32), tok(F32), jax.ShapeDtypeStruct((n_exp, LANES), F32)],
        scratch_shapes=[pltpu.VMEM((n_exp, LANES), F32)],
        compiler_params=_cparams(("arbitrary",)),
        name="route",
    )(logits_t, bias_col)


def _place_kernel(count_ref, eidx_ref, rank_ref, dest_ref, bexp_ref, seg_ref, *, n_blocks):
    n_exp = count_ref.shape[0]
    blk = float(EXPERT_BLOCK)
    counts = count_ref[...]
    padded = jnp.ceil(counts / blk) * blk
    ridx = lax.broadcasted_iota(I32, (n_exp, LANES), 0)
    ends = padded
    shift = 1
    while shift < n_exp:
        ends = ends + jnp.where(ridx >= shift, pltpu.roll(ends, shift, 0), 0.0)
        shift *= 2
    starts = ends - padded

    eidx = eidx_ref[...]
    dest = rank_ref[...].astype(F32)
    for e in range(n_exp):
        dest = dest + jnp.where(eidx == e, starts[e:e + 1, 0:1], 0.0)
    dest_ref[...] = dest.astype(I32)

    nbp = bexp_ref.shape[1]
    blk_start = (lax.broadcasted_iota(I32, (n_exp, nbp), 1) * EXPERT_BLOCK).astype(F32)
    passed = jnp.where(jnp.broadcast_to(ends[:, 0:1], (n_exp, nbp)) <= blk_start, 1.0, 0.0)
    bexp = jnp.minimum(jnp.sum(passed, axis=0, keepdims=True), float(n_exp - 1))
    used = ends[n_exp - 1:n_exp, 0:1] / blk
    lane = lax.broadcasted_iota(I32, (1, nbp), 1)
    bexp_ref[...] = jnp.where(lane == n_blocks, used, bexp).astype(I32)
    lane2 = lax.broadcasted_iota(I32, (n_exp, LANES), 1)
    seg_ref[...] = jnp.where(lane2 == 0, starts + counts, ends).astype(I32)


def _place(counts, eidx, rank, n_blocks):
    n_exp = counts.shape[0]
    n = eidx.shape[1]
    nbp = -(-(n_blocks + 1) // LANES) * LANES
    return pl.pallas_call(
        functools.partial(_place_kernel, n_blocks=n_blocks),
        out_shape=[jax.ShapeDtypeStruct((TOP_K, n), I32), jax.ShapeDtypeStruct((1, nbp), I32),
                   jax.ShapeDtypeStruct((n_exp, LANES), I32)],
        compiler_params=_cparams(None),
        name="route_place",
    )(counts, eidx, rank)


def _dispatch_kernel(dest_ref, seg_ref, x_ref, xs_ref, zero_ref, sem, zsem):
    tm = x_ref.shape[0]
    n_exp = seg_ref.shape[0]

    @pl.when(pl.program_id(0) == 0)
    def _():
        zero_ref[...] = jnp.zeros_like(zero_ref)

        def pad_copy(r):
            return pltpu.make_async_copy(zero_ref.at[pl.ds(0, 1), :], xs_ref.at[pl.ds(r, 1), :], zsem)

        def per_expert(e, total):
            lo, hi = seg_ref[e, 0], seg_ref[e, 1]
            lax.fori_loop(lo, hi, lambda r, c: (pad_copy(r).start(), c)[1], 0)
            return total + (hi - lo)

        total = lax.fori_loop(0, n_exp, per_expert, 0)
        lax.fori_loop(0, total, lambda r, c: (pad_copy(0).wait(), c)[1], 0)

    def row_copy(j, k):
        return pltpu.make_async_copy(x_ref.at[pl.ds(j, 1), :], xs_ref.at[pl.ds(dest_ref[k, j], 1), :], sem)

    def start_row(j, c):
        for k in range(TOP_K):
            row_copy(j, k).start()
        return c

    def wait_row(j, c):
        for k in range(TOP_K):
            row_copy(j, k).wait()
        return c

    lax.fori_loop(0, tm, start_row, 0)
    lax.fori_loop(0, tm, wait_row, 0)


def _dispatch(x2, dest, seg, n_rows):
    n, d = x2.shape
    tm = MOVE_TILE
    return pl.pallas_call(
        _dispatch_kernel,
        grid=(n // tm,),
        in_specs=[pl.BlockSpec((TOP_K, tm), lambda i: (0, i), memory_space=pltpu.SMEM),
                  pl.BlockSpec(seg.shape, lambda i: (0, 0), memory_space=pltpu.SMEM),
                  pl.BlockSpec((tm, d), lambda i: (i, 0))],
        out_specs=pl.BlockSpec(memory_space=pl.ANY),
        out_shape=jax.ShapeDtypeStruct((n_rows, d), F32),
        scratch_shapes=[pltpu.VMEM((SUBLANES, d), F32), pltpu.SemaphoreType.DMA, pltpu.SemaphoreType.DMA],
        compiler_params=_cparams(("arbitrary",)),
        name="dispatch",
    )(dest, seg, x2)


def _expert_kernel(bexp_ref, xs_ref, wg_ref, wu_ref, wd_ref, ys_ref, wg_bf, wu_bf, wd_bf, *, n_blocks):
    i = pl.program_id(0)
    used = bexp_ref[n_blocks]

    @pl.when(i < used)
    def _():
        @pl.when((i == 0) | (bexp_ref[i] != bexp_ref[jnp.maximum(i - 1, 0)]))
        def _():
            wg_bf[...] = wg_ref[...].astype(BF16)
            wu_bf[...] = wu_ref[...].astype(BF16)
            wd_bf[...] = wd_ref[...].astype(BF16)

        x = xs_ref[...].astype(BF16)
        gate = _dot(x, wg_bf[...])
        up = _dot(x, wu_bf[...])
        hid = gate / (1.0 + jnp.exp(-gate)) * up
        ys_ref[...] = _dot(hid.astype(BF16), wd_bf[...])


def _experts(bexp, xs, exp_gate, exp_up, exp_down, n_blocks):
    n_rows, d = xs.shape
    hidden = exp_gate.shape[2]
    blk = EXPERT_BLOCK

    def rows(i, b):
        return (jnp.minimum(i, jnp.maximum(b[n_blocks] - 1, 0)), 0)

    def weight(i, b):
        return (b[jnp.minimum(i, jnp.maximum(b[n_blocks] - 1, 0))], 0, 0)

    grid_spec = pltpu.PrefetchScalarGridSpec(
        num_scalar_prefetch=1,
        grid=(n_blocks,),
        in_specs=[pl.BlockSpec((blk, d), rows),
                  pl.BlockSpec((None, d, hidden), weight),
                  pl.BlockSpec((None, d, hidden), weight),
                  pl.BlockSpec((None, hidden, d), weight)],
        out_specs=pl.BlockSpec((blk, d), rows),
        scratch_shapes=[pltpu.VMEM((d, hidden), BF16), pltpu.VMEM((d, hidden), BF16), pltpu.VMEM((hidden, d), BF16)],
    )
    return pl.pallas_call(
        functools.partial(_expert_kernel, n_blocks=n_blocks),
        grid_spec=grid_spec,
        out_shape=jax.ShapeDtypeStruct((n_rows, d), F32),
        compiler_params=_cparams(("arbitrary",)),
        name="experts",
    )(bexp, xs, exp_gate, exp_up, exp_down)


def _combine_kernel(dest_ref, gate_ref, base_ref, g3_ref, b3_ref, ys_ref, out_ref, buf_ref, sem):
    tm = base_ref.shape[0]

    def row_copy(j, k):
        return pltpu.make_async_copy(ys_ref.at[pl.ds(dest_ref[k, j], 1), :], buf_ref.at[k, pl.ds(j, 1), :], sem)

    def start_row(j, c):
        for k in range(TOP_K):
            row_copy(j, k).start()
        return c

    def wait_row(j, c):
        for k in range(TOP_K):
            row_copy(j, k).wait()
        return c

    lax.fori_loop(0, tm, start_row, 0)
    gates = jnp.concatenate([gate_ref[...], jnp.zeros((LANES - TOP_K, tm), F32)], axis=0)
    gates = gates.T
    lax.fori_loop(0, tm, wait_row, 0)
    acc = base_ref[...]
    for k in range(TOP_K):
        acc = acc + gates[:, k:k + 1] * buf_ref[k]
    out_ref[...] = _layer_norm(acc, g3_ref[...], b3_ref[...])


def _combine(dest, gates, base, g3, b3, ys):
    n, d = base.shape
    tm = MOVE_TILE
    return pl.pallas_call(
        _combine_kernel,
        grid=(n // tm,),
        in_specs=[pl.BlockSpec((TOP_K, tm), lambda i: (0, i), memory_space=pltpu.SMEM),
                  pl.BlockSpec((TOP_K, tm), lambda i: (0, i)),
                  pl.BlockSpec((tm, d), lambda i: (i, 0)),
                  pl.BlockSpec(g3.shape, lambda i: (0, 0)),
                  pl.BlockSpec(b3.shape, lambda i: (0, 0)),
                  pl.BlockSpec(memory_space=pl.ANY)],
        out_specs=pl.BlockSpec((tm, d), lambda i: (i, 0)),
        out_shape=jax.ShapeDtypeStruct((n, d), F32),
        scratch_shapes=[pltpu.VMEM((TOP_K, tm, d), F32), pltpu.SemaphoreType.DMA],
        compiler_params=_cparams(("arbitrary",)),
        name="combine",
    )(dest, gates, base, g3, b3, ys)


def _layer(x2d, mem2d, w_in, pool_w, pool_scale, w_out, ln1_g, ln1_b, w_cq, w_ck, w_cv, w_co, ln2_g, ln2_b,
           w_router, router_bias, exp_gate, exp_up, exp_down, sh_gate, sh_up, sh_down, ln3_g, ln3_b,
           *, alpha, batch, seq, mem_len):
    n, d = x2d.shape
    n_exp = w_router.shape[1]
    row = lambda a: a.reshape(1, -1)
    bf = lambda a: a.astype(BF16)

    q, k, v, p = _in_proj(x2d, bf(w_in))
    branches = []
    for window, dil in DILATED_PATTERNS:
        assert window // dil == ATTN_BLOCK and seq % (dil * ATTN_BLOCK) == 0
        branches.append(_dil_attn(q, k, v, batch, seq, dil))
    (o1, l1), (o4, l4), (o16, l16) = branches
    kmem, vmem = _mem_kv(mem2d, bf(w_ck), bf(w_cv))

    wr_t = w_router.T
    wr_hi = bf(wr_t)
    wr_lo = bf(wr_t - wr_hi.astype(F32))
    x2, base, logits_t = _row_block(
        x2d, o1, o4, o16, l1, l4, l16, p, bf(pool_w), row(pool_scale), bf(w_out), row(ln1_g), row(ln1_b),
        bf(w_cq), kmem, vmem, bf(w_co), row(ln2_g), row(ln2_b), wr_hi, wr_lo, bf(sh_gate), bf(sh_up), bf(sh_down),
        alpha=alpha, batch=batch, seq=seq, mem_len=mem_len)

    bias_col = jnp.broadcast_to(router_bias.reshape(n_exp, 1), (n_exp, LANES))
    eidx, rank, gates, counts = _route(logits_t, bias_col)
    n_blocks = (n * TOP_K) // EXPERT_BLOCK + n_exp
    dest, bexp, seg = _place(counts, eidx, rank, n_blocks)
    xs = _dispatch(x2, dest, seg, n_blocks * EXPERT_BLOCK)
    ys = _experts(bexp.reshape(-1), xs, exp_gate, exp_up, exp_down, n_blocks)
    return _combine(dest, gates, base, row(ln3_g), row(ln3_b), ys)


def kernel(x, mem, w_in, pool_w, pool_scale, w_out, ln1_g, ln1_b, w_cq, w_ck, w_cv, w_co, ln2_g, ln2_b, w_router, router_bias, exp_gate, exp_up, exp_down, sh_gate, sh_up, sh_down, ln3_g, ln3_b):
    batch, seq, d = x.shape
    mem_len = mem.shape[1]
    depth = w_in.shape[0]
    alpha = (2.0 * depth) ** 0.25
    x2d = x.reshape(batch * seq, d)
    mem2d = mem.reshape(batch * mem_len, d)
    for l in range(depth):
        x2d = _layer(x2d, mem2d, w_in[l], pool_w[l], pool_scale[l], w_out[l], ln1_g[l], ln1_b[l], w_cq[l], w_ck[l],
                     w_cv[l], w_co[l], ln2_g[l], ln2_b[l], w_router[l], router_bias[l], exp_gate[l], exp_up[l],
                     exp_down[l], sh_gate[l], sh_up[l], sh_down[l], ln3_g[l], ln3_b[l],
                     alpha=alpha, batch=batch, seq=seq, mem_len=mem_len)
    return x2d.reshape(batch, seq, d)
```

```python
import functools

import jax
import jax.numpy as jnp
from jax import lax
from jax.experimental import pallas as pl
from jax.experimental.pallas import tpu as pltpu

F32 = jnp.float32
BF16 = jnp.bfloat16
I32 = jnp.int32
U32 = jnp.uint32

ATTN_HEADS = 8
HEAD_DIM = 64
ATTN_WIDTH = ATTN_HEADS * HEAD_DIM
DILATED_PATTERNS = ((128, 1), (512, 4), (2048, 16))
POOL_WINDOWS = (2, 4, 8, 16)
MEM_HEADS = 4
N_EXPERT_GROUPS = 8
TOPK_GROUPS = 4
TOP_K = 8
ROUTED_SCALE = 2.5
LN_EPS = 1e-5
NEG_INF = -1e30

LANES = 128
SUBLANES = 8
VMEM_LIMIT = 56 * 1024 * 1024

ATTN_BLOCK = 128
ROW_TILE = 256
IN_TILE = 512
ROUTE_TILE = 256
EXPERT_BLOCK = 256
MOVE_TILE = 128
DMA_PRIORITIES = 2


def _cparams(sem, vmem=VMEM_LIMIT):
    return pltpu.CompilerParams(dimension_semantics=sem, vmem_limit_bytes=vmem)


def _layer_norm(y, g, b):
    mu = jnp.mean(y, axis=-1, keepdims=True)
    d = y - mu
    var = jnp.mean(d * d, axis=-1, keepdims=True)
    return d * lax.rsqrt(var + LN_EPS) * g + b


def _dot(a, b):
    return jnp.dot(a, b, preferred_element_type=F32)


def _dot_nt(a, b):
    return lax.dot_general(a, b, (((1,), (1,)), ((), ())), preferred_element_type=F32)


def _split_bf16(x):
    hi = x.astype(BF16)
    lo = (x - hi.astype(F32)).astype(BF16)
    return hi, lo


def _bf16_bits(x):
    return lax.bitcast_convert_type(x.astype(BF16).astype(F32), U32)


def _pack_pair(lo, hi):
    return (_bf16_bits(lo) >> 16) | (_bf16_bits(hi) & jnp.uint32(0xFFFF0000))


def _unpack_pair(w):
    lo = lax.bitcast_convert_type(w << 16, F32)
    hi = lax.bitcast_convert_type(w & jnp.uint32(0xFFFF0000), F32)
    return lo, hi


def _in_proj_kernel(x_ref, w_ref, *refs, dils):
    n_qkv = 3 * len(dils)
    out_refs, p_ref, slab_ref = refs[:n_qkv], refs[n_qkv], refs[n_qkv + 1]
    tm = x_ref.shape[0]
    x = x_ref[...].astype(BF16)
    aw = ATTN_WIDTH
    n_slabs = aw // LANES
    for a in range(3):
        val = _dot(x, w_ref[:, a * aw:(a + 1) * aw])
        if a == 0:
            val = val * (HEAD_DIM ** -0.5)
        for s in range(n_slabs):
            slab_ref[s] = val[:, s * LANES:(s + 1) * LANES]
        for di, dil in enumerate(dils):
            dst = out_refs[3 * di + a]
            if dil == 1:
                dst[...] = val.astype(BF16)
                continue
            for r in range(dil):
                for s in range(n_slabs):
                    rows = slab_ref[s, pl.ds(r, tm // dil, stride=dil), :]
                    dst[:, r * aw + s * LANES:r * aw + (s + 1) * LANES] = rows.astype(BF16)
    p_ref[...] = _dot(x, w_ref[:, 3 * aw:])


def _in_proj(x2d, w_in_bf, dils):
    n, d = x2d.shape
    aw = ATTN_WIDTH
    pw = w_in_bf.shape[1] - 3 * aw
    tm = IN_TILE
    row = lambda rows, w: pl.BlockSpec((rows, w), lambda i: (i, 0))
    qkv_specs, qkv_shapes = [], []
    for dil in dils:
        qkv_specs += [row(tm // dil, dil * aw)] * 3
        qkv_shapes += [jax.ShapeDtypeStruct((n // dil, dil * aw), BF16)] * 3
    return pl.pallas_call(
        functools.partial(_in_proj_kernel, dils=dils),
        grid=(n // tm,),
        in_specs=[row(tm, d), pl.BlockSpec(w_in_bf.shape, lambda i: (0, 0))],
        out_specs=qkv_specs + [row(tm, pw)],
        out_shape=qkv_shapes + [jax.ShapeDtypeStruct((n, pw), F32)],
        scratch_shapes=[pltpu.VMEM((aw // LANES, tm, LANES), F32)],
        compiler_params=_cparams(("arbitrary",)),
        name="in_proj",
    )(x2d, w_in_bf)


def _dil_attn_kernel(q_ref, kc_ref, kp_ref, vc_ref, vp_ref, o_ref, lse_ref, *, dil):
    blk = ATTN_BLOCK
    i = pl.program_id(1)
    qi = lax.broadcasted_iota(I32, (blk, 2 * blk), 0)
    kj = lax.broadcasted_iota(I32, (blk, 2 * blk), 1)
    dist = qi + blk - kj
    valid = (dist >= 0) & (dist <= blk) & ((kj >= blk) | (i > 0))
    penalty = (dist * dil).astype(F32)
    lane = lax.broadcasted_iota(I32, (blk, LANES), 1)
    low_half = lane < HEAD_DIM
    lse_tile = jnp.zeros((blk, LANES), F32)
    n_pairs = ATTN_WIDTH // LANES
    pair_out = []
    for hp in range(n_pairs):
        sl = slice(hp * LANES, (hp + 1) * LANES)
        q2 = q_ref[:, sl]
        kcat = jnp.concatenate([kp_ref[:, sl], kc_ref[:, sl]], axis=0)
        vcat = jnp.concatenate([vp_ref[:, sl], vc_ref[:, sl]], axis=0)
        outs = []
        for e in range(LANES // HEAD_DIM):
            h = hp * (LANES // HEAD_DIM) + e
            slope = 2.0 ** (-8.0 * (h + 1) / ATTN_HEADS)
            keep = low_half if e == 0 else jnp.logical_not(low_half)
            qe = jnp.where(keep, q2, jnp.zeros_like(q2))
            s = _dot_nt(qe, kcat) - slope * penalty
            s = jnp.where(valid, s, NEG_INF)
            m = jnp.max(s, axis=1, keepdims=True)
            p = jnp.exp(s - m)
            l = jnp.sum(p, axis=1, keepdims=True)
            outs.append(_dot(p.astype(BF16), vcat) / l)
            lse_tile = jnp.where(lane == h, m + jnp.log(l), lse_tile)
        pair_out.append(jnp.where(low_half, outs[0], outs[1]))
    r = pl.program_id(2)
    rows = pl.ds(r, blk, stride=dil) if dil > 1 else slice(None)
    half = n_pairs // 2
    for w in range(half):
        o_ref[w, rows, :] = _pack_pair(pair_out[w], pair_out[w + half])
    lse_ref[rows, :] = lse_tile


def _dil_attn(q, k, v, batch, seq, dil):
    blk = ATTN_BLOCK
    sub = seq // dil
    nb = sub // blk
    view = lambda a: a.reshape(batch, sub, dil * ATTN_WIDTH)
    cur = pl.BlockSpec((None, blk, ATTN_WIDTH), lambda b, i, r: (b, i, r))
    prev = pl.BlockSpec((None, blk, ATTN_WIDTH), lambda b, i, r: (b, jnp.maximum(i - 1, 0), r))
    half = ATTN_WIDTH // LANES // 2
    o, lse = pl.pallas_call(
        functools.partial(_dil_attn_kernel, dil=dil),
        grid=(batch, nb, dil),
        in_specs=[cur, cur, prev, cur, prev],
        out_specs=[pl.BlockSpec((None, half, blk * dil, LANES), lambda b, i, r: (b, 0, i, 0)),
                   pl.BlockSpec((None, blk * dil, LANES), lambda b, i, r: (b, i, 0))],
        out_shape=[jax.ShapeDtypeStruct((batch, half, seq, LANES), U32),
                   jax.ShapeDtypeStruct((batch, seq, LANES), F32)],
        compiler_params=_cparams(("arbitrary",) * 3),
        name=f"dil_attn_d{dil}",
    )(view(q), view(k), view(k), view(v), view(v))
    return o, lse.reshape(batch * seq, LANES)


def _mem_kv_kernel(mem_ref, wk_ref, wv_ref, k_ref, v_ref):
    m = mem_ref[...].astype(BF16)
    k_ref[...] = _dot(m, wk_ref[...]).astype(BF16)
    v_ref[...] = _dot(m, wv_ref[...]).astype(BF16)


def _mem_kv(mem2d, w_ck_bf, w_cv_bf):
    rows = mem2d.shape[0]
    width = w_ck_bf.shape[1]
    return pl.pallas_call(
        _mem_kv_kernel,
        out_shape=[jax.ShapeDtypeStruct((rows, width), BF16)] * 2,
        compiler_params=_cparams(None),
        name="mem_kv",
    )(mem2d, w_ck_bf, w_cv_bf)


def _row_kernel(x_ref, o1_ref, o4_ref, o16_ref, l1_ref, l4_ref, l16_ref, p_ref, halo_ref,
                poolw_ref, pscale_ref, wout_ref, g1_ref, b1_ref,
                wcq_ref, km_ref, vm_ref, wco_ref, g2_ref, b2_ref,
                wrh_ref, wrl_ref, shg_ref, shu_ref, shd_ref,
                x2_ref, base_ref, logit_ref, *, alpha, seq):
    tm = x_ref.shape[0]
    i = pl.program_id(0)
    tile_pos = (i * tm) % seq

    l1, l4, l16 = l1_ref[...], l4_ref[...], l16_ref[...]
    mx = jnp.maximum(jnp.maximum(l1, l4), l16)
    e1, e4, e16 = jnp.exp(l1 - mx), jnp.exp(l4 - mx), jnp.exp(l16 - mx)
    inv = 1.0 / (e1 + e4 + e16)
    hrow = lax.broadcasted_iota(I32, (LANES, ATTN_WIDTH), 0)
    hcol = lax.broadcasted_iota(I32, (LANES, ATTN_WIDTH), 1) // HEAD_DIM
    spread = jnp.where(hrow == hcol, 1.0, 0.0).astype(BF16)

    def widen(w):
        hi, lo = _split_bf16(w)
        return _dot(hi, spread) + _dot(lo, spread)

    def branch(o_ref):
        lo, hi = zip(*[_unpack_pair(o_ref[w]) for w in range(o_ref.shape[0])])
        return jnp.concatenate(lo + hi, axis=1)

    attn = (widen(e1 * inv) * branch(o1_ref) + widen(e4 * inv) * branch(o4_ref)
            + widen(e16 * inv) * branch(o16_ref))

    halo_rows = halo_ref.shape[0]
    halo = jnp.where(tile_pos > 0, halo_ref[...], 0.0)
    ext = jnp.concatenate([halo, p_ref[...]], axis=0)
    pos = (tile_pos + lax.broadcasted_iota(I32, (tm, 1), 0)).astype(F32)
    gd = ext.shape[1] // len(POOL_WINDOWS)
    mixed = []
    for g, w in enumerate(POOL_WINDOWS):
        eg = ext[:, g * gd:(g + 1) * gd]
        acc, span = eg, 1
        while span < w:
            acc = acc + pltpu.roll(acc, span, 0)
            span *= 2
        count = jnp.minimum(pos + 1.0, float(w))
        pooled = acc[halo_rows:, :] / count - eg[halo_rows:, :]
        mixed.append(_dot(pooled.astype(BF16), poolw_ref[g]) * pscale_ref[:, g * gd:(g + 1) * gd])
    cat = jnp.concatenate([attn.astype(BF16)] + [m.astype(BF16) for m in mixed], axis=1)
    x1 = _layer_norm(alpha * x_ref[...] + _dot(cat, wout_ref[...]), g1_ref[...], b1_ref[...])

    qc = _dot(x1.astype(BF16), wcq_ref[...])
    mhd = qc.shape[1] // MEM_HEADS
    heads = []
    for h in range(MEM_HEADS):
        sl = slice(h * mhd, (h + 1) * mhd)
        s = _dot_nt(qc[:, sl].astype(BF16), km_ref[:, sl]) * (mhd ** -0.5)
        m = jnp.max(s, axis=1, keepdims=True)
        p = jnp.exp(s - m)
        l = jnp.sum(p, axis=1, keepdims=True)
        heads.append((_dot(p.astype(BF16), vm_ref[:, sl]) / l).astype(BF16))
    oc = jnp.concatenate(heads, axis=1)
    x2 = _layer_norm(alpha * x1 + _dot(oc, wco_ref[...]), g2_ref[...], b2_ref[...])
    dh = x2.shape[1] // 2
    x2_ref[...] = _pack_pair(x2[:, :dh], x2[:, dh:])

    xh, xl = _split_bf16(x2)
    logit_ref[...] = _dot_nt(wrh_ref[...], xh) + _dot_nt(wrh_ref[...], xl) + _dot_nt(wrl_ref[...], xh)

    gate = _dot(xh, shg_ref[...])
    up = _dot(xh, shu_ref[...])
    hid = gate / (1.0 + jnp.exp(-gate)) * up
    base_ref[...] = alpha * x2 + _dot(hid.astype(BF16), shd_ref[...])


def _row_block(x2d, o1, o4, o16, l1, l4, l16, p, pool_w_bf, pool_scale, w_out_bf, g1, b1,
               w_cq_bf, kmem, vmem, w_co_bf, g2, b2, wr_hi, wr_lo, sh_g, sh_u, sh_d,
               *, alpha, batch, seq, mem_len):
    n, d = x2d.shape
    tm = ROW_TILE
    halo_rows = max(POOL_WINDOWS)
    steps_per_seq = seq // tm
    row = lambda w: pl.BlockSpec((tm, w), lambda i: (i, 0))
    full = lambda a: pl.BlockSpec(a.shape, lambda i: (0,) * a.ndim)
    halo = pl.BlockSpec((halo_rows, p.shape[1]), lambda i: (jnp.maximum(i * (tm // halo_rows) - 1, 0), 0))
    memspec = pl.BlockSpec((mem_len, kmem.shape[1]), lambda i: (i // steps_per_seq, 0))
    n_exp = wr_hi.shape[0]
    branch = pl.BlockSpec((None, o1.shape[1], tm, LANES), lambda i: (i // steps_per_seq, 0, i % steps_per_seq, 0))
    return pl.pallas_call(
        functools.partial(_row_kernel, alpha=alpha, seq=seq),
        grid=(n // tm,),
        in_specs=[row(d), branch, branch, branch, row(LANES), row(LANES), row(LANES),
                  row(p.shape[1]), halo,
                  full(pool_w_bf), full(pool_scale), full(w_out_bf), full(g1), full(b1),
                  full(w_cq_bf), memspec, memspec, full(w_co_bf), full(g2), full(b2),
                  full(wr_hi), full(wr_lo), full(sh_g), full(sh_u), full(sh_d)],
        out_specs=[row(d // 2), row(d), pl.BlockSpec((n_exp, tm), lambda i: (0, i))],
        out_shape=[jax.ShapeDtypeStruct((n, d // 2), U32), jax.ShapeDtypeStruct((n, d), F32),
                   jax.ShapeDtypeStruct((n_exp, n), F32)],
        compiler_params=_cparams(("arbitrary",)),
        name="row_block",
    )(x2d, o1, o4, o16, l1, l4, l16, p, p, pool_w_bf, pool_scale, w_out_bf, g1, b1,
      w_cq_bf, kmem, vmem, w_co_bf, g2, b2, wr_hi, wr_lo, sh_g, sh_u, sh_d)


def _beat_counts(vals, n_rows):
    tn = vals.shape[1]
    n_tiles = n_rows // SUBLANES
    tiles = [vals[t * SUBLANES:(t + 1) * SUBLANES, :] for t in range(n_tiles)]
    sub = lax.broadcasted_iota(I32, (SUBLANES, tn), 0)
    counts = [jnp.zeros((SUBLANES, tn), F32) for _ in range(n_tiles)]
    for e in range(n_rows):
        te, je = divmod(e, SUBLANES)
        row = jnp.broadcast_to(vals[e:e + 1, :], (SUBLANES, tn))
        for t in range(n_tiles):
            strict = jnp.where(row > tiles[t], 1.0, 0.0)
            loose = jnp.where(row >= tiles[t], 1.0, 0.0)
            if t < te:
                beat = strict
            elif t > te:
                beat = loose
            else:
                beat = jnp.where(sub > je, loose, strict)
            counts[t] = counts[t] + beat
    return jnp.concatenate(counts, axis=0)


def _route_kernel(logit_ref, bias_ref, eidx_ref, rank_ref, gate_ref, count_ref, carry_ref, *, per_group):
    n_exp, tn = logit_ref.shape
    step = pl.program_id(0)

    @pl.when(step == 0)
    def _():
        carry_ref[...] = jnp.zeros_like(carry_ref)

    scores = 1.0 / (1.0 + jnp.exp(-logit_ref[...]))
    biased = scores + bias_ref[:, 0:1]
    n_groups = n_exp // per_group

    gscore = []
    for g in range(n_groups):
        tile = biased[g * per_group:(g + 1) * per_group, :]
        inner = _beat_counts(tile, per_group)
        gscore.append(jnp.sum(jnp.where(inner < 2.0, tile, 0.0), axis=0, keepdims=True))
    gscore = jnp.concatenate(gscore, axis=0)
    grank = _beat_counts(gscore, n_groups)
    grank = jnp.concatenate(
        [jnp.broadcast_to(grank[g:g + 1, :], (per_group, tn)) for g in range(n_groups)], axis=0)
    masked = jnp.where(grank < float(TOPK_GROUPS), biased, NEG_INF)
    self32 = jnp.where(_beat_counts(masked, n_exp) < float(TOP_K), 1.0, 0.0)
    selbf = self32.astype(BF16)

    er = lax.broadcasted_iota(I32, (n_exp, n_exp), 0)
    ec = lax.broadcasted_iota(I32, (n_exp, n_exp), 1)
    below = jnp.where(ec < er, 1.0, 0.0).astype(BF16)
    slot = _dot(below, selbf)
    tr = lax.broadcasted_iota(I32, (tn, tn), 0)
    tc = lax.broadcasted_iota(I32, (tn, tn), 1)
    before = jnp.where(tr < tc, 1.0, 0.0).astype(BF16)
    rank = _dot(selbf, before) + carry_ref[:, 0:1]
    carry_ref[...] = carry_ref[...] + jnp.sum(self32, axis=1, keepdims=True)
    count_ref[...] = carry_ref[...]

    ridx = lax.broadcasted_iota(I32, (n_exp, tn), 0).astype(F32)
    eidx, ranks, gates = [], [], []
    for k in range(TOP_K):
        pick = jnp.where(slot == float(k), self32, 0.0)
        eidx.append(jnp.sum(pick * ridx, axis=0, keepdims=True))
        ranks.append(jnp.sum(pick * rank, axis=0, keepdims=True))
        gates.append(jnp.sum(pick * scores, axis=0, keepdims=True))
    gates = jnp.concatenate(gates, axis=0)
    gates = gates / jnp.sum(gates, axis=0, keepdims=True) * ROUTED_SCALE
    eidx_ref[...] = jnp.concatenate(eidx, axis=0).astype(I32)
    rank_ref[...] = jnp.concatenate(ranks, axis=0).astype(I32)
    gate_ref[...] = gates


def _route(logits_t, bias_col):
    n_exp, n = logits_t.shape
    tn = ROUTE_TILE
    tok = lambda dt: jax.ShapeDtypeStruct((TOP_K, n), dt)
    tokspec = pl.BlockSpec((TOP_K, tn), lambda i: (0, i))
    return pl.pallas_call(
        functools.partial(_route_kernel, per_group=n_exp // N_EXPERT_GROUPS),
        grid=(n // tn,),
        in_specs=[pl.BlockSpec((n_exp, tn), lambda i: (0, i)), pl.BlockSpec(bias_col.shape, lambda i: (0, 0))],
        out_specs=[tokspec, tokspec, tokspec, pl.BlockSpec((n_exp, LANES), lambda i: (0, 0))],
        out_shape=[tok(I32), tok(I32), tok(F32), jax.ShapeDtypeStruct((n_exp, LANES), F32)],
        scratch_shapes=[pltpu.VMEM((n_exp, LANES), F32)],
        compiler_params=_cparams(("arbitrary",)),
        name="route",
    )(logits_t, bias_col)


def _place_kernel(count_ref, eidx_ref, rank_ref, dest_ref, bexp_ref, seg_ref, *, n_blocks):
    n_exp = count_ref.shape[0]
    blk = float(EXPERT_BLOCK)
    counts = count_ref[...]
    padded = jnp.ceil(counts / blk) * blk
    ridx = lax.broadcasted_iota(I32, (n_exp, LANES), 0)
    ends = padded
    shift = 1
    while shift < n_exp:
        ends = ends + jnp.where(ridx >= shift, pltpu.roll(ends, shift, 0), 0.0)
        shift *= 2
    starts = ends - padded

    eidx = eidx_ref[...]
    dest = rank_ref[...].astype(F32)
    for e in range(n_exp):
        dest = dest + jnp.where(eidx == e, starts[e:e + 1, 0:1], 0.0)
    dest_ref[...] = dest.astype(I32)

    nbp = bexp_ref.shape[1]
    blk_start = (lax.broadcasted_iota(I32, (n_exp, nbp), 1) * EXPERT_BLOCK).astype(F32)
    passed = jnp.where(jnp.broadcast_to(ends[:, 0:1], (n_exp, nbp)) <= blk_start, 1.0, 0.0)
    bexp = jnp.minimum(jnp.sum(passed, axis=0, keepdims=True), float(n_exp - 1))
    used = ends[n_exp - 1:n_exp, 0:1] / blk
    lane = lax.broadcasted_iota(I32, (1, nbp), 1)
    bexp_ref[...] = jnp.where(lane == n_blocks, used, bexp).astype(I32)
    lane2 = lax.broadcasted_iota(I32, (n_exp, LANES), 1)
    seg_ref[...] = jnp.where(lane2 == 0, starts + counts, ends).astype(I32)


def _place(counts, eidx, rank, n_blocks):
    n_exp = counts.shape[0]
    n = eidx.shape[1]
    nbp = -(-(n_blocks + 1) // LANES) * LANES
    return pl.pallas_call(
        functools.partial(_place_kernel, n_blocks=n_blocks),
        out_shape=[jax.ShapeDtypeStruct((TOP_K, n), I32), jax.ShapeDtypeStruct((1, nbp), I32),
                   jax.ShapeDtypeStruct((n_exp, LANES), I32)],
        compiler_params=_cparams(None),
        name="route_place",
    )(counts, eidx, rank)


def _dispatch_kernel(dest_ref, seg_ref, x_ref, xs_ref, zero_ref, sem, zsem):
    tm = x_ref.shape[0]
    n_exp = seg_ref.shape[0]

    @pl.when(pl.program_id(0) == 0)
    def _():
        zero_ref[...] = jnp.zeros_like(zero_ref)

        def pad_copy(r):
            return pltpu.make_async_copy(zero_ref.at[pl.ds(0, 1), :], xs_ref.at[pl.ds(r, 1), :], zsem)

        def per_expert(e, total):
            lo, hi = seg_ref[e, 0], seg_ref[e, 1]
            lax.fori_loop(lo, hi, lambda r, c: (pad_copy(r).start(), c)[1], 0)
            return total + (hi - lo)

        total = lax.fori_loop(0, n_exp, per_expert, 0)
        lax.fori_loop(0, total, lambda r, c: (pad_copy(0).wait(), c)[1], 0)

    def row_copy(j, k):
        return pltpu.make_async_copy(x_ref.at[pl.ds(j, 1), :], xs_ref.at[pl.ds(dest_ref[k, j], 1), :], sem)

    def start_row(j, c):
        for k in range(TOP_K):
            row_copy(j, k).start(priority=k % DMA_PRIORITIES)
        return c

    def wait_row(j, c):
        for k in range(TOP_K):
            row_copy(j, k).wait()
        return c

    lax.fori_loop(0, tm, start_row, 0)
    lax.fori_loop(0, tm, wait_row, 0)


def _dispatch(x2, dest, seg, n_rows):
    n, d = x2.shape
    tm = MOVE_TILE
    return pl.pallas_call(
        _dispatch_kernel,
        grid=(n // tm,),
        in_specs=[pl.BlockSpec((TOP_K, tm), lambda i: (0, i), memory_space=pltpu.SMEM),
                  pl.BlockSpec(seg.shape, lambda i: (0, 0), memory_space=pltpu.SMEM),
                  pl.BlockSpec((tm, d), lambda i: (i, 0))],
        out_specs=pl.BlockSpec(memory_space=pl.ANY),
        out_shape=jax.ShapeDtypeStruct((n_rows, d), x2.dtype),
        scratch_shapes=[pltpu.VMEM((SUBLANES, d), x2.dtype), pltpu.SemaphoreType.DMA, pltpu.SemaphoreType.DMA],
        compiler_params=_cparams(("arbitrary",)),
        name="dispatch",
    )(dest, seg, x2)


def _expert_kernel(bexp_ref, xs_ref, wg_ref, wu_ref, wd_ref, ys_ref, wg_bf, wu_bf, wd_bf, *, n_blocks):
    i = pl.program_id(0)
    used = bexp_ref[n_blocks]

    @pl.when(i < used)
    def _():
        @pl.when((i == 0) | (bexp_ref[i] != bexp_ref[jnp.maximum(i - 1, 0)]))
        def _():
            wg_bf[...] = wg_ref[...].astype(BF16)
            wu_bf[...] = wu_ref[...].astype(BF16)
            wd_bf[...] = wd_ref[...].astype(BF16)

        lo, hi = _unpack_pair(xs_ref[...])
        x = jnp.concatenate([lo.astype(BF16), hi.astype(BF16)], axis=1)
        gate = _dot(x, wg_bf[...])
        up = _dot(x, wu_bf[...])
        hid = gate / (1.0 + jnp.exp(-gate)) * up
        y = _dot(hid.astype(BF16), wd_bf[...])
        dh = y.shape[1] // 2
        ys_ref[...] = _pack_pair(y[:, :dh], y[:, dh:])


def _experts(bexp, xs, exp_gate, exp_up, exp_down, n_blocks):
    n_rows = xs.shape[0]
    d, hidden = exp_gate.shape[1:]
    blk = EXPERT_BLOCK

    def rows(i, b):
        return (jnp.minimum(i, jnp.maximum(b[n_blocks] - 1, 0)), 0)

    def weight(i, b):
        return (b[jnp.minimum(i, jnp.maximum(b[n_blocks] - 1, 0))], 0, 0)

    grid_spec = pltpu.PrefetchScalarGridSpec(
        num_scalar_prefetch=1,
        grid=(n_blocks,),
        in_specs=[pl.BlockSpec((blk, d // 2), rows),
                  pl.BlockSpec((None, d, hidden), weight),
                  pl.BlockSpec((None, d, hidden), weight),
                  pl.BlockSpec((None, hidden, d), weight)],
        out_specs=pl.BlockSpec((blk, d // 2), rows),
        scratch_shapes=[pltpu.VMEM((d, hidden), BF16), pltpu.VMEM((d, hidden), BF16), pltpu.VMEM((hidden, d), BF16)],
    )
    return pl.pallas_call(
        functools.partial(_expert_kernel, n_blocks=n_blocks),
        grid_spec=grid_spec,
        out_shape=jax.ShapeDtypeStruct((n_rows, d // 2), U32),
        compiler_params=_cparams(("arbitrary",)),
        name="experts",
    )(bexp, xs, exp_gate, exp_up, exp_down)


def _combine_kernel(dest_ref, gate_ref, base_ref, g3_ref, b3_ref, ys_ref, out_ref, buf_ref, sem):
    tm = base_ref.shape[0]

    def row_copy(j, k):
        return pltpu.make_async_copy(ys_ref.at[pl.ds(dest_ref[k, j], 1), :], buf_ref.at[k, pl.ds(j, 1), :], sem)

    def start_row(j, c):
        for k in range(TOP_K):
            row_copy(j, k).start(priority=k % DMA_PRIORITIES)
        return c

    def wait_row(j, c):
        for k in range(TOP_K):
            row_copy(j, k).wait()
        return c

    lax.fori_loop(0, tm, start_row, 0)
    gates = jnp.concatenate([gate_ref[...], jnp.zeros((LANES - TOP_K, tm), F32)], axis=0)
    gates = gates.T
    lax.fori_loop(0, tm, wait_row, 0)
    dh = base_ref.shape[1] // 2
    acc_lo, acc_hi = base_ref[:, :dh], base_ref[:, dh:]
    for k in range(TOP_K):
        lo, hi = _unpack_pair(buf_ref[k])
        acc_lo = acc_lo + gates[:, k:k + 1] * lo
        acc_hi = acc_hi + gates[:, k:k + 1] * hi
    out_ref[...] = _layer_norm(jnp.concatenate([acc_lo, acc_hi], axis=1), g3_ref[...], b3_ref[...])


def _combine(dest, gates, base, g3, b3, ys):
    n, d = base.shape
    tm = MOVE_TILE
    return pl.pallas_call(
        _combine_kernel,
        grid=(n // tm,),
        in_specs=[pl.BlockSpec((TOP_K, tm), lambda i: (0, i), memory_space=pltpu.SMEM),
                  pl.BlockSpec((TOP_K, tm), lambda i: (0, i)),
                  pl.BlockSpec((tm, d), lambda i: (i, 0)),
                  pl.BlockSpec(g3.shape, lambda i: (0, 0)),
                  pl.BlockSpec(b3.shape, lambda i: (0, 0)),
                  pl.BlockSpec(memory_space=pl.ANY)],
        out_specs=pl.BlockSpec((tm, d), lambda i: (i, 0)),
        out_shape=jax.ShapeDtypeStruct((n, d), F32),
        scratch_shapes=[pltpu.VMEM((TOP_K, tm, ys.shape[1]), ys.dtype), pltpu.SemaphoreType.DMA],
        compiler_params=_cparams(("arbitrary",)),
        name="combine",
    )(dest, gates, base, g3, b3, ys)


def _layer(x2d, mem2d, w_in, pool_w, pool_scale, w_out, ln1_g, ln1_b, w_cq, w_ck, w_cv, w_co, ln2_g, ln2_b,
           w_router, router_bias, exp_gate, exp_up, exp_down, sh_gate, sh_up, sh_down, ln3_g, ln3_b,
           *, alpha, batch, seq, mem_len):
    n, d = x2d.shape
    n_exp = w_router.shape[1]
    row = lambda a: a.reshape(1, -1)
    bf = lambda a: a.astype(BF16)

    dils = tuple(dil for _, dil in DILATED_PATTERNS)
    *qkv, p = _in_proj(x2d, bf(w_in), dils)
    branches = []
    for di, (window, dil) in enumerate(DILATED_PATTERNS):
        assert window // dil == ATTN_BLOCK and seq % (dil * ATTN_BLOCK) == 0 and IN_TILE % (dil * 2 * SUBLANES) == 0
        branches.append(_dil_attn(*qkv[3 * di:3 * di + 3], batch, seq, dil))
    (o1, l1), (o4, l4), (o16, l16) = branches
    kmem, vmem = _mem_kv(mem2d, bf(w_ck), bf(w_cv))

    wr_t = w_router.T
    wr_hi = bf(wr_t)
    wr_lo = bf(wr_t - wr_hi.astype(F32))
    x2, base, logits_t = _row_block(
        x2d, o1, o4, o16, l1, l4, l16, p, bf(pool_w), row(pool_scale), bf(w_out), row(ln1_g), row(ln1_b),
        bf(w_cq), kmem, vmem, bf(w_co), row(ln2_g), row(ln2_b), wr_hi, wr_lo, bf(sh_gate), bf(sh_up), bf(sh_down),
        alpha=alpha, batch=batch, seq=seq, mem_len=mem_len)

    bias_col = jnp.broadcast_to(router_bias.reshape(n_exp, 1), (n_exp, LANES))
    eidx, rank, gates, counts = _route(logits_t, bias_col)
    n_blocks = (n * TOP_K) // EXPERT_BLOCK + n_exp
    dest, bexp, seg = _place(counts, eidx, rank, n_blocks)
    xs = _dispatch(x2, dest, seg, n_blocks * EXPERT_BLOCK)
    ys = _experts(bexp.reshape(-1), xs, exp_gate, exp_up, exp_down, n_blocks)
    return _combine(dest, gates, base, row(ln3_g), row(ln3_b), ys)


def kernel(x, mem, w_in, pool_w, pool_scale, w_out, ln1_g, ln1_b, w_cq, w_ck, w_cv, w_co, ln2_g, ln2_b, w_router, router_bias, exp_gate, exp_up, exp_down, sh_gate, sh_up, sh_down, ln3_g, ln3_b):
    batch, seq, d = x.shape
    mem_len = mem.shape[1]
    depth = w_in.shape[0]
    alpha = (2.0 * depth) ** 0.25
    x2d = x.reshape(batch * seq, d)
    mem2d = mem.reshape(batch * mem_len, d)
    for l in range(depth):
        x2d = _layer(x2d, mem2d, w_in[l], pool_w[l], pool_scale[l], w_out[l], ln1_g[l], ln1_b[l], w_cq[l], w_ck[l],
                     w_cv[l], w_co[l], ln2_g[l], ln2_b[l], w_router[l], router_bias[l], exp_gate[l], exp_up[l],
                     exp_down[l], sh_gate[l], sh_up[l], sh_down[l], ln3_g[l], ln3_b[l],
                     alpha=alpha, batch=batch, seq=seq, mem_len=mem_len)
    return x2d.reshape(batch, seq, d)
```

```python
import functools

import jax
import jax.numpy as jnp
from jax import lax
from jax.experimental import pallas as pl
from jax.experimental.pallas import tpu as pltpu

F32 = jnp.float32
BF16 = jnp.bfloat16
I32 = jnp.int32
U32 = jnp.uint32

ATTN_HEADS = 8
HEAD_DIM = 64
ATTN_WIDTH = ATTN_HEADS * HEAD_DIM
DILATED_PATTERNS = ((128, 1), (512, 4), (2048, 16))
POOL_WINDOWS = (2, 4, 8, 16)
MEM_HEADS = 4
N_EXPERT_GROUPS = 8
TOPK_GROUPS = 4
TOP_K = 8
ROUTED_SCALE = 2.5
LN_EPS = 1e-5
NEG_INF = -1e30

LANES = 128
SUBLANES = 8
VMEM_LIMIT = 56 * 1024 * 1024

ATTN_BLOCK = 128
ROW_TILE = 256
IN_TILE = 512
SORT_TILE = 256
EXPERT_BLOCK = 256
RUN_ALIGN = SUBLANES


def _cparams(sem, vmem=VMEM_LIMIT):
    return pltpu.CompilerParams(dimension_semantics=sem, vmem_limit_bytes=vmem)


def _layer_norm(y, g, b):
    mu = jnp.mean(y, axis=-1, keepdims=True)
    d = y - mu
    var = jnp.mean(d * d, axis=-1, keepdims=True)
    return d * lax.rsqrt(var + LN_EPS) * g + b


def _dot(a, b):
    return jnp.dot(a, b, preferred_element_type=F32)


def _dot_nt(a, b):
    return lax.dot_general(a, b, (((1,), (1,)), ((), ())), preferred_element_type=F32)


def _split_bf16(x):
    hi = x.astype(BF16)
    lo = (x - hi.astype(F32)).astype(BF16)
    return hi, lo


def _bf16_bits(x):
    return lax.bitcast_convert_type(x.astype(BF16).astype(F32), U32)


def _pack_pair(lo, hi):
    return (_bf16_bits(lo) >> 16) | (_bf16_bits(hi) & jnp.uint32(0xFFFF0000))


def _unpack_pair(w):
    lo = lax.bitcast_convert_type(w << 16, F32)
    hi = lax.bitcast_convert_type(w & jnp.uint32(0xFFFF0000), F32)
    return lo, hi


def _in_proj_kernel(x_ref, w_ref, *refs, dils):
    n_qkv = 3 * len(dils)
    out_refs, p_ref, slab_ref = refs[:n_qkv], refs[n_qkv], refs[n_qkv + 1]
    tm = x_ref.shape[0]
    x = x_ref[...].astype(BF16)
    aw = ATTN_WIDTH
    n_slabs = aw // LANES
    for a in range(3):
        val = _dot(x, w_ref[:, a * aw:(a + 1) * aw])
        if a == 0:
            val = val * (HEAD_DIM ** -0.5)
        for s in range(n_slabs):
            slab_ref[s] = val[:, s * LANES:(s + 1) * LANES]
        for di, dil in enumerate(dils):
            dst = out_refs[3 * di + a]
            if dil == 1:
                dst[...] = val.astype(BF16)
                continue
            for r in range(dil):
                for s in range(n_slabs):
                    rows = slab_ref[s, pl.ds(r, tm // dil, stride=dil), :]
                    dst[:, r * aw + s * LANES:r * aw + (s + 1) * LANES] = rows.astype(BF16)
    p_ref[...] = _dot(x, w_ref[:, 3 * aw:])


def _in_proj(x2d, w_in_bf, dils):
    n, d = x2d.shape
    aw = ATTN_WIDTH
    pw = w_in_bf.shape[1] - 3 * aw
    tm = IN_TILE
    row = lambda rows, w: pl.BlockSpec((rows, w), lambda i: (i, 0))
    qkv_specs, qkv_shapes = [], []
    for dil in dils:
        qkv_specs += [row(tm // dil, dil * aw)] * 3
        qkv_shapes += [jax.ShapeDtypeStruct((n // dil, dil * aw), BF16)] * 3
    return pl.pallas_call(
        functools.partial(_in_proj_kernel, dils=dils),
        grid=(n // tm,),
        in_specs=[row(tm, d), pl.BlockSpec(w_in_bf.shape, lambda i: (0, 0))],
        out_specs=qkv_specs + [row(tm, pw)],
        out_shape=qkv_shapes + [jax.ShapeDtypeStruct((n, pw), F32)],
        scratch_shapes=[pltpu.VMEM((aw // LANES, tm, LANES), F32)],
        compiler_params=_cparams(("arbitrary",)),
        name="in_proj",
    )(x2d, w_in_bf)


def _dil_attn_kernel(q_ref, kc_ref, kp_ref, vc_ref, vp_ref, o_ref, lse_ref, *, dil):
    blk = ATTN_BLOCK
    i = pl.program_id(1)
    qi = lax.broadcasted_iota(I32, (blk, 2 * blk), 0)
    kj = lax.broadcasted_iota(I32, (blk, 2 * blk), 1)
    dist = qi + blk - kj
    valid = (dist >= 0) & (dist <= blk) & ((kj >= blk) | (i > 0))
    penalty = (dist * dil).astype(F32)
    lane = lax.broadcasted_iota(I32, (blk, LANES), 1)
    low_half = lane < HEAD_DIM
    lse_tile = jnp.zeros((blk, LANES), F32)
    n_pairs = ATTN_WIDTH // LANES
    pair_out = []
    for hp in range(n_pairs):
        sl = slice(hp * LANES, (hp + 1) * LANES)
        q2 = q_ref[:, sl]
        kcat = jnp.concatenate([kp_ref[:, sl], kc_ref[:, sl]], axis=0)
        vcat = jnp.concatenate([vp_ref[:, sl], vc_ref[:, sl]], axis=0)
        outs = []
        for e in range(LANES // HEAD_DIM):
            h = hp * (LANES // HEAD_DIM) + e
            slope = 2.0 ** (-8.0 * (h + 1) / ATTN_HEADS)
            keep = low_half if e == 0 else jnp.logical_not(low_half)
            qe = jnp.where(keep, q2, jnp.zeros_like(q2))
            s = _dot_nt(qe, kcat) - slope * penalty
            s = jnp.where(valid, s, NEG_INF)
            m = jnp.max(s, axis=1, keepdims=True)
            p = jnp.exp(s - m)
            l = jnp.sum(p, axis=1, keepdims=True)
            outs.append(_dot(p.astype(BF16), vcat) / l)
            lse_tile = jnp.where(lane == h, m + jnp.log(l), lse_tile)
        pair_out.append(jnp.where(low_half, outs[0], outs[1]))
    r = pl.program_id(2)
    rows = pl.ds(r, blk, stride=dil) if dil > 1 else slice(None)
    half = n_pairs // 2
    for w in range(half):
        o_ref[w, rows, :] = _pack_pair(pair_out[w], pair_out[w + half])
    lse_ref[rows, :] = lse_tile


def _dil_attn(q, k, v, batch, seq, dil):
    blk = ATTN_BLOCK
    sub = seq // dil
    nb = sub // blk
    view = lambda a: a.reshape(batch, sub, dil * ATTN_WIDTH)
    cur = pl.BlockSpec((None, blk, ATTN_WIDTH), lambda b, i, r: (b, i, r))
    prev = pl.BlockSpec((None, blk, ATTN_WIDTH), lambda b, i, r: (b, jnp.maximum(i - 1, 0), r))
    half = ATTN_WIDTH // LANES // 2
    o, lse = pl.pallas_call(
        functools.partial(_dil_attn_kernel, dil=dil),
        grid=(batch, nb, dil),
        in_specs=[cur, cur, prev, cur, prev],
        out_specs=[pl.BlockSpec((None, half, blk * dil, LANES), lambda b, i, r: (b, 0, i, 0)),
                   pl.BlockSpec((None, blk * dil, LANES), lambda b, i, r: (b, i, 0))],
        out_shape=[jax.ShapeDtypeStruct((batch, half, seq, LANES), U32),
                   jax.ShapeDtypeStruct((batch, seq, LANES), F32)],
        compiler_params=_cparams(("arbitrary",) * 3),
        name=f"dil_attn_d{dil}",
    )(view(q), view(k), view(k), view(v), view(v))
    return o, lse.reshape(batch * seq, LANES)


def _mem_kv_kernel(mem_ref, wk_ref, wv_ref, k_ref, v_ref):
    m = mem_ref[...].astype(BF16)
    k_ref[...] = _dot(m, wk_ref[...]).astype(BF16)
    v_ref[...] = _dot(m, wv_ref[...]).astype(BF16)


def _mem_kv(mem2d, w_ck_bf, w_cv_bf):
    rows = mem2d.shape[0]
    width = w_ck_bf.shape[1]
    return pl.pallas_call(
        _mem_kv_kernel,
        out_shape=[jax.ShapeDtypeStruct((rows, width), BF16)] * 2,
        compiler_params=_cparams(None),
        name="mem_kv",
    )(mem2d, w_ck_bf, w_cv_bf)


def _row_kernel(x_ref, o1_ref, o4_ref, o16_ref, l1_ref, l4_ref, l16_ref, p_ref, halo_ref,
                poolw_ref, pscale_ref, wout_ref, g1_ref, b1_ref,
                wcq_ref, km_ref, vm_ref, wco_ref, g2_ref, b2_ref,
                wrh_ref, wrl_ref, shg_ref, shu_ref, shd_ref,
                x2_ref, base_ref, logit_ref, *, alpha, seq):
    tm = x_ref.shape[0]
    i = pl.program_id(0)
    tile_pos = (i * tm) % seq

    l1, l4, l16 = l1_ref[...], l4_ref[...], l16_ref[...]
    mx = jnp.maximum(jnp.maximum(l1, l4), l16)
    e1, e4, e16 = jnp.exp(l1 - mx), jnp.exp(l4 - mx), jnp.exp(l16 - mx)
    inv = 1.0 / (e1 + e4 + e16)
    hrow = lax.broadcasted_iota(I32, (LANES, ATTN_WIDTH), 0)
    hcol = lax.broadcasted_iota(I32, (LANES, ATTN_WIDTH), 1) // HEAD_DIM
    spread = jnp.where(hrow == hcol, 1.0, 0.0).astype(BF16)

    def widen(w):
        hi, lo = _split_bf16(w)
        return _dot(hi, spread) + _dot(lo, spread)

    def branch(o_ref):
        lo, hi = zip(*[_unpack_pair(o_ref[w]) for w in range(o_ref.shape[0])])
        return jnp.concatenate(lo + hi, axis=1)

    attn = (widen(e1 * inv) * branch(o1_ref) + widen(e4 * inv) * branch(o4_ref)
            + widen(e16 * inv) * branch(o16_ref))

    halo_rows = halo_ref.shape[0]
    halo = jnp.where(tile_pos > 0, halo_ref[...], 0.0)
    ext = jnp.concatenate([halo, p_ref[...]], axis=0)
    pos = (tile_pos + lax.broadcasted_iota(I32, (tm, 1), 0)).astype(F32)
    gd = ext.shape[1] // len(POOL_WINDOWS)
    mixed = []
    for g, w in enumerate(POOL_WINDOWS):
        eg = ext[:, g * gd:(g + 1) * gd]
        acc, span = eg, 1
        while span < w:
            acc = acc + pltpu.roll(acc, span, 0)
            span *= 2
        count = jnp.minimum(pos + 1.0, float(w))
        pooled = acc[halo_rows:, :] / count - eg[halo_rows:, :]
        mixed.append(_dot(pooled.astype(BF16), poolw_ref[g]) * pscale_ref[:, g * gd:(g + 1) * gd])
    cat = jnp.concatenate([attn.astype(BF16)] + [m.astype(BF16) for m in mixed], axis=1)
    x1 = _layer_norm(alpha * x_ref[...] + _dot(cat, wout_ref[...]), g1_ref[...], b1_ref[...])

    qc = _dot(x1.astype(BF16), wcq_ref[...])
    mhd = qc.shape[1] // MEM_HEADS
    heads = []
    for h in range(MEM_HEADS):
        sl = slice(h * mhd, (h + 1) * mhd)
        s = _dot_nt(qc[:, sl].astype(BF16), km_ref[:, sl]) * (mhd ** -0.5)
        m = jnp.max(s, axis=1, keepdims=True)
        p = jnp.exp(s - m)
        l = jnp.sum(p, axis=1, keepdims=True)
        heads.append((_dot(p.astype(BF16), vm_ref[:, sl]) / l).astype(BF16))
    oc = jnp.concatenate(heads, axis=1)
    x2 = _layer_norm(alpha * x1 + _dot(oc, wco_ref[...]), g2_ref[...], b2_ref[...])
    dh = x2.shape[1] // 2
    x2_ref[...] = _pack_pair(x2[:, :dh], x2[:, dh:])

    xh, xl = _split_bf16(x2)
    logit_ref[...] = _dot_nt(wrh_ref[...], xh) + _dot_nt(wrh_ref[...], xl) + _dot_nt(wrl_ref[...], xh)

    gate = _dot(xh, shg_ref[...])
    up = _dot(xh, shu_ref[...])
    hid = gate / (1.0 + jnp.exp(-gate)) * up
    base_ref[...] = alpha * x2 + _dot(hid.astype(BF16), shd_ref[...])


def _row_block(x2d, o1, o4, o16, l1, l4, l16, p, pool_w_bf, pool_scale, w_out_bf, g1, b1,
               w_cq_bf, kmem, vmem, w_co_bf, g2, b2, wr_hi, wr_lo, sh_g, sh_u, sh_d,
               *, alpha, batch, seq, mem_len):
    n, d = x2d.shape
    tm = ROW_TILE
    halo_rows = max(POOL_WINDOWS)
    steps_per_seq = seq // tm
    row = lambda w: pl.BlockSpec((tm, w), lambda i: (i, 0))
    full = lambda a: pl.BlockSpec(a.shape, lambda i: (0,) * a.ndim)
    halo = pl.BlockSpec((halo_rows, p.shape[1]), lambda i: (jnp.maximum(i * (tm // halo_rows) - 1, 0), 0))
    memspec = pl.BlockSpec((mem_len, kmem.shape[1]), lambda i: (i // steps_per_seq, 0))
    n_exp = wr_hi.shape[0]
    branch = pl.BlockSpec((None, o1.shape[1], tm, LANES), lambda i: (i // steps_per_seq, 0, i % steps_per_seq, 0))
    return pl.pallas_call(
        functools.partial(_row_kernel, alpha=alpha, seq=seq),
        grid=(n // tm,),
        in_specs=[row(d), branch, branch, branch, row(LANES), row(LANES), row(LANES),
                  row(p.shape[1]), halo,
                  full(pool_w_bf), full(pool_scale), full(w_out_bf), full(g1), full(b1),
                  full(w_cq_bf), memspec, memspec, full(w_co_bf), full(g2), full(b2),
                  full(wr_hi), full(wr_lo), full(sh_g), full(sh_u), full(sh_d)],
        out_specs=[row(d // 2), row(d), pl.BlockSpec((n_exp, tm), lambda i: (0, i))],
        out_shape=[jax.ShapeDtypeStruct((n, d // 2), U32), jax.ShapeDtypeStruct((n, d), F32),
                   jax.ShapeDtypeStruct((n_exp, n), F32)],
        compiler_params=_cparams(("arbitrary",)),
        name="row_block",
    )(x2d, o1, o4, o16, l1, l4, l16, p, p, pool_w_bf, pool_scale, w_out_bf, g1, b1,
      w_cq_bf, kmem, vmem, w_co_bf, g2, b2, wr_hi, wr_lo, sh_g, sh_u, sh_d)


def _beat_counts(vals, n_rows):
    tn = vals.shape[1]
    n_tiles = n_rows // SUBLANES
    tiles = [vals[t * SUBLANES:(t + 1) * SUBLANES, :] for t in range(n_tiles)]
    sub = lax.broadcasted_iota(I32, (SUBLANES, tn), 0)
    counts = [jnp.zeros((SUBLANES, tn), F32) for _ in range(n_tiles)]
    for e in range(n_rows):
        te, je = divmod(e, SUBLANES)
        row = jnp.broadcast_to(vals[e:e + 1, :], (SUBLANES, tn))
        for t in range(n_tiles):
            strict = jnp.where(row > tiles[t], 1.0, 0.0)
            loose = jnp.where(row >= tiles[t], 1.0, 0.0)
            if t < te:
                beat = strict
            elif t > te:
                beat = loose
            else:
                beat = jnp.where(sub > je, loose, strict)
            counts[t] = counts[t] + beat
    return jnp.concatenate(counts, axis=0)


def _route_kernel(logit_ref, bias_ref, lpos_ref, gate_ref, tab_ref, count_ref, carry_ref, rcarry_ref, *, per_group):
    n_exp, tn = logit_ref.shape
    step = pl.program_id(0)

    @pl.when(step == 0)
    def _():
        carry_ref[...] = jnp.zeros_like(carry_ref)
        rcarry_ref[...] = jnp.zeros_like(rcarry_ref)

    scores = 1.0 / (1.0 + jnp.exp(-logit_ref[...]))
    biased = scores + bias_ref[:, 0:1]
    n_groups = n_exp // per_group

    gscore = []
    for g in range(n_groups):
        tile = biased[g * per_group:(g + 1) * per_group, :]
        inner = _beat_counts(tile, per_group)
        gscore.append(jnp.sum(jnp.where(inner < 2.0, tile, 0.0), axis=0, keepdims=True))
    gscore = jnp.concatenate(gscore, axis=0)
    grank = _beat_counts(gscore, n_groups)
    grank = jnp.concatenate(
        [jnp.broadcast_to(grank[g:g + 1, :], (per_group, tn)) for g in range(n_groups)], axis=0)
    masked = jnp.where(grank < float(TOPK_GROUPS), biased, NEG_INF)
    self32 = jnp.where(_beat_counts(masked, n_exp) < float(TOP_K), 1.0, 0.0)
    selbf = self32.astype(BF16)

    er = lax.broadcasted_iota(I32, (n_exp, n_exp), 0)
    ec = lax.broadcasted_iota(I32, (n_exp, n_exp), 1)
    below = jnp.where(ec < er, 1.0, 0.0).astype(BF16)
    slot = _dot(below, selbf)
    tr = lax.broadcasted_iota(I32, (tn, tn), 0)
    tc = lax.broadcasted_iota(I32, (tn, tn), 1)
    before = jnp.where(tr < tc, 1.0, 0.0).astype(BF16)
    lrank = _dot(selbf, before)
    align = float(RUN_ALIGN)
    lcount_col = jnp.broadcast_to(jnp.sum(self32, axis=1, keepdims=True), (n_exp, LANES))
    sel_wide = jnp.concatenate([selbf, jnp.zeros((LANES - n_exp, tn), BF16)], axis=0)
    lcount_row = _dot_nt(jnp.ones((SUBLANES, tn), BF16), sel_wide)
    lcount_col = jnp.ceil(lcount_col / align) * align
    lcount_row = jnp.ceil(lcount_row / align) * align
    lstart_col = _dot(below, lcount_col.astype(BF16))
    wr = lax.broadcasted_iota(I32, (LANES, LANES), 0)
    wc = lax.broadcasted_iota(I32, (LANES, LANES), 1)
    lstart_row = _dot(lcount_row.astype(BF16), jnp.where(wr < wc, 1.0, 0.0).astype(BF16))
    lpos = lstart_col[:, 0:1] + lrank

    sub = lax.broadcasted_iota(I32, (SUBLANES, LANES), 0)
    tab_ref[...] = jnp.where(sub == 0, lstart_row, jnp.where(sub == 1, lcount_row, jnp.where(
        sub == 2, rcarry_ref[...], 0.0))).astype(I32)
    rcarry_ref[...] = rcarry_ref[...] + lcount_row
    carry_ref[...] = carry_ref[...] + lcount_col
    count_ref[...] = carry_ref[...]

    lposk, gates = [], []
    for k in range(TOP_K):
        pick = jnp.where(slot == float(k), self32, 0.0)
        lposk.append(jnp.sum(pick * lpos, axis=0, keepdims=True))
        gates.append(jnp.sum(pick * scores, axis=0, keepdims=True))
    gates = jnp.concatenate(gates, axis=0)
    gates = gates / jnp.sum(gates, axis=0, keepdims=True) * ROUTED_SCALE
    lpos_ref[...] = jnp.concatenate(lposk, axis=0).astype(I32)
    gate_ref[...] = gates


def _route(logits_t, bias_col):
    n_exp, n = logits_t.shape
    tn = SORT_TILE
    tok = lambda dt: jax.ShapeDtypeStruct((TOP_K, n), dt)
    tokspec = pl.BlockSpec((TOP_K, tn), lambda i: (0, i))
    return pl.pallas_call(
        functools.partial(_route_kernel, per_group=n_exp // N_EXPERT_GROUPS),
        grid=(n // tn,),
        in_specs=[pl.BlockSpec((n_exp, tn), lambda i: (0, i)), pl.BlockSpec(bias_col.shape, lambda i: (0, 0))],
        out_specs=[tokspec, tokspec, pl.BlockSpec((None, SUBLANES, LANES), lambda i: (i, 0, 0)),
                   pl.BlockSpec((n_exp, LANES), lambda i: (0, 0))],
        out_shape=[tok(I32), tok(F32), jax.ShapeDtypeStruct((n // tn, SUBLANES, LANES), I32),
                   jax.ShapeDtypeStruct((n_exp, LANES), F32)],
        scratch_shapes=[pltpu.VMEM((n_exp, LANES), F32), pltpu.VMEM((SUBLANES, LANES), F32)],
        compiler_params=_cparams(("arbitrary",)),
        name="route",
    )(logits_t, bias_col)


def _place_kernel(count_ref, tab_ref, runs_ref, bexp_ref, seg_ref, *, n_blocks):
    n_exp = count_ref.shape[0]
    blk = float(EXPERT_BLOCK)
    counts = count_ref[...]
    padded = jnp.ceil(counts / blk) * blk
    ridx = lax.broadcasted_iota(I32, (n_exp, LANES), 0)
    ends = padded
    shift = 1
    while shift < n_exp:
        ends = ends + jnp.where(ridx >= shift, pltpu.roll(ends, shift, 0), 0.0)
        shift *= 2
    starts = ends - padded

    tab = tab_ref[...]
    last = tab[tab.shape[0] - 1]
    total_row = (last[2:3, :] + last[1:2, :]).astype(F32)
    padded_row = jnp.ceil(total_row / blk) * blk
    lidx = lax.broadcasted_iota(I32, (1, LANES), 1)
    ends_row = padded_row
    shift = 1
    while shift < LANES:
        ends_row = ends_row + jnp.where(lidx >= shift, pltpu.roll(ends_row, shift, 1), 0.0)
        shift *= 2
    starts_row = (ends_row - padded_row).astype(I32)
    sub = lax.broadcasted_iota(I32, tab.shape, 1)
    runs_ref[...] = tab + jnp.where(sub == 2, starts_row[None], 0)

    nbp = bexp_ref.shape[1]
    blk_start = (lax.broadcasted_iota(I32, (n_exp, nbp), 1) * EXPERT_BLOCK).astype(F32)
    passed = jnp.where(jnp.broadcast_to(ends[:, 0:1], (n_exp, nbp)) <= blk_start, 1.0, 0.0)
    bexp = jnp.minimum(jnp.sum(passed, axis=0, keepdims=True), float(n_exp - 1))
    used = ends[n_exp - 1:n_exp, 0:1] / blk
    lane = lax.broadcasted_iota(I32, (1, nbp), 1)
    bexp_ref[...] = jnp.where(lane == n_blocks, used, bexp).astype(I32)
    lane2 = lax.broadcasted_iota(I32, (n_exp, LANES), 1)
    seg_ref[...] = jnp.where(lane2 == 0, starts + counts, ends).astype(I32)


def _place(counts, tab, n_blocks):
    n_exp = counts.shape[0]
    nbp = -(-(n_blocks + 1) // LANES) * LANES
    return pl.pallas_call(
        functools.partial(_place_kernel, n_blocks=n_blocks),
        out_shape=[jax.ShapeDtypeStruct(tab.shape, I32), jax.ShapeDtypeStruct((1, nbp), I32),
                   jax.ShapeDtypeStruct((n_exp, LANES), I32)],
        compiler_params=_cparams(None),
        name="route_place",
    )(counts, tab)


def _copy_rows(src_ref, dst_ref, sem, src0, dst0, n, limit, wait=False):
    p = RUN_ALIGN
    while p * 2 <= limit:
        p *= 2
    while p >= RUN_ALIGN:
        @pl.when((n & p) != 0)
        def _(p=p):
            off = n & (-2 * p)
            src = src_ref.at[pl.ds(pl.multiple_of(src0 + off, RUN_ALIGN), p), :]
            dst = dst_ref.at[pl.ds(pl.multiple_of(dst0 + off, RUN_ALIGN), p), :]
            cp = pltpu.make_async_copy(src, dst, sem)
            if wait:
                cp.wait()
            else:
                cp.start()
        p //= 2


def _dispatch_kernel(runs_ref, seg_ref, lpos_ref, x_ref, xs_ref, buf_ref, zero_ref, sems, zsem, *, n_exp):
    step, n_steps = pl.program_id(0), pl.num_programs(0)
    tm = x_ref.shape[0]
    rows = buf_ref.shape[1]
    slot = step % 2

    @pl.when(step == 0)
    def _():
        zero_ref[...] = jnp.zeros_like(zero_ref)
        for wait in (False, True):
            def per_expert(e, c, wait=wait):
                first = seg_ref[e, 0]
                _copy_rows(zero_ref, xs_ref, zsem, 0, first, seg_ref[e, 1] - first, zero_ref.shape[0] - 1, wait)
                return c
            lax.fori_loop(0, n_exp, per_expert, 0)

    lo, hi = _unpack_pair(x_ref[...])
    x = jnp.concatenate([lo.astype(BF16), hi.astype(BF16)], axis=1)
    j = lax.broadcasted_iota(I32, (rows, tm), 0)
    onehot = jnp.zeros((rows, tm), F32)
    for k in range(TOP_K):
        onehot = jnp.where(j == lpos_ref[k:k + 1, :], 1.0, onehot)
    srt = _dot(onehot.astype(BF16), x)
    dh = srt.shape[1] // 2
    bits = lambda v: lax.bitcast_convert_type(v, U32)
    buf_ref[slot] = (bits(srt[:, :dh]) >> 16) | bits(srt[:, dh:])

    def per_run(e, c):
        _copy_rows(buf_ref.at[slot], xs_ref, sems.at[slot], runs_ref[0, e], runs_ref[2, e], runs_ref[1, e], tm)
        return c
    lax.fori_loop(0, n_exp, per_run, 0)
    used = runs_ref[0, n_exp - 1] + runs_ref[1, n_exp - 1]
    spare = xs_ref.shape[0] - 2 * (rows - TOP_K * tm) + slot * (rows - TOP_K * tm)
    _copy_rows(buf_ref.at[slot], xs_ref, sems.at[slot], used, spare, rows - used, rows - TOP_K * tm)

    def drain(s):
        pltpu.make_async_copy(buf_ref.at[s], xs_ref.at[pl.ds(0, rows), :], sems.at[s]).wait()

    @pl.when(step > 0)
    def _():
        drain(1 - slot)

    @pl.when(step == n_steps - 1)
    def _():
        drain(slot)


def _sorted_rows(tm, n_exp):
    return TOP_K * tm + n_exp * RUN_ALIGN


def _dispatch(x2, lpos, runs, seg, n_rows, n_exp):
    n, d = x2.shape
    tm = SORT_TILE
    rows = _sorted_rows(tm, n_exp)
    return pl.pallas_call(
        functools.partial(_dispatch_kernel, n_exp=n_exp),
        grid=(n // tm,),
        in_specs=[pl.BlockSpec((None,) + runs.shape[1:], lambda i: (i, 0, 0), memory_space=pltpu.SMEM),
                  pl.BlockSpec(seg.shape, lambda i: (0, 0), memory_space=pltpu.SMEM),
                  pl.BlockSpec((TOP_K, tm), lambda i: (0, i)),
                  pl.BlockSpec((tm, d), lambda i: (i, 0))],
        out_specs=pl.BlockSpec(memory_space=pl.ANY),
        out_shape=jax.ShapeDtypeStruct((n_rows + 2 * (rows - TOP_K * tm), d), x2.dtype),
        scratch_shapes=[pltpu.VMEM((2, rows, d), x2.dtype), pltpu.VMEM((EXPERT_BLOCK, d), x2.dtype),
                        pltpu.SemaphoreType.DMA((2,)), pltpu.SemaphoreType.DMA],
        compiler_params=_cparams(("arbitrary",)),
        name="dispatch",
    )(runs, seg, lpos, x2)


def _expert_kernel(bexp_ref, xs_ref, wg_ref, wu_ref, wd_ref, ys_ref, wg_bf, wu_bf, wd_bf, *, n_blocks):
    i = pl.program_id(0)
    used = bexp_ref[n_blocks]

    @pl.when(i < used)
    def _():
        @pl.when((i == 0) | (bexp_ref[i] != bexp_ref[jnp.maximum(i - 1, 0)]))
        def _():
            wg_bf[...] = wg_ref[...].astype(BF16)
            wu_bf[...] = wu_ref[...].astype(BF16)
            wd_bf[...] = wd_ref[...].astype(BF16)

        lo, hi = _unpack_pair(xs_ref[...])
        x = jnp.concatenate([lo.astype(BF16), hi.astype(BF16)], axis=1)
        gate = _dot(x, wg_bf[...])
        up = _dot(x, wu_bf[...])
        hid = gate / (1.0 + jnp.exp(-gate)) * up
        y = _dot(hid.astype(BF16), wd_bf[...])
        dh = y.shape[1] // 2
        ys_ref[...] = _pack_pair(y[:, :dh], y[:, dh:])


def _experts(bexp, xs, exp_gate, exp_up, exp_down, n_blocks):
    n_rows = xs.shape[0]
    d, hidden = exp_gate.shape[1:]
    blk = EXPERT_BLOCK

    def rows(i, b):
        return (jnp.minimum(i, jnp.maximum(b[n_blocks] - 1, 0)), 0)

    def weight(i, b):
        return (b[jnp.minimum(i, jnp.maximum(b[n_blocks] - 1, 0))], 0, 0)

    grid_spec = pltpu.PrefetchScalarGridSpec(
        num_scalar_prefetch=1,
        grid=(n_blocks,),
        in_specs=[pl.BlockSpec((blk, d // 2), rows),
                  pl.BlockSpec((None, d, hidden), weight),
                  pl.BlockSpec((None, d, hidden), weight),
                  pl.BlockSpec((None, hidden, d), weight)],
        out_specs=pl.BlockSpec((blk, d // 2), rows),
        scratch_shapes=[pltpu.VMEM((d, hidden), BF16), pltpu.VMEM((d, hidden), BF16), pltpu.VMEM((hidden, d), BF16)],
    )
    return pl.pallas_call(
        functools.partial(_expert_kernel, n_blocks=n_blocks),
        grid_spec=grid_spec,
        out_shape=jax.ShapeDtypeStruct((n_rows, d // 2), U32),
        compiler_params=_cparams(("arbitrary",)),
        name="experts",
    )(bexp, xs, exp_gate, exp_up, exp_down)


def _combine_kernel(runs_ref, next_runs_ref, lpos_ref, gate_ref, base_ref, g3_ref, b3_ref, ys_ref, out_ref,
                    buf_ref, sems, *, n_exp):
    step, n_steps = pl.program_id(0), pl.num_programs(0)
    tm = base_ref.shape[0]
    rows = buf_ref.shape[1]
    slot = step % 2

    def fetch(tab_ref, s):
        def per_run(e, c):
            _copy_rows(ys_ref, buf_ref.at[s], sems.at[s], tab_ref[2, e], tab_ref[0, e], tab_ref[1, e], tm)
            return c
        lax.fori_loop(0, n_exp, per_run, 0)
        used = tab_ref[0, n_exp - 1] + tab_ref[1, n_exp - 1]
        _copy_rows(ys_ref, buf_ref.at[s], sems.at[s], 0, used, rows - used, rows - TOP_K * tm)

    @pl.when(step == 0)
    def _():
        fetch(runs_ref, slot)

    @pl.when(step + 1 < n_steps)
    def _():
        fetch(next_runs_ref, 1 - slot)

    pad = jnp.zeros((LANES - TOP_K, tm), F32)
    lpos_t = jnp.concatenate([lpos_ref[...].astype(F32), pad], axis=0).T.astype(I32)
    gate_t = jnp.concatenate([gate_ref[...], pad], axis=0).T
    j = lax.broadcasted_iota(I32, (tm, rows), 1)
    w = jnp.zeros((tm, rows), F32)
    for k in range(TOP_K):
        w = jnp.where(j == lpos_t[:, k:k + 1], gate_t[:, k:k + 1], w)

    pltpu.make_async_copy(ys_ref.at[pl.ds(0, rows), :], buf_ref.at[slot], sems.at[slot]).wait()
    lo, hi = _unpack_pair(buf_ref[slot])
    y = jnp.concatenate([lo.astype(BF16), hi.astype(BF16)], axis=1)
    routed = _dot(w.astype(BF16), y)
    out_ref[...] = _layer_norm(base_ref[...] + routed, g3_ref[...], b3_ref[...])


def _combine(runs, lpos, gates, base, g3, b3, ys, n_exp):
    n, d = base.shape
    tm = SORT_TILE
    last = n // tm - 1
    tab = lambda f: pl.BlockSpec((None,) + runs.shape[1:], f, memory_space=pltpu.SMEM)
    return pl.pallas_call(
        functools.partial(_combine_kernel, n_exp=n_exp),
        grid=(n // tm,),
        in_specs=[tab(lambda i: (i, 0, 0)), tab(lambda i: (jnp.minimum(i + 1, last), 0, 0)),
                  pl.BlockSpec((TOP_K, tm), lambda i: (0, i)),
                  pl.BlockSpec((TOP_K, tm), lambda i: (0, i)),
                  pl.BlockSpec((tm, d), lambda i: (i, 0)),
                  pl.BlockSpec(g3.shape, lambda i: (0, 0)),
                  pl.BlockSpec(b3.shape, lambda i: (0, 0)),
                  pl.BlockSpec(memory_space=pl.ANY)],
        out_specs=pl.BlockSpec((tm, d), lambda i: (i, 0)),
        out_shape=jax.ShapeDtypeStruct((n, d), F32),
        scratch_shapes=[pltpu.VMEM((2, _sorted_rows(tm, n_exp), ys.shape[1]), ys.dtype),
                        pltpu.SemaphoreType.DMA((2,))],
        compiler_params=_cparams(("arbitrary",)),
        name="combine",
    )(runs, runs, lpos, gates, base, g3, b3, ys)


def _layer(x2d, mem2d, w_in, pool_w, pool_scale, w_out, ln1_g, ln1_b, w_cq, w_ck, w_cv, w_co, ln2_g, ln2_b,
           w_router, router_bias, exp_gate, exp_up, exp_down, sh_gate, sh_up, sh_down, ln3_g, ln3_b,
           *, alpha, batch, seq, mem_len):
    n, d = x2d.shape
    n_exp = w_router.shape[1]
    row = lambda a: a.reshape(1, -1)
    bf = lambda a: a.astype(BF16)

    dils = tuple(dil for _, dil in DILATED_PATTERNS)
    *qkv, p = _in_proj(x2d, bf(w_in), dils)
    branches = []
    for di, (window, dil) in enumerate(DILATED_PATTERNS):
        assert window // dil == ATTN_BLOCK and seq % (dil * ATTN_BLOCK) == 0 and IN_TILE % (dil * 2 * SUBLANES) == 0
        branches.append(_dil_attn(*qkv[3 * di:3 * di + 3], batch, seq, dil))
    (o1, l1), (o4, l4), (o16, l16) = branches
    kmem, vmem = _mem_kv(mem2d, bf(w_ck), bf(w_cv))

    wr_t = w_router.T
    wr_hi = bf(wr_t)
    wr_lo = bf(wr_t - wr_hi.astype(F32))
    x2, base, logits_t = _row_block(
        x2d, o1, o4, o16, l1, l4, l16, p, bf(pool_w), row(pool_scale), bf(w_out), row(ln1_g), row(ln1_b),
        bf(w_cq), kmem, vmem, bf(w_co), row(ln2_g), row(ln2_b), wr_hi, wr_lo, bf(sh_gate), bf(sh_up), bf(sh_down),
        alpha=alpha, batch=batch, seq=seq, mem_len=mem_len)

    bias_col = jnp.broadcast_to(router_bias.reshape(n_exp, 1), (n_exp, LANES))
    lpos, gates, tab, counts = _route(logits_t, bias_col)
    max_rows = n * TOP_K + (n // SORT_TILE) * n_exp * (RUN_ALIGN - 1)
    n_blocks = -(-max_rows // EXPERT_BLOCK) + n_exp
    runs, bexp, seg = _place(counts, tab, n_blocks)
    xs = _dispatch(x2, lpos, runs, seg, n_blocks * EXPERT_BLOCK, n_exp)
    ys = _experts(bexp.reshape(-1), xs, exp_gate, exp_up, exp_down, n_blocks)
    return _combine(runs, lpos, gates, base, row(ln3_g), row(ln3_b), ys, n_exp)


def kernel(x, mem, w_in, pool_w, pool_scale, w_out, ln1_g, ln1_b, w_cq, w_ck, w_cv, w_co, ln2_g, ln2_b, w_router, router_bias, exp_gate, exp_up, exp_down, sh_gate, sh_up, sh_down, ln3_g, ln3_b):
    batch, seq, d = x.shape
    mem_len = mem.shape[1]
    depth = w_in.shape[0]
    alpha = (2.0 * depth) ** 0.25
    x2d = x.reshape(batch * seq, d)
    mem2d = mem.reshape(batch * mem_len, d)
    for l in range(depth):
        x2d = _layer(x2d, mem2d, w_in[l], pool_w[l], pool_scale[l], w_out[l], ln1_g[l], ln1_b[l], w_cq[l], w_ck[l],
                     w_cv[l], w_co[l], ln2_g[l], ln2_b[l], w_router[l], router_bias[l], exp_gate[l], exp_up[l],
                     exp_down[l], sh_gate[l], sh_up[l], sh_down[l], ln3_g[l], ln3_b[l],
                     alpha=alpha, batch=batch, seq=seq, mem_len=mem_len)
    return x2d.reshape(batch, seq, d)
```

```python
import functools

import jax
import jax.numpy as jnp
from jax import lax
from jax.experimental import pallas as pl
from jax.experimental.pallas import tpu as pltpu

F32 = jnp.float32
BF16 = jnp.bfloat16
I32 = jnp.int32
U32 = jnp.uint32

ATTN_HEADS = 8
HEAD_DIM = 64
ATTN_WIDTH = ATTN_HEADS * HEAD_DIM
DILATED_PATTERNS = ((128, 1), (512, 4), (2048, 16))
POOL_WINDOWS = (2, 4, 8, 16)
MEM_HEADS = 4
N_EXPERT_GROUPS = 8
TOPK_GROUPS = 4
TOP_K = 8
ROUTED_SCALE = 2.5
LN_EPS = 1e-5
NEG_INF = -1e30

LANES = 128
SUBLANES = 8
VMEM_LIMIT = 56 * 1024 * 1024

ATTN_BLOCK = 128
ATTN_STEP_BLOCKS = 2
ROW_TILE = 512
IN_TILE = 512
SORT_TILE = 256
EXPERT_BLOCK = 512
RUN_ALIGN = SUBLANES
RUN_CHUNK = 64


def _cparams(sem, vmem=VMEM_LIMIT):
    return pltpu.CompilerParams(dimension_semantics=sem, vmem_limit_bytes=vmem)


def _layer_norm(y, g, b):
    mu = jnp.mean(y, axis=-1, keepdims=True)
    d = y - mu
    var = jnp.mean(d * d, axis=-1, keepdims=True)
    return d * lax.rsqrt(var + LN_EPS) * g + b


def _dot(a, b):
    return jnp.dot(a, b, preferred_element_type=F32)


def _dot_nt(a, b):
    return lax.dot_general(a, b, (((1,), (1,)), ((), ())), preferred_element_type=F32)


def _split_bf16(x):
    hi = x.astype(BF16)
    lo = (x - hi.astype(F32)).astype(BF16)
    return hi, lo


def _bf16_bits(x):
    return lax.bitcast_convert_type(x.astype(BF16).astype(F32), U32)


def _pack_pair(lo, hi):
    return (_bf16_bits(lo) >> 16) | (_bf16_bits(hi) & jnp.uint32(0xFFFF0000))


def _unpack_pair(w):
    lo = lax.bitcast_convert_type(w << 16, F32)
    hi = lax.bitcast_convert_type(w & jnp.uint32(0xFFFF0000), F32)
    return lo, hi


def _in_proj_kernel(x_ref, w_ref, *refs, dils):
    n_qkv = 3 * len(dils)
    out_refs, p_ref, slab_ref = refs[:n_qkv], refs[n_qkv], refs[n_qkv + 1]
    tm = x_ref.shape[0]
    x = x_ref[...].astype(BF16)
    aw = ATTN_WIDTH
    n_slabs = aw // LANES
    for a in range(3):
        val = _dot(x, w_ref[:, a * aw:(a + 1) * aw])
        if a == 0:
            val = val * (HEAD_DIM ** -0.5)
        for s in range(n_slabs):
            slab_ref[s] = val[:, s * LANES:(s + 1) * LANES]
        for di, dil in enumerate(dils):
            dst = out_refs[3 * di + a]
            if dil == 1:
                dst[...] = val.astype(BF16)
                continue
            for r in range(dil):
                for s in range(n_slabs):
                    rows = slab_ref[s, pl.ds(r, tm // dil, stride=dil), :]
                    dst[:, r * aw + s * LANES:r * aw + (s + 1) * LANES] = rows.astype(BF16)
    p_ref[...] = _dot(x, w_ref[:, 3 * aw:])


def _in_proj(x2d, w_in_bf, dils):
    n, d = x2d.shape
    aw = ATTN_WIDTH
    pw = w_in_bf.shape[1] - 3 * aw
    tm = IN_TILE
    row = lambda rows, w: pl.BlockSpec((rows, w), lambda i: (i, 0))
    qkv_specs, qkv_shapes = [], []
    for dil in dils:
        qkv_specs += [row(tm // dil, dil * aw)] * 3
        qkv_shapes += [jax.ShapeDtypeStruct((n // dil, dil * aw), BF16)] * 3
    return pl.pallas_call(
        functools.partial(_in_proj_kernel, dils=dils),
        grid=(n // tm,),
        in_specs=[row(tm, d), pl.BlockSpec(w_in_bf.shape, lambda i: (0, 0))],
        out_specs=qkv_specs + [row(tm, pw)],
        out_shape=qkv_shapes + [jax.ShapeDtypeStruct((n, pw), F32)],
        scratch_shapes=[pltpu.VMEM((aw // LANES, tm, LANES), F32)],
        compiler_params=_cparams(("arbitrary",)),
        name="in_proj",
    )(x2d, w_in_bf)


def _dil_attn_kernel(q_ref, kc_ref, kp_ref, vc_ref, vp_ref, o_ref, lse_ref, bias_ref, *, dil):
    blk = ATTN_BLOCK
    first_step = (pl.program_id(0) == 0) & (pl.program_id(1) == 0) & (pl.program_id(2) == 0)

    @pl.when(first_step)
    def _():
        qi = lax.broadcasted_iota(I32, (blk, 2 * blk), 0)
        kj = lax.broadcasted_iota(I32, (blk, 2 * blk), 1)
        dist = qi + blk - kj
        inside = (dist >= 0) & (dist <= blk)
        penalty = (dist * dil).astype(F32)
        for h in range(ATTN_HEADS):
            slope = 2.0 ** (-8.0 * (h + 1) / ATTN_HEADS)
            bias_ref[0, h] = jnp.where(inside, -slope * penalty, NEG_INF)
            bias_ref[1, h] = jnp.where(inside & (kj >= blk), -slope * penalty, NEG_INF)

    i = pl.program_id(1)
    r = pl.program_id(2)
    lane = lax.broadcasted_iota(I32, (blk, LANES), 1)
    low_half = lane < HEAD_DIM
    n_pairs = ATTN_WIDTH // LANES
    half = n_pairs // 2
    for j in range(ATTN_STEP_BLOCKS):
        table = jnp.where(i == 0, 1, 0) if j == 0 else 0
        lse_tile = jnp.zeros((blk, LANES), F32)
        pair_out = []
        for hp in range(n_pairs):
            sl = slice(hp * LANES, (hp + 1) * LANES)
            q2 = q_ref[j * blk:(j + 1) * blk, sl]
            if j == 0:
                kcat = jnp.concatenate([kp_ref[:, sl], kc_ref[0:blk, sl]], axis=0)
                vcat = jnp.concatenate([vp_ref[:, sl], vc_ref[0:blk, sl]], axis=0)
            else:
                kcat = kc_ref[(j - 1) * blk:(j + 1) * blk, sl]
                vcat = vc_ref[(j - 1) * blk:(j + 1) * blk, sl]
            outs = []
            for e in range(LANES // HEAD_DIM):
                h = hp * (LANES // HEAD_DIM) + e
                keep = low_half if e == 0 else jnp.logical_not(low_half)
                qe = jnp.where(keep, q2, jnp.zeros_like(q2))
                s = _dot_nt(qe, kcat) + bias_ref[table, h]
                m = jnp.max(s, axis=1, keepdims=True)
                p = jnp.exp(s - m)
                l = jnp.sum(p, axis=1, keepdims=True)
                outs.append(_dot(p.astype(BF16), vcat) / l)
                lse_tile = jnp.where(lane == h, m + jnp.log(l), lse_tile)
            pair_out.append(jnp.where(low_half, outs[0], outs[1]))
        rows = pl.ds(j * blk * dil + r, blk, stride=dil) if dil > 1 else slice(j * blk, (j + 1) * blk)
        for w in range(half):
            o_ref[w, rows, :] = _pack_pair(pair_out[w], pair_out[w + half])
        lse_ref[rows, :] = lse_tile


def _dil_attn(q, k, v, batch, seq, dil):
    blk = ATTN_BLOCK
    step_rows = ATTN_STEP_BLOCKS * blk
    sub = seq // dil
    view = lambda a: a.reshape(batch, sub, dil * ATTN_WIDTH)
    cur = pl.BlockSpec((None, step_rows, ATTN_WIDTH), lambda b, i, r: (b, i, r))
    prev = pl.BlockSpec((None, blk, ATTN_WIDTH), lambda b, i, r: (b, jnp.maximum(i * ATTN_STEP_BLOCKS - 1, 0), r))
    half = ATTN_WIDTH // LANES // 2
    o, lse = pl.pallas_call(
        functools.partial(_dil_attn_kernel, dil=dil),
        grid=(batch, sub // step_rows, dil),
        in_specs=[cur, cur, prev, cur, prev],
        out_specs=[pl.BlockSpec((None, half, step_rows * dil, LANES), lambda b, i, r: (b, 0, i, 0)),
                   pl.BlockSpec((None, step_rows * dil, LANES), lambda b, i, r: (b, i, 0))],
        out_shape=[jax.ShapeDtypeStruct((batch, half, seq, LANES), U32),
                   jax.ShapeDtypeStruct((batch, seq, LANES), F32)],
        scratch_shapes=[pltpu.VMEM((2, ATTN_HEADS, blk, 2 * blk), F32)],
        compiler_params=_cparams(("arbitrary",) * 3),
        name=f"dil_attn_d{dil}",
    )(view(q), view(k), view(k), view(v), view(v))
    return o, lse.reshape(batch * seq, LANES)


def _mem_kv_kernel(mem_ref, wk_ref, wv_ref, k_ref, v_ref):
    m = mem_ref[...].astype(BF16)
    k_ref[...] = _dot(m, wk_ref[...]).astype(BF16)
    v_ref[...] = _dot(m, wv_ref[...]).astype(BF16)


def _mem_kv(mem2d, w_ck_bf, w_cv_bf):
    rows = mem2d.shape[0]
    width = w_ck_bf.shape[1]
    return pl.pallas_call(
        _mem_kv_kernel,
        out_shape=[jax.ShapeDtypeStruct((rows, width), BF16)] * 2,
        compiler_params=_cparams(None),
        name="mem_kv",
    )(mem2d, w_ck_bf, w_cv_bf)


def _row_kernel(x_ref, o1_ref, o4_ref, o16_ref, l1_ref, l4_ref, l16_ref, p_ref, halo_ref,
                poolw_ref, pscale_ref, wout_ref, g1_ref, b1_ref,
                wcq_ref, km_ref, vm_ref, wco_ref, g2_ref, b2_ref,
                wrh_ref, wrl_ref, shg_ref, shu_ref, shd_ref,
                x2_ref, base_ref, logit_ref, *, alpha, seq):
    tm = x_ref.shape[0]
    i = pl.program_id(0)
    tile_pos = (i * tm) % seq

    l1, l4, l16 = l1_ref[...], l4_ref[...], l16_ref[...]
    mx = jnp.maximum(jnp.maximum(l1, l4), l16)
    e1, e4, e16 = jnp.exp(l1 - mx), jnp.exp(l4 - mx), jnp.exp(l16 - mx)
    inv = 1.0 / (e1 + e4 + e16)
    hrow = lax.broadcasted_iota(I32, (LANES, ATTN_WIDTH), 0)
    hcol = lax.broadcasted_iota(I32, (LANES, ATTN_WIDTH), 1) // HEAD_DIM
    spread = jnp.where(hrow == hcol, 1.0, 0.0).astype(BF16)

    def widen(w):
        hi, lo = _split_bf16(w)
        return _dot(hi, spread) + _dot(lo, spread)

    def branch(o_ref):
        lo, hi = zip(*[_unpack_pair(o_ref[w]) for w in range(o_ref.shape[0])])
        return jnp.concatenate(lo + hi, axis=1)

    attn = (widen(e1 * inv) * branch(o1_ref) + widen(e4 * inv) * branch(o4_ref)
            + widen(e16 * inv) * branch(o16_ref))

    halo_rows = halo_ref.shape[0]
    halo = jnp.where(tile_pos > 0, halo_ref[...], 0.0)
    ext = jnp.concatenate([halo, p_ref[...]], axis=0)
    pos = (tile_pos + lax.broadcasted_iota(I32, (tm, 1), 0)).astype(F32)
    gd = ext.shape[1] // len(POOL_WINDOWS)
    mixed = []
    for g, w in enumerate(POOL_WINDOWS):
        eg = ext[:, g * gd:(g + 1) * gd]
        acc, span = eg, 1
        while span < w:
            acc = acc + pltpu.roll(acc, span, 0)
            span *= 2
        count = jnp.minimum(pos + 1.0, float(w))
        pooled = acc[halo_rows:, :] / count - eg[halo_rows:, :]
        mixed.append(_dot(pooled.astype(BF16), poolw_ref[g]) * pscale_ref[:, g * gd:(g + 1) * gd])
    cat = jnp.concatenate([attn.astype(BF16)] + [m.astype(BF16) for m in mixed], axis=1)
    x1 = _layer_norm(alpha * x_ref[...] + _dot(cat, wout_ref[...]), g1_ref[...], b1_ref[...])

    qc = _dot(x1.astype(BF16), wcq_ref[...])
    mhd = qc.shape[1] // MEM_HEADS
    heads = []
    for h in range(MEM_HEADS):
        sl = slice(h * mhd, (h + 1) * mhd)
        s = _dot_nt(qc[:, sl].astype(BF16), km_ref[:, sl]) * (mhd ** -0.5)
        m = jnp.max(s, axis=1, keepdims=True)
        p = jnp.exp(s - m)
        l = jnp.sum(p, axis=1, keepdims=True)
        heads.append((_dot(p.astype(BF16), vm_ref[:, sl]) / l).astype(BF16))
    oc = jnp.concatenate(heads, axis=1)
    x2 = _layer_norm(alpha * x1 + _dot(oc, wco_ref[...]), g2_ref[...], b2_ref[...])
    dh = x2.shape[1] // 2
    x2_ref[...] = _pack_pair(x2[:, :dh], x2[:, dh:])

    xh, xl = _split_bf16(x2)
    logit_ref[...] = _dot_nt(wrh_ref[...], xh) + _dot_nt(wrh_ref[...], xl) + _dot_nt(wrl_ref[...], xh)

    gate = _dot(xh, shg_ref[...])
    up = _dot(xh, shu_ref[...])
    hid = gate / (1.0 + jnp.exp(-gate)) * up
    base_ref[...] = alpha * x2 + _dot(hid.astype(BF16), shd_ref[...])


def _row_block(x2d, o1, o4, o16, l1, l4, l16, p, pool_w_bf, pool_scale, w_out_bf, g1, b1,
               w_cq_bf, kmem, vmem, w_co_bf, g2, b2, wr_hi, wr_lo, sh_g, sh_u, sh_d,
               *, alpha, batch, seq, mem_len):
    n, d = x2d.shape
    tm = ROW_TILE
    halo_rows = max(POOL_WINDOWS)
    steps_per_seq = seq // tm
    row = lambda w: pl.BlockSpec((tm, w), lambda i: (i, 0))
    full = lambda a: pl.BlockSpec(a.shape, lambda i: (0,) * a.ndim)
    halo = pl.BlockSpec((halo_rows, p.shape[1]), lambda i: (jnp.maximum(i * (tm // halo_rows) - 1, 0), 0))
    memspec = pl.BlockSpec((mem_len, kmem.shape[1]), lambda i: (i // steps_per_seq, 0))
    n_exp = wr_hi.shape[0]
    branch = pl.BlockSpec((None, o1.shape[1], tm, LANES), lambda i: (i // steps_per_seq, 0, i % steps_per_seq, 0))
    return pl.pallas_call(
        functools.partial(_row_kernel, alpha=alpha, seq=seq),
        grid=(n // tm,),
        in_specs=[row(d), branch, branch, branch, row(LANES), row(LANES), row(LANES),
                  row(p.shape[1]), halo,
                  full(pool_w_bf), full(pool_scale), full(w_out_bf), full(g1), full(b1),
                  full(w_cq_bf), memspec, memspec, full(w_co_bf), full(g2), full(b2),
                  full(wr_hi), full(wr_lo), full(sh_g), full(sh_u), full(sh_d)],
        out_specs=[row(d // 2), row(d), pl.BlockSpec((n_exp, tm), lambda i: (0, i))],
        out_shape=[jax.ShapeDtypeStruct((n, d // 2), U32), jax.ShapeDtypeStruct((n, d), F32),
                   jax.ShapeDtypeStruct((n_exp, n), F32)],
        compiler_params=_cparams(("arbitrary",)),
        name="row_block",
    )(x2d, o1, o4, o16, l1, l4, l16, p, p, pool_w_bf, pool_scale, w_out_bf, g1, b1,
      w_cq_bf, kmem, vmem, w_co_bf, g2, b2, wr_hi, wr_lo, sh_g, sh_u, sh_d)


def _beat_counts(vals, n_rows):
    tn = vals.shape[1]
    n_tiles = n_rows // SUBLANES
    tiles = [vals[t * SUBLANES:(t + 1) * SUBLANES, :] for t in range(n_tiles)]
    sub = lax.broadcasted_iota(I32, (SUBLANES, tn), 0)
    counts = [jnp.zeros((SUBLANES, tn), F32) for _ in range(n_tiles)]
    for e in range(n_rows):
        te, je = divmod(e, SUBLANES)
        row = jnp.broadcast_to(vals[e:e + 1, :], (SUBLANES, tn))
        for t in range(n_tiles):
            strict = jnp.where(row > tiles[t], 1.0, 0.0)
            loose = jnp.where(row >= tiles[t], 1.0, 0.0)
            if t < te:
                beat = strict
            elif t > te:
                beat = loose
            else:
                beat = jnp.where(sub > je, loose, strict)
            counts[t] = counts[t] + beat
    return jnp.concatenate(counts, axis=0)


def _route_kernel(logit_ref, bias_ref, lpos_ref, gate_ref, tab_ref, count_ref, carry_ref, rcarry_ref, *, per_group):
    n_exp, tn = logit_ref.shape
    step = pl.program_id(0)

    @pl.when(step == 0)
    def _():
        carry_ref[...] = jnp.zeros_like(carry_ref)
        rcarry_ref[...] = jnp.zeros_like(rcarry_ref)

    scores = 1.0 / (1.0 + jnp.exp(-logit_ref[...]))
    biased = scores + bias_ref[:, 0:1]
    n_groups = n_exp // per_group

    gscore = []
    for g in range(n_groups):
        tile = biased[g * per_group:(g + 1) * per_group, :]
        inner = _beat_counts(tile, per_group)
        gscore.append(jnp.sum(jnp.where(inner < 2.0, tile, 0.0), axis=0, keepdims=True))
    gscore = jnp.concatenate(gscore, axis=0)
    grank = _beat_counts(gscore, n_groups)
    grank = jnp.concatenate(
        [jnp.broadcast_to(grank[g:g + 1, :], (per_group, tn)) for g in range(n_groups)], axis=0)
    masked = jnp.where(grank < float(TOPK_GROUPS), biased, NEG_INF)
    self32 = jnp.where(_beat_counts(masked, n_exp) < float(TOP_K), 1.0, 0.0)
    selbf = self32.astype(BF16)

    er = lax.broadcasted_iota(I32, (n_exp, n_exp), 0)
    ec = lax.broadcasted_iota(I32, (n_exp, n_exp), 1)
    below = jnp.where(ec < er, 1.0, 0.0).astype(BF16)
    slot = _dot(below, selbf)
    tr = lax.broadcasted_iota(I32, (tn, tn), 0)
    tc = lax.broadcasted_iota(I32, (tn, tn), 1)
    before = jnp.where(tr < tc, 1.0, 0.0).astype(BF16)
    lrank = _dot(selbf, before)
    align = float(RUN_ALIGN)
    lcount_col = jnp.broadcast_to(jnp.sum(self32, axis=1, keepdims=True), (n_exp, LANES))
    sel_wide = jnp.concatenate([selbf, jnp.zeros((LANES - n_exp, tn), BF16)], axis=0)
    lcount_row = _dot_nt(jnp.ones((SUBLANES, tn), BF16), sel_wide)
    lcount_col = jnp.ceil(lcount_col / align) * align
    lcount_row = jnp.ceil(lcount_row / align) * align
    lstart_col = _dot(below, lcount_col.astype(BF16))
    wr = lax.broadcasted_iota(I32, (LANES, LANES), 0)
    wc = lax.broadcasted_iota(I32, (LANES, LANES), 1)
    lstart_row = _dot(lcount_row.astype(BF16), jnp.where(wr < wc, 1.0, 0.0).astype(BF16))
    lpos = lstart_col[:, 0:1] + lrank

    sub = lax.broadcasted_iota(I32, (SUBLANES, LANES), 0)
    tab_ref[...] = jnp.where(sub == 0, lstart_row, jnp.where(sub == 1, lcount_row, jnp.where(
        sub == 2, rcarry_ref[...], 0.0))).astype(I32)
    rcarry_ref[...] = rcarry_ref[...] + lcount_row
    carry_ref[...] = carry_ref[...] + lcount_col
    count_ref[...] = carry_ref[...]

    lposk, gates = [], []
    for k in range(TOP_K):
        pick = jnp.where(slot == float(k), self32, 0.0)
        lposk.append(jnp.sum(pick * lpos, axis=0, keepdims=True))
        gates.append(jnp.sum(pick * scores, axis=0, keepdims=True))
    gates = jnp.concatenate(gates, axis=0)
    gates = gates / jnp.sum(gates, axis=0, keepdims=True) * ROUTED_SCALE
    lpos_ref[...] = jnp.concatenate(lposk, axis=0).astype(I32)
    gate_ref[...] = gates


def _route(logits_t, bias_col):
    n_exp, n = logits_t.shape
    tn = SORT_TILE
    tok = lambda dt: jax.ShapeDtypeStruct((TOP_K, n), dt)
    tokspec = pl.BlockSpec((TOP_K, tn), lambda i: (0, i))
    return pl.pallas_call(
        functools.partial(_route_kernel, per_group=n_exp // N_EXPERT_GROUPS),
        grid=(n // tn,),
        in_specs=[pl.BlockSpec((n_exp, tn), lambda i: (0, i)), pl.BlockSpec(bias_col.shape, lambda i: (0, 0))],
        out_specs=[tokspec, tokspec, pl.BlockSpec((None, SUBLANES, LANES), lambda i: (i, 0, 0)),
                   pl.BlockSpec((n_exp, LANES), lambda i: (0, 0))],
        out_shape=[tok(I32), tok(F32), jax.ShapeDtypeStruct((n // tn, SUBLANES, LANES), I32),
                   jax.ShapeDtypeStruct((n_exp, LANES), F32)],
        scratch_shapes=[pltpu.VMEM((n_exp, LANES), F32), pltpu.VMEM((SUBLANES, LANES), F32)],
        compiler_params=_cparams(("arbitrary",)),
        name="route",
    )(logits_t, bias_col)


def _place_kernel(count_ref, tab_ref, runs_ref, bexp_ref, seg_ref, *, n_blocks):
    n_exp = count_ref.shape[0]
    blk = float(EXPERT_BLOCK)
    counts = count_ref[...]
    padded = jnp.ceil(counts / blk) * blk
    ridx = lax.broadcasted_iota(I32, (n_exp, LANES), 0)
    ends = padded
    shift = 1
    while shift < n_exp:
        ends = ends + jnp.where(ridx >= shift, pltpu.roll(ends, shift, 0), 0.0)
        shift *= 2
    starts = ends - padded

    tab = tab_ref[...]
    last = tab[tab.shape[0] - 1]
    total_row = (last[2:3, :] + last[1:2, :]).astype(F32)
    padded_row = jnp.ceil(total_row / blk) * blk
    lidx = lax.broadcasted_iota(I32, (1, LANES), 1)
    ends_row = padded_row
    shift = 1
    while shift < LANES:
        ends_row = ends_row + jnp.where(lidx >= shift, pltpu.roll(ends_row, shift, 1), 0.0)
        shift *= 2
    starts_row = (ends_row - padded_row).astype(I32)
    sub = lax.broadcasted_iota(I32, tab.shape, 1)
    runs_ref[...] = tab + jnp.where(sub == 2, starts_row[None], 0)

    nbp = bexp_ref.shape[1]
    blk_start = (lax.broadcasted_iota(I32, (n_exp, nbp), 1) * EXPERT_BLOCK).astype(F32)
    passed = jnp.where(jnp.broadcast_to(ends[:, 0:1], (n_exp, nbp)) <= blk_start, 1.0, 0.0)
    bexp = jnp.minimum(jnp.sum(passed, axis=0, keepdims=True), float(n_exp - 1))
    used = ends[n_exp - 1:n_exp, 0:1] / blk
    lane = lax.broadcasted_iota(I32, (1, nbp), 1)
    bexp_ref[...] = jnp.where(lane == n_blocks, used, bexp).astype(I32)
    lane2 = lax.broadcasted_iota(I32, (n_exp, LANES), 1)
    seg_ref[...] = jnp.where(lane2 == 0, starts + counts, ends).astype(I32)


def _place(counts, tab, n_blocks):
    n_exp = counts.shape[0]
    nbp = -(-(n_blocks + 1) // LANES) * LANES
    return pl.pallas_call(
        functools.partial(_place_kernel, n_blocks=n_blocks),
        out_shape=[jax.ShapeDtypeStruct(tab.shape, I32), jax.ShapeDtypeStruct((1, nbp), I32),
                   jax.ShapeDtypeStruct((n_exp, LANES), I32)],
        compiler_params=_cparams(None),
        name="route_place",
    )(counts, tab)


def _copy_rows(src_ref, dst_ref, sem, src0, dst0, n, wait=False):
    def piece(off, p):
        src = src_ref.at[pl.ds(pl.multiple_of(src0 + off, RUN_ALIGN), p), :]
        dst = dst_ref.at[pl.ds(pl.multiple_of(dst0 + off, RUN_ALIGN), p), :]
        cp = pltpu.make_async_copy(src, dst, sem)
        if wait:
            cp.wait()
        else:
            cp.start()

    def chunk(c, carry):
        piece(c * RUN_CHUNK, RUN_CHUNK)
        return carry
    lax.fori_loop(0, lax.shift_right_logical(n, RUN_CHUNK.bit_length() - 1), chunk, 0)
    p = RUN_CHUNK // 2
    while p >= RUN_ALIGN:
        @pl.when((n & p) != 0)
        def _(p=p):
            piece(n & (-2 * p), p)
        p //= 2


def _dispatch_kernel(runs_ref, seg_ref, lpos_ref, x_ref, xs_ref, buf_ref, zero_ref, sems, zsem, *, n_exp):
    step, n_steps = pl.program_id(0), pl.num_programs(0)
    tm = x_ref.shape[0]
    rows = buf_ref.shape[1]
    slot = step % 2

    @pl.when(step == 0)
    def _():
        zero_ref[...] = jnp.zeros_like(zero_ref)
        for wait in (False, True):
            def per_expert(e, c, wait=wait):
                first = seg_ref[e, 0]
                _copy_rows(zero_ref, xs_ref, zsem, 0, first, seg_ref[e, 1] - first, wait)
                return c
            lax.fori_loop(0, n_exp, per_expert, 0)

    lo, hi = _unpack_pair(x_ref[...])
    x = jnp.concatenate([lo.astype(BF16), hi.astype(BF16)], axis=1)
    j = lax.broadcasted_iota(I32, (rows, tm), 0)
    onehot = jnp.zeros((rows, tm), F32)
    for k in range(TOP_K):
        onehot = jnp.where(j == lpos_ref[k:k + 1, :], 1.0, onehot)
    srt = _dot(onehot.astype(BF16), x)
    dh = srt.shape[1] // 2
    bits = lambda v: lax.bitcast_convert_type(v, U32)
    buf_ref[slot] = (bits(srt[:, :dh]) >> 16) | bits(srt[:, dh:])

    def per_run(e, c):
        _copy_rows(buf_ref.at[slot], xs_ref, sems.at[slot], runs_ref[0, e], runs_ref[2, e], runs_ref[1, e])
        return c
    lax.fori_loop(0, n_exp, per_run, 0)
    used = runs_ref[0, n_exp - 1] + runs_ref[1, n_exp - 1]
    spare = xs_ref.shape[0] - 2 * (rows - TOP_K * tm) + slot * (rows - TOP_K * tm)
    _copy_rows(buf_ref.at[slot], xs_ref, sems.at[slot], used, spare, rows - used)

    def drain(s):
        pltpu.make_async_copy(buf_ref.at[s], xs_ref.at[pl.ds(0, rows), :], sems.at[s]).wait()

    @pl.when(step > 0)
    def _():
        drain(1 - slot)

    @pl.when(step == n_steps - 1)
    def _():
        drain(slot)


def _sorted_rows(tm, n_exp):
    return TOP_K * tm + n_exp * RUN_ALIGN


def _dispatch(x2, lpos, runs, seg, n_rows, n_exp):
    n, d = x2.shape
    tm = SORT_TILE
    rows = _sorted_rows(tm, n_exp)
    return pl.pallas_call(
        functools.partial(_dispatch_kernel, n_exp=n_exp),
        grid=(n // tm,),
        in_specs=[pl.BlockSpec((None,) + runs.shape[1:], lambda i: (i, 0, 0), memory_space=pltpu.SMEM),
                  pl.BlockSpec(seg.shape, lambda i: (0, 0), memory_space=pltpu.SMEM),
                  pl.BlockSpec((TOP_K, tm), lambda i: (0, i)),
                  pl.BlockSpec((tm, d), lambda i: (i, 0))],
        out_specs=pl.BlockSpec(memory_space=pl.ANY),
        out_shape=jax.ShapeDtypeStruct((n_rows + 2 * (rows - TOP_K * tm), d), x2.dtype),
        scratch_shapes=[pltpu.VMEM((2, rows, d), x2.dtype), pltpu.VMEM((EXPERT_BLOCK, d), x2.dtype),
                        pltpu.SemaphoreType.DMA((2,)), pltpu.SemaphoreType.DMA],
        compiler_params=_cparams(("arbitrary",)),
        name="dispatch",
    )(runs, seg, lpos, x2)


def _expert_kernel(bexp_ref, xs_ref, wg_ref, wu_ref, wd_ref, ys_ref, wg_bf, wu_bf, wd_bf, *, n_blocks):
    i = pl.program_id(0)
    used = bexp_ref[n_blocks]

    @pl.when(i < used)
    def _():
        @pl.when((i == 0) | (bexp_ref[i] != bexp_ref[jnp.maximum(i - 1, 0)]))
        def _():
            wg_bf[...] = wg_ref[...].astype(BF16)
            wu_bf[...] = wu_ref[...].astype(BF16)
            wd_bf[...] = wd_ref[...].astype(BF16)

        lo, hi = _unpack_pair(xs_ref[...])
        x = jnp.concatenate([lo.astype(BF16), hi.astype(BF16)], axis=1)
        gate = _dot(x, wg_bf[...])
        up = _dot(x, wu_bf[...])
        hid = gate / (1.0 + jnp.exp(-gate)) * up
        y = _dot(hid.astype(BF16), wd_bf[...])
        dh = y.shape[1] // 2
        ys_ref[...] = _pack_pair(y[:, :dh], y[:, dh:])


def _experts(bexp, xs, exp_gate, exp_up, exp_down, n_blocks):
    n_rows = xs.shape[0]
    d, hidden = exp_gate.shape[1:]
    blk = EXPERT_BLOCK

    def rows(i, b):
        return (jnp.minimum(i, jnp.maximum(b[n_blocks] - 1, 0)), 0)

    def weight(i, b):
        return (b[jnp.minimum(i, jnp.maximum(b[n_blocks] - 1, 0))], 0, 0)

    grid_spec = pltpu.PrefetchScalarGridSpec(
        num_scalar_prefetch=1,
        grid=(n_blocks,),
        in_specs=[pl.BlockSpec((blk, d // 2), rows),
                  pl.BlockSpec((None, d, hidden), weight),
                  pl.BlockSpec((None, d, hidden), weight),
                  pl.BlockSpec((None, hidden, d), weight)],
        out_specs=pl.BlockSpec((blk, d // 2), rows),
        scratch_shapes=[pltpu.VMEM((d, hidden), BF16), pltpu.VMEM((d, hidden), BF16), pltpu.VMEM((hidden, d), BF16)],
    )
    return pl.pallas_call(
        functools.partial(_expert_kernel, n_blocks=n_blocks),
        grid_spec=grid_spec,
        out_shape=jax.ShapeDtypeStruct((n_rows, d // 2), U32),
        compiler_params=_cparams(("arbitrary",)),
        name="experts",
    )(bexp, xs, exp_gate, exp_up, exp_down)


def _combine_kernel(runs_ref, next_runs_ref, lpos_ref, gate_ref, base_ref, g3_ref, b3_ref, ys_ref, out_ref,
                    buf_ref, sems, *, n_exp):
    step, n_steps = pl.program_id(0), pl.num_programs(0)
    tm = base_ref.shape[0]
    rows = buf_ref.shape[1]
    slot = step % 2

    def fetch(tab_ref, s):
        def per_run(e, c):
            _copy_rows(ys_ref, buf_ref.at[s], sems.at[s], tab_ref[2, e], tab_ref[0, e], tab_ref[1, e])
            return c
        lax.fori_loop(0, n_exp, per_run, 0)
        used = tab_ref[0, n_exp - 1] + tab_ref[1, n_exp - 1]
        _copy_rows(ys_ref, buf_ref.at[s], sems.at[s], 0, used, rows - used)

    @pl.when(step == 0)
    def _():
        fetch(runs_ref, slot)

    @pl.when(step + 1 < n_steps)
    def _():
        fetch(next_runs_ref, 1 - slot)

    pad = jnp.zeros((LANES - TOP_K, tm), F32)
    lpos_t = jnp.concatenate([lpos_ref[...].astype(F32), pad], axis=0).T.astype(I32)
    gate_t = jnp.concatenate([gate_ref[...], pad], axis=0).T
    j = lax.broadcasted_iota(I32, (tm, rows), 1)
    w = jnp.zeros((tm, rows), F32)
    for k in range(TOP_K):
        w = jnp.where(j == lpos_t[:, k:k + 1], gate_t[:, k:k + 1], w)

    pltpu.make_async_copy(ys_ref.at[pl.ds(0, rows), :], buf_ref.at[slot], sems.at[slot]).wait()
    lo, hi = _unpack_pair(buf_ref[slot])
    y = jnp.concatenate([lo.astype(BF16), hi.astype(BF16)], axis=1)
    routed = _dot(w.astype(BF16), y)
    out_ref[...] = _layer_norm(base_ref[...] + routed, g3_ref[...], b3_ref[...])


def _combine(runs, lpos, gates, base, g3, b3, ys, n_exp):
    n, d = base.shape
    tm = SORT_TILE
    last = n // tm - 1
    tab = lambda f: pl.BlockSpec((None,) + runs.shape[1:], f, memory_space=pltpu.SMEM)
    return pl.pallas_call(
        functools.partial(_combine_kernel, n_exp=n_exp),
        grid=(n // tm,),
        in_specs=[tab(lambda i: (i, 0, 0)), tab(lambda i: (jnp.minimum(i + 1, last), 0, 0)),
                  pl.BlockSpec((TOP_K, tm), lambda i: (0, i)),
                  pl.BlockSpec((TOP_K, tm), lambda i: (0, i)),
                  pl.BlockSpec((tm, d), lambda i: (i, 0)),
                  pl.BlockSpec(g3.shape, lambda i: (0, 0)),
                  pl.BlockSpec(b3.shape, lambda i: (0, 0)),
                  pl.BlockSpec(memory_space=pl.ANY)],
        out_specs=pl.BlockSpec((tm, d), lambda i: (i, 0)),
        out_shape=jax.ShapeDtypeStruct((n, d), F32),
        scratch_shapes=[pltpu.VMEM((2, _sorted_rows(tm, n_exp), ys.shape[1]), ys.dtype),
                        pltpu.SemaphoreType.DMA((2,))],
        compiler_params=_cparams(("arbitrary",)),
        name="combine",
    )(runs, runs, lpos, gates, base, g3, b3, ys)


def _layer(x2d, mem2d, w_in, pool_w, pool_scale, w_out, ln1_g, ln1_b, w_cq, w_ck, w_cv, w_co, ln2_g, ln2_b,
           w_router, router_bias, exp_gate, exp_up, exp_down, sh_gate, sh_up, sh_down, ln3_g, ln3_b,
           *, alpha, batch, seq, mem_len):
    n, d = x2d.shape
    n_exp = w_router.shape[1]
    row = lambda a: a.reshape(1, -1)
    bf = lambda a: a.astype(BF16)

    dils = tuple(dil for _, dil in DILATED_PATTERNS)
    *qkv, p = _in_proj(x2d, bf(w_in), dils)
    branches = []
    for di, (window, dil) in enumerate(DILATED_PATTERNS):
        assert window // dil == ATTN_BLOCK and seq % (dil * ATTN_BLOCK * ATTN_STEP_BLOCKS) == 0
        assert IN_TILE % (dil * 2 * SUBLANES) == 0
        branches.append(_dil_attn(*qkv[3 * di:3 * di + 3], batch, seq, dil))
    (o1, l1), (o4, l4), (o16, l16) = branches
    kmem, vmem = _mem_kv(mem2d, bf(w_ck), bf(w_cv))

    wr_t = w_router.T
    wr_hi = bf(wr_t)
    wr_lo = bf(wr_t - wr_hi.astype(F32))
    x2, base, logits_t = _row_block(
        x2d, o1, o4, o16, l1, l4, l16, p, bf(pool_w), row(pool_scale), bf(w_out), row(ln1_g), row(ln1_b),
        bf(w_cq), kmem, vmem, bf(w_co), row(ln2_g), row(ln2_b), wr_hi, wr_lo, bf(sh_gate), bf(sh_up), bf(sh_down),
        alpha=alpha, batch=batch, seq=seq, mem_len=mem_len)

    bias_col = jnp.broadcast_to(router_bias.reshape(n_exp, 1), (n_exp, LANES))
    lpos, gates, tab, counts = _route(logits_t, bias_col)
    max_rows = n * TOP_K + (n // SORT_TILE) * n_exp * (RUN_ALIGN - 1)
    n_blocks = -(-max_rows // EXPERT_BLOCK) + n_exp
    runs, bexp, seg = _place(counts, tab, n_blocks)
    xs = _dispatch(x2, lpos, runs, seg, n_blocks * EXPERT_BLOCK, n_exp)
    ys = _experts(bexp.reshape(-1), xs, exp_gate, exp_up, exp_down, n_blocks)
    return _combine(runs, lpos, gates, base, row(ln3_g), row(ln3_b), ys, n_exp)


def kernel(x, mem, w_in, pool_w, pool_scale, w_out, ln1_g, ln1_b, w_cq, w_ck, w_cv, w_co, ln2_g, ln2_b, w_router, router_bias, exp_gate, exp_up, exp_down, sh_gate, sh_up, sh_down, ln3_g, ln3_b):
    batch, seq, d = x.shape
    mem_len = mem.shape[1]
    depth = w_in.shape[0]
    alpha = (2.0 * depth) ** 0.25
    x2d = x.reshape(batch * seq, d)
    mem2d = mem.reshape(batch * mem_len, d)
    for l in range(depth):
        x2d = _layer(x2d, mem2d, w_in[l], pool_w[l], pool_scale[l], w_out[l], ln1_g[l], ln1_b[l], w_cq[l], w_ck[l],
                     w_cv[l], w_co[l], ln2_g[l], ln2_b[l], w_router[l], router_bias[l], exp_gate[l], exp_up[l],
                     exp_down[l], sh_gate[l], sh_up[l], sh_down[l], ln3_g[l], ln3_b[l],
                     alpha=alpha, batch=batch, seq=seq, mem_len=mem_len)
    return x2d.reshape(batch, seq, d)
```

```python
import functools

import jax
import jax.numpy as jnp
from jax import lax
from jax.experimental import pallas as pl
from jax.experimental.pallas import tpu as pltpu

F32 = jnp.float32
BF16 = jnp.bfloat16
I32 = jnp.int32
U32 = jnp.uint32

ATTN_HEADS = 8
HEAD_DIM = 64
ATTN_WIDTH = ATTN_HEADS * HEAD_DIM
DILATED_PATTERNS = ((128, 1), (512, 4), (2048, 16))
POOL_WINDOWS = (2, 4, 8, 16)
MEM_HEADS = 4
N_EXPERT_GROUPS = 8
TOPK_GROUPS = 4
TOP_K = 8
ROUTED_SCALE = 2.5
LN_EPS = 1e-5
NEG_INF = -1e30

LANES = 128
SUBLANES = 8
VMEM_LIMIT = 56 * 1024 * 1024

ATTN_BLOCK = 128
ATTN_STEP_BLOCKS = 2
ROW_TILE = 512
IN_TILE = 512
SORT_TILE = 256
EXPERT_BLOCK = 512
RUN_ALIGN = SUBLANES
RUN_CHUNK = 64
POS_BASE = 64


def _cparams(sem, vmem=VMEM_LIMIT):
    return pltpu.CompilerParams(dimension_semantics=sem, vmem_limit_bytes=vmem)


def _layer_norm(y, g, b):
    mu = jnp.mean(y, axis=-1, keepdims=True)
    d = y - mu
    var = jnp.mean(d * d, axis=-1, keepdims=True)
    return d * lax.rsqrt(var + LN_EPS) * g + b


def _dot(a, b):
    return jnp.dot(a, b, preferred_element_type=F32)


def _dot_nt(a, b):
    return lax.dot_general(a, b, (((1,), (1,)), ((), ())), preferred_element_type=F32)


def _split_bf16(x):
    hi = x.astype(BF16)
    lo = (x - hi.astype(F32)).astype(BF16)
    return hi, lo


def _bf16_bits(x):
    return lax.bitcast_convert_type(x.astype(BF16).astype(F32), U32)


def _pack_pair(lo, hi):
    return (_bf16_bits(lo) >> 16) | (_bf16_bits(hi) & jnp.uint32(0xFFFF0000))


def _unpack_pair(w):
    lo = lax.bitcast_convert_type(w << 16, F32)
    hi = lax.bitcast_convert_type(w & jnp.uint32(0xFFFF0000), F32)
    return lo, hi


def _in_proj_kernel(x_ref, w_ref, *refs, dils):
    n_qkv = 3 * len(dils)
    out_refs, p_ref, slab_ref = refs[:n_qkv], refs[n_qkv], refs[n_qkv + 1]
    tm = x_ref.shape[0]
    x = x_ref[...].astype(BF16)
    aw = ATTN_WIDTH
    n_slabs = aw // LANES
    for a in range(3):
        val = _dot(x, w_ref[:, a * aw:(a + 1) * aw])
        if a == 0:
            val = val * (HEAD_DIM ** -0.5)
        for s in range(n_slabs):
            slab_ref[s] = val[:, s * LANES:(s + 1) * LANES]
        for di, dil in enumerate(dils):
            dst = out_refs[3 * di + a]
            if dil == 1:
                dst[...] = val.astype(BF16)
                continue
            for r in range(dil):
                for s in range(n_slabs):
                    rows = slab_ref[s, pl.ds(r, tm // dil, stride=dil), :]
                    dst[:, r * aw + s * LANES:r * aw + (s + 1) * LANES] = rows.astype(BF16)
    p_ref[...] = _dot(x, w_ref[:, 3 * aw:])


def _in_proj(x2d, w_in_bf, dils):
    n, d = x2d.shape
    aw = ATTN_WIDTH
    pw = w_in_bf.shape[1] - 3 * aw
    tm = IN_TILE
    row = lambda rows, w: pl.BlockSpec((rows, w), lambda i: (i, 0))
    qkv_specs, qkv_shapes = [], []
    for dil in dils:
        qkv_specs += [row(tm // dil, dil * aw)] * 3
        qkv_shapes += [jax.ShapeDtypeStruct((n // dil, dil * aw), BF16)] * 3
    return pl.pallas_call(
        functools.partial(_in_proj_kernel, dils=dils),
        grid=(n // tm,),
        in_specs=[row(tm, d), pl.BlockSpec(w_in_bf.shape, lambda i: (0, 0))],
        out_specs=qkv_specs + [row(tm, pw)],
        out_shape=qkv_shapes + [jax.ShapeDtypeStruct((n, pw), F32)],
        scratch_shapes=[pltpu.VMEM((aw // LANES, tm, LANES), F32)],
        compiler_params=_cparams(("arbitrary",)),
        name="in_proj",
    )(x2d, w_in_bf)


def _dil_attn_kernel(q_ref, kc_ref, kp_ref, vc_ref, vp_ref, o_ref, lse_ref, bias_ref, *, dil):
    blk = ATTN_BLOCK
    first_step = (pl.program_id(0) == 0) & (pl.program_id(1) == 0) & (pl.program_id(2) == 0)

    @pl.when(first_step)
    def _():
        qi = lax.broadcasted_iota(I32, (blk, 2 * blk), 0)
        kj = lax.broadcasted_iota(I32, (blk, 2 * blk), 1)
        dist = qi + blk - kj
        inside = (dist >= 0) & (dist <= blk)
        penalty = (dist * dil).astype(F32)
        for h in range(ATTN_HEADS):
            slope = 2.0 ** (-8.0 * (h + 1) / ATTN_HEADS)
            bias_ref[0, h] = jnp.where(inside, -slope * penalty, NEG_INF)
            bias_ref[1, h] = jnp.where(inside & (kj >= blk), -slope * penalty, NEG_INF)

    i = pl.program_id(1)
    r = pl.program_id(2)
    lane = lax.broadcasted_iota(I32, (blk, LANES), 1)
    low_half = lane < HEAD_DIM
    n_pairs = ATTN_WIDTH // LANES
    half = n_pairs // 2
    for j in range(ATTN_STEP_BLOCKS):
        table = jnp.where(i == 0, 1, 0) if j == 0 else 0
        lse_tile = jnp.zeros((blk, LANES), F32)
        pair_out = []
        for hp in range(n_pairs):
            sl = slice(hp * LANES, (hp + 1) * LANES)
            q2 = q_ref[j * blk:(j + 1) * blk, sl]
            if j == 0:
                kcat = jnp.concatenate([kp_ref[:, sl], kc_ref[0:blk, sl]], axis=0)
                vcat = jnp.concatenate([vp_ref[:, sl], vc_ref[0:blk, sl]], axis=0)
            else:
                kcat = kc_ref[(j - 1) * blk:(j + 1) * blk, sl]
                vcat = vc_ref[(j - 1) * blk:(j + 1) * blk, sl]
            outs = []
            for e in range(LANES // HEAD_DIM):
                h = hp * (LANES // HEAD_DIM) + e
                keep = low_half if e == 0 else jnp.logical_not(low_half)
                qe = jnp.where(keep, q2, jnp.zeros_like(q2))
                s = _dot_nt(qe, kcat) + bias_ref[table, h]
                m = jnp.max(s, axis=1, keepdims=True)
                p = jnp.exp(s - m)
                l = jnp.sum(p, axis=1, keepdims=True)
                outs.append(_dot(p.astype(BF16), vcat) / l)
                lse_tile = jnp.where(lane == h, m + jnp.log(l), lse_tile)
            pair_out.append(jnp.where(low_half, outs[0], outs[1]))
        rows = pl.ds(j * blk * dil + r, blk, stride=dil) if dil > 1 else slice(j * blk, (j + 1) * blk)
        for w in range(half):
            o_ref[w, rows, :] = _pack_pair(pair_out[w], pair_out[w + half])
        lse_ref[rows, :] = lse_tile


def _dil_attn(q, k, v, batch, seq, dil):
    blk = ATTN_BLOCK
    step_rows = ATTN_STEP_BLOCKS * blk
    sub = seq // dil
    view = lambda a: a.reshape(batch, sub, dil * ATTN_WIDTH)
    cur = pl.BlockSpec((None, step_rows, ATTN_WIDTH), lambda b, i, r: (b, i, r))
    prev = pl.BlockSpec((None, blk, ATTN_WIDTH), lambda b, i, r: (b, jnp.maximum(i * ATTN_STEP_BLOCKS - 1, 0), r))
    half = ATTN_WIDTH // LANES // 2
    o, lse = pl.pallas_call(
        functools.partial(_dil_attn_kernel, dil=dil),
        grid=(batch, sub // step_rows, dil),
        in_specs=[cur, cur, prev, cur, prev],
        out_specs=[pl.BlockSpec((None, half, step_rows * dil, LANES), lambda b, i, r: (b, 0, i, 0)),
                   pl.BlockSpec((None, step_rows * dil, LANES), lambda b, i, r: (b, i, 0))],
        out_shape=[jax.ShapeDtypeStruct((batch, half, seq, LANES), U32),
                   jax.ShapeDtypeStruct((batch, seq, LANES), F32)],
        scratch_shapes=[pltpu.VMEM((2, ATTN_HEADS, blk, 2 * blk), F32)],
        compiler_params=_cparams(("arbitrary",) * 3),
        name=f"dil_attn_d{dil}",
    )(view(q), view(k), view(k), view(v), view(v))
    return o, lse.reshape(batch * seq, LANES)


def _mem_kv_kernel(mem_ref, wk_ref, wv_ref, k_ref, v_ref):
    m = mem_ref[...].astype(BF16)
    k_ref[...] = _dot(m, wk_ref[...]).astype(BF16)
    v_ref[...] = _dot(m, wv_ref[...]).astype(BF16)


def _mem_kv(mem2d, w_ck_bf, w_cv_bf):
    rows = mem2d.shape[0]
    width = w_ck_bf.shape[1]
    return pl.pallas_call(
        _mem_kv_kernel,
        out_shape=[jax.ShapeDtypeStruct((rows, width), BF16)] * 2,
        compiler_params=_cparams(None),
        name="mem_kv",
    )(mem2d, w_ck_bf, w_cv_bf)


def _row_kernel(x_ref, o1_ref, o4_ref, o16_ref, l1_ref, l4_ref, l16_ref, p_ref, halo_ref,
                poolw_ref, pscale_ref, wout_ref, g1_ref, b1_ref,
                wcq_ref, km_ref, vm_ref, wco_ref, g2_ref, b2_ref,
                wrh_ref, wrl_ref, shg_ref, shu_ref, shd_ref,
                x2_ref, base_ref, logit_ref, *, alpha, seq):
    tm = x_ref.shape[0]
    i = pl.program_id(0)
    tile_pos = (i * tm) % seq

    l1, l4, l16 = l1_ref[...], l4_ref[...], l16_ref[...]
    mx = jnp.maximum(jnp.maximum(l1, l4), l16)
    e1, e4, e16 = jnp.exp(l1 - mx), jnp.exp(l4 - mx), jnp.exp(l16 - mx)
    inv = 1.0 / (e1 + e4 + e16)
    hrow = lax.broadcasted_iota(I32, (LANES, ATTN_WIDTH), 0)
    hcol = lax.broadcasted_iota(I32, (LANES, ATTN_WIDTH), 1) // HEAD_DIM
    spread = jnp.where(hrow == hcol, 1.0, 0.0).astype(BF16)

    def widen(w):
        hi, lo = _split_bf16(w)
        return _dot(hi, spread) + _dot(lo, spread)

    def branch(o_ref):
        lo, hi = zip(*[_unpack_pair(o_ref[w]) for w in range(o_ref.shape[0])])
        return jnp.concatenate(lo + hi, axis=1)

    attn = (widen(e1 * inv) * branch(o1_ref) + widen(e4 * inv) * branch(o4_ref)
            + widen(e16 * inv) * branch(o16_ref))

    halo_rows = halo_ref.shape[0]
    halo = jnp.where(tile_pos > 0, halo_ref[...], 0.0)
    ext = jnp.concatenate([halo, p_ref[...]], axis=0)
    pos = (tile_pos + lax.broadcasted_iota(I32, (tm, 1), 0)).astype(F32)
    gd = ext.shape[1] // len(POOL_WINDOWS)
    mixed = []
    for g, w in enumerate(POOL_WINDOWS):
        eg = ext[:, g * gd:(g + 1) * gd]
        acc, span = eg, 1
        while span < w:
            acc = acc + pltpu.roll(acc, span, 0)
            span *= 2
        count = jnp.minimum(pos + 1.0, float(w))
        pooled = acc[halo_rows:, :] / count - eg[halo_rows:, :]
        mixed.append(_dot(pooled.astype(BF16), poolw_ref[g]) * pscale_ref[:, g * gd:(g + 1) * gd])
    cat = jnp.concatenate([attn.astype(BF16)] + [m.astype(BF16) for m in mixed], axis=1)
    x1 = _layer_norm(alpha * x_ref[...] + _dot(cat, wout_ref[...]), g1_ref[...], b1_ref[...])

    qc = _dot(x1.astype(BF16), wcq_ref[...])
    mhd = qc.shape[1] // MEM_HEADS
    heads = []
    for h in range(MEM_HEADS):
        sl = slice(h * mhd, (h + 1) * mhd)
        s = _dot_nt(qc[:, sl].astype(BF16), km_ref[:, sl]) * (mhd ** -0.5)
        m = jnp.max(s, axis=1, keepdims=True)
        p = jnp.exp(s - m)
        l = jnp.sum(p, axis=1, keepdims=True)
        heads.append((_dot(p.astype(BF16), vm_ref[:, sl]) / l).astype(BF16))
    oc = jnp.concatenate(heads, axis=1)
    x2 = _layer_norm(alpha * x1 + _dot(oc, wco_ref[...]), g2_ref[...], b2_ref[...])
    dh = x2.shape[1] // 2
    x2_ref[...] = _pack_pair(x2[:, :dh], x2[:, dh:])

    xh, xl = _split_bf16(x2)
    logit_ref[...] = _dot_nt(wrh_ref[...], xh) + _dot_nt(wrh_ref[...], xl) + _dot_nt(wrl_ref[...], xh)

    gate = _dot(xh, shg_ref[...])
    up = _dot(xh, shu_ref[...])
    hid = gate / (1.0 + jnp.exp(-gate)) * up
    base_ref[...] = alpha * x2 + _dot(hid.astype(BF16), shd_ref[...])


def _row_block(x2d, o1, o4, o16, l1, l4, l16, p, pool_w_bf, pool_scale, w_out_bf, g1, b1,
               w_cq_bf, kmem, vmem, w_co_bf, g2, b2, wr_hi, wr_lo, sh_g, sh_u, sh_d,
               *, alpha, batch, seq, mem_len):
    n, d = x2d.shape
    tm = ROW_TILE
    halo_rows = max(POOL_WINDOWS)
    steps_per_seq = seq // tm
    row = lambda w: pl.BlockSpec((tm, w), lambda i: (i, 0))
    full = lambda a: pl.BlockSpec(a.shape, lambda i: (0,) * a.ndim)
    halo = pl.BlockSpec((halo_rows, p.shape[1]), lambda i: (jnp.maximum(i * (tm // halo_rows) - 1, 0), 0))
    memspec = pl.BlockSpec((mem_len, kmem.shape[1]), lambda i: (i // steps_per_seq, 0))
    n_exp = wr_hi.shape[0]
    branch = pl.BlockSpec((None, o1.shape[1], tm, LANES), lambda i: (i // steps_per_seq, 0, i % steps_per_seq, 0))
    return pl.pallas_call(
        functools.partial(_row_kernel, alpha=alpha, seq=seq),
        grid=(n // tm,),
        in_specs=[row(d), branch, branch, branch, row(LANES), row(LANES), row(LANES),
                  row(p.shape[1]), halo,
                  full(pool_w_bf), full(pool_scale), full(w_out_bf), full(g1), full(b1),
                  full(w_cq_bf), memspec, memspec, full(w_co_bf), full(g2), full(b2),
                  full(wr_hi), full(wr_lo), full(sh_g), full(sh_u), full(sh_d)],
        out_specs=[row(d // 2), row(d), pl.BlockSpec((n_exp, tm), lambda i: (0, i))],
        out_shape=[jax.ShapeDtypeStruct((n, d // 2), U32), jax.ShapeDtypeStruct((n, d), F32),
                   jax.ShapeDtypeStruct((n_exp, n), F32)],
        compiler_params=_cparams(("arbitrary",)),
        name="row_block",
    )(x2d, o1, o4, o16, l1, l4, l16, p, p, pool_w_bf, pool_scale, w_out_bf, g1, b1,
      w_cq_bf, kmem, vmem, w_co_bf, g2, b2, wr_hi, wr_lo, sh_g, sh_u, sh_d)


def _beat_counts(vals, n_rows):
    tn = vals.shape[1]
    n_tiles = n_rows // SUBLANES
    tiles = [vals[t * SUBLANES:(t + 1) * SUBLANES, :] for t in range(n_tiles)]
    sub = lax.broadcasted_iota(I32, (SUBLANES, tn), 0)
    counts = [jnp.zeros((SUBLANES, tn), F32) for _ in range(n_tiles)]
    for e in range(n_rows):
        te, je = divmod(e, SUBLANES)
        row = jnp.broadcast_to(vals[e:e + 1, :], (SUBLANES, tn))
        for t in range(n_tiles):
            strict = jnp.where(row > tiles[t], 1.0, 0.0)
            loose = jnp.where(row >= tiles[t], 1.0, 0.0)
            if t < te:
                beat = strict
            elif t > te:
                beat = loose
            else:
                beat = jnp.where(sub > je, loose, strict)
            counts[t] = counts[t] + beat
    return jnp.concatenate(counts, axis=0)


def _route_kernel(logit_ref, bias_ref, phi_ref, plo_ref, phit_ref, plot_ref, gatet_ref, tab_ref, count_ref,
                  carry_ref, rcarry_ref, *, per_group):
    n_exp, tn = logit_ref.shape
    step = pl.program_id(0)

    @pl.when(step == 0)
    def _():
        carry_ref[...] = jnp.zeros_like(carry_ref)
        rcarry_ref[...] = jnp.zeros_like(rcarry_ref)

    scores = 1.0 / (1.0 + jnp.exp(-logit_ref[...]))
    biased = scores + bias_ref[:, 0:1]
    n_groups = n_exp // per_group

    gscore = []
    for g in range(n_groups):
        tile = biased[g * per_group:(g + 1) * per_group, :]
        inner = _beat_counts(tile, per_group)
        gscore.append(jnp.sum(jnp.where(inner < 2.0, tile, 0.0), axis=0, keepdims=True))
    gscore = jnp.concatenate(gscore, axis=0)
    grank = _beat_counts(gscore, n_groups)
    grank = jnp.concatenate(
        [jnp.broadcast_to(grank[g:g + 1, :], (per_group, tn)) for g in range(n_groups)], axis=0)
    masked = jnp.where(grank < float(TOPK_GROUPS), biased, NEG_INF)
    self32 = jnp.where(_beat_counts(masked, n_exp) < float(TOP_K), 1.0, 0.0)
    selbf = self32.astype(BF16)

    er = lax.broadcasted_iota(I32, (n_exp, n_exp), 0)
    ec = lax.broadcasted_iota(I32, (n_exp, n_exp), 1)
    below = jnp.where(ec < er, 1.0, 0.0).astype(BF16)
    tr = lax.broadcasted_iota(I32, (tn, tn), 0)
    tc = lax.broadcasted_iota(I32, (tn, tn), 1)
    before = jnp.where(tr < tc, 1.0, 0.0).astype(BF16)
    lrank = _dot(selbf, before)
    align = float(RUN_ALIGN)
    lcount_col = jnp.broadcast_to(jnp.sum(self32, axis=1, keepdims=True), (n_exp, LANES))
    sel_wide = jnp.concatenate([selbf, jnp.zeros((LANES - n_exp, tn), BF16)], axis=0)
    lcount_row = _dot_nt(jnp.ones((SUBLANES, tn), BF16), sel_wide)
    lcount_col = jnp.ceil(lcount_col / align) * align
    lcount_row = jnp.ceil(lcount_row / align) * align
    lstart_col = _dot(below, lcount_col.astype(BF16))
    wr = lax.broadcasted_iota(I32, (LANES, LANES), 0)
    wc = lax.broadcasted_iota(I32, (LANES, LANES), 1)
    lstart_row = _dot(lcount_row.astype(BF16), jnp.where(wr < wc, 1.0, 0.0).astype(BF16))
    lpos = lstart_col[:, 0:1] + lrank

    sub = lax.broadcasted_iota(I32, (SUBLANES, LANES), 0)
    tab_ref[...] = jnp.where(sub == 0, lstart_row, jnp.where(sub == 1, lcount_row, jnp.where(
        sub == 2, rcarry_ref[...], 0.0))).astype(I32)
    rcarry_ref[...] = rcarry_ref[...] + lcount_row
    carry_ref[...] = carry_ref[...] + lcount_col
    count_ref[...] = carry_ref[...]

    base = float(POS_BASE)
    digit_hi = jnp.floor(lpos / base)
    digit_lo = lpos - digit_hi * base
    denom = jnp.sum(self32 * scores, axis=0, keepdims=True)
    gate = self32 * scores / denom * ROUTED_SCALE
    pad = jnp.zeros((LANES - n_exp, tn), F32)
    widen = lambda v: jnp.concatenate([v, pad], axis=0)
    tabs = [widen(jnp.where(self32 > 0.0, digit_hi, -1.0)), widen(jnp.where(self32 > 0.0, digit_lo, -1.0)), widen(gate)]
    for ref, ref_t, v in zip((phi_ref, plo_ref, None), (phit_ref, plot_ref, gatet_ref), tabs):
        if ref is not None:
            ref[...] = v.astype(BF16)
        ref_t[...] = v.T.astype(BF16)


def _route(logits_t, bias_col):
    n_exp, n = logits_t.shape
    tn = SORT_TILE
    emaj = jax.ShapeDtypeStruct((LANES, n), BF16)
    tmaj = jax.ShapeDtypeStruct((n, LANES), BF16)
    emaj_spec = pl.BlockSpec((LANES, tn), lambda i: (0, i))
    tmaj_spec = pl.BlockSpec((tn, LANES), lambda i: (i, 0))
    return pl.pallas_call(
        functools.partial(_route_kernel, per_group=n_exp // N_EXPERT_GROUPS),
        grid=(n // tn,),
        in_specs=[pl.BlockSpec((n_exp, tn), lambda i: (0, i)), pl.BlockSpec(bias_col.shape, lambda i: (0, 0))],
        out_specs=[emaj_spec, emaj_spec, tmaj_spec, tmaj_spec, tmaj_spec,
                   pl.BlockSpec((None, SUBLANES, LANES), lambda i: (i, 0, 0)),
                   pl.BlockSpec((n_exp, LANES), lambda i: (0, 0))],
        out_shape=[emaj, emaj, tmaj, tmaj, tmaj, jax.ShapeDtypeStruct((n // tn, SUBLANES, LANES), I32),
                   jax.ShapeDtypeStruct((n_exp, LANES), F32)],
        scratch_shapes=[pltpu.VMEM((n_exp, LANES), F32), pltpu.VMEM((SUBLANES, LANES), F32)],
        compiler_params=_cparams(("arbitrary",)),
        name="route",
    )(logits_t, bias_col)


def _place_kernel(count_ref, tab_ref, runs_ref, bexp_ref, seg_ref, *, n_blocks):
    n_exp = count_ref.shape[0]
    blk = float(EXPERT_BLOCK)
    counts = count_ref[...]
    padded = jnp.ceil(counts / blk) * blk
    ridx = lax.broadcasted_iota(I32, (n_exp, LANES), 0)
    ends = padded
    shift = 1
    while shift < n_exp:
        ends = ends + jnp.where(ridx >= shift, pltpu.roll(ends, shift, 0), 0.0)
        shift *= 2
    starts = ends - padded

    tab = tab_ref[...]
    last = tab[tab.shape[0] - 1]
    total_row = (last[2:3, :] + last[1:2, :]).astype(F32)
    padded_row = jnp.ceil(total_row / blk) * blk
    lidx = lax.broadcasted_iota(I32, (1, LANES), 1)
    ends_row = padded_row
    shift = 1
    while shift < LANES:
        ends_row = ends_row + jnp.where(lidx >= shift, pltpu.roll(ends_row, shift, 1), 0.0)
        shift *= 2
    starts_row = (ends_row - padded_row).astype(I32)
    sub = lax.broadcasted_iota(I32, tab.shape, 1)
    runs_ref[...] = tab + jnp.where(sub == 2, starts_row[None], 0)

    nbp = bexp_ref.shape[1]
    blk_start = (lax.broadcasted_iota(I32, (n_exp, nbp), 1) * EXPERT_BLOCK).astype(F32)
    passed = jnp.where(jnp.broadcast_to(ends[:, 0:1], (n_exp, nbp)) <= blk_start, 1.0, 0.0)
    bexp = jnp.minimum(jnp.sum(passed, axis=0, keepdims=True), float(n_exp - 1))
    used = ends[n_exp - 1:n_exp, 0:1] / blk
    lane = lax.broadcasted_iota(I32, (1, nbp), 1)
    bexp_ref[...] = jnp.where(lane == n_blocks, used, bexp).astype(I32)
    lane2 = lax.broadcasted_iota(I32, (n_exp, LANES), 1)
    seg_ref[...] = jnp.where(lane2 == 0, starts + counts, ends).astype(I32)


def _place(counts, tab, n_blocks):
    n_exp = counts.shape[0]
    nbp = -(-(n_blocks + 1) // LANES) * LANES
    return pl.pallas_call(
        functools.partial(_place_kernel, n_blocks=n_blocks),
        out_shape=[jax.ShapeDtypeStruct(tab.shape, I32), jax.ShapeDtypeStruct((1, nbp), I32),
                   jax.ShapeDtypeStruct((n_exp, LANES), I32)],
        compiler_params=_cparams(None),
        name="route_place",
    )(counts, tab)


def _copy_rows(src_ref, dst_ref, sem, src0, dst0, n, wait=False):
    def piece(off, p):
        src = src_ref.at[pl.ds(pl.multiple_of(src0 + off, RUN_ALIGN), p), :]
        dst = dst_ref.at[pl.ds(pl.multiple_of(dst0 + off, RUN_ALIGN), p), :]
        cp = pltpu.make_async_copy(src, dst, sem)
        if wait:
            cp.wait()
        else:
            cp.start()

    def chunk(c, carry):
        piece(c * RUN_CHUNK, RUN_CHUNK)
        return carry
    lax.fori_loop(0, lax.shift_right_logical(n, RUN_CHUNK.bit_length() - 1), chunk, 0)
    p = RUN_CHUNK // 2
    while p >= RUN_ALIGN:
        @pl.when((n & p) != 0)
        def _(p=p):
            piece(n & (-2 * p), p)
        p //= 2


def _start_run(src_ref, dst_ref, sem, src0, dst0, n, limit, enable=None):
    p = RUN_ALIGN
    while p * 2 <= limit:
        p *= 2
    while p >= RUN_ALIGN:
        hit = (n & p) != 0
        @pl.when(hit if enable is None else hit & enable)
        def _(p=p):
            off = n & (-2 * p)
            src = src_ref.at[pl.ds(pl.multiple_of(src0 + off, RUN_ALIGN), p), :]
            dst = dst_ref.at[pl.ds(pl.multiple_of(dst0 + off, RUN_ALIGN), p), :]
            pltpu.make_async_copy(src, dst, sem).start()
        p //= 2


def _run_membership(tab, rows):
    j = lax.broadcasted_iota(I32, (rows, LANES), 0)
    start = tab[0:1, :]
    end = start + tab[1:2, :]
    return jnp.where(j >= start, jnp.where(j < end, 1.0, 0.0), 0.0).astype(BF16)


def _dispatch_kernel(runs_ref, seg_ref, tab_ref, phi_ref, plo_ref, x_ref, xs_ref, buf_ref, zero_ref, sems, zsem,
                     *, n_exp):
    step, n_steps = pl.program_id(0), pl.num_programs(0)
    tm = x_ref.shape[0]
    rows = buf_ref.shape[1]
    slot = step % 2

    @pl.when(step == 0)
    def _():
        zero_ref[...] = jnp.zeros_like(zero_ref)
        for wait in (False, True):
            def per_expert(e, c, wait=wait):
                first = seg_ref[e, 0]
                _copy_rows(zero_ref, xs_ref, zsem, 0, first, seg_ref[e, 1] - first, wait)
                return c
            lax.fori_loop(0, n_exp, per_expert, 0)

    lo, hi = _unpack_pair(x_ref[...])
    x = jnp.concatenate([lo.astype(BF16), hi.astype(BF16)], axis=1)
    member = _run_membership(tab_ref[...], rows)
    pos = _dot(member, phi_ref[...]) * float(POS_BASE) + _dot(member, plo_ref[...])
    j = lax.broadcasted_iota(I32, (rows, tm), 0).astype(F32)
    onehot = jnp.where(pos == j, 1.0, 0.0)
    srt = _dot(onehot.astype(BF16), x)
    dh = srt.shape[1] // 2
    bits = lambda v: lax.bitcast_convert_type(v, U32)
    buf_ref[slot] = (bits(srt[:, :dh]) >> 16) | bits(srt[:, dh:])

    for e in range(n_exp):
        _start_run(buf_ref.at[slot], xs_ref, sems.at[slot], runs_ref[0, e], runs_ref[2, e], runs_ref[1, e], tm)
    used = runs_ref[0, n_exp - 1] + runs_ref[1, n_exp - 1]
    spare = xs_ref.shape[0] - 2 * (rows - TOP_K * tm) + slot * (rows - TOP_K * tm)
    _start_run(buf_ref.at[slot], xs_ref, sems.at[slot], used, spare, rows - used, rows - TOP_K * tm)

    def drain(s):
        pltpu.make_async_copy(buf_ref.at[s], xs_ref.at[pl.ds(0, rows), :], sems.at[s]).wait()

    @pl.when(step > 0)
    def _():
        drain(1 - slot)

    @pl.when(step == n_steps - 1)
    def _():
        drain(slot)


def _sorted_rows(tm, n_exp):
    return TOP_K * tm + n_exp * RUN_ALIGN


def _dispatch(x2, pos_hi, pos_lo, runs, seg, n_rows, n_exp):
    n, d = x2.shape
    tm = SORT_TILE
    rows = _sorted_rows(tm, n_exp)
    tab = (None,) + runs.shape[1:]
    return pl.pallas_call(
        functools.partial(_dispatch_kernel, n_exp=n_exp),
        grid=(n // tm,),
        in_specs=[pl.BlockSpec(tab, lambda i: (i, 0, 0), memory_space=pltpu.SMEM),
                  pl.BlockSpec(seg.shape, lambda i: (0, 0), memory_space=pltpu.SMEM),
                  pl.BlockSpec(tab, lambda i: (i, 0, 0)),
                  pl.BlockSpec((LANES, tm), lambda i: (0, i)),
                  pl.BlockSpec((LANES, tm), lambda i: (0, i)),
                  pl.BlockSpec((tm, d), lambda i: (i, 0))],
        out_specs=pl.BlockSpec(memory_space=pl.ANY),
        out_shape=jax.ShapeDtypeStruct((n_rows + 2 * (rows - TOP_K * tm), d), x2.dtype),
        scratch_shapes=[pltpu.VMEM((2, rows, d), x2.dtype), pltpu.VMEM((EXPERT_BLOCK, d), x2.dtype),
                        pltpu.SemaphoreType.DMA((2,)), pltpu.SemaphoreType.DMA],
        compiler_params=_cparams(("arbitrary",)),
        name="dispatch",
    )(runs, seg, runs, pos_hi, pos_lo, x2)


def _expert_kernel(bexp_ref, xs_ref, wg_ref, wu_ref, wd_ref, ys_ref, wg_bf, wu_bf, wd_bf, *, n_blocks):
    i = pl.program_id(0)
    used = bexp_ref[n_blocks]

    @pl.when(i < used)
    def _():
        @pl.when((i == 0) | (bexp_ref[i] != bexp_ref[jnp.maximum(i - 1, 0)]))
        def _():
            wg_bf[...] = wg_ref[...].astype(BF16)
            wu_bf[...] = wu_ref[...].astype(BF16)
            wd_bf[...] = wd_ref[...].astype(BF16)

        lo, hi = _unpack_pair(xs_ref[...])
        x = jnp.concatenate([lo.astype(BF16), hi.astype(BF16)], axis=1)
        gate = _dot(x, wg_bf[...])
        up = _dot(x, wu_bf[...])
        hid = gate / (1.0 + jnp.exp(-gate)) * up
        y = _dot(hid.astype(BF16), wd_bf[...])
        dh = y.shape[1] // 2
        ys_ref[...] = _pack_pair(y[:, :dh], y[:, dh:])


def _experts(bexp, xs, exp_gate, exp_up, exp_down, n_blocks):
    n_rows = xs.shape[0]
    d, hidden = exp_gate.shape[1:]
    blk = EXPERT_BLOCK

    def rows(i, b):
        return (jnp.minimum(i, jnp.maximum(b[n_blocks] - 1, 0)), 0)

    def weight(i, b):
        return (b[jnp.minimum(i, jnp.maximum(b[n_blocks] - 1, 0))], 0, 0)

    grid_spec = pltpu.PrefetchScalarGridSpec(
        num_scalar_prefetch=1,
        grid=(n_blocks,),
        in_specs=[pl.BlockSpec((blk, d // 2), rows),
                  pl.BlockSpec((None, d, hidden), weight),
                  pl.BlockSpec((None, d, hidden), weight),
                  pl.BlockSpec((None, hidden, d), weight)],
        out_specs=pl.BlockSpec((blk, d // 2), rows),
        scratch_shapes=[pltpu.VMEM((d, hidden), BF16), pltpu.VMEM((d, hidden), BF16), pltpu.VMEM((hidden, d), BF16)],
    )
    return pl.pallas_call(
        functools.partial(_expert_kernel, n_blocks=n_blocks),
        grid_spec=grid_spec,
        out_shape=jax.ShapeDtypeStruct((n_rows, d // 2), U32),
        compiler_params=_cparams(("arbitrary",)),
        name="experts",
    )(bexp, xs, exp_gate, exp_up, exp_down)


def _combine_kernel(runs_ref, next_runs_ref, tab_ref, phit_ref, plot_ref, gatet_ref, base_ref, g3_ref, b3_ref,
                    ys_ref, out_ref, buf_ref, sems, *, n_exp):
    step, n_steps = pl.program_id(0), pl.num_programs(0)
    tm = base_ref.shape[0]
    rows = buf_ref.shape[1]
    slot = step % 2

    def fetch(tab_ref, s, enable=None):
        for e in range(n_exp):
            _start_run(ys_ref, buf_ref.at[s], sems.at[s], tab_ref[2, e], tab_ref[0, e], tab_ref[1, e], tm, enable)
        used = tab_ref[0, n_exp - 1] + tab_ref[1, n_exp - 1]
        _start_run(ys_ref, buf_ref.at[s], sems.at[s], 0, used, rows - used, rows - TOP_K * tm, enable)

    @pl.when(step == 0)
    def _():
        fetch(runs_ref, slot)

    fetch(next_runs_ref, 1 - slot, step + 1 < n_steps)

    member = _run_membership(tab_ref[...], rows)
    pos = _dot_nt(phit_ref[...], member) * float(POS_BASE) + _dot_nt(plot_ref[...], member)
    j = lax.broadcasted_iota(I32, (tm, rows), 1).astype(F32)
    w = jnp.where(pos == j, _dot_nt(gatet_ref[...], member), 0.0)

    pltpu.make_async_copy(ys_ref.at[pl.ds(0, rows), :], buf_ref.at[slot], sems.at[slot]).wait()
    lo, hi = _unpack_pair(buf_ref[slot])
    y = jnp.concatenate([lo.astype(BF16), hi.astype(BF16)], axis=1)
    routed = _dot(w.astype(BF16), y)
    out_ref[...] = _layer_norm(base_ref[...] + routed, g3_ref[...], b3_ref[...])


def _combine(runs, pos_hi_t, pos_lo_t, gate_t, base, g3, b3, ys, n_exp):
    n, d = base.shape
    tm = SORT_TILE
    last = n // tm - 1
    tab = lambda f: pl.BlockSpec((None,) + runs.shape[1:], f, memory_space=pltpu.SMEM)
    tmaj = pl.BlockSpec((tm, LANES), lambda i: (i, 0))
    return pl.pallas_call(
        functools.partial(_combine_kernel, n_exp=n_exp),
        grid=(n // tm,),
        in_specs=[tab(lambda i: (i, 0, 0)), tab(lambda i: (jnp.minimum(i + 1, last), 0, 0)),
                  pl.BlockSpec((None,) + runs.shape[1:], lambda i: (i, 0, 0)),
                  tmaj, tmaj, tmaj,
                  pl.BlockSpec((tm, d), lambda i: (i, 0)),
                  pl.BlockSpec(g3.shape, lambda i: (0, 0)),
                  pl.BlockSpec(b3.shape, lambda i: (0, 0)),
                  pl.BlockSpec(memory_space=pl.ANY)],
        out_specs=pl.BlockSpec((tm, d), lambda i: (i, 0)),
        out_shape=jax.ShapeDtypeStruct((n, d), F32),
        scratch_shapes=[pltpu.VMEM((2, _sorted_rows(tm, n_exp), ys.shape[1]), ys.dtype),
                        pltpu.SemaphoreType.DMA((2,))],
        compiler_params=_cparams(("arbitrary",)),
        name="combine",
    )(runs, runs, runs, pos_hi_t, pos_lo_t, gate_t, base, g3, b3, ys)


def _layer(x2d, mem2d, w_in, pool_w, pool_scale, w_out, ln1_g, ln1_b, w_cq, w_ck, w_cv, w_co, ln2_g, ln2_b,
           w_router, router_bias, exp_gate, exp_up, exp_down, sh_gate, sh_up, sh_down, ln3_g, ln3_b,
           *, alpha, batch, seq, mem_len):
    n, d = x2d.shape
    n_exp = w_router.shape[1]
    row = lambda a: a.reshape(1, -1)
    bf = lambda a: a.astype(BF16)

    dils = tuple(dil for _, dil in DILATED_PATTERNS)
    *qkv, p = _in_proj(x2d, bf(w_in), dils)
    branches = []
    for di, (window, dil) in enumerate(DILATED_PATTERNS):
        assert window // dil == ATTN_BLOCK and seq % (dil * ATTN_BLOCK * ATTN_STEP_BLOCKS) == 0
        assert IN_TILE % (dil * 2 * SUBLANES) == 0
        branches.append(_dil_attn(*qkv[3 * di:3 * di + 3], batch, seq, dil))
    (o1, l1), (o4, l4), (o16, l16) = branches
    kmem, vmem = _mem_kv(mem2d, bf(w_ck), bf(w_cv))

    wr_t = w_router.T
    wr_hi = bf(wr_t)
    wr_lo = bf(wr_t - wr_hi.astype(F32))
    x2, base, logits_t = _row_block(
        x2d, o1, o4, o16, l1, l4, l16, p, bf(pool_w), row(pool_scale), bf(w_out), row(ln1_g), row(ln1_b),
        bf(w_cq), kmem, vmem, bf(w_co), row(ln2_g), row(ln2_b), wr_hi, wr_lo, bf(sh_gate), bf(sh_up), bf(sh_down),
        alpha=alpha, batch=batch, seq=seq, mem_len=mem_len)

    bias_col = jnp.broadcast_to(router_bias.reshape(n_exp, 1), (n_exp, LANES))
    pos_hi, pos_lo, pos_hi_t, pos_lo_t, gate_t, tab, counts = _route(logits_t, bias_col)
    max_rows = n * TOP_K + (n // SORT_TILE) * n_exp * (RUN_ALIGN - 1)
    n_blocks = -(-max_rows // EXPERT_BLOCK) + n_exp
    runs, bexp, seg = _place(counts, tab, n_blocks)
    xs = _dispatch(x2, pos_hi, pos_lo, runs, seg, n_blocks * EXPERT_BLOCK, n_exp)
    ys = _experts(bexp.reshape(-1), xs, exp_gate, exp_up, exp_down, n_blocks)
    return _combine(runs, pos_hi_t, pos_lo_t, gate_t, base, row(ln3_g), row(ln3_b), ys, n_exp)


def kernel(x, mem, w_in, pool_w, pool_scale, w_out, ln1_g, ln1_b, w_cq, w_ck, w_cv, w_co, ln2_g, ln2_b, w_router, router_bias, exp_gate, exp_up, exp_down, sh_gate, sh_up, sh_down, ln3_g, ln3_b):
    batch, seq, d = x.shape
    mem_len = mem.shape[1]
    depth = w_in.shape[0]
    alpha = (2.0 * depth) ** 0.25
    x2d = x.reshape(batch * seq, d)
    mem2d = mem.reshape(batch * mem_len, d)
    for l in range(depth):
        x2d = _layer(x2d, mem2d, w_in[l], pool_w[l], pool_scale[l], w_out[l], ln1_g[l], ln1_b[l], w_cq[l], w_ck[l],
                     w_cv[l], w_co[l], ln2_g[l], ln2_b[l], w_router[l], router_bias[l], exp_gate[l], exp_up[l],
                     exp_down[l], sh_gate[l], sh_up[l], sh_down[l], ln3_g[l], ln3_b[l],
                     alpha=alpha, batch=batch, seq=seq, mem_len=mem_len)
    return x2d.reshape(batch, seq, d)
```

```python
import functools

import jax
import jax.numpy as jnp
from jax import lax
from jax.experimental import pallas as pl
from jax.experimental.pallas import tpu as pltpu

F32 = jnp.float32
BF16 = jnp.bfloat16
I32 = jnp.int32
U32 = jnp.uint32

ATTN_HEADS = 8
HEAD_DIM = 64
ATTN_WIDTH = ATTN_HEADS * HEAD_DIM
DILATED_PATTERNS = ((128, 1), (512, 4), (2048, 16))
POOL_WINDOWS = (2, 4, 8, 16)
MEM_HEADS = 4
N_EXPERT_GROUPS = 8
TOPK_GROUPS = 4
TOP_K = 8
ROUTED_SCALE = 2.5
LN_EPS = 1e-5
NEG_INF = -1e30

LANES = 128
SUBLANES = 8
VMEM_LIMIT = 56 * 1024 * 1024

ATTN_BLOCK = 128
ATTN_STEP_BLOCKS = 4
ROW_TILE = 512
IN_TILE = 512
SORT_TILE = 256
EXPERT_BLOCK = 512
EXPERT_RING = 3
RUN_ALIGN = SUBLANES
RUN_CHUNK = 64
POS_BASE = 64


def _cparams(sem, vmem=VMEM_LIMIT):
    return pltpu.CompilerParams(dimension_semantics=sem, vmem_limit_bytes=vmem)


def _layer_norm(y, g, b):
    mu = jnp.mean(y, axis=-1, keepdims=True)
    d = y - mu
    var = jnp.mean(d * d, axis=-1, keepdims=True)
    return d * lax.rsqrt(var + LN_EPS) * g + b


def _dot(a, b):
    return jnp.dot(a, b, preferred_element_type=F32)


def _dot_nt(a, b):
    return lax.dot_general(a, b, (((1,), (1,)), ((), ())), preferred_element_type=F32)


def _split_bf16(x):
    hi = x.astype(BF16)
    lo = (x - hi.astype(F32)).astype(BF16)
    return hi, lo


def _bf16_bits(x):
    return lax.bitcast_convert_type(x.astype(BF16).astype(F32), U32)


def _pack_pair(lo, hi):
    return (_bf16_bits(lo) >> 16) | (_bf16_bits(hi) & jnp.uint32(0xFFFF0000))


def _unpack_pair(w):
    lo = lax.bitcast_convert_type(w << 16, F32)
    hi = lax.bitcast_convert_type(w & jnp.uint32(0xFFFF0000), F32)
    return lo, hi


def _in_proj_kernel(x_ref, w_ref, *refs, dils):
    n_qkv = 3 * len(dils)
    out_refs, p_ref, slab_ref = refs[:n_qkv], refs[n_qkv], refs[n_qkv + 1]
    tm = x_ref.shape[0]
    x = x_ref[...].astype(BF16)
    aw = ATTN_WIDTH
    n_slabs = aw // LANES
    for a in range(3):
        val = _dot(x, w_ref[:, a * aw:(a + 1) * aw])
        if a == 0:
            val = val * (HEAD_DIM ** -0.5)
        for s in range(n_slabs):
            slab_ref[s] = val[:, s * LANES:(s + 1) * LANES]
        for di, dil in enumerate(dils):
            dst = out_refs[3 * di + a]
            if dil == 1:
                dst[...] = val.astype(BF16)
                continue
            for r in range(dil):
                for s in range(n_slabs):
                    rows = slab_ref[s, pl.ds(r, tm // dil, stride=dil), :]
                    dst[:, r * aw + s * LANES:r * aw + (s + 1) * LANES] = rows.astype(BF16)
    p_ref[...] = _dot(x, w_ref[:, 3 * aw:])


def _in_proj(x2d, w_in_bf, dils):
    n, d = x2d.shape
    aw = ATTN_WIDTH
    pw = w_in_bf.shape[1] - 3 * aw
    tm = IN_TILE
    row = lambda rows, w: pl.BlockSpec((rows, w), lambda i: (i, 0))
    qkv_specs, qkv_shapes = [], []
    for dil in dils:
        qkv_specs += [row(tm // dil, dil * aw)] * 3
        qkv_shapes += [jax.ShapeDtypeStruct((n // dil, dil * aw), BF16)] * 3
    return pl.pallas_call(
        functools.partial(_in_proj_kernel, dils=dils),
        grid=(n // tm,),
        in_specs=[row(tm, d), pl.BlockSpec(w_in_bf.shape, lambda i: (0, 0))],
        out_specs=qkv_specs + [row(tm, pw)],
        out_shape=qkv_shapes + [jax.ShapeDtypeStruct((n, pw), F32)],
        scratch_shapes=[pltpu.VMEM((aw // LANES, tm, LANES), F32)],
        compiler_params=_cparams(("arbitrary",)),
        name="in_proj",
    )(x2d, w_in_bf)


def _dil_attn_kernel(q_ref, kc_ref, kp_ref, vc_ref, vp_ref, o_ref, lse_ref, bias_ref, *, dil):
    blk = ATTN_BLOCK
    first_step = (pl.program_id(0) == 0) & (pl.program_id(1) == 0) & (pl.program_id(2) == 0)

    @pl.when(first_step)
    def _():
        qi = lax.broadcasted_iota(I32, (blk, 2 * blk), 0)
        kj = lax.broadcasted_iota(I32, (blk, 2 * blk), 1)
        dist = qi + blk - kj
        inside = (dist >= 0) & (dist <= blk)
        penalty = (dist * dil).astype(F32)
        for h in range(ATTN_HEADS):
            slope = 2.0 ** (-8.0 * (h + 1) / ATTN_HEADS)
            bias_ref[0, h] = jnp.where(inside, -slope * penalty, NEG_INF)
            bias_ref[1, h] = jnp.where(inside & (kj >= blk), -slope * penalty, NEG_INF)

    i = pl.program_id(1)
    r = pl.program_id(2)
    lane = lax.broadcasted_iota(I32, (blk, LANES), 1)
    low_half = lane < HEAD_DIM
    n_pairs = ATTN_WIDTH // LANES
    half = n_pairs // 2
    for j in range(ATTN_STEP_BLOCKS):
        table = jnp.where(i == 0, 1, 0) if j == 0 else 0
        lse_tile = jnp.zeros((blk, LANES), F32)
        pair_out = []
        for hp in range(n_pairs):
            sl = slice(hp * LANES, (hp + 1) * LANES)
            q2 = q_ref[j * blk:(j + 1) * blk, sl]
            if j == 0:
                kcat = jnp.concatenate([kp_ref[:, sl], kc_ref[0:blk, sl]], axis=0)
                vcat = jnp.concatenate([vp_ref[:, sl], vc_ref[0:blk, sl]], axis=0)
            else:
                kcat = kc_ref[(j - 1) * blk:(j + 1) * blk, sl]
                vcat = vc_ref[(j - 1) * blk:(j + 1) * blk, sl]
            outs = []
            for e in range(LANES // HEAD_DIM):
                h = hp * (LANES // HEAD_DIM) + e
                keep = low_half if e == 0 else jnp.logical_not(low_half)
                qe = jnp.where(keep, q2, jnp.zeros_like(q2))
                s = _dot_nt(qe, kcat) + bias_ref[table, h]
                m = jnp.max(s, axis=1, keepdims=True)
                p = jnp.exp(s - m)
                l = jnp.sum(p, axis=1, keepdims=True)
                outs.append(_dot(p.astype(BF16), vcat) / l)
                lse_tile = jnp.where(lane == h, m + jnp.log(l), lse_tile)
            pair_out.append(jnp.where(low_half, outs[0], outs[1]))
        rows = pl.ds(j * blk * dil + r, blk, stride=dil) if dil > 1 else slice(j * blk, (j + 1) * blk)
        for w in range(half):
            o_ref[w, rows, :] = _pack_pair(pair_out[w], pair_out[w + half])
        lse_ref[rows, :] = lse_tile


def _dil_attn(q, k, v, batch, seq, dil):
    blk = ATTN_BLOCK
    step_rows = ATTN_STEP_BLOCKS * blk
    sub = seq // dil
    view = lambda a: a.reshape(batch, sub, dil * ATTN_WIDTH)
    cur = pl.BlockSpec((None, step_rows, ATTN_WIDTH), lambda b, i, r: (b, i, r))
    prev = pl.BlockSpec((None, blk, ATTN_WIDTH), lambda b, i, r: (b, jnp.maximum(i * ATTN_STEP_BLOCKS - 1, 0), r))
    half = ATTN_WIDTH // LANES // 2
    o, lse = pl.pallas_call(
        functools.partial(_dil_attn_kernel, dil=dil),
        grid=(batch, sub // step_rows, dil),
        in_specs=[cur, cur, prev, cur, prev],
        out_specs=[pl.BlockSpec((None, half, step_rows * dil, LANES), lambda b, i, r: (b, 0, i, 0)),
                   pl.BlockSpec((None, step_rows * dil, LANES), lambda b, i, r: (b, i, 0))],
        out_shape=[jax.ShapeDtypeStruct((batch, half, seq, LANES), U32),
                   jax.ShapeDtypeStruct((batch, seq, LANES), F32)],
        scratch_shapes=[pltpu.VMEM((2, ATTN_HEADS, blk, 2 * blk), F32)],
        compiler_params=_cparams(("arbitrary",) * 3),
        name=f"dil_attn_d{dil}",
    )(view(q), view(k), view(k), view(v), view(v))
    return o, lse.reshape(batch * seq, LANES)


def _mem_kv_kernel(mem_ref, wk_ref, wv_ref, k_ref, v_ref):
    m = mem_ref[...].astype(BF16)
    k_ref[...] = _dot(m, wk_ref[...]).astype(BF16)
    v_ref[...] = _dot(m, wv_ref[...]).astype(BF16)


def _mem_kv(mem2d, w_ck_bf, w_cv_bf):
    rows = mem2d.shape[0]
    width = w_ck_bf.shape[1]
    return pl.pallas_call(
        _mem_kv_kernel,
        out_shape=[jax.ShapeDtypeStruct((rows, width), BF16)] * 2,
        compiler_params=_cparams(None),
        name="mem_kv",
    )(mem2d, w_ck_bf, w_cv_bf)


def _row_kernel(x_ref, o1_ref, o4_ref, o16_ref, l1_ref, l4_ref, l16_ref, p_ref, halo_ref,
                poolw_ref, pscale_ref, wout_ref, g1_ref, b1_ref,
                wcq_ref, km_ref, vm_ref, wco_ref, g2_ref, b2_ref,
                wrh_ref, wrl_ref, shg_ref, shu_ref, shd_ref,
                x2_ref, base_ref, logit_ref, *, alpha, seq):
    tm = x_ref.shape[0]
    i = pl.program_id(0)
    tile_pos = (i * tm) % seq

    l1, l4, l16 = l1_ref[...], l4_ref[...], l16_ref[...]
    mx = jnp.maximum(jnp.maximum(l1, l4), l16)
    e1, e4, e16 = jnp.exp(l1 - mx), jnp.exp(l4 - mx), jnp.exp(l16 - mx)
    inv = 1.0 / (e1 + e4 + e16)
    hrow = lax.broadcasted_iota(I32, (LANES, ATTN_WIDTH), 0)
    hcol = lax.broadcasted_iota(I32, (LANES, ATTN_WIDTH), 1) // HEAD_DIM
    spread = jnp.where(hrow == hcol, 1.0, 0.0).astype(BF16)

    def widen(w):
        hi, lo = _split_bf16(w)
        return _dot(hi, spread) + _dot(lo, spread)

    def branch(o_ref):
        lo, hi = zip(*[_unpack_pair(o_ref[w]) for w in range(o_ref.shape[0])])
        return jnp.concatenate(lo + hi, axis=1)

    attn = (widen(e1 * inv) * branch(o1_ref) + widen(e4 * inv) * branch(o4_ref)
            + widen(e16 * inv) * branch(o16_ref))

    halo_rows = halo_ref.shape[0]
    halo = jnp.where(tile_pos > 0, halo_ref[...], 0.0)
    ext = jnp.concatenate([halo, p_ref[...]], axis=0)
    pos = (tile_pos + lax.broadcasted_iota(I32, (tm, 1), 0)).astype(F32)
    gd = ext.shape[1] // len(POOL_WINDOWS)
    mixed = []
    for g, w in enumerate(POOL_WINDOWS):
        eg = ext[:, g * gd:(g + 1) * gd]
        acc, span = eg, 1
        while span < w:
            acc = acc + pltpu.roll(acc, span, 0)
            span *= 2
        count = jnp.minimum(pos + 1.0, float(w))
        pooled = acc[halo_rows:, :] / count - eg[halo_rows:, :]
        mixed.append(_dot(pooled.astype(BF16), poolw_ref[g]) * pscale_ref[:, g * gd:(g + 1) * gd])
    cat = jnp.concatenate([attn.astype(BF16)] + [m.astype(BF16) for m in mixed], axis=1)
    x1 = _layer_norm(alpha * x_ref[...] + _dot(cat, wout_ref[...]), g1_ref[...], b1_ref[...])

    qc = _dot(x1.astype(BF16), wcq_ref[...])
    mhd = qc.shape[1] // MEM_HEADS
    heads = []
    for h in range(MEM_HEADS):
        sl = slice(h * mhd, (h + 1) * mhd)
        s = _dot_nt(qc[:, sl].astype(BF16), km_ref[:, sl]) * (mhd ** -0.5)
        m = jnp.max(s, axis=1, keepdims=True)
        p = jnp.exp(s - m)
        l = jnp.sum(p, axis=1, keepdims=True)
        heads.append((_dot(p.astype(BF16), vm_ref[:, sl]) / l).astype(BF16))
    oc = jnp.concatenate(heads, axis=1)
    x2 = _layer_norm(alpha * x1 + _dot(oc, wco_ref[...]), g2_ref[...], b2_ref[...])
    dh = x2.shape[1] // 2
    x2_ref[...] = _pack_pair(x2[:, :dh], x2[:, dh:])

    xh, xl = _split_bf16(x2)
    logit_ref[...] = _dot_nt(wrh_ref[...], xh) + _dot_nt(wrh_ref[...], xl) + _dot_nt(wrl_ref[...], xh)

    gate = _dot(xh, shg_ref[...])
    up = _dot(xh, shu_ref[...])
    hid = gate / (1.0 + jnp.exp(-gate)) * up
    base_ref[...] = alpha * x2 + _dot(hid.astype(BF16), shd_ref[...])


def _row_block(x2d, o1, o4, o16, l1, l4, l16, p, pool_w_bf, pool_scale, w_out_bf, g1, b1,
               w_cq_bf, kmem, vmem, w_co_bf, g2, b2, wr_hi, wr_lo, sh_g, sh_u, sh_d,
               *, alpha, batch, seq, mem_len):
    n, d = x2d.shape
    tm = ROW_TILE
    halo_rows = max(POOL_WINDOWS)
    steps_per_seq = seq // tm
    row = lambda w: pl.BlockSpec((tm, w), lambda i: (i, 0))
    full = lambda a: pl.BlockSpec(a.shape, lambda i: (0,) * a.ndim)
    halo = pl.BlockSpec((halo_rows, p.shape[1]), lambda i: (jnp.maximum(i * (tm // halo_rows) - 1, 0), 0))
    memspec = pl.BlockSpec((mem_len, kmem.shape[1]), lambda i: (i // steps_per_seq, 0))
    n_exp = wr_hi.shape[0]
    branch = pl.BlockSpec((None, o1.shape[1], tm, LANES), lambda i: (i // steps_per_seq, 0, i % steps_per_seq, 0))
    return pl.pallas_call(
        functools.partial(_row_kernel, alpha=alpha, seq=seq),
        grid=(n // tm,),
        in_specs=[row(d), branch, branch, branch, row(LANES), row(LANES), row(LANES),
                  row(p.shape[1]), halo,
                  full(pool_w_bf), full(pool_scale), full(w_out_bf), full(g1), full(b1),
                  full(w_cq_bf), memspec, memspec, full(w_co_bf), full(g2), full(b2),
                  full(wr_hi), full(wr_lo), full(sh_g), full(sh_u), full(sh_d)],
        out_specs=[row(d // 2), row(d), pl.BlockSpec((n_exp, tm), lambda i: (0, i))],
        out_shape=[jax.ShapeDtypeStruct((n, d // 2), U32), jax.ShapeDtypeStruct((n, d), F32),
                   jax.ShapeDtypeStruct((n_exp, n), F32)],
        compiler_params=_cparams(("arbitrary",)),
        name="row_block",
    )(x2d, o1, o4, o16, l1, l4, l16, p, p, pool_w_bf, pool_scale, w_out_bf, g1, b1,
      w_cq_bf, kmem, vmem, w_co_bf, g2, b2, wr_hi, wr_lo, sh_g, sh_u, sh_d)


def _beat_counts(vals, n_rows):
    tn = vals.shape[1]
    n_tiles = n_rows // SUBLANES
    tiles = [vals[t * SUBLANES:(t + 1) * SUBLANES, :] for t in range(n_tiles)]
    sub = lax.broadcasted_iota(I32, (SUBLANES, tn), 0)
    counts = [jnp.zeros((SUBLANES, tn), F32) for _ in range(n_tiles)]
    for e in range(n_rows):
        te, je = divmod(e, SUBLANES)
        row = jnp.broadcast_to(vals[e:e + 1, :], (SUBLANES, tn))
        for t in range(n_tiles):
            strict = jnp.where(row > tiles[t], 1.0, 0.0)
            loose = jnp.where(row >= tiles[t], 1.0, 0.0)
            if t < te:
                beat = strict
            elif t > te:
                beat = loose
            else:
                beat = jnp.where(sub > je, loose, strict)
            counts[t] = counts[t] + beat
    return jnp.concatenate(counts, axis=0)


def _route_kernel(logit_ref, bias_ref, rank_ref, rankt_ref, gatet_ref, tab_ref, count_ref, carry_ref, rcarry_ref,
                  *, per_group):
    n_exp, tn = logit_ref.shape
    step = pl.program_id(0)

    @pl.when(step == 0)
    def _():
        carry_ref[...] = jnp.zeros_like(carry_ref)
        rcarry_ref[...] = jnp.zeros_like(rcarry_ref)

    scores = 1.0 / (1.0 + jnp.exp(-logit_ref[...]))
    biased = scores + bias_ref[:, 0:1]
    n_groups = n_exp // per_group

    gscore = []
    for g in range(n_groups):
        tile = biased[g * per_group:(g + 1) * per_group, :]
        inner = _beat_counts(tile, per_group)
        gscore.append(jnp.sum(jnp.where(inner < 2.0, tile, 0.0), axis=0, keepdims=True))
    gscore = jnp.concatenate(gscore, axis=0)
    grank = _beat_counts(gscore, n_groups)
    grank = jnp.concatenate(
        [jnp.broadcast_to(grank[g:g + 1, :], (per_group, tn)) for g in range(n_groups)], axis=0)
    masked = jnp.where(grank < float(TOPK_GROUPS), biased, NEG_INF)
    self32 = jnp.where(_beat_counts(masked, n_exp) < float(TOP_K), 1.0, 0.0)
    selbf = self32.astype(BF16)

    tr = lax.broadcasted_iota(I32, (tn, tn), 0)
    tc = lax.broadcasted_iota(I32, (tn, tn), 1)
    before = jnp.where(tr < tc, 1.0, 0.0).astype(BF16)
    lrank = _dot(selbf, before)
    align = float(RUN_ALIGN)
    lcount_col = jnp.broadcast_to(jnp.sum(self32, axis=1, keepdims=True), (n_exp, LANES))
    sel_wide = jnp.concatenate([selbf, jnp.zeros((LANES - n_exp, tn), BF16)], axis=0)
    lcount_row = _dot_nt(jnp.ones((SUBLANES, tn), BF16), sel_wide)
    lcount_col = jnp.ceil(lcount_col / align) * align
    lcount_row = jnp.ceil(lcount_row / align) * align
    wr = lax.broadcasted_iota(I32, (LANES, LANES), 0)
    wc = lax.broadcasted_iota(I32, (LANES, LANES), 1)
    lstart_row = _dot(lcount_row.astype(BF16), jnp.where(wr < wc, 1.0, 0.0).astype(BF16))

    sub = lax.broadcasted_iota(I32, (SUBLANES, LANES), 0)
    tab_ref[...] = jnp.where(sub == 0, lstart_row, jnp.where(sub == 1, lcount_row, jnp.where(
        sub == 2, rcarry_ref[...], 0.0))).astype(I32)
    rcarry_ref[...] = rcarry_ref[...] + lcount_row
    carry_ref[...] = carry_ref[...] + lcount_col
    count_ref[...] = carry_ref[...]

    denom = jnp.sum(self32 * scores, axis=0, keepdims=True)
    gate = self32 * scores / denom * ROUTED_SCALE
    pad = jnp.zeros((LANES - n_exp, tn), F32)
    widen = lambda v: jnp.concatenate([v, pad], axis=0)
    rank_tab = widen(jnp.where(self32 > 0.0, lrank, -1.0))
    rank_ref[...] = rank_tab.astype(BF16)
    rankt_ref[...] = rank_tab.T.astype(BF16)
    gatet_ref[...] = widen(gate).T.astype(BF16)


def _route(logits_t, bias_col):
    n_exp, n = logits_t.shape
    tn = SORT_TILE
    emaj = jax.ShapeDtypeStruct((LANES, n), BF16)
    tmaj = jax.ShapeDtypeStruct((n, LANES), BF16)
    emaj_spec = pl.BlockSpec((LANES, tn), lambda i: (0, i))
    tmaj_spec = pl.BlockSpec((tn, LANES), lambda i: (i, 0))
    return pl.pallas_call(
        functools.partial(_route_kernel, per_group=n_exp // N_EXPERT_GROUPS),
        grid=(n // tn,),
        in_specs=[pl.BlockSpec((n_exp, tn), lambda i: (0, i)), pl.BlockSpec(bias_col.shape, lambda i: (0, 0))],
        out_specs=[emaj_spec, tmaj_spec, tmaj_spec,
                   pl.BlockSpec((None, SUBLANES, LANES), lambda i: (i, 0, 0)),
                   pl.BlockSpec((n_exp, LANES), lambda i: (0, 0))],
        out_shape=[emaj, tmaj, tmaj, jax.ShapeDtypeStruct((n // tn, SUBLANES, LANES), I32),
                   jax.ShapeDtypeStruct((n_exp, LANES), F32)],
        scratch_shapes=[pltpu.VMEM((n_exp, LANES), F32), pltpu.VMEM((SUBLANES, LANES), F32)],
        compiler_params=_cparams(("arbitrary",)),
        name="route",
    )(logits_t, bias_col)


def _place_kernel(count_ref, tab_ref, runs_ref, bexp_ref, seg_ref, *, n_blocks):
    n_exp = count_ref.shape[0]
    blk = float(EXPERT_BLOCK)
    counts = count_ref[...]
    padded = jnp.ceil(counts / blk) * blk
    ridx = lax.broadcasted_iota(I32, (n_exp, LANES), 0)
    ends = padded
    shift = 1
    while shift < n_exp:
        ends = ends + jnp.where(ridx >= shift, pltpu.roll(ends, shift, 0), 0.0)
        shift *= 2
    starts = ends - padded

    tab = tab_ref[...]
    last = tab[tab.shape[0] - 1]
    total_row = (last[2:3, :] + last[1:2, :]).astype(F32)
    padded_row = jnp.ceil(total_row / blk) * blk
    lidx = lax.broadcasted_iota(I32, (1, LANES), 1)
    ends_row = padded_row
    shift = 1
    while shift < LANES:
        ends_row = ends_row + jnp.where(lidx >= shift, pltpu.roll(ends_row, shift, 1), 0.0)
        shift *= 2
    starts_row = (ends_row - padded_row).astype(I32)
    sub = lax.broadcasted_iota(I32, tab.shape, 1)
    runs_ref[...] = tab + jnp.where(sub == 2, starts_row[None], 0)

    nbp = bexp_ref.shape[1]
    blk_start = (lax.broadcasted_iota(I32, (n_exp, nbp), 1) * EXPERT_BLOCK).astype(F32)
    passed = jnp.where(jnp.broadcast_to(ends[:, 0:1], (n_exp, nbp)) <= blk_start, 1.0, 0.0)
    bexp = jnp.minimum(jnp.sum(passed, axis=0, keepdims=True), float(n_exp - 1))
    used = ends[n_exp - 1:n_exp, 0:1] / blk
    lane = lax.broadcasted_iota(I32, (1, nbp), 1)
    bexp_ref[...] = jnp.where(lane == n_blocks, used, bexp).astype(I32)
    lane2 = lax.broadcasted_iota(I32, (n_exp, LANES), 1)
    seg_ref[...] = jnp.where(lane2 == 0, starts + counts, ends).astype(I32)


def _place(counts, tab, n_blocks):
    n_exp = counts.shape[0]
    nbp = -(-(n_blocks + 1) // LANES) * LANES
    return pl.pallas_call(
        functools.partial(_place_kernel, n_blocks=n_blocks),
        out_shape=[jax.ShapeDtypeStruct(tab.shape, I32), jax.ShapeDtypeStruct((1, nbp), I32),
                   jax.ShapeDtypeStruct((n_exp, LANES), I32)],
        compiler_params=_cparams(None),
        name="route_place",
    )(counts, tab)


def _copy_rows(src_ref, dst_ref, sem, src0, dst0, n, wait=False):
    def piece(off, p):
        src = src_ref.at[pl.ds(pl.multiple_of(src0 + off, RUN_ALIGN), p), :]
        dst = dst_ref.at[pl.ds(pl.multiple_of(dst0 + off, RUN_ALIGN), p), :]
        cp = pltpu.make_async_copy(src, dst, sem)
        if wait:
            cp.wait()
        else:
            cp.start()

    def chunk(c, carry):
        piece(c * RUN_CHUNK, RUN_CHUNK)
        return carry
    lax.fori_loop(0, lax.shift_right_logical(n, RUN_CHUNK.bit_length() - 1), chunk, 0)
    p = RUN_CHUNK // 2
    while p >= RUN_ALIGN:
        @pl.when((n & p) != 0)
        def _(p=p):
            piece(n & (-2 * p), p)
        p //= 2


def _start_run(src_ref, dst_ref, sem, src0, dst0, n, limit, enable=None):
    p = RUN_ALIGN
    while p * 2 <= limit:
        p *= 2
    while p >= RUN_ALIGN:
        hit = (n & p) != 0
        @pl.when(hit if enable is None else hit & enable)
        def _(p=p):
            off = n & (-2 * p)
            src = src_ref.at[pl.ds(pl.multiple_of(src0 + off, RUN_ALIGN), p), :]
            dst = dst_ref.at[pl.ds(pl.multiple_of(dst0 + off, RUN_ALIGN), p), :]
            pltpu.make_async_copy(src, dst, sem).start()
        p //= 2


def _run_membership(tab, rows):
    j = lax.broadcasted_iota(I32, (rows, LANES), 0)
    start = tab[0:1, :]
    end = start + tab[1:2, :]
    return jnp.where(j >= start, jnp.where(j < end, 1.0, 0.0), 0.0)


def _dispatch_kernel(runs_ref, seg_ref, tab_ref, rank_ref, x_ref, xs_ref, buf_ref, zero_ref, sems, zsem, *, n_exp):
    step, n_steps = pl.program_id(0), pl.num_programs(0)
    tm = x_ref.shape[0]
    rows = buf_ref.shape[1]
    slot = step % 2

    @pl.when(step == 0)
    def _():
        zero_ref[...] = jnp.zeros_like(zero_ref)
        for wait in (False, True):
            def per_expert(e, c, wait=wait):
                first = seg_ref[e, 0]
                _copy_rows(zero_ref, xs_ref, zsem, 0, first, seg_ref[e, 1] - first, wait)
                return c
            lax.fori_loop(0, n_exp, per_expert, 0)

    lo, hi = _unpack_pair(x_ref[...])
    x = jnp.concatenate([lo.astype(BF16), hi.astype(BF16)], axis=1)
    tab = tab_ref[...]
    member = _run_membership(tab, rows)
    run_start = jnp.sum(member * tab[0:1, :].astype(F32), axis=1, keepdims=True)
    in_run = lax.broadcasted_iota(I32, (rows, 1), 0).astype(F32) - run_start
    onehot = jnp.where(_dot(member.astype(BF16), rank_ref[...]) == in_run, 1.0, 0.0)
    srt = _dot(onehot.astype(BF16), x)
    dh = srt.shape[1] // 2
    bits = lambda v: lax.bitcast_convert_type(v, U32)
    buf_ref[slot] = (bits(srt[:, :dh]) >> 16) | bits(srt[:, dh:])

    for e in range(n_exp):
        _start_run(buf_ref.at[slot], xs_ref, sems.at[slot], runs_ref[0, e], runs_ref[2, e], runs_ref[1, e], tm)
    used = runs_ref[0, n_exp - 1] + runs_ref[1, n_exp - 1]
    spare = xs_ref.shape[0] - 2 * (rows - TOP_K * tm) + slot * (rows - TOP_K * tm)
    _start_run(buf_ref.at[slot], xs_ref, sems.at[slot], used, spare, rows - used, rows - TOP_K * tm)

    def drain(s):
        pltpu.make_async_copy(buf_ref.at[s], xs_ref.at[pl.ds(0, rows), :], sems.at[s]).wait()

    @pl.when(step > 0)
    def _():
        drain(1 - slot)

    @pl.when(step == n_steps - 1)
    def _():
        drain(slot)


def _sorted_rows(tm, n_exp):
    return TOP_K * tm + n_exp * RUN_ALIGN


def _dispatch(x2, rank_tab, runs, seg, n_rows, n_exp):
    n, d = x2.shape
    tm = SORT_TILE
    rows = _sorted_rows(tm, n_exp)
    tab = (None,) + runs.shape[1:]
    return pl.pallas_call(
        functools.partial(_dispatch_kernel, n_exp=n_exp),
        grid=(n // tm,),
        in_specs=[pl.BlockSpec(tab, lambda i: (i, 0, 0), memory_space=pltpu.SMEM),
                  pl.BlockSpec(seg.shape, lambda i: (0, 0), memory_space=pltpu.SMEM),
                  pl.BlockSpec(tab, lambda i: (i, 0, 0)),
                  pl.BlockSpec((LANES, tm), lambda i: (0, i)),
                  pl.BlockSpec((tm, d), lambda i: (i, 0))],
        out_specs=pl.BlockSpec(memory_space=pl.ANY),
        out_shape=jax.ShapeDtypeStruct((n_rows + 2 * (rows - TOP_K * tm), d), x2.dtype),
        scratch_shapes=[pltpu.VMEM((2, rows, d), x2.dtype), pltpu.VMEM((EXPERT_BLOCK, d), x2.dtype),
                        pltpu.SemaphoreType.DMA((2,)), pltpu.SemaphoreType.DMA],
        compiler_params=_cparams(("arbitrary",)),
        name="dispatch",
    )(runs, seg, runs, rank_tab, x2)


def _expert_kernel(bexp_ref, xs_ref, wg_ref, wu_ref, wd_ref, ys_ref, xbuf_ref, xsems, wg_bf, wu_bf, wd_bf,
                   *, n_blocks):
    i = pl.program_id(0)
    used = bexp_ref[n_blocks]
    depth, blk = xbuf_ref.shape[:2]

    def fetch(j):
        rows = pl.ds(pl.multiple_of(j * blk, blk), blk)
        return pltpu.make_async_copy(xs_ref.at[rows, :], xbuf_ref.at[j % depth], xsems.at[j % depth])

    @pl.when(i == 0)
    def _():
        for j in range(depth - 1):
            @pl.when(j < used)
            def _(j=j):
                fetch(j).start()

    @pl.when(i + depth - 1 < used)
    def _():
        fetch(i + depth - 1).start()

    @pl.when(i < used)
    def _():
        @pl.when((i == 0) | (bexp_ref[i] != bexp_ref[jnp.maximum(i - 1, 0)]))
        def _():
            wg_bf[...] = wg_ref[...].astype(BF16)
            wu_bf[...] = wu_ref[...].astype(BF16)
            wd_bf[...] = wd_ref[...].astype(BF16)

        fetch(i).wait()
        lo, hi = _unpack_pair(xbuf_ref[i % depth])
        x = jnp.concatenate([lo.astype(BF16), hi.astype(BF16)], axis=1)
        gate = _dot(x, wg_bf[...])
        up = _dot(x, wu_bf[...])
        hid = gate / (1.0 + jnp.exp(-gate)) * up
        y = _dot(hid.astype(BF16), wd_bf[...])
        dh = y.shape[1] // 2
        ys_ref[...] = _pack_pair(y[:, :dh], y[:, dh:])


def _experts(bexp, xs, exp_gate, exp_up, exp_down, n_blocks):
    n_rows = xs.shape[0]
    d, hidden = exp_gate.shape[1:]
    blk = EXPERT_BLOCK

    def rows(i, b):
        return (jnp.minimum(i, jnp.maximum(b[n_blocks] - 1, 0)), 0)

    def weight(i, b):
        return (b[jnp.minimum(i, jnp.maximum(b[n_blocks] - 1, 0))], 0, 0)

    grid_spec = pltpu.PrefetchScalarGridSpec(
        num_scalar_prefetch=1,
        grid=(n_blocks,),
        in_specs=[pl.BlockSpec(memory_space=pl.ANY),
                  pl.BlockSpec((None, d, hidden), weight),
                  pl.BlockSpec((None, d, hidden), weight),
                  pl.BlockSpec((None, hidden, d), weight)],
        out_specs=pl.BlockSpec((blk, d // 2), rows),
        scratch_shapes=[pltpu.VMEM((EXPERT_RING, blk, d // 2), xs.dtype), pltpu.SemaphoreType.DMA((EXPERT_RING,)),
                        pltpu.VMEM((d, hidden), BF16), pltpu.VMEM((d, hidden), BF16), pltpu.VMEM((hidden, d), BF16)],
    )
    return pl.pallas_call(
        functools.partial(_expert_kernel, n_blocks=n_blocks),
        grid_spec=grid_spec,
        out_shape=jax.ShapeDtypeStruct((n_rows, d // 2), U32),
        compiler_params=_cparams(("arbitrary",)),
        name="experts",
    )(bexp, xs, exp_gate, exp_up, exp_down)


def _combine_kernel(runs_ref, next_runs_ref, tab_ref, rankt_ref, gatet_ref, base_ref, g3_ref, b3_ref,
                    ys_ref, out_ref, buf_ref, sems, *, n_exp):
    step, n_steps = pl.program_id(0), pl.num_programs(0)
    tm = base_ref.shape[0]
    rows = buf_ref.shape[1]
    slot = step % 2

    def fetch(tab_ref, s, enable=None):
        for e in range(n_exp):
            _start_run(ys_ref, buf_ref.at[s], sems.at[s], tab_ref[2, e], tab_ref[0, e], tab_ref[1, e], tm, enable)
        used = tab_ref[0, n_exp - 1] + tab_ref[1, n_exp - 1]
        _start_run(ys_ref, buf_ref.at[s], sems.at[s], 0, used, rows - used, rows - TOP_K * tm, enable)

    @pl.when(step == 0)
    def _():
        fetch(runs_ref, slot)

    fetch(next_runs_ref, 1 - slot, step + 1 < n_steps)

    tab = tab_ref[...]
    member = _run_membership(tab, rows).astype(BF16)
    start = tab[0:1, :].astype(F32)
    start_hi = jnp.floor(start / float(POS_BASE))
    sub = lax.broadcasted_iota(I32, (SUBLANES, LANES), 0)
    digits = jnp.where(sub == 0, start_hi, jnp.where(sub == 1, start - start_hi * float(POS_BASE), 0.0))
    picked = _dot_nt(digits.astype(BF16), member)
    in_run = (lax.broadcasted_iota(I32, (1, rows), 1).astype(F32)
              - (picked[0:1, :] * float(POS_BASE) + picked[1:2, :]))
    w = jnp.where(_dot_nt(rankt_ref[...], member) == in_run, _dot_nt(gatet_ref[...], member), 0.0)

    pltpu.make_async_copy(ys_ref.at[pl.ds(0, rows), :], buf_ref.at[slot], sems.at[slot]).wait()
    lo, hi = _unpack_pair(buf_ref[slot])
    y = jnp.concatenate([lo.astype(BF16), hi.astype(BF16)], axis=1)
    routed = _dot(w.astype(BF16), y)
    out_ref[...] = _layer_norm(base_ref[...] + routed, g3_ref[...], b3_ref[...])


def _combine(runs, rank_t, gate_t, base, g3, b3, ys, n_exp):
    n, d = base.shape
    tm = SORT_TILE
    last = n // tm - 1
    tab = lambda f: pl.BlockSpec((None,) + runs.shape[1:], f, memory_space=pltpu.SMEM)
    tmaj = pl.BlockSpec((tm, LANES), lambda i: (i, 0))
    return pl.pallas_call(
        functools.partial(_combine_kernel, n_exp=n_exp),
        grid=(n // tm,),
        in_specs=[tab(lambda i: (i, 0, 0)), tab(lambda i: (jnp.minimum(i + 1, last), 0, 0)),
                  pl.BlockSpec((None,) + runs.shape[1:], lambda i: (i, 0, 0)),
                  tmaj, tmaj,
                  pl.BlockSpec((tm, d), lambda i: (i, 0)),
                  pl.BlockSpec(g3.shape, lambda i: (0, 0)),
                  pl.BlockSpec(b3.shape, lambda i: (0, 0)),
                  pl.BlockSpec(memory_space=pl.ANY)],
        out_specs=pl.BlockSpec((tm, d), lambda i: (i, 0)),
        out_shape=jax.ShapeDtypeStruct((n, d), F32),
        scratch_shapes=[pltpu.VMEM((2, _sorted_rows(tm, n_exp), ys.shape[1]), ys.dtype),
                        pltpu.SemaphoreType.DMA((2,))],
        compiler_params=_cparams(("arbitrary",)),
        name="combine",
    )(runs, runs, runs, rank_t, gate_t, base, g3, b3, ys)


def _layer(x2d, mem2d, w_in, pool_w, pool_scale, w_out, ln1_g, ln1_b, w_cq, w_ck, w_cv, w_co, ln2_g, ln2_b,
           w_router, router_bias, exp_gate, exp_up, exp_down, sh_gate, sh_up, sh_down, ln3_g, ln3_b,
           *, alpha, batch, seq, mem_len):
    n, d = x2d.shape
    n_exp = w_router.shape[1]
    row = lambda a: a.reshape(1, -1)
    bf = lambda a: a.astype(BF16)

    dils = tuple(dil for _, dil in DILATED_PATTERNS)
    *qkv, p = _in_proj(x2d, bf(w_in), dils)
    branches = []
    for di, (window, dil) in enumerate(DILATED_PATTERNS):
        assert window // dil == ATTN_BLOCK and seq % (dil * ATTN_BLOCK * ATTN_STEP_BLOCKS) == 0
        assert IN_TILE % (dil * 2 * SUBLANES) == 0
        branches.append(_dil_attn(*qkv[3 * di:3 * di + 3], batch, seq, dil))
    (o1, l1), (o4, l4), (o16, l16) = branches
    kmem, vmem = _mem_kv(mem2d, bf(w_ck), bf(w_cv))

    wr_t = w_router.T
    wr_hi = bf(wr_t)
    wr_lo = bf(wr_t - wr_hi.astype(F32))
    x2, base, logits_t = _row_block(
        x2d, o1, o4, o16, l1, l4, l16, p, bf(pool_w), row(pool_scale), bf(w_out), row(ln1_g), row(ln1_b),
        bf(w_cq), kmem, vmem, bf(w_co), row(ln2_g), row(ln2_b), wr_hi, wr_lo, bf(sh_gate), bf(sh_up), bf(sh_down),
        alpha=alpha, batch=batch, seq=seq, mem_len=mem_len)

    bias_col = jnp.broadcast_to(router_bias.reshape(n_exp, 1), (n_exp, LANES))
    rank_tab, rank_t, gate_t, tab, counts = _route(logits_t, bias_col)
    max_rows = n * TOP_K + (n // SORT_TILE) * n_exp * (RUN_ALIGN - 1)
    n_blocks = -(-max_rows // EXPERT_BLOCK) + n_exp
    runs, bexp, seg = _place(counts, tab, n_blocks)
    xs = _dispatch(x2, rank_tab, runs, seg, n_blocks * EXPERT_BLOCK, n_exp)
    ys = _experts(bexp.reshape(-1), xs, exp_gate, exp_up, exp_down, n_blocks)
    return _combine(runs, rank_t, gate_t, base, row(ln3_g), row(ln3_b), ys, n_exp)


def kernel(x, mem, w_in, pool_w, pool_scale, w_out, ln1_g, ln1_b, w_cq, w_ck, w_cv, w_co, ln2_g, ln2_b, w_router, router_bias, exp_gate, exp_up, exp_down, sh_gate, sh_up, sh_down, ln3_g, ln3_b):
    batch, seq, d = x.shape
    mem_len = mem.shape[1]
    depth = w_in.shape[0]
    alpha = (2.0 * depth) ** 0.25
    x2d = x.reshape(batch * seq, d)
    mem2d = mem.reshape(batch * mem_len, d)
    for l in range(depth):
        x2d = _layer(x2d, mem2d, w_in[l], pool_w[l], pool_scale[l], w_out[l], ln1_g[l], ln1_b[l], w_cq[l], w_ck[l],
                     w_cv[l], w_co[l], ln2_g[l], ln2_b[l], w_router[l], router_bias[l], exp_gate[l], exp_up[l],
                     exp_down[l], sh_gate[l], sh_up[l], sh_down[l], ln3_g[l], ln3_b[l],
                     alpha=alpha, batch=batch, seq=seq, mem_len=mem_len)
    return x2d.reshape(batch, seq, d)
```

```python
import functools

import jax
import jax.numpy as jnp
from jax import lax
from jax.experimental import pallas as pl
from jax.experimental.pallas import tpu as pltpu

F32 = jnp.float32
BF16 = jnp.bfloat16
I32 = jnp.int32
U32 = jnp.uint32

ATTN_HEADS = 8
HEAD_DIM = 64
ATTN_WIDTH = ATTN_HEADS * HEAD_DIM
DILATED_PATTERNS = ((128, 1), (512, 4), (2048, 16))
POOL_WINDOWS = (2, 4, 8, 16)
MEM_HEADS = 4
N_EXPERT_GROUPS = 8
TOPK_GROUPS = 4
TOP_K = 8
ROUTED_SCALE = 2.5
LN_EPS = 1e-5
NEG_INF = -1e30

LANES = 128
SUBLANES = 8
VMEM_LIMIT = 56 * 1024 * 1024

ATTN_BLOCK = 128
ATTN_STEP_BLOCKS = 4
ROW_TILE = 512
IN_TILE = 512
SORT_TILE = 256
EXPERT_BLOCK = 512
EXPERT_RING = 3
RUN_ALIGN = SUBLANES
RUN_CHUNK = 64
POS_BASE = 64


def _cparams(sem, vmem=VMEM_LIMIT):
    return pltpu.CompilerParams(dimension_semantics=sem, vmem_limit_bytes=vmem)


def _layer_norm(y, g, b):
    mu = jnp.mean(y, axis=-1, keepdims=True)
    d = y - mu
    var = jnp.mean(d * d, axis=-1, keepdims=True)
    return d * lax.rsqrt(var + LN_EPS) * g + b


def _dot(a, b):
    return jnp.dot(a, b, preferred_element_type=F32)


def _dot_nt(a, b):
    return lax.dot_general(a, b, (((1,), (1,)), ((), ())), preferred_element_type=F32)


def _split_bf16(x):
    hi = x.astype(BF16)
    lo = (x - hi.astype(F32)).astype(BF16)
    return hi, lo


def _bf16_bits(x):
    return lax.bitcast_convert_type(x.astype(BF16).astype(F32), U32)


def _pack_pair(lo, hi):
    return (_bf16_bits(lo) >> 16) | (_bf16_bits(hi) & jnp.uint32(0xFFFF0000))


def _unpack_pair(w):
    lo = lax.bitcast_convert_type(w << 16, F32)
    hi = lax.bitcast_convert_type(w & jnp.uint32(0xFFFF0000), F32)
    return lo, hi


def _in_proj_kernel(x_ref, w_ref, *refs, dils):
    n_qkv = 3 * len(dils)
    out_refs, p_ref, slab_ref = refs[:n_qkv], refs[n_qkv], refs[n_qkv + 1]
    tm = x_ref.shape[0]
    x = x_ref[...].astype(BF16)
    aw = ATTN_WIDTH
    n_slabs = aw // LANES
    for a in range(3):
        val = _dot(x, w_ref[:, a * aw:(a + 1) * aw])
        if a == 0:
            val = val * (HEAD_DIM ** -0.5)
        for s in range(n_slabs):
            slab_ref[s] = val[:, s * LANES:(s + 1) * LANES]
        for di, dil in enumerate(dils):
            dst = out_refs[3 * di + a]
            if dil == 1:
                dst[...] = val.astype(BF16)
                continue
            for r in range(dil):
                for s in range(n_slabs):
                    rows = slab_ref[s, pl.ds(r, tm // dil, stride=dil), :]
                    dst[:, r * aw + s * LANES:r * aw + (s + 1) * LANES] = rows.astype(BF16)
    p_ref[...] = _dot(x, w_ref[:, 3 * aw:])


def _in_proj(x2d, w_in_bf, dils):
    n, d = x2d.shape
    aw = ATTN_WIDTH
    pw = w_in_bf.shape[1] - 3 * aw
    tm = IN_TILE
    row = lambda rows, w: pl.BlockSpec((rows, w), lambda i: (i, 0))
    qkv_specs, qkv_shapes = [], []
    for dil in dils:
        qkv_specs += [row(tm // dil, dil * aw)] * 3
        qkv_shapes += [jax.ShapeDtypeStruct((n // dil, dil * aw), BF16)] * 3
    return pl.pallas_call(
        functools.partial(_in_proj_kernel, dils=dils),
        grid=(n // tm,),
        in_specs=[row(tm, d), pl.BlockSpec(w_in_bf.shape, lambda i: (0, 0))],
        out_specs=qkv_specs + [row(tm, pw)],
        out_shape=qkv_shapes + [jax.ShapeDtypeStruct((n, pw), F32)],
        scratch_shapes=[pltpu.VMEM((aw // LANES, tm, LANES), F32)],
        compiler_params=_cparams(("arbitrary",)),
        name="in_proj",
    )(x2d, w_in_bf)


def _dil_attn_kernel(q_ref, kc_ref, kp_ref, vc_ref, vp_ref, o_ref, lse_ref, bias_ref, *, dil):
    blk = ATTN_BLOCK
    first_step = (pl.program_id(0) == 0) & (pl.program_id(1) == 0) & (pl.program_id(2) == 0)

    @pl.when(first_step)
    def _():
        qi = lax.broadcasted_iota(I32, (blk, 2 * blk), 0)
        kj = lax.broadcasted_iota(I32, (blk, 2 * blk), 1)
        dist = qi + blk - kj
        inside = (dist >= 0) & (dist <= blk)
        penalty = (dist * dil).astype(F32)
        for h in range(ATTN_HEADS):
            slope = 2.0 ** (-8.0 * (h + 1) / ATTN_HEADS)
            bias_ref[0, h] = jnp.where(inside, -slope * penalty, NEG_INF)
            bias_ref[1, h] = jnp.where(inside & (kj >= blk), -slope * penalty, NEG_INF)

    i = pl.program_id(1)
    r = pl.program_id(2)
    lane = lax.broadcasted_iota(I32, (blk, LANES), 1)
    low_half = lane < HEAD_DIM
    n_pairs = ATTN_WIDTH // LANES
    half = n_pairs // 2
    for j in range(ATTN_STEP_BLOCKS):
        table = jnp.where(i == 0, 1, 0) if j == 0 else 0
        lse_tile = jnp.zeros((blk, LANES), F32)
        pair_out = []
        for hp in range(n_pairs):
            sl = slice(hp * LANES, (hp + 1) * LANES)
            q2 = q_ref[j * blk:(j + 1) * blk, sl]
            if j == 0:
                kcat = jnp.concatenate([kp_ref[:, sl], kc_ref[0:blk, sl]], axis=0)
                vcat = jnp.concatenate([vp_ref[:, sl], vc_ref[0:blk, sl]], axis=0)
            else:
                kcat = kc_ref[(j - 1) * blk:(j + 1) * blk, sl]
                vcat = vc_ref[(j - 1) * blk:(j + 1) * blk, sl]
            outs = []
            for e in range(LANES // HEAD_DIM):
                h = hp * (LANES // HEAD_DIM) + e
                keep = low_half if e == 0 else jnp.logical_not(low_half)
                qe = jnp.where(keep, q2, jnp.zeros_like(q2))
                s = _dot_nt(qe, kcat) + bias_ref[table, h]
                m = jnp.max(s, axis=1, keepdims=True)
                p = jnp.exp(s - m)
                l = jnp.sum(p, axis=1, keepdims=True)
                outs.append(_dot(p.astype(BF16), vcat) / l)
                lse_tile = jnp.where(lane == h, m + jnp.log(l), lse_tile)
            pair_out.append(jnp.where(low_half, outs[0], outs[1]))
        rows = pl.ds(j * blk * dil + r, blk, stride=dil) if dil > 1 else slice(j * blk, (j + 1) * blk)
        for w in range(half):
            o_ref[w, rows, :] = _pack_pair(pair_out[w], pair_out[w + half])
        lse_ref[rows, :] = lse_tile


def _dil_attn(q, k, v, batch, seq, dil):
    blk = ATTN_BLOCK
    step_rows = ATTN_STEP_BLOCKS * blk
    sub = seq // dil
    view = lambda a: a.reshape(batch, sub, dil * ATTN_WIDTH)
    cur = pl.BlockSpec((None, step_rows, ATTN_WIDTH), lambda b, i, r: (b, i, r))
    prev = pl.BlockSpec((None, blk, ATTN_WIDTH), lambda b, i, r: (b, jnp.maximum(i * ATTN_STEP_BLOCKS - 1, 0), r))
    half = ATTN_WIDTH // LANES // 2
    o, lse = pl.pallas_call(
        functools.partial(_dil_attn_kernel, dil=dil),
        grid=(batch, sub // step_rows, dil),
        in_specs=[cur, cur, prev, cur, prev],
        out_specs=[pl.BlockSpec((None, half, step_rows * dil, LANES), lambda b, i, r: (b, 0, i, 0)),
                   pl.BlockSpec((None, step_rows * dil, LANES), lambda b, i, r: (b, i, 0))],
        out_shape=[jax.ShapeDtypeStruct((batch, half, seq, LANES), U32),
                   jax.ShapeDtypeStruct((batch, seq, LANES), F32)],
        scratch_shapes=[pltpu.VMEM((2, ATTN_HEADS, blk, 2 * blk), F32)],
        compiler_params=_cparams(("arbitrary",) * 3),
        name=f"dil_attn_d{dil}",
    )(view(q), view(k), view(k), view(v), view(v))
    return o, lse.reshape(batch * seq, LANES)


def _mem_kv_kernel(mem_ref, wk_ref, wv_ref, k_ref, v_ref):
    m = mem_ref[...].astype(BF16)
    k_ref[...] = _dot(m, wk_ref[...]).astype(BF16)
    v_ref[...] = _dot(m, wv_ref[...]).astype(BF16)


def _mem_kv(mem2d, w_ck_bf, w_cv_bf):
    rows = mem2d.shape[0]
    width = w_ck_bf.shape[1]
    return pl.pallas_call(
        _mem_kv_kernel,
        out_shape=[jax.ShapeDtypeStruct((rows, width), BF16)] * 2,
        compiler_params=_cparams(None),
        name="mem_kv",
    )(mem2d, w_ck_bf, w_cv_bf)


def _row_kernel(x_ref, o1_ref, o4_ref, o16_ref, l1_ref, l4_ref, l16_ref, p_ref, halo_ref,
                poolw_ref, pscale_ref, wout_ref, g1_ref, b1_ref,
                wcq_ref, km_ref, vm_ref, wco_ref, g2_ref, b2_ref,
                wrh_ref, wrl_ref, shg_ref, shu_ref, shd_ref,
                x2_ref, base_ref, logit_ref, *, alpha, seq):
    tm = x_ref.shape[0]
    i = pl.program_id(0)
    tile_pos = (i * tm) % seq

    l1, l4, l16 = l1_ref[...], l4_ref[...], l16_ref[...]
    mx = jnp.maximum(jnp.maximum(l1, l4), l16)
    e1, e4, e16 = jnp.exp(l1 - mx), jnp.exp(l4 - mx), jnp.exp(l16 - mx)
    inv = 1.0 / (e1 + e4 + e16)
    hrow = lax.broadcasted_iota(I32, (LANES, ATTN_WIDTH), 0)
    hcol = lax.broadcasted_iota(I32, (LANES, ATTN_WIDTH), 1) // HEAD_DIM
    spread = jnp.where(hrow == hcol, 1.0, 0.0).astype(BF16)

    def widen(w):
        hi, lo = _split_bf16(w)
        return _dot(hi, spread) + _dot(lo, spread)

    def branch(o_ref):
        lo, hi = zip(*[_unpack_pair(o_ref[w]) for w in range(o_ref.shape[0])])
        return jnp.concatenate(lo + hi, axis=1)

    attn = (widen(e1 * inv) * branch(o1_ref) + widen(e4 * inv) * branch(o4_ref)
            + widen(e16 * inv) * branch(o16_ref))

    halo_rows = halo_ref.shape[0]
    halo = jnp.where(tile_pos > 0, halo_ref[...], 0.0)
    ext = jnp.concatenate([halo, p_ref[...]], axis=0)
    pos = (tile_pos + lax.broadcasted_iota(I32, (tm, 1), 0)).astype(F32)
    gd = ext.shape[1] // len(POOL_WINDOWS)
    mixed = []
    for g, w in enumerate(POOL_WINDOWS):
        eg = ext[:, g * gd:(g + 1) * gd]
        acc, span = eg, 1
        while span < w:
            acc = acc + pltpu.roll(acc, span, 0)
            span *= 2
        count = jnp.minimum(pos + 1.0, float(w))
        pooled = acc[halo_rows:, :] / count - eg[halo_rows:, :]
        mixed.append(_dot(pooled.astype(BF16), poolw_ref[g]) * pscale_ref[:, g * gd:(g + 1) * gd])
    cat = jnp.concatenate([attn.astype(BF16)] + [m.astype(BF16) for m in mixed], axis=1)
    x1 = _layer_norm(alpha * x_ref[...] + _dot(cat, wout_ref[...]), g1_ref[...], b1_ref[...])

    qc = _dot(x1.astype(BF16), wcq_ref[...])
    mhd = qc.shape[1] // MEM_HEADS
    heads = []
    for h in range(MEM_HEADS):
        sl = slice(h * mhd, (h + 1) * mhd)
        s = _dot_nt(qc[:, sl].astype(BF16), km_ref[:, sl]) * (mhd ** -0.5)
        m = jnp.max(s, axis=1, keepdims=True)
        p = jnp.exp(s - m)
        l = jnp.sum(p, axis=1, keepdims=True)
        heads.append((_dot(p.astype(BF16), vm_ref[:, sl]) / l).astype(BF16))
    oc = jnp.concatenate(heads, axis=1)
    x2 = _layer_norm(alpha * x1 + _dot(oc, wco_ref[...]), g2_ref[...], b2_ref[...])
    dh = x2.shape[1] // 2
    x2_ref[...] = _pack_pair(x2[:, :dh], x2[:, dh:])

    xh, xl = _split_bf16(x2)
    logit_ref[...] = _dot_nt(wrh_ref[...], xh) + _dot_nt(wrh_ref[...], xl) + _dot_nt(wrl_ref[...], xh)

    gate = _dot(xh, shg_ref[...])
    up = _dot(xh, shu_ref[...])
    hid = gate / (1.0 + jnp.exp(-gate)) * up
    base_ref[...] = alpha * x2 + _dot(hid.astype(BF16), shd_ref[...])


def _row_block(x2d, o1, o4, o16, l1, l4, l16, p, pool_w_bf, pool_scale, w_out_bf, g1, b1,
               w_cq_bf, kmem, vmem, w_co_bf, g2, b2, wr_hi, wr_lo, sh_g, sh_u, sh_d,
               *, alpha, batch, seq, mem_len):
    n, d = x2d.shape
    tm = ROW_TILE
    halo_rows = max(POOL_WINDOWS)
    steps_per_seq = seq // tm
    row = lambda w: pl.BlockSpec((tm, w), lambda i: (i, 0))
    full = lambda a: pl.BlockSpec(a.shape, lambda i: (0,) * a.ndim)
    halo = pl.BlockSpec((halo_rows, p.shape[1]), lambda i: (jnp.maximum(i * (tm // halo_rows) - 1, 0), 0))
    memspec = pl.BlockSpec((mem_len, kmem.shape[1]), lambda i: (i // steps_per_seq, 0))
    n_exp = wr_hi.shape[0]
    branch = pl.BlockSpec((None, o1.shape[1], tm, LANES), lambda i: (i // steps_per_seq, 0, i % steps_per_seq, 0))
    return pl.pallas_call(
        functools.partial(_row_kernel, alpha=alpha, seq=seq),
        grid=(n // tm,),
        in_specs=[row(d), branch, branch, branch, row(LANES), row(LANES), row(LANES),
                  row(p.shape[1]), halo,
                  full(pool_w_bf), full(pool_scale), full(w_out_bf), full(g1), full(b1),
                  full(w_cq_bf), memspec, memspec, full(w_co_bf), full(g2), full(b2),
                  full(wr_hi), full(wr_lo), full(sh_g), full(sh_u), full(sh_d)],
        out_specs=[row(d // 2), row(d), pl.BlockSpec((n_exp, tm), lambda i: (0, i))],
        out_shape=[jax.ShapeDtypeStruct((n, d // 2), U32), jax.ShapeDtypeStruct((n, d), F32),
                   jax.ShapeDtypeStruct((n_exp, n), F32)],
        compiler_params=_cparams(("arbitrary",)),
        name="row_block",
    )(x2d, o1, o4, o16, l1, l4, l16, p, p, pool_w_bf, pool_scale, w_out_bf, g1, b1,
      w_cq_bf, kmem, vmem, w_co_bf, g2, b2, wr_hi, wr_lo, sh_g, sh_u, sh_d)


def _beat_counts(vals, n_rows):
    tn = vals.shape[1]
    n_tiles = n_rows // SUBLANES
    tiles = [vals[t * SUBLANES:(t + 1) * SUBLANES, :] for t in range(n_tiles)]
    sub = lax.broadcasted_iota(I32, (SUBLANES, tn), 0)
    counts = [jnp.zeros((SUBLANES, tn), F32) for _ in range(n_tiles)]
    for e in range(n_rows):
        te, je = divmod(e, SUBLANES)
        row = jnp.broadcast_to(vals[e:e + 1, :], (SUBLANES, tn))
        for t in range(n_tiles):
            strict = jnp.where(row > tiles[t], 1.0, 0.0)
            loose = jnp.where(row >= tiles[t], 1.0, 0.0)
            if t < te:
                beat = strict
            elif t > te:
                beat = loose
            else:
                beat = jnp.where(sub > je, loose, strict)
            counts[t] = counts[t] + beat
    return jnp.concatenate(counts, axis=0)


def _route_kernel(logit_ref, bias_ref, rank_ref, rankt_ref, gatet_ref, tab_ref, count_ref, carry_ref, rcarry_ref,
                  *, per_group):
    n_exp, tn = logit_ref.shape
    step = pl.program_id(0)

    @pl.when(step == 0)
    def _():
        carry_ref[...] = jnp.zeros_like(carry_ref)
        rcarry_ref[...] = jnp.zeros_like(rcarry_ref)

    scores = 1.0 / (1.0 + jnp.exp(-logit_ref[...]))
    biased = scores + bias_ref[:, 0:1]
    n_groups = n_exp // per_group

    gscore = []
    for g in range(n_groups):
        tile = biased[g * per_group:(g + 1) * per_group, :]
        inner = _beat_counts(tile, per_group)
        gscore.append(jnp.sum(jnp.where(inner < 2.0, tile, 0.0), axis=0, keepdims=True))
    gscore = jnp.concatenate(gscore, axis=0)
    grank = _beat_counts(gscore, n_groups)
    grank = jnp.concatenate(
        [jnp.broadcast_to(grank[g:g + 1, :], (per_group, tn)) for g in range(n_groups)], axis=0)
    masked = jnp.where(grank < float(TOPK_GROUPS), biased, NEG_INF)
    self32 = jnp.where(_beat_counts(masked, n_exp) < float(TOP_K), 1.0, 0.0)
    selbf = self32.astype(BF16)

    tr = lax.broadcasted_iota(I32, (tn, tn), 0)
    tc = lax.broadcasted_iota(I32, (tn, tn), 1)
    before = jnp.where(tr < tc, 1.0, 0.0).astype(BF16)
    lrank = _dot(selbf, before)
    align = float(RUN_ALIGN)
    lcount_col = jnp.broadcast_to(jnp.sum(self32, axis=1, keepdims=True), (n_exp, LANES))
    sel_wide = jnp.concatenate([selbf, jnp.zeros((LANES - n_exp, tn), BF16)], axis=0)
    lcount_row = _dot_nt(jnp.ones((SUBLANES, tn), BF16), sel_wide)
    lcount_col = jnp.ceil(lcount_col / align) * align
    lcount_row = jnp.ceil(lcount_row / align) * align
    wr = lax.broadcasted_iota(I32, (LANES, LANES), 0)
    wc = lax.broadcasted_iota(I32, (LANES, LANES), 1)
    lstart_row = _dot(lcount_row.astype(BF16), jnp.where(wr < wc, 1.0, 0.0).astype(BF16))

    sub = lax.broadcasted_iota(I32, (SUBLANES, LANES), 0)
    tab_ref[...] = jnp.where(sub == 0, lstart_row, jnp.where(sub == 1, lcount_row, jnp.where(
        sub == 2, rcarry_ref[...], 0.0))).astype(I32)
    rcarry_ref[...] = rcarry_ref[...] + lcount_row
    carry_ref[...] = carry_ref[...] + lcount_col
    count_ref[...] = carry_ref[...]

    denom = jnp.sum(self32 * scores, axis=0, keepdims=True)
    gate = self32 * scores / denom * ROUTED_SCALE
    pad = jnp.zeros((LANES - n_exp, tn), F32)
    widen = lambda v: jnp.concatenate([v, pad], axis=0)
    rank_tab = widen(jnp.where(self32 > 0.0, lrank, -1.0))
    rank_ref[...] = rank_tab.astype(BF16)
    rankt_ref[...] = rank_tab.T.astype(BF16)
    gatet_ref[...] = widen(gate).T.astype(BF16)


def _route(logits_t, bias_col):
    n_exp, n = logits_t.shape
    tn = SORT_TILE
    emaj = jax.ShapeDtypeStruct((LANES, n), BF16)
    tmaj = jax.ShapeDtypeStruct((n, LANES), BF16)
    emaj_spec = pl.BlockSpec((LANES, tn), lambda i: (0, i))
    tmaj_spec = pl.BlockSpec((tn, LANES), lambda i: (i, 0))
    return pl.pallas_call(
        functools.partial(_route_kernel, per_group=n_exp // N_EXPERT_GROUPS),
        grid=(n // tn,),
        in_specs=[pl.BlockSpec((n_exp, tn), lambda i: (0, i)), pl.BlockSpec(bias_col.shape, lambda i: (0, 0))],
        out_specs=[emaj_spec, tmaj_spec, tmaj_spec,
                   pl.BlockSpec((None, SUBLANES, LANES), lambda i: (i, 0, 0)),
                   pl.BlockSpec((n_exp, LANES), lambda i: (0, 0))],
        out_shape=[emaj, tmaj, tmaj, jax.ShapeDtypeStruct((n // tn, SUBLANES, LANES), I32),
                   jax.ShapeDtypeStruct((n_exp, LANES), F32)],
        scratch_shapes=[pltpu.VMEM((n_exp, LANES), F32), pltpu.VMEM((SUBLANES, LANES), F32)],
        compiler_params=_cparams(("arbitrary",)),
        name="route",
    )(logits_t, bias_col)


def _place_kernel(count_ref, tab_ref, runs_ref, bexp_ref, seg_ref, *, n_blocks):
    n_exp = count_ref.shape[0]
    blk = float(EXPERT_BLOCK)
    counts = count_ref[...]
    padded = jnp.ceil(counts / blk) * blk
    ridx = lax.broadcasted_iota(I32, (n_exp, LANES), 0)
    ends = padded
    shift = 1
    while shift < n_exp:
        ends = ends + jnp.where(ridx >= shift, pltpu.roll(ends, shift, 0), 0.0)
        shift *= 2
    starts = ends - padded

    tab = tab_ref[...]
    last = tab[tab.shape[0] - 1]
    total_row = (last[2:3, :] + last[1:2, :]).astype(F32)
    padded_row = jnp.ceil(total_row / blk) * blk
    lidx = lax.broadcasted_iota(I32, (1, LANES), 1)
    ends_row = padded_row
    shift = 1
    while shift < LANES:
        ends_row = ends_row + jnp.where(lidx >= shift, pltpu.roll(ends_row, shift, 1), 0.0)
        shift *= 2
    starts_row = (ends_row - padded_row).astype(I32)
    sub = lax.broadcasted_iota(I32, tab.shape, 1)
    runs_ref[...] = tab + jnp.where(sub == 2, starts_row[None], 0)

    nbp = bexp_ref.shape[1]
    blk_start = (lax.broadcasted_iota(I32, (n_exp, nbp), 1) * EXPERT_BLOCK).astype(F32)
    ends_wide = jnp.broadcast_to(ends[:, 0:1], (n_exp, nbp))
    count_le = lambda bound: jnp.minimum(
        jnp.sum(jnp.where(ends_wide <= bound, 1.0, 0.0), axis=0, keepdims=True), float(n_exp - 1))
    bexp = count_le(blk_start)
    used = ends[n_exp - 1:n_exp, 0:1] / blk
    seg_end = jnp.min(jnp.where(ends_wide > blk_start, ends_wide, 2.0 * float(nbp * EXPERT_BLOCK)),
                      axis=0, keepdims=True)
    sub8 = lax.broadcasted_iota(I32, (SUBLANES, nbp), 0)
    lane = lax.broadcasted_iota(I32, (SUBLANES, nbp), 1)
    plan = jnp.where(sub8 == 0, jnp.where(lane == n_blocks, used, bexp),
                     jnp.where(sub8 == 1, count_le(seg_end), jnp.where(sub8 == 2, seg_end / blk, 0.0)))
    bexp_ref[...] = plan.astype(I32)
    lane2 = lax.broadcasted_iota(I32, (n_exp, LANES), 1)
    seg_ref[...] = jnp.where(lane2 == 0, starts + counts, ends).astype(I32)


def _place(counts, tab, n_blocks):
    n_exp = counts.shape[0]
    nbp = -(-(n_blocks + 1) // LANES) * LANES
    return pl.pallas_call(
        functools.partial(_place_kernel, n_blocks=n_blocks),
        out_shape=[jax.ShapeDtypeStruct(tab.shape, I32), jax.ShapeDtypeStruct((SUBLANES, nbp), I32),
                   jax.ShapeDtypeStruct((n_exp, LANES), I32)],
        compiler_params=_cparams(None),
        name="route_place",
    )(counts, tab)


def _copy_rows(src_ref, dst_ref, sem, src0, dst0, n, wait=False):
    def piece(off, p):
        src = src_ref.at[pl.ds(pl.multiple_of(src0 + off, RUN_ALIGN), p), :]
        dst = dst_ref.at[pl.ds(pl.multiple_of(dst0 + off, RUN_ALIGN), p), :]
        cp = pltpu.make_async_copy(src, dst, sem)
        if wait:
            cp.wait()
        else:
            cp.start()

    def chunk(c, carry):
        piece(c * RUN_CHUNK, RUN_CHUNK)
        return carry
    lax.fori_loop(0, lax.shift_right_logical(n, RUN_CHUNK.bit_length() - 1), chunk, 0)
    p = RUN_CHUNK // 2
    while p >= RUN_ALIGN:
        @pl.when((n & p) != 0)
        def _(p=p):
            piece(n & (-2 * p), p)
        p //= 2


def _start_run(src_ref, dst_ref, sem, src0, dst0, n, limit, enable=None):
    p = RUN_ALIGN
    while p * 2 <= limit:
        p *= 2
    while p >= RUN_ALIGN:
        hit = (n & p) != 0
        @pl.when(hit if enable is None else hit & enable)
        def _(p=p):
            off = n & (-2 * p)
            src = src_ref.at[pl.ds(pl.multiple_of(src0 + off, RUN_ALIGN), p), :]
            dst = dst_ref.at[pl.ds(pl.multiple_of(dst0 + off, RUN_ALIGN), p), :]
            pltpu.make_async_copy(src, dst, sem).start()
        p //= 2


def _run_membership(tab, rows):
    j = lax.broadcasted_iota(I32, (rows, LANES), 0)
    start = tab[0:1, :]
    end = start + tab[1:2, :]
    return jnp.where(j >= start, jnp.where(j < end, 1.0, 0.0), 0.0)


def _dispatch_kernel(runs_ref, seg_ref, tab_ref, rank_ref, x_ref, xs_ref, buf_ref, zero_ref, sems, zsem, *, n_exp):
    step, n_steps = pl.program_id(0), pl.num_programs(0)
    tm = x_ref.shape[0]
    rows = buf_ref.shape[1]
    slot = step % 2

    @pl.when(step == 0)
    def _():
        zero_ref[...] = jnp.zeros_like(zero_ref)
        for wait in (False, True):
            def per_expert(e, c, wait=wait):
                first = seg_ref[e, 0]
                _copy_rows(zero_ref, xs_ref, zsem, 0, first, seg_ref[e, 1] - first, wait)
                return c
            lax.fori_loop(0, n_exp, per_expert, 0)

    lo, hi = _unpack_pair(x_ref[...])
    x = jnp.concatenate([lo.astype(BF16), hi.astype(BF16)], axis=1)
    tab = tab_ref[...]
    member = _run_membership(tab, rows)
    run_start = jnp.sum(member * tab[0:1, :].astype(F32), axis=1, keepdims=True)
    in_run = lax.broadcasted_iota(I32, (rows, 1), 0).astype(F32) - run_start
    onehot = jnp.where(_dot(member.astype(BF16), rank_ref[...]) == in_run, 1.0, 0.0)
    srt = _dot(onehot.astype(BF16), x)
    dh = srt.shape[1] // 2
    bits = lambda v: lax.bitcast_convert_type(v, U32)
    buf_ref[slot] = (bits(srt[:, :dh]) >> 16) | bits(srt[:, dh:])

    for e in range(n_exp):
        _start_run(buf_ref.at[slot], xs_ref, sems.at[slot], runs_ref[0, e], runs_ref[2, e], runs_ref[1, e], tm)
    used = runs_ref[0, n_exp - 1] + runs_ref[1, n_exp - 1]
    spare = xs_ref.shape[0] - 2 * (rows - TOP_K * tm) + slot * (rows - TOP_K * tm)
    _start_run(buf_ref.at[slot], xs_ref, sems.at[slot], used, spare, rows - used, rows - TOP_K * tm)

    def drain(s):
        pltpu.make_async_copy(buf_ref.at[s], xs_ref.at[pl.ds(0, rows), :], sems.at[s]).wait()

    @pl.when(step > 0)
    def _():
        drain(1 - slot)

    @pl.when(step == n_steps - 1)
    def _():
        drain(slot)


def _sorted_rows(tm, n_exp):
    return TOP_K * tm + n_exp * RUN_ALIGN


def _dispatch(x2, rank_tab, runs, seg, n_rows, n_exp):
    n, d = x2.shape
    tm = SORT_TILE
    rows = _sorted_rows(tm, n_exp)
    tab = (None,) + runs.shape[1:]
    return pl.pallas_call(
        functools.partial(_dispatch_kernel, n_exp=n_exp),
        grid=(n // tm,),
        in_specs=[pl.BlockSpec(tab, lambda i: (i, 0, 0), memory_space=pltpu.SMEM),
                  pl.BlockSpec(seg.shape, lambda i: (0, 0), memory_space=pltpu.SMEM),
                  pl.BlockSpec(tab, lambda i: (i, 0, 0)),
                  pl.BlockSpec((LANES, tm), lambda i: (0, i)),
                  pl.BlockSpec((tm, d), lambda i: (i, 0))],
        out_specs=pl.BlockSpec(memory_space=pl.ANY),
        out_shape=jax.ShapeDtypeStruct((n_rows + 2 * (rows - TOP_K * tm), d), x2.dtype),
        scratch_shapes=[pltpu.VMEM((2, rows, d), x2.dtype), pltpu.VMEM((EXPERT_BLOCK, d), x2.dtype),
                        pltpu.SemaphoreType.DMA((2,)), pltpu.SemaphoreType.DMA],
        compiler_params=_cparams(("arbitrary",)),
        name="dispatch",
    )(runs, seg, runs, rank_tab, x2)


def _expert_kernel(plan_ref, xs_ref, wg_ref, wu_ref, wd_ref, ys_ref, xbuf_ref, xsems, wg_f32, wu_f32, wd_f32,
                   wsems, wslot_ref, wg_bf, wu_bf, wd_bf, *, n_blocks):
    i = pl.program_id(0)
    used = plan_ref[0, n_blocks]
    depth, blk = xbuf_ref.shape[:2]

    def fetch(j):
        rows = pl.ds(pl.multiple_of(j * blk, blk), blk)
        return pltpu.make_async_copy(xs_ref.at[rows, :], xbuf_ref.at[j % depth], xsems.at[j % depth])

    def weights(e, s):
        return [pltpu.make_async_copy(src.at[e], dst.at[s], wsems.at[s])
                for src, dst in ((wg_ref, wg_f32), (wu_ref, wu_f32), (wd_ref, wd_f32))]

    @pl.when(i == 0)
    def _():
        for j in range(depth - 1):
            @pl.when(j < used)
            def _(j=j):
                fetch(j).start()

    @pl.when(i + depth - 1 < used)
    def _():
        fetch(i + depth - 1).start()

    @pl.when((i == 0) & (used > 0))
    def _():
        wslot_ref[0] = 0
        for cp in weights(plan_ref[0, 0], 0):
            cp.start()

    @pl.when(i < used)
    def _():
        @pl.when((i == 0) | (plan_ref[0, i] != plan_ref[0, jnp.maximum(i - 1, 0)]))
        def _():
            s = wslot_ref[0]
            for cp in weights(plan_ref[0, i], s):
                cp.wait()

            @pl.when(plan_ref[2, i] < used)
            def _():
                for cp in weights(plan_ref[1, i], 1 - s):
                    cp.start()
            wslot_ref[0] = 1 - s
            wg_bf[...] = wg_f32[s].astype(BF16)
            wu_bf[...] = wu_f32[s].astype(BF16)
            wd_bf[...] = wd_f32[s].astype(BF16)

        fetch(i).wait()
        lo, hi = _unpack_pair(xbuf_ref[i % depth])
        x = jnp.concatenate([lo.astype(BF16), hi.astype(BF16)], axis=1)
        gate = _dot(x, wg_bf[...])
        up = _dot(x, wu_bf[...])
        hid = gate / (1.0 + jnp.exp(-gate)) * up
        y = _dot(hid.astype(BF16), wd_bf[...])
        dh = y.shape[1] // 2
        ys_ref[...] = _pack_pair(y[:, :dh], y[:, dh:])


def _experts(bexp, xs, exp_gate, exp_up, exp_down, n_blocks):
    n_rows = xs.shape[0]
    d, hidden = exp_gate.shape[1:]
    blk = EXPERT_BLOCK

    def rows(i, b):
        return (jnp.minimum(i, jnp.maximum(b[0, n_blocks] - 1, 0)), 0)

    hbm = pl.BlockSpec(memory_space=pl.ANY)
    grid_spec = pltpu.PrefetchScalarGridSpec(
        num_scalar_prefetch=1,
        grid=(n_blocks,),
        in_specs=[hbm, hbm, hbm, hbm],
        out_specs=pl.BlockSpec((blk, d // 2), rows),
        scratch_shapes=[pltpu.VMEM((EXPERT_RING, blk, d // 2), xs.dtype), pltpu.SemaphoreType.DMA((EXPERT_RING,)),
                        pltpu.VMEM((2, d, hidden), F32), pltpu.VMEM((2, d, hidden), F32),
                        pltpu.VMEM((2, hidden, d), F32), pltpu.SemaphoreType.DMA((2,)), pltpu.SMEM((1,), I32),
                        pltpu.VMEM((d, hidden), BF16), pltpu.VMEM((d, hidden), BF16), pltpu.VMEM((hidden, d), BF16)],
    )
    return pl.pallas_call(
        functools.partial(_expert_kernel, n_blocks=n_blocks),
        grid_spec=grid_spec,
        out_shape=jax.ShapeDtypeStruct((n_rows, d // 2), U32),
        compiler_params=_cparams(("arbitrary",)),
        name="experts",
    )(bexp, xs, exp_gate, exp_up, exp_down)


def _combine_kernel(runs_ref, next_runs_ref, tab_ref, rankt_ref, gatet_ref, base_ref, g3_ref, b3_ref,
                    ys_ref, out_ref, buf_ref, sems, *, n_exp):
    step, n_steps = pl.program_id(0), pl.num_programs(0)
    tm = base_ref.shape[0]
    rows = buf_ref.shape[1]
    slot = step % 2

    def fetch(tab_ref, s, enable=None):
        for e in range(n_exp):
            _start_run(ys_ref, buf_ref.at[s], sems.at[s], tab_ref[2, e], tab_ref[0, e], tab_ref[1, e], tm, enable)
        used = tab_ref[0, n_exp - 1] + tab_ref[1, n_exp - 1]
        _start_run(ys_ref, buf_ref.at[s], sems.at[s], 0, used, rows - used, rows - TOP_K * tm, enable)

    @pl.when(step == 0)
    def _():
        fetch(runs_ref, slot)

    fetch(next_runs_ref, 1 - slot, step + 1 < n_steps)

    tab = tab_ref[...]
    member = _run_membership(tab, rows).astype(BF16)
    start = tab[0:1, :].astype(F32)
    start_hi = jnp.floor(start / float(POS_BASE))
    sub = lax.broadcasted_iota(I32, (SUBLANES, LANES), 0)
    digits = jnp.where(sub == 0, start_hi, jnp.where(sub == 1, start - start_hi * float(POS_BASE), 0.0))
    picked = _dot_nt(digits.astype(BF16), member)
    in_run = (lax.broadcasted_iota(I32, (1, rows), 1).astype(F32)
              - (picked[0:1, :] * float(POS_BASE) + picked[1:2, :]))
    w = jnp.where(_dot_nt(rankt_ref[...], member) == in_run, _dot_nt(gatet_ref[...], member), 0.0)

    pltpu.make_async_copy(ys_ref.at[pl.ds(0, rows), :], buf_ref.at[slot], sems.at[slot]).wait()
    lo, hi = _unpack_pair(buf_ref[slot])
    y = jnp.concatenate([lo.astype(BF16), hi.astype(BF16)], axis=1)
    routed = _dot(w.astype(BF16), y)
    out_ref[...] = _layer_norm(base_ref[...] + routed, g3_ref[...], b3_ref[...])


def _combine(runs, rank_t, gate_t, base, g3, b3, ys, n_exp):
    n, d = base.shape
    tm = SORT_TILE
    last = n // tm - 1
    tab = lambda f: pl.BlockSpec((None,) + runs.shape[1:], f, memory_space=pltpu.SMEM)
    tmaj = pl.BlockSpec((tm, LANES), lambda i: (i, 0))
    return pl.pallas_call(
        functools.partial(_combine_kernel, n_exp=n_exp),
        grid=(n // tm,),
        in_specs=[tab(lambda i: (i, 0, 0)), tab(lambda i: (jnp.minimum(i + 1, last), 0, 0)),
                  pl.BlockSpec((None,) + runs.shape[1:], lambda i: (i, 0, 0)),
                  tmaj, tmaj,
                  pl.BlockSpec((tm, d), lambda i: (i, 0)),
                  pl.BlockSpec(g3.shape, lambda i: (0, 0)),
                  pl.BlockSpec(b3.shape, lambda i: (0, 0)),
                  pl.BlockSpec(memory_space=pl.ANY)],
        out_specs=pl.BlockSpec((tm, d), lambda i: (i, 0)),
        out_shape=jax.ShapeDtypeStruct((n, d), F32),
        scratch_shapes=[pltpu.VMEM((2, _sorted_rows(tm, n_exp), ys.shape[1]), ys.dtype),
                        pltpu.SemaphoreType.DMA((2,))],
        compiler_params=_cparams(("arbitrary",)),
        name="combine",
    )(runs, runs, runs, rank_t, gate_t, base, g3, b3, ys)


def _layer(x2d, mem2d, w_in, pool_w, pool_scale, w_out, ln1_g, ln1_b, w_cq, w_ck, w_cv, w_co, ln2_g, ln2_b,
           w_router, router_bias, exp_gate, exp_up, exp_down, sh_gate, sh_up, sh_down, ln3_g, ln3_b,
           *, alpha, batch, seq, mem_len):
    n, d = x2d.shape
    n_exp = w_router.shape[1]
    row = lambda a: a.reshape(1, -1)
    bf = lambda a: a.astype(BF16)

    dils = tuple(dil for _, dil in DILATED_PATTERNS)
    *qkv, p = _in_proj(x2d, bf(w_in), dils)
    branches = []
    for di, (window, dil) in enumerate(DILATED_PATTERNS):
        assert window // dil == ATTN_BLOCK and seq % (dil * ATTN_BLOCK * ATTN_STEP_BLOCKS) == 0
        assert IN_TILE % (dil * 2 * SUBLANES) == 0
        branches.append(_dil_attn(*qkv[3 * di:3 * di + 3], batch, seq, dil))
    (o1, l1), (o4, l4), (o16, l16) = branches
    kmem, vmem = _mem_kv(mem2d, bf(w_ck), bf(w_cv))

    wr_t = w_router.T
    wr_hi = bf(wr_t)
    wr_lo = bf(wr_t - wr_hi.astype(F32))
    x2, base, logits_t = _row_block(
        x2d, o1, o4, o16, l1, l4, l16, p, bf(pool_w), row(pool_scale), bf(w_out), row(ln1_g), row(ln1_b),
        bf(w_cq), kmem, vmem, bf(w_co), row(ln2_g), row(ln2_b), wr_hi, wr_lo, bf(sh_gate), bf(sh_up), bf(sh_down),
        alpha=alpha, batch=batch, seq=seq, mem_len=mem_len)

    bias_col = jnp.broadcast_to(router_bias.reshape(n_exp, 1), (n_exp, LANES))
    rank_tab, rank_t, gate_t, tab, counts = _route(logits_t, bias_col)
    max_rows = n * TOP_K + (n // SORT_TILE) * n_exp * (RUN_ALIGN - 1)
    n_blocks = -(-max_rows // EXPERT_BLOCK) + n_exp
    runs, bexp, seg = _place(counts, tab, n_blocks)
    xs = _dispatch(x2, rank_tab, runs, seg, n_blocks * EXPERT_BLOCK, n_exp)
    ys = _experts(bexp, xs, exp_gate, exp_up, exp_down, n_blocks)
    return _combine(runs, rank_t, gate_t, base, row(ln3_g), row(ln3_b), ys, n_exp)


def kernel(x, mem, w_in, pool_w, pool_scale, w_out, ln1_g, ln1_b, w_cq, w_ck, w_cv, w_co, ln2_g, ln2_b, w_router, router_bias, exp_gate, exp_up, exp_down, sh_gate, sh_up, sh_down, ln3_g, ln3_b):
    batch, seq, d = x.shape
    mem_len = mem.shape[1]
    depth = w_in.shape[0]
    alpha = (2.0 * depth) ** 0.25
    x2d = x.reshape(batch * seq, d)
    mem2d = mem.reshape(batch * mem_len, d)
    for l in range(depth):
        x2d = _layer(x2d, mem2d, w_in[l], pool_w[l], pool_scale[l], w_out[l], ln1_g[l], ln1_b[l], w_cq[l], w_ck[l],
                     w_cv[l], w_co[l], ln2_g[l], ln2_b[l], w_router[l], router_bias[l], exp_gate[l], exp_up[l],
                     exp_down[l], sh_gate[l], sh_up[l], sh_down[l], ln3_g[l], ln3_b[l],
                     alpha=alpha, batch=batch, seq=seq, mem_len=mem_len)
    return x2d.reshape(batch, seq, d)
```

```python
import functools

import jax
import jax.numpy as jnp
from jax import lax
from jax.experimental import pallas as pl
from jax.experimental.pallas import tpu as pltpu

F32 = jnp.float32
BF16 = jnp.bfloat16
I32 = jnp.int32
U32 = jnp.uint32

ATTN_HEADS = 8
HEAD_DIM = 64
ATTN_WIDTH = ATTN_HEADS * HEAD_DIM
DILATED_PATTERNS = ((128, 1), (512, 4), (2048, 16))
POOL_WINDOWS = (2, 4, 8, 16)
MEM_HEADS = 4
N_EXPERT_GROUPS = 8
TOPK_GROUPS = 4
TOP_K = 8
ROUTED_SCALE = 2.5
LN_EPS = 1e-5
NEG_INF = -1e30

LANES = 128
SUBLANES = 8
VMEM_LIMIT = 56 * 1024 * 1024

ATTN_BLOCK = 128
ATTN_STEP_BLOCKS = 4
ROW_TILE = 512
IN_TILE = 512
SORT_TILE = 256
EXPERT_BLOCK = 512
EXPERT_RING = 3
RUN_ALIGN = SUBLANES
RUN_CHUNK = 64
POS_BASE = 64
DISPATCH_SLOTS = 3
DISPATCH_CHUNKS = 4


def _cparams(sem, vmem=VMEM_LIMIT):
    return pltpu.CompilerParams(dimension_semantics=sem, vmem_limit_bytes=vmem)


def _layer_norm(y, g, b):
    mu = jnp.mean(y, axis=-1, keepdims=True)
    d = y - mu
    var = jnp.mean(d * d, axis=-1, keepdims=True)
    return d * lax.rsqrt(var + LN_EPS) * g + b


def _dot(a, b):
    return jnp.dot(a, b, preferred_element_type=F32)


def _dot_nt(a, b):
    return lax.dot_general(a, b, (((1,), (1,)), ((), ())), preferred_element_type=F32)


def _split_bf16(x):
    hi = x.astype(BF16)
    lo = (x - hi.astype(F32)).astype(BF16)
    return hi, lo


def _bf16_bits(x):
    return lax.bitcast_convert_type(x.astype(BF16).astype(F32), U32)


def _pack_pair(lo, hi):
    return (_bf16_bits(lo) >> 16) | (_bf16_bits(hi) & jnp.uint32(0xFFFF0000))


def _unpack_pair(w):
    lo = lax.bitcast_convert_type(w << 16, F32)
    hi = lax.bitcast_convert_type(w & jnp.uint32(0xFFFF0000), F32)
    return lo, hi


def _in_proj_kernel(x_ref, w_ref, *refs, dils):
    n_qkv = 3 * len(dils)
    out_refs, p_ref, slab_ref = refs[:n_qkv], refs[n_qkv], refs[n_qkv + 1]
    tm = x_ref.shape[0]
    x = x_ref[...].astype(BF16)
    aw = ATTN_WIDTH
    n_slabs = aw // LANES
    for a in range(3):
        val = _dot(x, w_ref[:, a * aw:(a + 1) * aw])
        if a == 0:
            val = val * (HEAD_DIM ** -0.5)
        for s in range(n_slabs):
            slab_ref[s] = val[:, s * LANES:(s + 1) * LANES]
        for di, dil in enumerate(dils):
            dst = out_refs[3 * di + a]
            if dil == 1:
                dst[...] = val.astype(BF16)
                continue
            for r in range(dil):
                for s in range(n_slabs):
                    rows = slab_ref[s, pl.ds(r, tm // dil, stride=dil), :]
                    dst[:, r * aw + s * LANES:r * aw + (s + 1) * LANES] = rows.astype(BF16)
    p_ref[...] = _dot(x, w_ref[:, 3 * aw:])


def _in_proj(x2d, w_in_bf, dils):
    n, d = x2d.shape
    aw = ATTN_WIDTH
    pw = w_in_bf.shape[1] - 3 * aw
    tm = IN_TILE
    row = lambda rows, w: pl.BlockSpec((rows, w), lambda i: (i, 0))
    qkv_specs, qkv_shapes = [], []
    for dil in dils:
        qkv_specs += [row(tm // dil, dil * aw)] * 3
        qkv_shapes += [jax.ShapeDtypeStruct((n // dil, dil * aw), BF16)] * 3
    return pl.pallas_call(
        functools.partial(_in_proj_kernel, dils=dils),
        grid=(n // tm,),
        in_specs=[row(tm, d), pl.BlockSpec(w_in_bf.shape, lambda i: (0, 0))],
        out_specs=qkv_specs + [row(tm, pw)],
        out_shape=qkv_shapes + [jax.ShapeDtypeStruct((n, pw), F32)],
        scratch_shapes=[pltpu.VMEM((aw // LANES, tm, LANES), F32)],
        compiler_params=_cparams(("arbitrary",)),
        name="in_proj",
    )(x2d, w_in_bf)


def _dil_attn_kernel(q_ref, kc_ref, kp_ref, vc_ref, vp_ref, o_ref, lse_ref, bias_ref, *, dil):
    blk = ATTN_BLOCK
    first_step = (pl.program_id(0) == 0) & (pl.program_id(1) == 0) & (pl.program_id(2) == 0)

    @pl.when(first_step)
    def _():
        qi = lax.broadcasted_iota(I32, (blk, 2 * blk), 0)
        kj = lax.broadcasted_iota(I32, (blk, 2 * blk), 1)
        dist = qi + blk - kj
        inside = (dist >= 0) & (dist <= blk)
        penalty = (dist * dil).astype(F32)
        for h in range(ATTN_HEADS):
            slope = 2.0 ** (-8.0 * (h + 1) / ATTN_HEADS)
            bias_ref[0, h] = jnp.where(inside, -slope * penalty, NEG_INF)
            bias_ref[1, h] = jnp.where(inside & (kj >= blk), -slope * penalty, NEG_INF)

    i = pl.program_id(1)
    r = pl.program_id(2)
    lane = lax.broadcasted_iota(I32, (blk, LANES), 1)
    low_half = lane < HEAD_DIM
    n_pairs = ATTN_WIDTH // LANES
    half = n_pairs // 2
    for j in range(ATTN_STEP_BLOCKS):
        table = jnp.where(i == 0, 1, 0) if j == 0 else 0
        lse_tile = jnp.zeros((blk, LANES), F32)
        pair_out = []
        for hp in range(n_pairs):
            sl = slice(hp * LANES, (hp + 1) * LANES)
            q2 = q_ref[j * blk:(j + 1) * blk, sl]
            if j == 0:
                kcat = jnp.concatenate([kp_ref[:, sl], kc_ref[0:blk, sl]], axis=0)
                vcat = jnp.concatenate([vp_ref[:, sl], vc_ref[0:blk, sl]], axis=0)
            else:
                kcat = kc_ref[(j - 1) * blk:(j + 1) * blk, sl]
                vcat = vc_ref[(j - 1) * blk:(j + 1) * blk, sl]
            outs = []
            for e in range(LANES // HEAD_DIM):
                h = hp * (LANES // HEAD_DIM) + e
                keep = low_half if e == 0 else jnp.logical_not(low_half)
                qe = jnp.where(keep, q2, jnp.zeros_like(q2))
                s = _dot_nt(qe, kcat) + bias_ref[table, h]
                m = jnp.max(s, axis=1, keepdims=True)
                p = jnp.exp(s - m)
                l = jnp.sum(p, axis=1, keepdims=True)
                outs.append(_dot(p.astype(BF16), vcat) / l)
                lse_tile = jnp.where(lane == h, m + jnp.log(l), lse_tile)
            pair_out.append(jnp.where(low_half, outs[0], outs[1]))
        rows = pl.ds(j * blk * dil + r, blk, stride=dil) if dil > 1 else slice(j * blk, (j + 1) * blk)
        for w in range(half):
            o_ref[w, rows, :] = _pack_pair(pair_out[w], pair_out[w + half])
        lse_ref[rows, :] = lse_tile


def _dil_attn(q, k, v, batch, seq, dil):
    blk = ATTN_BLOCK
    step_rows = ATTN_STEP_BLOCKS * blk
    sub = seq // dil
    view = lambda a: a.reshape(batch, sub, dil * ATTN_WIDTH)
    cur = pl.BlockSpec((None, step_rows, ATTN_WIDTH), lambda b, i, r: (b, i, r))
    prev = pl.BlockSpec((None, blk, ATTN_WIDTH), lambda b, i, r: (b, jnp.maximum(i * ATTN_STEP_BLOCKS - 1, 0), r))
    half = ATTN_WIDTH // LANES // 2
    o, lse = pl.pallas_call(
        functools.partial(_dil_attn_kernel, dil=dil),
        grid=(batch, sub // step_rows, dil),
        in_specs=[cur, cur, prev, cur, prev],
        out_specs=[pl.BlockSpec((None, half, step_rows * dil, LANES), lambda b, i, r: (b, 0, i, 0)),
                   pl.BlockSpec((None, step_rows * dil, LANES), lambda b, i, r: (b, i, 0))],
        out_shape=[jax.ShapeDtypeStruct((batch, half, seq, LANES), U32),
                   jax.ShapeDtypeStruct((batch, seq, LANES), F32)],
        scratch_shapes=[pltpu.VMEM((2, ATTN_HEADS, blk, 2 * blk), F32)],
        compiler_params=_cparams(("arbitrary",) * 3),
        name=f"dil_attn_d{dil}",
    )(view(q), view(k), view(k), view(v), view(v))
    return o, lse.reshape(batch * seq, LANES)


def _mem_kv_kernel(mem_ref, wk_ref, wv_ref, k_ref, v_ref):
    m = mem_ref[...].astype(BF16)
    k_ref[...] = _dot(m, wk_ref[...]).astype(BF16)
    v_ref[...] = _dot(m, wv_ref[...]).astype(BF16)


def _mem_kv(mem2d, w_ck_bf, w_cv_bf):
    rows = mem2d.shape[0]
    width = w_ck_bf.shape[1]
    return pl.pallas_call(
        _mem_kv_kernel,
        out_shape=[jax.ShapeDtypeStruct((rows, width), BF16)] * 2,
        compiler_params=_cparams(None),
        name="mem_kv",
    )(mem2d, w_ck_bf, w_cv_bf)


def _row_kernel(x_ref, o1_ref, o4_ref, o16_ref, l1_ref, l4_ref, l16_ref, p_ref, halo_ref,
                poolw_ref, pscale_ref, wout_ref, g1_ref, b1_ref,
                wcq_ref, km_ref, vm_ref, wco_ref, g2_ref, b2_ref,
                wrh_ref, wrl_ref, shg_ref, shu_ref, shd_ref,
                x2_ref, base_ref, logit_ref, *, alpha, seq):
    tm = x_ref.shape[0]
    i = pl.program_id(0)
    tile_pos = (i * tm) % seq

    l1, l4, l16 = l1_ref[...], l4_ref[...], l16_ref[...]
    mx = jnp.maximum(jnp.maximum(l1, l4), l16)
    e1, e4, e16 = jnp.exp(l1 - mx), jnp.exp(l4 - mx), jnp.exp(l16 - mx)
    inv = 1.0 / (e1 + e4 + e16)
    hrow = lax.broadcasted_iota(I32, (LANES, ATTN_WIDTH), 0)
    hcol = lax.broadcasted_iota(I32, (LANES, ATTN_WIDTH), 1) // HEAD_DIM
    spread = jnp.where(hrow == hcol, 1.0, 0.0).astype(BF16)

    def widen(w):
        hi, lo = _split_bf16(w)
        return _dot(hi, spread) + _dot(lo, spread)

    def branch(o_ref):
        lo, hi = zip(*[_unpack_pair(o_ref[w]) for w in range(o_ref.shape[0])])
        return jnp.concatenate(lo + hi, axis=1)

    attn = (widen(e1 * inv) * branch(o1_ref) + widen(e4 * inv) * branch(o4_ref)
            + widen(e16 * inv) * branch(o16_ref))

    halo_rows = halo_ref.shape[0]
    halo = jnp.where(tile_pos > 0, halo_ref[...], 0.0)
    ext = jnp.concatenate([halo, p_ref[...]], axis=0)
    pos = (tile_pos + lax.broadcasted_iota(I32, (tm, 1), 0)).astype(F32)
    gd = ext.shape[1] // len(POOL_WINDOWS)
    mixed = []
    for g, w in enumerate(POOL_WINDOWS):
        eg = ext[:, g * gd:(g + 1) * gd]
        acc, span = eg, 1
        while span < w:
            acc = acc + pltpu.roll(acc, span, 0)
            span *= 2
        count = jnp.minimum(pos + 1.0, float(w))
        pooled = acc[halo_rows:, :] / count - eg[halo_rows:, :]
        mixed.append(_dot(pooled.astype(BF16), poolw_ref[g]) * pscale_ref[:, g * gd:(g + 1) * gd])
    cat = jnp.concatenate([attn.astype(BF16)] + [m.astype(BF16) for m in mixed], axis=1)
    x1 = _layer_norm(alpha * x_ref[...] + _dot(cat, wout_ref[...]), g1_ref[...], b1_ref[...])

    qc = _dot(x1.astype(BF16), wcq_ref[...])
    mhd = qc.shape[1] // MEM_HEADS
    heads = []
    for h in range(MEM_HEADS):
        sl = slice(h * mhd, (h + 1) * mhd)
        s = _dot_nt(qc[:, sl].astype(BF16), km_ref[:, sl]) * (mhd ** -0.5)
        m = jnp.max(s, axis=1, keepdims=True)
        p = jnp.exp(s - m)
        l = jnp.sum(p, axis=1, keepdims=True)
        heads.append((_dot(p.astype(BF16), vm_ref[:, sl]) / l).astype(BF16))
    oc = jnp.concatenate(heads, axis=1)
    x2 = _layer_norm(alpha * x1 + _dot(oc, wco_ref[...]), g2_ref[...], b2_ref[...])
    dh = x2.shape[1] // 2
    x2_ref[...] = _pack_pair(x2[:, :dh], x2[:, dh:])

    xh, xl = _split_bf16(x2)
    logit_ref[...] = _dot_nt(wrh_ref[...], xh) + _dot_nt(wrh_ref[...], xl) + _dot_nt(wrl_ref[...], xh)

    gate = _dot(xh, shg_ref[...])
    up = _dot(xh, shu_ref[...])
    hid = gate / (1.0 + jnp.exp(-gate)) * up
    base_ref[...] = alpha * x2 + _dot(hid.astype(BF16), shd_ref[...])


def _row_block(x2d, o1, o4, o16, l1, l4, l16, p, pool_w_bf, pool_scale, w_out_bf, g1, b1,
               w_cq_bf, kmem, vmem, w_co_bf, g2, b2, wr_hi, wr_lo, sh_g, sh_u, sh_d,
               *, alpha, batch, seq, mem_len):
    n, d = x2d.shape
    tm = ROW_TILE
    halo_rows = max(POOL_WINDOWS)
    steps_per_seq = seq // tm
    row = lambda w: pl.BlockSpec((tm, w), lambda i: (i, 0))
    full = lambda a: pl.BlockSpec(a.shape, lambda i: (0,) * a.ndim)
    halo = pl.BlockSpec((halo_rows, p.shape[1]), lambda i: (jnp.maximum(i * (tm // halo_rows) - 1, 0), 0))
    memspec = pl.BlockSpec((mem_len, kmem.shape[1]), lambda i: (i // steps_per_seq, 0))
    n_exp = wr_hi.shape[0]
    branch = pl.BlockSpec((None, o1.shape[1], tm, LANES), lambda i: (i // steps_per_seq, 0, i % steps_per_seq, 0))
    return pl.pallas_call(
        functools.partial(_row_kernel, alpha=alpha, seq=seq),
        grid=(n // tm,),
        in_specs=[row(d), branch, branch, branch, row(LANES), row(LANES), row(LANES),
                  row(p.shape[1]), halo,
                  full(pool_w_bf), full(pool_scale), full(w_out_bf), full(g1), full(b1),
                  full(w_cq_bf), memspec, memspec, full(w_co_bf), full(g2), full(b2),
                  full(wr_hi), full(wr_lo), full(sh_g), full(sh_u), full(sh_d)],
        out_specs=[row(d // 2), row(d), pl.BlockSpec((n_exp, tm), lambda i: (0, i))],
        out_shape=[jax.ShapeDtypeStruct((n, d // 2), U32), jax.ShapeDtypeStruct((n, d), F32),
                   jax.ShapeDtypeStruct((n_exp, n), F32)],
        compiler_params=_cparams(("arbitrary",)),
        name="row_block",
    )(x2d, o1, o4, o16, l1, l4, l16, p, p, pool_w_bf, pool_scale, w_out_bf, g1, b1,
      w_cq_bf, kmem, vmem, w_co_bf, g2, b2, wr_hi, wr_lo, sh_g, sh_u, sh_d)


def _beat_counts(vals, n_rows):
    tn = vals.shape[1]
    n_tiles = n_rows // SUBLANES
    tiles = [vals[t * SUBLANES:(t + 1) * SUBLANES, :] for t in range(n_tiles)]
    sub = lax.broadcasted_iota(I32, (SUBLANES, tn), 0)
    counts = [jnp.zeros((SUBLANES, tn), F32) for _ in range(n_tiles)]
    for e in range(n_rows):
        te, je = divmod(e, SUBLANES)
        row = jnp.broadcast_to(vals[e:e + 1, :], (SUBLANES, tn))
        for t in range(n_tiles):
            strict = jnp.where(row > tiles[t], 1.0, 0.0)
            loose = jnp.where(row >= tiles[t], 1.0, 0.0)
            if t < te:
                beat = strict
            elif t > te:
                beat = loose
            else:
                beat = jnp.where(sub > je, loose, strict)
            counts[t] = counts[t] + beat
    return jnp.concatenate(counts, axis=0)


def _route_kernel(logit_ref, bias_ref, rank_ref, rankt_ref, gatet_ref, tab_ref, count_ref, carry_ref, rcarry_ref,
                  *, per_group):
    n_exp, tn = logit_ref.shape
    step = pl.program_id(0)

    @pl.when(step == 0)
    def _():
        carry_ref[...] = jnp.zeros_like(carry_ref)
        rcarry_ref[...] = jnp.zeros_like(rcarry_ref)

    scores = 1.0 / (1.0 + jnp.exp(-logit_ref[...]))
    biased = scores + bias_ref[:, 0:1]
    n_groups = n_exp // per_group

    gscore = []
    for g in range(n_groups):
        tile = biased[g * per_group:(g + 1) * per_group, :]
        inner = _beat_counts(tile, per_group)
        gscore.append(jnp.sum(jnp.where(inner < 2.0, tile, 0.0), axis=0, keepdims=True))
    gscore = jnp.concatenate(gscore, axis=0)
    grank = _beat_counts(gscore, n_groups)
    grank = jnp.concatenate(
        [jnp.broadcast_to(grank[g:g + 1, :], (per_group, tn)) for g in range(n_groups)], axis=0)
    masked = jnp.where(grank < float(TOPK_GROUPS), biased, NEG_INF)
    self32 = jnp.where(_beat_counts(masked, n_exp) < float(TOP_K), 1.0, 0.0)
    selbf = self32.astype(BF16)

    tr = lax.broadcasted_iota(I32, (tn, tn), 0)
    tc = lax.broadcasted_iota(I32, (tn, tn), 1)
    before = jnp.where(tr < tc, 1.0, 0.0).astype(BF16)
    lrank = _dot(selbf, before)
    align = float(RUN_ALIGN)
    lcount_col = jnp.broadcast_to(jnp.sum(self32, axis=1, keepdims=True), (n_exp, LANES))
    sel_wide = jnp.concatenate([selbf, jnp.zeros((LANES - n_exp, tn), BF16)], axis=0)
    lcount_row = _dot_nt(jnp.ones((SUBLANES, tn), BF16), sel_wide)
    lcount_col = jnp.ceil(lcount_col / align) * align
    lcount_row = jnp.ceil(lcount_row / align) * align
    wr = lax.broadcasted_iota(I32, (LANES, LANES), 0)
    wc = lax.broadcasted_iota(I32, (LANES, LANES), 1)
    lstart_row = _dot(lcount_row.astype(BF16), jnp.where(wr < wc, 1.0, 0.0).astype(BF16))

    sub = lax.broadcasted_iota(I32, (SUBLANES, LANES), 0)
    tab_ref[...] = jnp.where(sub == 0, lstart_row, jnp.where(sub == 1, lcount_row, jnp.where(
        sub == 2, rcarry_ref[...], 0.0))).astype(I32)
    rcarry_ref[...] = rcarry_ref[...] + lcount_row
    carry_ref[...] = carry_ref[...] + lcount_col
    count_ref[...] = carry_ref[...]

    denom = jnp.sum(self32 * scores, axis=0, keepdims=True)
    gate = self32 * scores / denom * ROUTED_SCALE
    pad = jnp.zeros((LANES - n_exp, tn), F32)
    widen = lambda v: jnp.concatenate([v, pad], axis=0)
    rank_tab = widen(jnp.where(self32 > 0.0, lrank, -1.0))
    rank_ref[...] = rank_tab.astype(BF16)
    rankt_ref[...] = rank_tab.T.astype(BF16)
    gatet_ref[...] = widen(gate).T.astype(BF16)


def _route(logits_t, bias_col):
    n_exp, n = logits_t.shape
    tn = SORT_TILE
    emaj = jax.ShapeDtypeStruct((LANES, n), BF16)
    tmaj = jax.ShapeDtypeStruct((n, LANES), BF16)
    emaj_spec = pl.BlockSpec((LANES, tn), lambda i: (0, i))
    tmaj_spec = pl.BlockSpec((tn, LANES), lambda i: (i, 0))
    return pl.pallas_call(
        functools.partial(_route_kernel, per_group=n_exp // N_EXPERT_GROUPS),
        grid=(n // tn,),
        in_specs=[pl.BlockSpec((n_exp, tn), lambda i: (0, i)), pl.BlockSpec(bias_col.shape, lambda i: (0, 0))],
        out_specs=[emaj_spec, tmaj_spec, tmaj_spec,
                   pl.BlockSpec((None, SUBLANES, LANES), lambda i: (i, 0, 0)),
                   pl.BlockSpec((n_exp, LANES), lambda i: (0, 0))],
        out_shape=[emaj, tmaj, tmaj, jax.ShapeDtypeStruct((n // tn, SUBLANES, LANES), I32),
                   jax.ShapeDtypeStruct((n_exp, LANES), F32)],
        scratch_shapes=[pltpu.VMEM((n_exp, LANES), F32), pltpu.VMEM((SUBLANES, LANES), F32)],
        compiler_params=_cparams(("arbitrary",)),
        name="route",
    )(logits_t, bias_col)


def _place_kernel(count_ref, tab_ref, runs_ref, bexp_ref, seg_ref, *, n_blocks):
    n_exp = count_ref.shape[0]
    blk = float(EXPERT_BLOCK)
    counts = count_ref[...]
    padded = jnp.ceil(counts / blk) * blk
    ridx = lax.broadcasted_iota(I32, (n_exp, LANES), 0)
    ends = padded
    shift = 1
    while shift < n_exp:
        ends = ends + jnp.where(ridx >= shift, pltpu.roll(ends, shift, 0), 0.0)
        shift *= 2
    starts = ends - padded

    tab = tab_ref[...]
    last = tab[tab.shape[0] - 1]
    total_row = (last[2:3, :] + last[1:2, :]).astype(F32)
    padded_row = jnp.ceil(total_row / blk) * blk
    lidx = lax.broadcasted_iota(I32, (1, LANES), 1)
    ends_row = padded_row
    shift = 1
    while shift < LANES:
        ends_row = ends_row + jnp.where(lidx >= shift, pltpu.roll(ends_row, shift, 1), 0.0)
        shift *= 2
    starts_row = (ends_row - padded_row).astype(I32)
    sub = lax.broadcasted_iota(I32, tab.shape, 1)
    runs_ref[...] = tab + jnp.where(sub == 2, starts_row[None], 0)

    nbp = bexp_ref.shape[1]
    blk_start = (lax.broadcasted_iota(I32, (n_exp, nbp), 1) * EXPERT_BLOCK).astype(F32)
    ends_wide = jnp.broadcast_to(ends[:, 0:1], (n_exp, nbp))
    count_le = lambda bound: jnp.minimum(
        jnp.sum(jnp.where(ends_wide <= bound, 1.0, 0.0), axis=0, keepdims=True), float(n_exp - 1))
    bexp = count_le(blk_start)
    used = ends[n_exp - 1:n_exp, 0:1] / blk
    seg_end = jnp.min(jnp.where(ends_wide > blk_start, ends_wide, 2.0 * float(nbp * EXPERT_BLOCK)),
                      axis=0, keepdims=True)
    sub8 = lax.broadcasted_iota(I32, (SUBLANES, nbp), 0)
    lane = lax.broadcasted_iota(I32, (SUBLANES, nbp), 1)
    plan = jnp.where(sub8 == 0, jnp.where(lane == n_blocks, used, bexp),
                     jnp.where(sub8 == 1, count_le(seg_end), jnp.where(sub8 == 2, seg_end / blk, 0.0)))
    bexp_ref[...] = plan.astype(I32)
    lane2 = lax.broadcasted_iota(I32, (n_exp, LANES), 1)
    seg_ref[...] = jnp.where(lane2 == 0, starts + counts, ends).astype(I32)


def _place(counts, tab, n_blocks):
    n_exp = counts.shape[0]
    nbp = -(-(n_blocks + 1) // LANES) * LANES
    return pl.pallas_call(
        functools.partial(_place_kernel, n_blocks=n_blocks),
        out_shape=[jax.ShapeDtypeStruct(tab.shape, I32), jax.ShapeDtypeStruct((SUBLANES, nbp), I32),
                   jax.ShapeDtypeStruct((n_exp, LANES), I32)],
        compiler_params=_cparams(None),
        name="route_place",
    )(counts, tab)


def _copy_rows(src_ref, dst_ref, sem, src0, dst0, n, wait=False):
    def piece(off, p):
        src = src_ref.at[pl.ds(pl.multiple_of(src0 + off, RUN_ALIGN), p), :]
        dst = dst_ref.at[pl.ds(pl.multiple_of(dst0 + off, RUN_ALIGN), p), :]
        cp = pltpu.make_async_copy(src, dst, sem)
        if wait:
            cp.wait()
        else:
            cp.start()

    def chunk(c, carry):
        piece(c * RUN_CHUNK, RUN_CHUNK)
        return carry
    lax.fori_loop(0, lax.shift_right_logical(n, RUN_CHUNK.bit_length() - 1), chunk, 0)
    p = RUN_CHUNK // 2
    while p >= RUN_ALIGN:
        @pl.when((n & p) != 0)
        def _(p=p):
            piece(n & (-2 * p), p)
        p //= 2


def _start_run(src_ref, dst_ref, sem, src0, dst0, n, limit, enable=None):
    p = RUN_ALIGN
    while p * 2 <= limit:
        p *= 2
    while p >= RUN_ALIGN:
        hit = (n & p) != 0
        @pl.when(hit if enable is None else hit & enable)
        def _(p=p):
            off = n & (-2 * p)
            src = src_ref.at[pl.ds(pl.multiple_of(src0 + off, RUN_ALIGN), p), :]
            dst = dst_ref.at[pl.ds(pl.multiple_of(dst0 + off, RUN_ALIGN), p), :]
            pltpu.make_async_copy(src, dst, sem).start()
        p //= 2


def _run_membership(tab, rows, first_row=0):
    j = lax.broadcasted_iota(I32, (rows, LANES), 0) + first_row
    start = tab[0:1, :]
    end = start + tab[1:2, :]
    return jnp.where(j >= start, jnp.where(j < end, 1.0, 0.0), 0.0)


def _dispatch_kernel(prev_runs_ref, runs_ref, seg_ref, tab_ref, rank_ref, x_ref, xs_ref, buf_ref, zero_ref, sems,
                     zsem, *, n_exp):
    step, n_steps = pl.program_id(0), pl.num_programs(0)
    tm = x_ref.shape[0]
    n_slots, rows = buf_ref.shape[:2]
    pad_rows = rows - TOP_K * tm
    cur, prev, old = step % n_slots, (step + n_slots - 1) % n_slots, (step + n_slots - 2) % n_slots

    def start_runs(tab_ref, s, experts, enable=None):
        for e in experts:
            _start_run(buf_ref.at[s], xs_ref, sems.at[s], tab_ref[0, e], tab_ref[2, e], tab_ref[1, e], tm, enable)

    def start_filler(tab_ref, s, enable=None):
        used = tab_ref[0, n_exp - 1] + tab_ref[1, n_exp - 1]
        spare = xs_ref.shape[0] - (n_slots - s) * pad_rows
        _start_run(buf_ref.at[s], xs_ref, sems.at[s], used, spare, rows - used, pad_rows, enable)

    def drain(s):
        pltpu.make_async_copy(buf_ref.at[s], xs_ref.at[pl.ds(0, rows), :], sems.at[s]).wait()

    @pl.when(step == 0)
    def _():
        zero_ref[...] = jnp.zeros_like(zero_ref)
        for wait in (False, True):
            def per_expert(e, c, wait=wait):
                first = seg_ref[e, 0]
                _copy_rows(zero_ref, xs_ref, zsem, 0, first, seg_ref[e, 1] - first, wait)
                return c
            lax.fori_loop(0, n_exp, per_expert, 0)

    lo, hi = _unpack_pair(x_ref[...])
    x = jnp.concatenate([lo.astype(BF16), hi.astype(BF16)], axis=1)
    tab = tab_ref[...]
    bits = lambda v: lax.bitcast_convert_type(v, U32)
    chunk, per = rows // DISPATCH_CHUNKS, n_exp // DISPATCH_CHUNKS
    for c in range(DISPATCH_CHUNKS):
        member = _run_membership(tab, chunk, c * chunk)
        run_start = jnp.sum(member * tab[0:1, :].astype(F32), axis=1, keepdims=True)
        in_run = (lax.broadcasted_iota(I32, (chunk, 1), 0) + c * chunk).astype(F32) - run_start
        onehot = jnp.where(_dot(member.astype(BF16), rank_ref[...]) == in_run, 1.0, 0.0)
        srt = _dot(onehot.astype(BF16), x)
        dh = srt.shape[1] // 2
        buf_ref[cur, c * chunk:(c + 1) * chunk, :] = (bits(srt[:, :dh]) >> 16) | bits(srt[:, dh:])
        start_runs(prev_runs_ref, prev, range(c * per, (c + 1) * per), step > 0)
    start_filler(prev_runs_ref, prev, step > 0)

    @pl.when(step >= 2)
    def _():
        drain(old)

    @pl.when(step == n_steps - 1)
    def _():
        start_runs(runs_ref, cur, range(n_exp))
        start_filler(runs_ref, cur)

        @pl.when(step >= 1)
        def _():
            drain(prev)
        drain(cur)


def _sorted_rows(tm, n_exp):
    return TOP_K * tm + n_exp * RUN_ALIGN


def _dispatch(x2, rank_tab, runs, seg, n_rows, n_exp):
    n, d = x2.shape
    tm = SORT_TILE
    rows = _sorted_rows(tm, n_exp)
    assert rows % DISPATCH_CHUNKS == 0 and n_exp % DISPATCH_CHUNKS == 0
    tab = (None,) + runs.shape[1:]
    return pl.pallas_call(
        functools.partial(_dispatch_kernel, n_exp=n_exp),
        grid=(n // tm,),
        in_specs=[pl.BlockSpec(tab, lambda i: (jnp.maximum(i - 1, 0), 0, 0), memory_space=pltpu.SMEM),
                  pl.BlockSpec(tab, lambda i: (i, 0, 0), memory_space=pltpu.SMEM),
                  pl.BlockSpec(seg.shape, lambda i: (0, 0), memory_space=pltpu.SMEM),
                  pl.BlockSpec(tab, lambda i: (i, 0, 0)),
                  pl.BlockSpec((LANES, tm), lambda i: (0, i)),
                  pl.BlockSpec((tm, d), lambda i: (i, 0))],
        out_specs=pl.BlockSpec(memory_space=pl.ANY),
        out_shape=jax.ShapeDtypeStruct((n_rows + DISPATCH_SLOTS * (rows - TOP_K * tm), d), x2.dtype),
        scratch_shapes=[pltpu.VMEM((DISPATCH_SLOTS, rows, d), x2.dtype), pltpu.VMEM((EXPERT_BLOCK, d), x2.dtype),
                        pltpu.SemaphoreType.DMA((DISPATCH_SLOTS,)), pltpu.SemaphoreType.DMA],
        compiler_params=_cparams(("arbitrary",)),
        name="dispatch",
    )(runs, runs, seg, runs, rank_tab, x2)


def _expert_kernel(plan_ref, xs_ref, wg_ref, wu_ref, wd_ref, ys_ref, xbuf_ref, xsems, wg_f32, wu_f32, wd_f32,
                   wsems, wslot_ref, wg_bf, wu_bf, wd_bf, *, n_blocks):
    i = pl.program_id(0)
    used = plan_ref[0, n_blocks]
    depth, blk = xbuf_ref.shape[:2]

    def fetch(j):
        rows = pl.ds(pl.multiple_of(j * blk, blk), blk)
        return pltpu.make_async_copy(xs_ref.at[rows, :], xbuf_ref.at[j % depth], xsems.at[j % depth])

    def weights(e, s):
        return [pltpu.make_async_copy(src.at[e], dst.at[s], wsems.at[s])
                for src, dst in ((wg_ref, wg_f32), (wu_ref, wu_f32), (wd_ref, wd_f32))]

    @pl.when(i == 0)
    def _():
        for j in range(depth - 1):
            @pl.when(j < used)
            def _(j=j):
                fetch(j).start()

    @pl.when(i + depth - 1 < used)
    def _():
        fetch(i + depth - 1).start()

    @pl.when((i == 0) & (used > 0))
    def _():
        wslot_ref[0] = 0
        for cp in weights(plan_ref[0, 0], 0):
            cp.start()

    @pl.when(i < used)
    def _():
        @pl.when((i == 0) | (plan_ref[0, i] != plan_ref[0, jnp.maximum(i - 1, 0)]))
        def _():
            s = wslot_ref[0]
            for cp in weights(plan_ref[0, i], s):
                cp.wait()

            @pl.when(plan_ref[2, i] < used)
            def _():
                for cp in weights(plan_ref[1, i], 1 - s):
                    cp.start()
            wslot_ref[0] = 1 - s
            wg_bf[...] = wg_f32[s].astype(BF16)
            wu_bf[...] = wu_f32[s].astype(BF16)
            wd_bf[...] = wd_f32[s].astype(BF16)

        fetch(i).wait()
        lo, hi = _unpack_pair(xbuf_ref[i % depth])
        x = jnp.concatenate([lo.astype(BF16), hi.astype(BF16)], axis=1)
        gate = _dot(x, wg_bf[...])
        up = _dot(x, wu_bf[...])
        hid = gate / (1.0 + jnp.exp(-gate)) * up
        y = _dot(hid.astype(BF16), wd_bf[...])
        dh = y.shape[1] // 2
        ys_ref[...] = _pack_pair(y[:, :dh], y[:, dh:])


def _experts(bexp, xs, exp_gate, exp_up, exp_down, n_blocks):
    n_rows = xs.shape[0]
    d, hidden = exp_gate.shape[1:]
    blk = EXPERT_BLOCK

    def rows(i, b):
        return (jnp.minimum(i, jnp.maximum(b[0, n_blocks] - 1, 0)), 0)

    hbm = pl.BlockSpec(memory_space=pl.ANY)
    grid_spec = pltpu.PrefetchScalarGridSpec(
        num_scalar_prefetch=1,
        grid=(n_blocks,),
        in_specs=[hbm, hbm, hbm, hbm],
        out_specs=pl.BlockSpec((blk, d // 2), rows),
        scratch_shapes=[pltpu.VMEM((EXPERT_RING, blk, d // 2), xs.dtype), pltpu.SemaphoreType.DMA((EXPERT_RING,)),
                        pltpu.VMEM((2, d, hidden), F32), pltpu.VMEM((2, d, hidden), F32),
                        pltpu.VMEM((2, hidden, d), F32), pltpu.SemaphoreType.DMA((2,)), pltpu.SMEM((1,), I32),
                        pltpu.VMEM((d, hidden), BF16), pltpu.VMEM((d, hidden), BF16), pltpu.VMEM((hidden, d), BF16)],
    )
    return pl.pallas_call(
        functools.partial(_expert_kernel, n_blocks=n_blocks),
        grid_spec=grid_spec,
        out_shape=jax.ShapeDtypeStruct((n_rows, d // 2), U32),
        compiler_params=_cparams(("arbitrary",)),
        name="experts",
    )(bexp, xs, exp_gate, exp_up, exp_down)


def _combine_kernel(runs_ref, next_runs_ref, tab_ref, rankt_ref, gatet_ref, base_ref, g3_ref, b3_ref,
                    ys_ref, out_ref, buf_ref, sems, *, n_exp):
    step, n_steps = pl.program_id(0), pl.num_programs(0)
    tm = base_ref.shape[0]
    rows = buf_ref.shape[1]
    slot = step % 2

    def fetch(tab_ref, s, experts, enable=None):
        for e in experts:
            _start_run(ys_ref, buf_ref.at[s], sems.at[s], tab_ref[2, e], tab_ref[0, e], tab_ref[1, e], tm, enable)

    def fetch_filler(tab_ref, s, enable=None):
        used = tab_ref[0, n_exp - 1] + tab_ref[1, n_exp - 1]
        _start_run(ys_ref, buf_ref.at[s], sems.at[s], 0, used, rows - used, rows - TOP_K * tm, enable)

    @pl.when(step == 0)
    def _():
        fetch(runs_ref, slot, range(n_exp))
        fetch_filler(runs_ref, slot)

    pltpu.make_async_copy(ys_ref.at[pl.ds(0, rows), :], buf_ref.at[slot], sems.at[slot]).wait()

    tab = tab_ref[...]
    start = tab[0:1, :].astype(F32)
    start_hi = jnp.floor(start / float(POS_BASE))
    sub = lax.broadcasted_iota(I32, (SUBLANES, LANES), 0)
    digits = jnp.where(sub == 0, start_hi, jnp.where(sub == 1, start - start_hi * float(POS_BASE), 0.0)).astype(BF16)
    have_next = step + 1 < n_steps
    chunk, per = rows // DISPATCH_CHUNKS, n_exp // DISPATCH_CHUNKS
    routed = jnp.zeros(base_ref.shape, F32)
    for c in range(DISPATCH_CHUNKS):
        member = _run_membership(tab, chunk, c * chunk).astype(BF16)
        picked = _dot_nt(digits, member)
        in_run = ((lax.broadcasted_iota(I32, (1, chunk), 1) + c * chunk).astype(F32)
                  - (picked[0:1, :] * float(POS_BASE) + picked[1:2, :]))
        w = jnp.where(_dot_nt(rankt_ref[...], member) == in_run, _dot_nt(gatet_ref[...], member), 0.0)
        lo, hi = _unpack_pair(buf_ref[slot, c * chunk:(c + 1) * chunk, :])
        y = jnp.concatenate([lo.astype(BF16), hi.astype(BF16)], axis=1)
        routed = routed + _dot(w.astype(BF16), y)
        fetch(next_runs_ref, 1 - slot, range(c * per, (c + 1) * per), have_next)
    fetch_filler(next_runs_ref, 1 - slot, have_next)
    out_ref[...] = _layer_norm(base_ref[...] + routed, g3_ref[...], b3_ref[...])


def _combine(runs, rank_t, gate_t, base, g3, b3, ys, n_exp):
    n, d = base.shape
    tm = SORT_TILE
    last = n // tm - 1
    tab = lambda f: pl.BlockSpec((None,) + runs.shape[1:], f, memory_space=pltpu.SMEM)
    tmaj = pl.BlockSpec((tm, LANES), lambda i: (i, 0))
    return pl.pallas_call(
        functools.partial(_combine_kernel, n_exp=n_exp),
        grid=(n // tm,),
        in_specs=[tab(lambda i: (i, 0, 0)), tab(lambda i: (jnp.minimum(i + 1, last), 0, 0)),
                  pl.BlockSpec((None,) + runs.shape[1:], lambda i: (i, 0, 0)),
                  tmaj, tmaj,
                  pl.BlockSpec((tm, d), lambda i: (i, 0)),
                  pl.BlockSpec(g3.shape, lambda i: (0, 0)),
                  pl.BlockSpec(b3.shape, lambda i: (0, 0)),
                  pl.BlockSpec(memory_space=pl.ANY)],
        out_specs=pl.BlockSpec((tm, d), lambda i: (i, 0)),
        out_shape=jax.ShapeDtypeStruct((n, d), F32),
        scratch_shapes=[pltpu.VMEM((2, _sorted_rows(tm, n_exp), ys.shape[1]), ys.dtype),
                        pltpu.SemaphoreType.DMA((2,))],
        compiler_params=_cparams(("arbitrary",)),
        name="combine",
    )(runs, runs, runs, rank_t, gate_t, base, g3, b3, ys)


def _layer(x2d, mem2d, w_in, pool_w, pool_scale, w_out, ln1_g, ln1_b, w_cq, w_ck, w_cv, w_co, ln2_g, ln2_b,
           w_router, router_bias, exp_gate, exp_up, exp_down, sh_gate, sh_up, sh_down, ln3_g, ln3_b,
           *, alpha, batch, seq, mem_len):
    n, d = x2d.shape
    n_exp = w_router.shape[1]
    row = lambda a: a.reshape(1, -1)
    bf = lambda a: a.astype(BF16)

    dils = tuple(dil for _, dil in DILATED_PATTERNS)
    *qkv, p = _in_proj(x2d, bf(w_in), dils)
    branches = []
    for di, (window, dil) in enumerate(DILATED_PATTERNS):
        assert window // dil == ATTN_BLOCK and seq % (dil * ATTN_BLOCK * ATTN_STEP_BLOCKS) == 0
        assert IN_TILE % (dil * 2 * SUBLANES) == 0
        branches.append(_dil_attn(*qkv[3 * di:3 * di + 3], batch, seq, dil))
    (o1, l1), (o4, l4), (o16, l16) = branches
    kmem, vmem = _mem_kv(mem2d, bf(w_ck), bf(w_cv))

    wr_t = w_router.T
    wr_hi = bf(wr_t)
    wr_lo = bf(wr_t - wr_hi.astype(F32))
    x2, base, logits_t = _row_block(
        x2d, o1, o4, o16, l1, l4, l16, p, bf(pool_w), row(pool_scale), bf(w_out), row(ln1_g), row(ln1_b),
        bf(w_cq), kmem, vmem, bf(w_co), row(ln2_g), row(ln2_b), wr_hi, wr_lo, bf(sh_gate), bf(sh_up), bf(sh_down),
        alpha=alpha, batch=batch, seq=seq, mem_len=mem_len)

    bias_col = jnp.broadcast_to(router_bias.reshape(n_exp, 1), (n_exp, LANES))
    rank_tab, rank_t, gate_t, tab, counts = _route(logits_t, bias_col)
    max_rows = n * TOP_K + (n // SORT_TILE) * n_exp * (RUN_ALIGN - 1)
    n_blocks = -(-max_rows // EXPERT_BLOCK) + n_exp
    runs, bexp, seg = _place(counts, tab, n_blocks)
    xs = _dispatch(x2, rank_tab, runs, seg, n_blocks * EXPERT_BLOCK, n_exp)
    ys = _experts(bexp, xs, exp_gate, exp_up, exp_down, n_blocks)
    return _combine(runs, rank_t, gate_t, base, row(ln3_g), row(ln3_b), ys, n_exp)


def kernel(x, mem, w_in, pool_w, pool_scale, w_out, ln1_g, ln1_b, w_cq, w_ck, w_cv, w_co, ln2_g, ln2_b, w_router, router_bias, exp_gate, exp_up, exp_down, sh_gate, sh_up, sh_down, ln3_g, ln3_b):
    batch, seq, d = x.shape
    mem_len = mem.shape[1]
    depth = w_in.shape[0]
    alpha = (2.0 * depth) ** 0.25
    x2d = x.reshape(batch * seq, d)
    mem2d = mem.reshape(batch * mem_len, d)
    for l in range(depth):
        x2d = _layer(x2d, mem2d, w_in[l], pool_w[l], pool_scale[l], w_out[l], ln1_g[l], ln1_b[l], w_cq[l], w_ck[l],
                     w_cv[l], w_co[l], ln2_g[l], ln2_b[l], w_router[l], router_bias[l], exp_gate[l], exp_up[l],
                     exp_down[l], sh_gate[l], sh_up[l], sh_down[l], ln3_g[l], ln3_b[l],
                     alpha=alpha, batch=batch, seq=seq, mem_len=mem_len)
    return x2d.reshape(batch, seq, d)
```

```python
import functools

import jax
import jax.numpy as jnp
from jax import lax
from jax.experimental import pallas as pl
from jax.experimental.pallas import tpu as pltpu

F32 = jnp.float32
BF16 = jnp.bfloat16
I32 = jnp.int32
U32 = jnp.uint32

ATTN_HEADS = 8
HEAD_DIM = 64
ATTN_WIDTH = ATTN_HEADS * HEAD_DIM
DILATED_PATTERNS = ((128, 1), (512, 4), (2048, 16))
POOL_WINDOWS = (2, 4, 8, 16)
MEM_HEADS = 4
N_EXPERT_GROUPS = 8
TOPK_GROUPS = 4
TOP_K = 8
ROUTED_SCALE = 2.5
LN_EPS = 1e-5
NEG_INF = -1e30

LANES = 128
SUBLANES = 8
VMEM_LIMIT = 56 * 1024 * 1024

ATTN_BLOCK = 128
ATTN_STEP_BLOCKS = 4
ROW_TILE = 512
IN_TILE = 512
SORT_TILE = 256
EXPERT_BLOCK = 512
EXPERT_RING = 3
RUN_ALIGN = SUBLANES
RUN_CHUNK = 64
POS_BASE = 64
DISPATCH_SLOTS = 3
DISPATCH_CHUNKS = 4


def _cparams(sem, vmem=VMEM_LIMIT):
    return pltpu.CompilerParams(dimension_semantics=sem, vmem_limit_bytes=vmem)


def _layer_norm(y, g, b):
    mu = jnp.mean(y, axis=-1, keepdims=True)
    d = y - mu
    var = jnp.mean(d * d, axis=-1, keepdims=True)
    return d * lax.rsqrt(var + LN_EPS) * g + b


def _dot(a, b):
    return jnp.dot(a, b, preferred_element_type=F32)


def _dot_nt(a, b):
    return lax.dot_general(a, b, (((1,), (1,)), ((), ())), preferred_element_type=F32)


def _split_bf16(x):
    hi = x.astype(BF16)
    lo = (x - hi.astype(F32)).astype(BF16)
    return hi, lo


def _bf16_bits(x):
    return lax.bitcast_convert_type(x.astype(BF16).astype(F32), U32)


def _pack_pair(lo, hi):
    return (_bf16_bits(lo) >> 16) | (_bf16_bits(hi) & jnp.uint32(0xFFFF0000))


def _unpack_pair(w):
    lo = lax.bitcast_convert_type(w << 16, F32)
    hi = lax.bitcast_convert_type(w & jnp.uint32(0xFFFF0000), F32)
    return lo, hi


def _in_proj_kernel(x_ref, w_ref, *refs, dils):
    n_qkv = 3 * len(dils)
    out_refs, p_ref, slab_ref = refs[:n_qkv], refs[n_qkv], refs[n_qkv + 1]
    tm = x_ref.shape[0]
    x = x_ref[...].astype(BF16)
    aw = ATTN_WIDTH
    n_slabs = aw // LANES
    for a in range(3):
        val = _dot(x, w_ref[:, a * aw:(a + 1) * aw])
        if a == 0:
            val = val * (HEAD_DIM ** -0.5)
        for s in range(n_slabs):
            slab_ref[s] = val[:, s * LANES:(s + 1) * LANES]
        for di, dil in enumerate(dils):
            dst = out_refs[3 * di + a]
            if dil == 1:
                dst[...] = val.astype(BF16)
                continue
            for r in range(dil):
                for s in range(n_slabs):
                    rows = slab_ref[s, pl.ds(r, tm // dil, stride=dil), :]
                    dst[:, r * aw + s * LANES:r * aw + (s + 1) * LANES] = rows.astype(BF16)
    p_ref[...] = _dot(x, w_ref[:, 3 * aw:])


def _in_proj(x2d, w_in_bf, dils):
    n, d = x2d.shape
    aw = ATTN_WIDTH
    pw = w_in_bf.shape[1] - 3 * aw
    tm = IN_TILE
    row = lambda rows, w: pl.BlockSpec((rows, w), lambda i: (i, 0))
    qkv_specs, qkv_shapes = [], []
    for dil in dils:
        qkv_specs += [row(tm // dil, dil * aw)] * 3
        qkv_shapes += [jax.ShapeDtypeStruct((n // dil, dil * aw), BF16)] * 3
    return pl.pallas_call(
        functools.partial(_in_proj_kernel, dils=dils),
        grid=(n // tm,),
        in_specs=[row(tm, d), pl.BlockSpec(w_in_bf.shape, lambda i: (0, 0))],
        out_specs=qkv_specs + [row(tm, pw)],
        out_shape=qkv_shapes + [jax.ShapeDtypeStruct((n, pw), F32)],
        scratch_shapes=[pltpu.VMEM((aw // LANES, tm, LANES), F32)],
        compiler_params=_cparams(("arbitrary",)),
        name="in_proj",
    )(x2d, w_in_bf)


def _dil_attn_kernel(q_ref, kc_ref, kp_ref, vc_ref, vp_ref, o_ref, lse_ref, bias_ref, *, dil):
    blk = ATTN_BLOCK
    first_step = (pl.program_id(0) == 0) & (pl.program_id(1) == 0) & (pl.program_id(2) == 0)

    @pl.when(first_step)
    def _():
        qi = lax.broadcasted_iota(I32, (blk, 2 * blk), 0)
        kj = lax.broadcasted_iota(I32, (blk, 2 * blk), 1)
        dist = qi + blk - kj
        inside = (dist >= 0) & (dist <= blk)
        penalty = (dist * dil).astype(F32)
        for h in range(ATTN_HEADS):
            slope = 2.0 ** (-8.0 * (h + 1) / ATTN_HEADS)
            bias_ref[0, h] = jnp.where(inside, -slope * penalty, NEG_INF)
            bias_ref[1, h] = jnp.where(inside & (kj >= blk), -slope * penalty, NEG_INF)

    i = pl.program_id(1)
    r = pl.program_id(2)
    lane = lax.broadcasted_iota(I32, (blk, LANES), 1)
    low_half = lane < HEAD_DIM
    n_pairs = ATTN_WIDTH // LANES
    half = n_pairs // 2
    for j in range(ATTN_STEP_BLOCKS):
        table = jnp.where(i == 0, 1, 0) if j == 0 else 0
        lse_tile = jnp.zeros((blk, LANES), F32)
        pair_out = []
        for hp in range(n_pairs):
            sl = slice(hp * LANES, (hp + 1) * LANES)
            q2 = q_ref[j * blk:(j + 1) * blk, sl]
            if j == 0:
                kcat = jnp.concatenate([kp_ref[:, sl], kc_ref[0:blk, sl]], axis=0)
                vcat = jnp.concatenate([vp_ref[:, sl], vc_ref[0:blk, sl]], axis=0)
            else:
                kcat = kc_ref[(j - 1) * blk:(j + 1) * blk, sl]
                vcat = vc_ref[(j - 1) * blk:(j + 1) * blk, sl]
            outs = []
            for e in range(LANES // HEAD_DIM):
                h = hp * (LANES // HEAD_DIM) + e
                keep = low_half if e == 0 else jnp.logical_not(low_half)
                qe = jnp.where(keep, q2, jnp.zeros_like(q2))
                s = _dot_nt(qe, kcat) + bias_ref[table, h]
                m = jnp.max(s, axis=1, keepdims=True)
                p = jnp.exp(s - m)
                l = jnp.sum(p, axis=1, keepdims=True)
                outs.append(_dot(p.astype(BF16), vcat) / l)
                lse_tile = jnp.where(lane == h, m + jnp.log(l), lse_tile)
            pair_out.append(jnp.where(low_half, outs[0], outs[1]))
        rows = pl.ds(j * blk * dil + r, blk, stride=dil) if dil > 1 else slice(j * blk, (j + 1) * blk)
        for w in range(half):
            o_ref[w, rows, :] = _pack_pair(pair_out[w], pair_out[w + half])
        lse_ref[rows, :] = lse_tile


def _dil_attn(q, k, v, batch, seq, dil):
    blk = ATTN_BLOCK
    step_rows = ATTN_STEP_BLOCKS * blk
    sub = seq // dil
    view = lambda a: a.reshape(batch, sub, dil * ATTN_WIDTH)
    cur = pl.BlockSpec((None, step_rows, ATTN_WIDTH), lambda b, i, r: (b, i, r))
    prev = pl.BlockSpec((None, blk, ATTN_WIDTH), lambda b, i, r: (b, jnp.maximum(i * ATTN_STEP_BLOCKS - 1, 0), r))
    half = ATTN_WIDTH // LANES // 2
    o, lse = pl.pallas_call(
        functools.partial(_dil_attn_kernel, dil=dil),
        grid=(batch, sub // step_rows, dil),
        in_specs=[cur, cur, prev, cur, prev],
        out_specs=[pl.BlockSpec((None, half, step_rows * dil, LANES), lambda b, i, r: (b, 0, i, 0)),
                   pl.BlockSpec((None, step_rows * dil, LANES), lambda b, i, r: (b, i, 0))],
        out_shape=[jax.ShapeDtypeStruct((batch, half, seq, LANES), U32),
                   jax.ShapeDtypeStruct((batch, seq, LANES), F32)],
        scratch_shapes=[pltpu.VMEM((2, ATTN_HEADS, blk, 2 * blk), F32)],
        compiler_params=_cparams(("arbitrary",) * 3),
        name=f"dil_attn_d{dil}",
    )(view(q), view(k), view(k), view(v), view(v))
    return o, lse.reshape(batch * seq, LANES)


def _mem_kv_kernel(mem_ref, wk_ref, wv_ref, k_ref, v_ref):
    m = mem_ref[...].astype(BF16)
    k_ref[...] = _dot(m, wk_ref[...]).astype(BF16)
    v_ref[...] = _dot(m, wv_ref[...]).astype(BF16)


def _mem_kv(mem2d, w_ck_bf, w_cv_bf):
    rows = mem2d.shape[0]
    width = w_ck_bf.shape[1]
    return pl.pallas_call(
        _mem_kv_kernel,
        out_shape=[jax.ShapeDtypeStruct((rows, width), BF16)] * 2,
        compiler_params=_cparams(None),
        name="mem_kv",
    )(mem2d, w_ck_bf, w_cv_bf)


def _row_kernel(x_ref, o1_ref, o4_ref, o16_ref, l1_ref, l4_ref, l16_ref, p_ref, halo_ref,
                poolw_ref, pscale_ref, wout_ref, g1_ref, b1_ref,
                wcq_ref, km_ref, vm_ref, wco_ref, g2_ref, b2_ref,
                wrh_ref, wrl_ref, shg_ref, shu_ref, shd_ref,
                x2_ref, base_ref, logit_ref, *, alpha, seq):
    tm = x_ref.shape[0]
    i = pl.program_id(0)
    tile_pos = (i * tm) % seq

    l1, l4, l16 = l1_ref[...], l4_ref[...], l16_ref[...]
    mx = jnp.maximum(jnp.maximum(l1, l4), l16)
    e1, e4, e16 = jnp.exp(l1 - mx), jnp.exp(l4 - mx), jnp.exp(l16 - mx)
    inv = 1.0 / (e1 + e4 + e16)
    hrow = lax.broadcasted_iota(I32, (LANES, ATTN_WIDTH), 0)
    hcol = lax.broadcasted_iota(I32, (LANES, ATTN_WIDTH), 1) // HEAD_DIM
    spread = jnp.where(hrow == hcol, 1.0, 0.0).astype(BF16)

    def widen(w):
        hi, lo = _split_bf16(w)
        return _dot(hi, spread) + _dot(lo, spread)

    def branch(o_ref):
        lo, hi = zip(*[_unpack_pair(o_ref[w]) for w in range(o_ref.shape[0])])
        return jnp.concatenate(lo + hi, axis=1)

    attn = (widen(e1 * inv) * branch(o1_ref) + widen(e4 * inv) * branch(o4_ref)
            + widen(e16 * inv) * branch(o16_ref))

    halo_rows = halo_ref.shape[0]
    halo = jnp.where(tile_pos > 0, halo_ref[...], 0.0)
    ext = jnp.concatenate([halo, p_ref[...]], axis=0)
    pos = (tile_pos + lax.broadcasted_iota(I32, (tm, 1), 0)).astype(F32)
    gd = ext.shape[1] // len(POOL_WINDOWS)
    mixed = []
    for g, w in enumerate(POOL_WINDOWS):
        eg = ext[:, g * gd:(g + 1) * gd]
        acc, span = eg, 1
        while span < w:
            acc = acc + pltpu.roll(acc, span, 0)
            span *= 2
        count = jnp.minimum(pos + 1.0, float(w))
        pooled = acc[halo_rows:, :] / count - eg[halo_rows:, :]
        mixed.append(_dot(pooled.astype(BF16), poolw_ref[g]) * pscale_ref[:, g * gd:(g + 1) * gd])
    cat = jnp.concatenate([attn.astype(BF16)] + [m.astype(BF16) for m in mixed], axis=1)
    x1 = _layer_norm(alpha * x_ref[...] + _dot(cat, wout_ref[...]), g1_ref[...], b1_ref[...])

    qc = _dot(x1.astype(BF16), wcq_ref[...])
    mhd = qc.shape[1] // MEM_HEADS
    heads = []
    for h in range(MEM_HEADS):
        sl = slice(h * mhd, (h + 1) * mhd)
        s = _dot_nt(qc[:, sl].astype(BF16), km_ref[:, sl]) * (mhd ** -0.5)
        m = jnp.max(s, axis=1, keepdims=True)
        p = jnp.exp(s - m)
        l = jnp.sum(p, axis=1, keepdims=True)
        heads.append((_dot(p.astype(BF16), vm_ref[:, sl]) / l).astype(BF16))
    oc = jnp.concatenate(heads, axis=1)
    x2 = _layer_norm(alpha * x1 + _dot(oc, wco_ref[...]), g2_ref[...], b2_ref[...])
    dh = x2.shape[1] // 2
    x2_ref[...] = _pack_pair(x2[:, :dh], x2[:, dh:])

    xh, xl = _split_bf16(x2)
    logit_ref[...] = _dot_nt(wrh_ref[...], xh) + _dot_nt(wrh_ref[...], xl) + _dot_nt(wrl_ref[...], xh)

    gate = _dot(xh, shg_ref[...])
    up = _dot(xh, shu_ref[...])
    hid = gate / (1.0 + jnp.exp(-gate)) * up
    base_ref[...] = alpha * x2 + _dot(hid.astype(BF16), shd_ref[...])


def _row_block(x2d, o1, o4, o16, l1, l4, l16, p, pool_w_bf, pool_scale, w_out_bf, g1, b1,
               w_cq_bf, kmem, vmem, w_co_bf, g2, b2, wr_hi, wr_lo, sh_g, sh_u, sh_d,
               *, alpha, batch, seq, mem_len):
    n, d = x2d.shape
    tm = ROW_TILE
    halo_rows = max(POOL_WINDOWS)
    steps_per_seq = seq // tm
    row = lambda w: pl.BlockSpec((tm, w), lambda i: (i, 0))
    full = lambda a: pl.BlockSpec(a.shape, lambda i: (0,) * a.ndim)
    halo = pl.BlockSpec((halo_rows, p.shape[1]), lambda i: (jnp.maximum(i * (tm // halo_rows) - 1, 0), 0))
    memspec = pl.BlockSpec((mem_len, kmem.shape[1]), lambda i: (i // steps_per_seq, 0))
    n_exp = wr_hi.shape[0]
    branch = pl.BlockSpec((None, o1.shape[1], tm, LANES), lambda i: (i // steps_per_seq, 0, i % steps_per_seq, 0))
    return pl.pallas_call(
        functools.partial(_row_kernel, alpha=alpha, seq=seq),
        grid=(n // tm,),
        in_specs=[row(d), branch, branch, branch, row(LANES), row(LANES), row(LANES),
                  row(p.shape[1]), halo,
                  full(pool_w_bf), full(pool_scale), full(w_out_bf), full(g1), full(b1),
                  full(w_cq_bf), memspec, memspec, full(w_co_bf), full(g2), full(b2),
                  full(wr_hi), full(wr_lo), full(sh_g), full(sh_u), full(sh_d)],
        out_specs=[row(d // 2), row(d), pl.BlockSpec((n_exp, tm), lambda i: (0, i))],
        out_shape=[jax.ShapeDtypeStruct((n, d // 2), U32), jax.ShapeDtypeStruct((n, d), F32),
                   jax.ShapeDtypeStruct((n_exp, n), F32)],
        compiler_params=_cparams(("arbitrary",)),
        name="row_block",
    )(x2d, o1, o4, o16, l1, l4, l16, p, p, pool_w_bf, pool_scale, w_out_bf, g1, b1,
      w_cq_bf, kmem, vmem, w_co_bf, g2, b2, wr_hi, wr_lo, sh_g, sh_u, sh_d)


def _beat_counts(vals, n_rows):
    tn = vals.shape[1]
    n_tiles = n_rows // SUBLANES
    tiles = [vals[t * SUBLANES:(t + 1) * SUBLANES, :] for t in range(n_tiles)]
    sub = lax.broadcasted_iota(I32, (SUBLANES, tn), 0)
    counts = [jnp.zeros((SUBLANES, tn), F32) for _ in range(n_tiles)]
    for e in range(n_rows):
        te, je = divmod(e, SUBLANES)
        row = jnp.broadcast_to(vals[e:e + 1, :], (SUBLANES, tn))
        for t in range(n_tiles):
            strict = jnp.where(row > tiles[t], 1.0, 0.0)
            loose = jnp.where(row >= tiles[t], 1.0, 0.0)
            if t < te:
                beat = strict
            elif t > te:
                beat = loose
            else:
                beat = jnp.where(sub > je, loose, strict)
            counts[t] = counts[t] + beat
    return jnp.concatenate(counts, axis=0)


def _route_kernel(logit_ref, bias_ref, rank_ref, rankt_ref, gatet_ref, tab_ref, count_ref, carry_ref, rcarry_ref,
                  *, per_group):
    n_exp, tn = logit_ref.shape
    step = pl.program_id(0)

    @pl.when(step == 0)
    def _():
        carry_ref[...] = jnp.zeros_like(carry_ref)
        rcarry_ref[...] = jnp.zeros_like(rcarry_ref)

    scores = 1.0 / (1.0 + jnp.exp(-logit_ref[...]))
    biased = scores + bias_ref[:, 0:1]
    n_groups = n_exp // per_group

    gscore = []
    for g in range(n_groups):
        tile = biased[g * per_group:(g + 1) * per_group, :]
        inner = _beat_counts(tile, per_group)
        gscore.append(jnp.sum(jnp.where(inner < 2.0, tile, 0.0), axis=0, keepdims=True))
    gscore = jnp.concatenate(gscore, axis=0)
    grank = _beat_counts(gscore, n_groups)
    grank = jnp.concatenate(
        [jnp.broadcast_to(grank[g:g + 1, :], (per_group, tn)) for g in range(n_groups)], axis=0)
    masked = jnp.where(grank < float(TOPK_GROUPS), biased, NEG_INF)
    self32 = jnp.where(_beat_counts(masked, n_exp) < float(TOP_K), 1.0, 0.0)
    selbf = self32.astype(BF16)

    tr = lax.broadcasted_iota(I32, (tn, tn), 0)
    tc = lax.broadcasted_iota(I32, (tn, tn), 1)
    before = jnp.where(tr < tc, 1.0, 0.0).astype(BF16)
    lrank = _dot(selbf, before)
    align = float(RUN_ALIGN)
    lcount_col = jnp.broadcast_to(jnp.sum(self32, axis=1, keepdims=True), (n_exp, LANES))
    sel_wide = jnp.concatenate([selbf, jnp.zeros((LANES - n_exp, tn), BF16)], axis=0)
    lcount_row = _dot_nt(jnp.ones((SUBLANES, tn), BF16), sel_wide)
    lcount_col = jnp.ceil(lcount_col / align) * align
    lcount_row = jnp.ceil(lcount_row / align) * align
    wr = lax.broadcasted_iota(I32, (LANES, LANES), 0)
    wc = lax.broadcasted_iota(I32, (LANES, LANES), 1)
    lstart_row = _dot(lcount_row.astype(BF16), jnp.where(wr < wc, 1.0, 0.0).astype(BF16))

    sub = lax.broadcasted_iota(I32, (SUBLANES, LANES), 0)
    tab_ref[...] = jnp.where(sub == 0, lstart_row, jnp.where(sub == 1, lcount_row, jnp.where(
        sub == 2, rcarry_ref[...], 0.0))).astype(I32)
    rcarry_ref[...] = rcarry_ref[...] + lcount_row
    carry_ref[...] = carry_ref[...] + lcount_col
    count_ref[...] = carry_ref[...]

    denom = jnp.sum(self32 * scores, axis=0, keepdims=True)
    gate = self32 * scores / denom * ROUTED_SCALE
    pad = jnp.zeros((LANES - n_exp, tn), F32)
    widen = lambda v: jnp.concatenate([v, pad], axis=0)
    rank_tab = widen(jnp.where(self32 > 0.0, lrank, -1.0))
    rank_ref[...] = rank_tab.astype(BF16)
    rankt_ref[...] = rank_tab.T.astype(BF16)
    gatet_ref[...] = widen(gate).T.astype(BF16)


def _route(logits_t, bias_col):
    n_exp, n = logits_t.shape
    tn = SORT_TILE
    emaj = jax.ShapeDtypeStruct((LANES, n), BF16)
    tmaj = jax.ShapeDtypeStruct((n, LANES), BF16)
    emaj_spec = pl.BlockSpec((LANES, tn), lambda i: (0, i))
    tmaj_spec = pl.BlockSpec((tn, LANES), lambda i: (i, 0))
    return pl.pallas_call(
        functools.partial(_route_kernel, per_group=n_exp // N_EXPERT_GROUPS),
        grid=(n // tn,),
        in_specs=[pl.BlockSpec((n_exp, tn), lambda i: (0, i)), pl.BlockSpec(bias_col.shape, lambda i: (0, 0))],
        out_specs=[emaj_spec, tmaj_spec, tmaj_spec,
                   pl.BlockSpec((None, SUBLANES, LANES), lambda i: (i, 0, 0)),
                   pl.BlockSpec((n_exp, LANES), lambda i: (0, 0))],
        out_shape=[emaj, tmaj, tmaj, jax.ShapeDtypeStruct((n // tn, SUBLANES, LANES), I32),
                   jax.ShapeDtypeStruct((n_exp, LANES), F32)],
        scratch_shapes=[pltpu.VMEM((n_exp, LANES), F32), pltpu.VMEM((SUBLANES, LANES), F32)],
        compiler_params=_cparams(("arbitrary",)),
        name="route",
    )(logits_t, bias_col)


def _place_kernel(count_ref, tab_ref, runs_ref, bexp_ref, seg_ref, *, n_blocks):
    n_exp = count_ref.shape[0]
    blk = float(EXPERT_BLOCK)
    counts = count_ref[...]
    padded = jnp.ceil(counts / blk) * blk
    ridx = lax.broadcasted_iota(I32, (n_exp, LANES), 0)
    ends = padded
    shift = 1
    while shift < n_exp:
        ends = ends + jnp.where(ridx >= shift, pltpu.roll(ends, shift, 0), 0.0)
        shift *= 2
    starts = ends - padded

    tab = tab_ref[...]
    last = tab[tab.shape[0] - 1]
    total_row = (last[2:3, :] + last[1:2, :]).astype(F32)
    padded_row = jnp.ceil(total_row / blk) * blk
    lidx = lax.broadcasted_iota(I32, (1, LANES), 1)
    ends_row = padded_row
    shift = 1
    while shift < LANES:
        ends_row = ends_row + jnp.where(lidx >= shift, pltpu.roll(ends_row, shift, 1), 0.0)
        shift *= 2
    starts_row = (ends_row - padded_row).astype(I32)
    sub = lax.broadcasted_iota(I32, tab.shape, 1)
    runs_ref[...] = tab + jnp.where(sub == 2, starts_row[None], 0)

    nbp = bexp_ref.shape[1]
    blk_start = (lax.broadcasted_iota(I32, (n_exp, nbp), 1) * EXPERT_BLOCK).astype(F32)
    ends_wide = jnp.broadcast_to(ends[:, 0:1], (n_exp, nbp))
    count_le = lambda bound: jnp.minimum(
        jnp.sum(jnp.where(ends_wide <= bound, 1.0, 0.0), axis=0, keepdims=True), float(n_exp - 1))
    bexp = count_le(blk_start)
    used = ends[n_exp - 1:n_exp, 0:1] / blk
    seg_end = jnp.min(jnp.where(ends_wide > blk_start, ends_wide, 2.0 * float(nbp * EXPERT_BLOCK)),
                      axis=0, keepdims=True)
    sub8 = lax.broadcasted_iota(I32, (SUBLANES, nbp), 0)
    lane = lax.broadcasted_iota(I32, (SUBLANES, nbp), 1)
    plan = jnp.where(sub8 == 0, jnp.where(lane == n_blocks, used, bexp),
                     jnp.where(sub8 == 1, count_le(seg_end), jnp.where(sub8 == 2, seg_end / blk, 0.0)))
    bexp_ref[...] = plan.astype(I32)
    lane2 = lax.broadcasted_iota(I32, (n_exp, LANES), 1)
    seg_ref[...] = jnp.where(lane2 == 0, starts + counts, ends).astype(I32)


def _place(counts, tab, n_blocks):
    n_exp = counts.shape[0]
    nbp = -(-(n_blocks + 1) // LANES) * LANES
    return pl.pallas_call(
        functools.partial(_place_kernel, n_blocks=n_blocks),
        out_shape=[jax.ShapeDtypeStruct(tab.shape, I32), jax.ShapeDtypeStruct((SUBLANES, nbp), I32),
                   jax.ShapeDtypeStruct((n_exp, LANES), I32)],
        compiler_params=_cparams(None),
        name="route_place",
    )(counts, tab)


def _copy_rows(src_ref, dst_ref, sem, src0, dst0, n, wait=False):
    def piece(off, p):
        src = src_ref.at[pl.ds(pl.multiple_of(src0 + off, RUN_ALIGN), p), :]
        dst = dst_ref.at[pl.ds(pl.multiple_of(dst0 + off, RUN_ALIGN), p), :]
        cp = pltpu.make_async_copy(src, dst, sem)
        if wait:
            cp.wait()
        else:
            cp.start()

    def chunk(c, carry):
        piece(c * RUN_CHUNK, RUN_CHUNK)
        return carry
    lax.fori_loop(0, lax.shift_right_logical(n, RUN_CHUNK.bit_length() - 1), chunk, 0)
    p = RUN_CHUNK // 2
    while p >= RUN_ALIGN:
        @pl.when((n & p) != 0)
        def _(p=p):
            piece(n & (-2 * p), p)
        p //= 2


def _start_run(src_ref, dst_ref, sem, src0, dst0, n, limit, enable=None):
    p = RUN_ALIGN
    while p * 2 <= limit:
        p *= 2
    while p >= RUN_ALIGN:
        hit = (n & p) != 0
        @pl.when(hit if enable is None else hit & enable)
        def _(p=p):
            off = n & (-2 * p)
            src = src_ref.at[pl.ds(pl.multiple_of(src0 + off, RUN_ALIGN), p), :]
            dst = dst_ref.at[pl.ds(pl.multiple_of(dst0 + off, RUN_ALIGN), p), :]
            pltpu.make_async_copy(src, dst, sem).start()
        p //= 2


def _run_membership(tab, rows, first_row=0):
    j = lax.broadcasted_iota(I32, (rows, LANES), 0) + first_row
    start = tab[0:1, :]
    end = start + tab[1:2, :]
    return jnp.where(j >= start, jnp.where(j < end, 1.0, 0.0), 0.0)


def _dispatch_kernel(prev_runs_ref, runs_ref, seg_ref, tab_ref, rank_ref, x_ref, xs_ref, buf_ref, zero_ref, sems,
                     zsem, *, n_exp):
    step, n_steps = pl.program_id(0), pl.num_programs(0)
    tm = x_ref.shape[0]
    n_slots, rows = buf_ref.shape[:2]
    pad_rows = rows - TOP_K * tm
    cur, prev, old = step % n_slots, (step + n_slots - 1) % n_slots, (step + n_slots - 2) % n_slots

    def start_runs(tab_ref, s, experts, enable=None):
        for e in experts:
            _start_run(buf_ref.at[s], xs_ref, sems.at[s], tab_ref[0, e], tab_ref[2, e], tab_ref[1, e], tm, enable)

    def start_filler(tab_ref, s, enable=None):
        used = tab_ref[0, n_exp - 1] + tab_ref[1, n_exp - 1]
        spare = xs_ref.shape[0] - (n_slots - s) * pad_rows
        _start_run(buf_ref.at[s], xs_ref, sems.at[s], used, spare, rows - used, pad_rows, enable)

    def drain(s):
        pltpu.make_async_copy(buf_ref.at[s], xs_ref.at[pl.ds(0, rows), :], sems.at[s]).wait()

    @pl.when(step == 0)
    def _():
        zero_ref[...] = jnp.zeros_like(zero_ref)
        for wait in (False, True):
            def per_expert(e, c, wait=wait):
                first = seg_ref[e, 0]
                _copy_rows(zero_ref, xs_ref, zsem, 0, first, seg_ref[e, 1] - first, wait)
                return c
            lax.fori_loop(0, n_exp, per_expert, 0)

    lo, hi = _unpack_pair(x_ref[...])
    x = jnp.concatenate([lo.astype(BF16), hi.astype(BF16)], axis=1)
    tab = tab_ref[...]
    bits = lambda v: lax.bitcast_convert_type(v, U32)
    chunk, per = rows // DISPATCH_CHUNKS, n_exp // DISPATCH_CHUNKS
    for c in range(DISPATCH_CHUNKS):
        member = _run_membership(tab, chunk, c * chunk)
        run_start = jnp.sum(member * tab[0:1, :].astype(F32), axis=1, keepdims=True)
        in_run = (lax.broadcasted_iota(I32, (chunk, 1), 0) + c * chunk).astype(F32) - run_start
        onehot = jnp.where(_dot(member.astype(BF16), rank_ref[...]) == in_run, 1.0, 0.0)
        srt = _dot(onehot.astype(BF16), x)
        dh = srt.shape[1] // 2
        buf_ref[cur, c * chunk:(c + 1) * chunk, :] = (bits(srt[:, :dh]) >> 16) | bits(srt[:, dh:])
        start_runs(prev_runs_ref, prev, range(c * per, (c + 1) * per), step > 0)
    start_filler(prev_runs_ref, prev, step > 0)

    @pl.when(step >= 2)
    def _():
        drain(old)

    @pl.when(step == n_steps - 1)
    def _():
        start_runs(runs_ref, cur, range(n_exp))
        start_filler(runs_ref, cur)

        @pl.when(step >= 1)
        def _():
            drain(prev)
        drain(cur)


def _sorted_rows(tm, n_exp):
    return TOP_K * tm + n_exp * RUN_ALIGN


def _dispatch(x2, rank_tab, runs, seg, n_rows, n_exp):
    n, d = x2.shape
    tm = SORT_TILE
    rows = _sorted_rows(tm, n_exp)
    assert rows % DISPATCH_CHUNKS == 0 and n_exp % DISPATCH_CHUNKS == 0
    tab = (None,) + runs.shape[1:]
    return pl.pallas_call(
        functools.partial(_dispatch_kernel, n_exp=n_exp),
        grid=(n // tm,),
        in_specs=[pl.BlockSpec(tab, lambda i: (jnp.maximum(i - 1, 0), 0, 0), memory_space=pltpu.SMEM),
                  pl.BlockSpec(tab, lambda i: (i, 0, 0), memory_space=pltpu.SMEM),
                  pl.BlockSpec(seg.shape, lambda i: (0, 0), memory_space=pltpu.SMEM),
                  pl.BlockSpec(tab, lambda i: (i, 0, 0)),
                  pl.BlockSpec((LANES, tm), lambda i: (0, i)),
                  pl.BlockSpec((tm, d), lambda i: (i, 0))],
        out_specs=pl.BlockSpec(memory_space=pl.ANY),
        out_shape=jax.ShapeDtypeStruct((n_rows + DISPATCH_SLOTS * (rows - TOP_K * tm), d), x2.dtype),
        scratch_shapes=[pltpu.VMEM((DISPATCH_SLOTS, rows, d), x2.dtype), pltpu.VMEM((EXPERT_BLOCK, d), x2.dtype),
                        pltpu.SemaphoreType.DMA((DISPATCH_SLOTS,)), pltpu.SemaphoreType.DMA],
        compiler_params=_cparams(("arbitrary",)),
        name="dispatch",
    )(runs, runs, seg, runs, rank_tab, x2)


def _expert_kernel(plan_ref, xs_ref, wg_ref, wu_ref, wd_ref, ys_ref, xbuf_ref, xsems, wg_f32, wu_f32, wd_f32,
                   wsems, wslot_ref, wg_bf, wu_bf, wd_bf, *, n_blocks):
    i = pl.program_id(0)
    used = plan_ref[0, n_blocks]
    depth, blk = xbuf_ref.shape[:2]

    def fetch(j):
        rows = pl.ds(pl.multiple_of(j * blk, blk), blk)
        return pltpu.make_async_copy(xs_ref.at[rows, :], xbuf_ref.at[j % depth], xsems.at[j % depth])

    def weights(e, s):
        return [pltpu.make_async_copy(src.at[e], dst.at[s], wsems.at[s])
                for src, dst in ((wg_ref, wg_f32), (wu_ref, wu_f32), (wd_ref, wd_f32))]

    @pl.when(i == 0)
    def _():
        for j in range(depth - 1):
            @pl.when(j < used)
            def _(j=j):
                fetch(j).start()

    @pl.when(i + depth - 1 < used)
    def _():
        fetch(i + depth - 1).start()

    @pl.when((i == 0) & (used > 0))
    def _():
        wslot_ref[0] = 0
        for cp in weights(plan_ref[0, 0], 0):
            cp.start()

    @pl.when(i < used)
    def _():
        @pl.when((i == 0) | (plan_ref[0, i] != plan_ref[0, jnp.maximum(i - 1, 0)]))
        def _():
            s = wslot_ref[0]
            for cp in weights(plan_ref[0, i], s):
                cp.wait()

            @pl.when(plan_ref[2, i] < used)
            def _():
                for cp in weights(plan_ref[1, i], 1 - s):
                    cp.start()
            wslot_ref[0] = 1 - s
            wg_bf[...] = wg_f32[s].astype(BF16)
            wu_bf[...] = wu_f32[s].astype(BF16)
            wd_bf[...] = wd_f32[s].astype(BF16)

        fetch(i).wait()
        lo, hi = _unpack_pair(xbuf_ref[i % depth])
        x = jnp.concatenate([lo.astype(BF16), hi.astype(BF16)], axis=1)
        gate = _dot(x, wg_bf[...])
        up = _dot(x, wu_bf[...])
        hid = gate / (1.0 + jnp.exp(-gate)) * up
        y = _dot(hid.astype(BF16), wd_bf[...])
        dh = y.shape[1] // 2
        ys_ref[...] = _pack_pair(y[:, :dh], y[:, dh:])


def _experts(bexp, xs, exp_gate, exp_up, exp_down, n_blocks):
    n_rows = xs.shape[0]
    d, hidden = exp_gate.shape[1:]
    blk = EXPERT_BLOCK

    def rows(i, b):
        return (jnp.minimum(i, jnp.maximum(b[0, n_blocks] - 1, 0)), 0)

    hbm = pl.BlockSpec(memory_space=pl.ANY)
    grid_spec = pltpu.PrefetchScalarGridSpec(
        num_scalar_prefetch=1,
        grid=(n_blocks,),
        in_specs=[hbm, hbm, hbm, hbm],
        out_specs=pl.BlockSpec((blk, d // 2), rows),
        scratch_shapes=[pltpu.VMEM((EXPERT_RING, blk, d // 2), xs.dtype), pltpu.SemaphoreType.DMA((EXPERT_RING,)),
                        pltpu.VMEM((2, d, hidden), F32), pltpu.VMEM((2, d, hidden), F32),
                        pltpu.VMEM((2, hidden, d), F32), pltpu.SemaphoreType.DMA((2,)), pltpu.SMEM((1,), I32),
                        pltpu.VMEM((d, hidden), BF16), pltpu.VMEM((d, hidden), BF16), pltpu.VMEM((hidden, d), BF16)],
    )
    return pl.pallas_call(
        functools.partial(_expert_kernel, n_blocks=n_blocks),
        grid_spec=grid_spec,
        out_shape=jax.ShapeDtypeStruct((n_rows, d // 2), U32),
        compiler_params=_cparams(("arbitrary",)),
        name="experts",
    )(bexp, xs, exp_gate, exp_up, exp_down)


def _combine_kernel(runs_ref, next_runs_ref, ahead_runs_ref, tab_ref, rankt_ref, gatet_ref, base_ref, g3_ref, b3_ref,
                    ys_ref, out_ref, buf_ref, sems, *, n_exp):
    step, n_steps = pl.program_id(0), pl.num_programs(0)
    tm = base_ref.shape[0]
    n_slots, rows = buf_ref.shape[:2]
    slot, ahead = step % n_slots, (step + 2) % n_slots

    def fetch(tab_ref, s, experts, enable=None):
        for e in experts:
            _start_run(ys_ref, buf_ref.at[s], sems.at[s], tab_ref[2, e], tab_ref[0, e], tab_ref[1, e], tm, enable)

    def fetch_filler(tab_ref, s, enable=None):
        used = tab_ref[0, n_exp - 1] + tab_ref[1, n_exp - 1]
        _start_run(ys_ref, buf_ref.at[s], sems.at[s], 0, used, rows - used, rows - TOP_K * tm, enable)

    @pl.when(step == 0)
    def _():
        fetch(runs_ref, 0, range(n_exp))
        fetch_filler(runs_ref, 0)
        fetch(next_runs_ref, 1, range(n_exp), n_steps > 1)
        fetch_filler(next_runs_ref, 1, n_steps > 1)

    pltpu.make_async_copy(ys_ref.at[pl.ds(0, rows), :], buf_ref.at[slot], sems.at[slot]).wait()

    tab = tab_ref[...]
    start = tab[0:1, :].astype(F32)
    start_hi = jnp.floor(start / float(POS_BASE))
    sub = lax.broadcasted_iota(I32, (SUBLANES, LANES), 0)
    digits = jnp.where(sub == 0, start_hi, jnp.where(sub == 1, start - start_hi * float(POS_BASE), 0.0)).astype(BF16)
    have_ahead = step + 2 < n_steps
    chunk, per = rows // DISPATCH_CHUNKS, n_exp // DISPATCH_CHUNKS
    routed = jnp.zeros(base_ref.shape, F32)
    for c in range(DISPATCH_CHUNKS):
        member = _run_membership(tab, chunk, c * chunk).astype(BF16)
        picked = _dot_nt(digits, member)
        in_run = ((lax.broadcasted_iota(I32, (1, chunk), 1) + c * chunk).astype(F32)
                  - (picked[0:1, :] * float(POS_BASE) + picked[1:2, :]))
        w = jnp.where(_dot_nt(rankt_ref[...], member) == in_run, _dot_nt(gatet_ref[...], member), 0.0)
        lo, hi = _unpack_pair(buf_ref[slot, c * chunk:(c + 1) * chunk, :])
        y = jnp.concatenate([lo.astype(BF16), hi.astype(BF16)], axis=1)
        routed = routed + _dot(w.astype(BF16), y)
        fetch(ahead_runs_ref, ahead, range(c * per, (c + 1) * per), have_ahead)
    fetch_filler(ahead_runs_ref, ahead, have_ahead)
    out_ref[...] = _layer_norm(base_ref[...] + routed, g3_ref[...], b3_ref[...])


def _combine(runs, rank_t, gate_t, base, g3, b3, ys, n_exp):
    n, d = base.shape
    tm = SORT_TILE
    last = n // tm - 1
    tab = lambda f: pl.BlockSpec((None,) + runs.shape[1:], f, memory_space=pltpu.SMEM)
    tmaj = pl.BlockSpec((tm, LANES), lambda i: (i, 0))
    return pl.pallas_call(
        functools.partial(_combine_kernel, n_exp=n_exp),
        grid=(n // tm,),
        in_specs=[tab(lambda i: (i, 0, 0)), tab(lambda i: (jnp.minimum(i + 1, last), 0, 0)),
                  tab(lambda i: (jnp.minimum(i + 2, last), 0, 0)),
                  pl.BlockSpec((None,) + runs.shape[1:], lambda i: (i, 0, 0)),
                  tmaj, tmaj,
                  pl.BlockSpec((tm, d), lambda i: (i, 0)),
                  pl.BlockSpec(g3.shape, lambda i: (0, 0)),
                  pl.BlockSpec(b3.shape, lambda i: (0, 0)),
                  pl.BlockSpec(memory_space=pl.ANY)],
        out_specs=pl.BlockSpec((tm, d), lambda i: (i, 0)),
        out_shape=jax.ShapeDtypeStruct((n, d), F32),
        scratch_shapes=[pltpu.VMEM((DISPATCH_SLOTS, _sorted_rows(tm, n_exp), ys.shape[1]), ys.dtype),
                        pltpu.SemaphoreType.DMA((DISPATCH_SLOTS,))],
        compiler_params=_cparams(("arbitrary",)),
        name="combine",
    )(runs, runs, runs, runs, rank_t, gate_t, base, g3, b3, ys)


def _layer(x2d, mem2d, w_in, pool_w, pool_scale, w_out, ln1_g, ln1_b, w_cq, w_ck, w_cv, w_co, ln2_g, ln2_b,
           w_router, router_bias, exp_gate, exp_up, exp_down, sh_gate, sh_up, sh_down, ln3_g, ln3_b,
           *, alpha, batch, seq, mem_len):
    n, d = x2d.shape
    n_exp = w_router.shape[1]
    row = lambda a: a.reshape(1, -1)
    bf = lambda a: a.astype(BF16)

    dils = tuple(dil for _, dil in DILATED_PATTERNS)
    *qkv, p = _in_proj(x2d, bf(w_in), dils)
    branches = []
    for di, (window, dil) in enumerate(DILATED_PATTERNS):
        assert window // dil == ATTN_BLOCK and seq % (dil * ATTN_BLOCK * ATTN_STEP_BLOCKS) == 0
        assert IN_TILE % (dil * 2 * SUBLANES) == 0
        branches.append(_dil_attn(*qkv[3 * di:3 * di + 3], batch, seq, dil))
    (o1, l1), (o4, l4), (o16, l16) = branches
    kmem, vmem = _mem_kv(mem2d, bf(w_ck), bf(w_cv))

    wr_t = w_router.T
    wr_hi = bf(wr_t)
    wr_lo = bf(wr_t - wr_hi.astype(F32))
    x2, base, logits_t = _row_block(
        x2d, o1, o4, o16, l1, l4, l16, p, bf(pool_w), row(pool_scale), bf(w_out), row(ln1_g), row(ln1_b),
        bf(w_cq), kmem, vmem, bf(w_co), row(ln2_g), row(ln2_b), wr_hi, wr_lo, bf(sh_gate), bf(sh_up), bf(sh_down),
        alpha=alpha, batch=batch, seq=seq, mem_len=mem_len)

    bias_col = jnp.broadcast_to(router_bias.reshape(n_exp, 1), (n_exp, LANES))
    rank_tab, rank_t, gate_t, tab, counts = _route(logits_t, bias_col)
    max_rows = n * TOP_K + (n // SORT_TILE) * n_exp * (RUN_ALIGN - 1)
    n_blocks = -(-max_rows // EXPERT_BLOCK) + n_exp
    runs, bexp, seg = _place(counts, tab, n_blocks)
    xs = _dispatch(x2, rank_tab, runs, seg, n_blocks * EXPERT_BLOCK, n_exp)
    ys = _experts(bexp, xs, exp_gate, exp_up, exp_down, n_blocks)
    return _combine(runs, rank_t, gate_t, base, row(ln3_g), row(ln3_b), ys, n_exp)


def kernel(x, mem, w_in, pool_w, pool_scale, w_out, ln1_g, ln1_b, w_cq, w_ck, w_cv, w_co, ln2_g, ln2_b, w_router, router_bias, exp_gate, exp_up, exp_down, sh_gate, sh_up, sh_down, ln3_g, ln3_b):
    batch, seq, d = x.shape
    mem_len = mem.shape[1]
    depth = w_in.shape[0]
    alpha = (2.0 * depth) ** 0.25
    x2d = x.reshape(batch * seq, d)
    mem2d = mem.reshape(batch * mem_len, d)
    for l in range(depth):
        x2d = _layer(x2d, mem2d, w_in[l], pool_w[l], pool_scale[l], w_out[l], ln1_g[l], ln1_b[l], w_cq[l], w_ck[l],
                     w_cv[l], w_co[l], ln2_g[l], ln2_b[l], w_router[l], router_bias[l], exp_gate[l], exp_up[l],
                     exp_down[l], sh_gate[l], sh_up[l], sh_down[l], ln3_g[l], ln3_b[l],
                     alpha=alpha, batch=batch, seq=seq, mem_len=mem_len)
    return x2d.reshape(batch, seq, d)
```

```python
import functools

import jax
import jax.numpy as jnp
from jax import lax
from jax.experimental import pallas as pl
from jax.experimental.pallas import tpu as pltpu

F32 = jnp.float32
BF16 = jnp.bfloat16
I32 = jnp.int32
U32 = jnp.uint32

ATTN_HEADS = 8
HEAD_DIM = 64
ATTN_WIDTH = ATTN_HEADS * HEAD_DIM
DILATED_PATTERNS = ((128, 1), (512, 4), (2048, 16))
POOL_WINDOWS = (2, 4, 8, 16)
MEM_HEADS = 4
N_EXPERT_GROUPS = 8
TOPK_GROUPS = 4
TOP_K = 8
ROUTED_SCALE = 2.5
LN_EPS = 1e-5
NEG_INF = -1e30

LANES = 128
SUBLANES = 8
VMEM_LIMIT = 56 * 1024 * 1024

ATTN_BLOCK = 128
ATTN_STEP_BLOCKS = 8
ROW_TILE = 512
IN_TILE = 512
SORT_TILE = 256
EXPERT_BLOCK = 512
EXPERT_RING = 3
RUN_ALIGN = SUBLANES
RUN_CHUNK = 64
POS_BASE = 64
DISPATCH_SLOTS = 3
DISPATCH_CHUNKS = 4


def _cparams(sem, vmem=VMEM_LIMIT):
    return pltpu.CompilerParams(dimension_semantics=sem, vmem_limit_bytes=vmem)


def _layer_norm(y, g, b):
    mu = jnp.mean(y, axis=-1, keepdims=True)
    d = y - mu
    var = jnp.mean(d * d, axis=-1, keepdims=True)
    return d * lax.rsqrt(var + LN_EPS) * g + b


def _dot(a, b):
    return jnp.dot(a, b, preferred_element_type=F32)


def _dot_nt(a, b):
    return lax.dot_general(a, b, (((1,), (1,)), ((), ())), preferred_element_type=F32)


def _split_bf16(x):
    hi = x.astype(BF16)
    lo = (x - hi.astype(F32)).astype(BF16)
    return hi, lo


def _bf16_bits(x):
    return lax.bitcast_convert_type(x.astype(BF16).astype(F32), U32)


def _pack_pair(lo, hi):
    return (_bf16_bits(lo) >> 16) | (_bf16_bits(hi) & jnp.uint32(0xFFFF0000))


def _unpack_pair(w):
    lo = lax.bitcast_convert_type(w << 16, F32)
    hi = lax.bitcast_convert_type(w & jnp.uint32(0xFFFF0000), F32)
    return lo, hi


def _in_proj_kernel(x_ref, w_ref, *refs, dils):
    n_qkv = 3 * len(dils)
    out_refs, p_ref, slab_ref = refs[:n_qkv], refs[n_qkv], refs[n_qkv + 1]
    tm = x_ref.shape[0]
    x = x_ref[...].astype(BF16)
    aw = ATTN_WIDTH
    n_slabs = aw // LANES
    for a in range(3):
        val = _dot(x, w_ref[:, a * aw:(a + 1) * aw])
        if a == 0:
            val = val * (HEAD_DIM ** -0.5)
        for s in range(n_slabs):
            slab_ref[s] = val[:, s * LANES:(s + 1) * LANES]
        for di, dil in enumerate(dils):
            dst = out_refs[3 * di + a]
            if dil == 1:
                dst[...] = val.astype(BF16)
                continue
            for r in range(dil):
                for s in range(n_slabs):
                    rows = slab_ref[s, pl.ds(r, tm // dil, stride=dil), :]
                    dst[:, r * aw + s * LANES:r * aw + (s + 1) * LANES] = rows.astype(BF16)
    p_ref[...] = _dot(x, w_ref[:, 3 * aw:])


def _in_proj(x2d, w_in_bf, dils):
    n, d = x2d.shape
    aw = ATTN_WIDTH
    pw = w_in_bf.shape[1] - 3 * aw
    tm = IN_TILE
    row = lambda rows, w: pl.BlockSpec((rows, w), lambda i: (i, 0))
    qkv_specs, qkv_shapes = [], []
    for dil in dils:
        qkv_specs += [row(tm // dil, dil * aw)] * 3
        qkv_shapes += [jax.ShapeDtypeStruct((n // dil, dil * aw), BF16)] * 3
    return pl.pallas_call(
        functools.partial(_in_proj_kernel, dils=dils),
        grid=(n // tm,),
        in_specs=[row(tm, d), pl.BlockSpec(w_in_bf.shape, lambda i: (0, 0))],
        out_specs=qkv_specs + [row(tm, pw)],
        out_shape=qkv_shapes + [jax.ShapeDtypeStruct((n, pw), F32)],
        scratch_shapes=[pltpu.VMEM((aw // LANES, tm, LANES), F32)],
        compiler_params=_cparams(("arbitrary",)),
        name="in_proj",
    )(x2d, w_in_bf)


def _dil_attn_kernel(q_ref, kc_ref, kp_ref, vc_ref, vp_ref, o_ref, lse_ref, bias_ref, *, dil):
    blk = ATTN_BLOCK
    first_step = (pl.program_id(0) == 0) & (pl.program_id(1) == 0) & (pl.program_id(2) == 0)

    @pl.when(first_step)
    def _():
        qi = lax.broadcasted_iota(I32, (blk, 2 * blk), 0)
        kj = lax.broadcasted_iota(I32, (blk, 2 * blk), 1)
        dist = qi + blk - kj
        inside = (dist >= 0) & (dist <= blk)
        penalty = (dist * dil).astype(F32)
        for h in range(ATTN_HEADS):
            slope = 2.0 ** (-8.0 * (h + 1) / ATTN_HEADS)
            bias_ref[0, h] = jnp.where(inside, -slope * penalty, NEG_INF)
            bias_ref[1, h] = jnp.where(inside & (kj >= blk), -slope * penalty, NEG_INF)

    i = pl.program_id(1)
    r = pl.program_id(2)
    lane = lax.broadcasted_iota(I32, (blk, LANES), 1)
    low_half = lane < HEAD_DIM
    n_pairs = ATTN_WIDTH // LANES
    half = n_pairs // 2
    for j in range(q_ref.shape[0] // blk):
        table = jnp.where(i == 0, 1, 0) if j == 0 else 0
        lse_tile = jnp.zeros((blk, LANES), F32)
        pair_out = []
        for hp in range(n_pairs):
            sl = slice(hp * LANES, (hp + 1) * LANES)
            q2 = q_ref[j * blk:(j + 1) * blk, sl]
            if j == 0:
                kcat = jnp.concatenate([kp_ref[:, sl], kc_ref[0:blk, sl]], axis=0)
                vcat = jnp.concatenate([vp_ref[:, sl], vc_ref[0:blk, sl]], axis=0)
            else:
                kcat = kc_ref[(j - 1) * blk:(j + 1) * blk, sl]
                vcat = vc_ref[(j - 1) * blk:(j + 1) * blk, sl]
            outs = []
            for e in range(LANES // HEAD_DIM):
                h = hp * (LANES // HEAD_DIM) + e
                keep = low_half if e == 0 else jnp.logical_not(low_half)
                qe = jnp.where(keep, q2, jnp.zeros_like(q2))
                s = _dot_nt(qe, kcat) + bias_ref[table, h]
                m = jnp.max(s, axis=1, keepdims=True)
                p = jnp.exp(s - m)
                l = jnp.sum(p, axis=1, keepdims=True)
                outs.append(_dot(p.astype(BF16), vcat) / l)
                lse_tile = jnp.where(lane == h, m + jnp.log(l), lse_tile)
            pair_out.append(jnp.where(low_half, outs[0], outs[1]))
        rows = pl.ds(j * blk * dil + r, blk, stride=dil) if dil > 1 else slice(j * blk, (j + 1) * blk)
        for w in range(half):
            o_ref[w, rows, :] = _pack_pair(pair_out[w], pair_out[w + half])
        lse_ref[rows, :] = lse_tile


def _dil_attn(q, k, v, batch, seq, dil):
    blk = ATTN_BLOCK
    sub = seq // dil
    step_blocks = min(ATTN_STEP_BLOCKS, sub // blk)
    step_rows = step_blocks * blk
    assert sub % step_rows == 0
    view = lambda a: a.reshape(batch, sub, dil * ATTN_WIDTH)
    cur = pl.BlockSpec((None, step_rows, ATTN_WIDTH), lambda b, i, r: (b, i, r))
    prev = pl.BlockSpec((None, blk, ATTN_WIDTH), lambda b, i, r: (b, jnp.maximum(i * step_blocks - 1, 0), r))
    half = ATTN_WIDTH // LANES // 2
    o, lse = pl.pallas_call(
        functools.partial(_dil_attn_kernel, dil=dil),
        grid=(batch, sub // step_rows, dil),
        in_specs=[cur, cur, prev, cur, prev],
        out_specs=[pl.BlockSpec((None, half, step_rows * dil, LANES), lambda b, i, r: (b, 0, i, 0)),
                   pl.BlockSpec((None, step_rows * dil, LANES), lambda b, i, r: (b, i, 0))],
        out_shape=[jax.ShapeDtypeStruct((batch, half, seq, LANES), U32),
                   jax.ShapeDtypeStruct((batch, seq, LANES), F32)],
        scratch_shapes=[pltpu.VMEM((2, ATTN_HEADS, blk, 2 * blk), F32)],
        compiler_params=_cparams(("arbitrary",) * 3),
        name=f"dil_attn_d{dil}",
    )(view(q), view(k), view(k), view(v), view(v))
    return o, lse.reshape(batch * seq, LANES)


def _mem_kv_kernel(mem_ref, wk_ref, wv_ref, k_ref, v_ref):
    m = mem_ref[...].astype(BF16)
    k_ref[...] = _dot(m, wk_ref[...]).astype(BF16)
    v_ref[...] = _dot(m, wv_ref[...]).astype(BF16)


def _mem_kv(mem2d, w_ck_bf, w_cv_bf):
    rows = mem2d.shape[0]
    width = w_ck_bf.shape[1]
    return pl.pallas_call(
        _mem_kv_kernel,
        out_shape=[jax.ShapeDtypeStruct((rows, width), BF16)] * 2,
        compiler_params=_cparams(None),
        name="mem_kv",
    )(mem2d, w_ck_bf, w_cv_bf)


def _row_kernel(x_ref, o1_ref, o4_ref, o16_ref, l1_ref, l4_ref, l16_ref, p_ref, halo_ref,
                poolw_ref, pscale_ref, wout_ref, g1_ref, b1_ref,
                wcq_ref, km_ref, vm_ref, wco_ref, g2_ref, b2_ref,
                wrh_ref, wrl_ref, shg_ref, shu_ref, shd_ref,
                x2_ref, base_ref, logit_ref, *, alpha, seq):
    tm = x_ref.shape[0]
    i = pl.program_id(0)
    tile_pos = (i * tm) % seq

    l1, l4, l16 = l1_ref[...], l4_ref[...], l16_ref[...]
    mx = jnp.maximum(jnp.maximum(l1, l4), l16)
    e1, e4, e16 = jnp.exp(l1 - mx), jnp.exp(l4 - mx), jnp.exp(l16 - mx)
    inv = 1.0 / (e1 + e4 + e16)
    hrow = lax.broadcasted_iota(I32, (LANES, ATTN_WIDTH), 0)
    hcol = lax.broadcasted_iota(I32, (LANES, ATTN_WIDTH), 1) // HEAD_DIM
    spread = jnp.where(hrow == hcol, 1.0, 0.0).astype(BF16)

    def widen(w):
        hi, lo = _split_bf16(w)
        return _dot(hi, spread) + _dot(lo, spread)

    def branch(o_ref):
        lo, hi = zip(*[_unpack_pair(o_ref[w]) for w in range(o_ref.shape[0])])
        return jnp.concatenate(lo + hi, axis=1)

    last = branch(o16_ref)
    attn = last + widen(e1 * inv) * (branch(o1_ref) - last) + widen(e4 * inv) * (branch(o4_ref) - last)

    halo_rows = halo_ref.shape[0]
    halo = jnp.where(tile_pos > 0, halo_ref[...], 0.0)
    ext = jnp.concatenate([halo, p_ref[...]], axis=0)
    pos = (tile_pos + lax.broadcasted_iota(I32, (tm, 1), 0)).astype(F32)
    gd = ext.shape[1] // len(POOL_WINDOWS)
    mixed = []
    for g, w in enumerate(POOL_WINDOWS):
        eg = ext[:, g * gd:(g + 1) * gd]
        acc, span = eg, 1
        while span < w:
            acc = acc + pltpu.roll(acc, span, 0)
            span *= 2
        count = jnp.minimum(pos + 1.0, float(w))
        pooled = acc[halo_rows:, :] / count - eg[halo_rows:, :]
        mixed.append(_dot(pooled.astype(BF16), poolw_ref[g]) * pscale_ref[:, g * gd:(g + 1) * gd])
    cat = jnp.concatenate([attn.astype(BF16)] + [m.astype(BF16) for m in mixed], axis=1)
    x1 = _layer_norm(alpha * x_ref[...] + _dot(cat, wout_ref[...]), g1_ref[...], b1_ref[...])

    qc = _dot(x1.astype(BF16), wcq_ref[...])
    mhd = qc.shape[1] // MEM_HEADS
    heads = []
    for h in range(MEM_HEADS):
        sl = slice(h * mhd, (h + 1) * mhd)
        s = _dot_nt(qc[:, sl].astype(BF16), km_ref[:, sl]) * (mhd ** -0.5)
        m = jnp.max(s, axis=1, keepdims=True)
        p = jnp.exp(s - m)
        l = jnp.sum(p, axis=1, keepdims=True)
        heads.append((_dot(p.astype(BF16), vm_ref[:, sl]) / l).astype(BF16))
    oc = jnp.concatenate(heads, axis=1)
    x2 = _layer_norm(alpha * x1 + _dot(oc, wco_ref[...]), g2_ref[...], b2_ref[...])
    dh = x2.shape[1] // 2
    x2_ref[...] = _pack_pair(x2[:, :dh], x2[:, dh:])

    xh, xl = _split_bf16(x2)
    logit_ref[...] = _dot_nt(wrh_ref[...], xh) + _dot_nt(wrh_ref[...], xl) + _dot_nt(wrl_ref[...], xh)

    gate = _dot(xh, shg_ref[...])
    up = _dot(xh, shu_ref[...])
    hid = gate / (1.0 + jnp.exp(-gate)) * up
    base_ref[...] = alpha * x2 + _dot(hid.astype(BF16), shd_ref[...])


def _row_block(x2d, o1, o4, o16, l1, l4, l16, p, pool_w_bf, pool_scale, w_out_bf, g1, b1,
               w_cq_bf, kmem, vmem, w_co_bf, g2, b2, wr_hi, wr_lo, sh_g, sh_u, sh_d,
               *, alpha, batch, seq, mem_len):
    n, d = x2d.shape
    tm = ROW_TILE
    halo_rows = max(POOL_WINDOWS)
    steps_per_seq = seq // tm
    row = lambda w: pl.BlockSpec((tm, w), lambda i: (i, 0))
    full = lambda a: pl.BlockSpec(a.shape, lambda i: (0,) * a.ndim)
    halo = pl.BlockSpec((halo_rows, p.shape[1]), lambda i: (jnp.maximum(i * (tm // halo_rows) - 1, 0), 0))
    memspec = pl.BlockSpec((mem_len, kmem.shape[1]), lambda i: (i // steps_per_seq, 0))
    n_exp = wr_hi.shape[0]
    branch = pl.BlockSpec((None, o1.shape[1], tm, LANES), lambda i: (i // steps_per_seq, 0, i % steps_per_seq, 0))
    return pl.pallas_call(
        functools.partial(_row_kernel, alpha=alpha, seq=seq),
        grid=(n // tm,),
        in_specs=[row(d), branch, branch, branch, row(LANES), row(LANES), row(LANES),
                  row(p.shape[1]), halo,
                  full(pool_w_bf), full(pool_scale), full(w_out_bf), full(g1), full(b1),
                  full(w_cq_bf), memspec, memspec, full(w_co_bf), full(g2), full(b2),
                  full(wr_hi), full(wr_lo), full(sh_g), full(sh_u), full(sh_d)],
        out_specs=[row(d // 2), row(d), pl.BlockSpec((n_exp, tm), lambda i: (0, i))],
        out_shape=[jax.ShapeDtypeStruct((n, d // 2), U32), jax.ShapeDtypeStruct((n, d), F32),
                   jax.ShapeDtypeStruct((n_exp, n), F32)],
        compiler_params=_cparams(("arbitrary",)),
        name="row_block",
    )(x2d, o1, o4, o16, l1, l4, l16, p, p, pool_w_bf, pool_scale, w_out_bf, g1, b1,
      w_cq_bf, kmem, vmem, w_co_bf, g2, b2, wr_hi, wr_lo, sh_g, sh_u, sh_d)


def _beat_counts(vals, n_rows):
    tn = vals.shape[1]
    n_tiles = n_rows // SUBLANES
    tiles = [vals[t * SUBLANES:(t + 1) * SUBLANES, :] for t in range(n_tiles)]
    sub = lax.broadcasted_iota(I32, (SUBLANES, tn), 0)
    counts = [jnp.zeros((SUBLANES, tn), F32) for _ in range(n_tiles)]
    for e in range(n_rows):
        te, je = divmod(e, SUBLANES)
        row = jnp.broadcast_to(vals[e:e + 1, :], (SUBLANES, tn))
        for t in range(n_tiles):
            strict = jnp.where(row > tiles[t], 1.0, 0.0)
            loose = jnp.where(row >= tiles[t], 1.0, 0.0)
            if t < te:
                beat = strict
            elif t > te:
                beat = loose
            else:
                beat = jnp.where(sub > je, loose, strict)
            counts[t] = counts[t] + beat
    return jnp.concatenate(counts, axis=0)


def _route_kernel(logit_ref, bias_ref, rank_ref, rankt_ref, gatet_ref, tab_ref, count_ref, carry_ref, rcarry_ref,
                  *, per_group):
    n_exp, tn = logit_ref.shape
    step = pl.program_id(0)

    @pl.when(step == 0)
    def _():
        carry_ref[...] = jnp.zeros_like(carry_ref)
        rcarry_ref[...] = jnp.zeros_like(rcarry_ref)

    scores = 1.0 / (1.0 + jnp.exp(-logit_ref[...]))
    biased = scores + bias_ref[:, 0:1]
    n_groups = n_exp // per_group

    gscore = []
    for g in range(n_groups):
        tile = biased[g * per_group:(g + 1) * per_group, :]
        inner = _beat_counts(tile, per_group)
        gscore.append(jnp.sum(jnp.where(inner < 2.0, tile, 0.0), axis=0, keepdims=True))
    gscore = jnp.concatenate(gscore, axis=0)
    grank = _beat_counts(gscore, n_groups)
    grank = jnp.concatenate(
        [jnp.broadcast_to(grank[g:g + 1, :], (per_group, tn)) for g in range(n_groups)], axis=0)
    masked = jnp.where(grank < float(TOPK_GROUPS), biased, NEG_INF)
    self32 = jnp.where(_beat_counts(masked, n_exp) < float(TOP_K), 1.0, 0.0)
    selbf = self32.astype(BF16)

    tr = lax.broadcasted_iota(I32, (tn, tn), 0)
    tc = lax.broadcasted_iota(I32, (tn, tn), 1)
    before = jnp.where(tr < tc, 1.0, 0.0).astype(BF16)
    lrank = _dot(selbf, before)
    align = float(RUN_ALIGN)
    lcount_col = jnp.broadcast_to(jnp.sum(self32, axis=1, keepdims=True), (n_exp, LANES))
    sel_wide = jnp.concatenate([selbf, jnp.zeros((LANES - n_exp, tn), BF16)], axis=0)
    lcount_row = _dot_nt(jnp.ones((SUBLANES, tn), BF16), sel_wide)
    lcount_col = jnp.ceil(lcount_col / align) * align
    lcount_row = jnp.ceil(lcount_row / align) * align
    wr = lax.broadcasted_iota(I32, (LANES, LANES), 0)
    wc = lax.broadcasted_iota(I32, (LANES, LANES), 1)
    lstart_row = _dot(lcount_row.astype(BF16), jnp.where(wr < wc, 1.0, 0.0).astype(BF16))

    sub = lax.broadcasted_iota(I32, (SUBLANES, LANES), 0)
    tab_ref[...] = jnp.where(sub == 0, lstart_row, jnp.where(sub == 1, lcount_row, jnp.where(
        sub == 2, rcarry_ref[...], 0.0))).astype(I32)
    rcarry_ref[...] = rcarry_ref[...] + lcount_row
    carry_ref[...] = carry_ref[...] + lcount_col
    count_ref[...] = carry_ref[...]

    denom = jnp.sum(self32 * scores, axis=0, keepdims=True)
    gate = self32 * scores / denom * ROUTED_SCALE
    pad = jnp.zeros((LANES - n_exp, tn), F32)
    widen = lambda v: jnp.concatenate([v, pad], axis=0)
    rank_tab = widen(jnp.where(self32 > 0.0, lrank, -1.0))
    rank_ref[...] = rank_tab.astype(BF16)
    rankt_ref[...] = rank_tab.T.astype(BF16)
    gatet_ref[...] = widen(gate).T.astype(BF16)


def _route(logits_t, bias_col):
    n_exp, n = logits_t.shape
    tn = SORT_TILE
    emaj = jax.ShapeDtypeStruct((LANES, n), BF16)
    tmaj = jax.ShapeDtypeStruct((n, LANES), BF16)
    emaj_spec = pl.BlockSpec((LANES, tn), lambda i: (0, i))
    tmaj_spec = pl.BlockSpec((tn, LANES), lambda i: (i, 0))
    return pl.pallas_call(
        functools.partial(_route_kernel, per_group=n_exp // N_EXPERT_GROUPS),
        grid=(n // tn,),
        in_specs=[pl.BlockSpec((n_exp, tn), lambda i: (0, i)), pl.BlockSpec(bias_col.shape, lambda i: (0, 0))],
        out_specs=[emaj_spec, tmaj_spec, tmaj_spec,
                   pl.BlockSpec((None, SUBLANES, LANES), lambda i: (i, 0, 0)),
                   pl.BlockSpec((n_exp, LANES), lambda i: (0, 0))],
        out_shape=[emaj, tmaj, tmaj, jax.ShapeDtypeStruct((n // tn, SUBLANES, LANES), I32),
                   jax.ShapeDtypeStruct((n_exp, LANES), F32)],
        scratch_shapes=[pltpu.VMEM((n_exp, LANES), F32), pltpu.VMEM((SUBLANES, LANES), F32)],
        compiler_params=_cparams(("arbitrary",)),
        name="route",
    )(logits_t, bias_col)


def _place_kernel(count_ref, tab_ref, runs_ref, bexp_ref, seg_ref, *, n_blocks):
    n_exp = count_ref.shape[0]
    blk = float(EXPERT_BLOCK)
    counts = count_ref[...]
    padded = jnp.ceil(counts / blk) * blk
    ridx = lax.broadcasted_iota(I32, (n_exp, LANES), 0)
    ends = padded
    shift = 1
    while shift < n_exp:
        ends = ends + jnp.where(ridx >= shift, pltpu.roll(ends, shift, 0), 0.0)
        shift *= 2
    starts = ends - padded

    tab = tab_ref[...]
    last = tab[tab.shape[0] - 1]
    total_row = (last[2:3, :] + last[1:2, :]).astype(F32)
    padded_row = jnp.ceil(total_row / blk) * blk
    lidx = lax.broadcasted_iota(I32, (1, LANES), 1)
    ends_row = padded_row
    shift = 1
    while shift < LANES:
        ends_row = ends_row + jnp.where(lidx >= shift, pltpu.roll(ends_row, shift, 1), 0.0)
        shift *= 2
    starts_row = (ends_row - padded_row).astype(I32)
    sub = lax.broadcasted_iota(I32, tab.shape, 1)
    runs_ref[...] = tab + jnp.where(sub == 2, starts_row[None], 0)

    nbp = bexp_ref.shape[1]
    blk_start = (lax.broadcasted_iota(I32, (n_exp, nbp), 1) * EXPERT_BLOCK).astype(F32)
    ends_wide = jnp.broadcast_to(ends[:, 0:1], (n_exp, nbp))
    count_le = lambda bound: jnp.minimum(
        jnp.sum(jnp.where(ends_wide <= bound, 1.0, 0.0), axis=0, keepdims=True), float(n_exp - 1))
    bexp = count_le(blk_start)
    used = ends[n_exp - 1:n_exp, 0:1] / blk
    seg_end = jnp.min(jnp.where(ends_wide > blk_start, ends_wide, 2.0 * float(nbp * EXPERT_BLOCK)),
                      axis=0, keepdims=True)
    sub8 = lax.broadcasted_iota(I32, (SUBLANES, nbp), 0)
    lane = lax.broadcasted_iota(I32, (SUBLANES, nbp), 1)
    plan = jnp.where(sub8 == 0, jnp.where(lane == n_blocks, used, bexp),
                     jnp.where(sub8 == 1, count_le(seg_end), jnp.where(sub8 == 2, seg_end / blk, 0.0)))
    bexp_ref[...] = plan.astype(I32)
    lane2 = lax.broadcasted_iota(I32, (n_exp, LANES), 1)
    seg_ref[...] = jnp.where(lane2 == 0, starts + counts, ends).astype(I32)


def _place(counts, tab, n_blocks):
    n_exp = counts.shape[0]
    nbp = -(-(n_blocks + 1) // LANES) * LANES
    return pl.pallas_call(
        functools.partial(_place_kernel, n_blocks=n_blocks),
        out_shape=[jax.ShapeDtypeStruct(tab.shape, I32), jax.ShapeDtypeStruct((SUBLANES, nbp), I32),
                   jax.ShapeDtypeStruct((n_exp, LANES), I32)],
        compiler_params=_cparams(None),
        name="route_place",
    )(counts, tab)


def _copy_rows(src_ref, dst_ref, sem, src0, dst0, n, wait=False):
    def piece(off, p):
        src = src_ref.at[pl.ds(pl.multiple_of(src0 + off, RUN_ALIGN), p), :]
        dst = dst_ref.at[pl.ds(pl.multiple_of(dst0 + off, RUN_ALIGN), p), :]
        cp = pltpu.make_async_copy(src, dst, sem)
        if wait:
            cp.wait()
        else:
            cp.start()

    def chunk(c, carry):
        piece(c * RUN_CHUNK, RUN_CHUNK)
        return carry
    lax.fori_loop(0, lax.shift_right_logical(n, RUN_CHUNK.bit_length() - 1), chunk, 0)
    p = RUN_CHUNK // 2
    while p >= RUN_ALIGN:
        @pl.when((n & p) != 0)
        def _(p=p):
            piece(n & (-2 * p), p)
        p //= 2


def _start_run(src_ref, dst_ref, sem, src0, dst0, n, limit, enable=None):
    p = RUN_ALIGN
    while p * 2 <= limit:
        p *= 2
    while p >= RUN_ALIGN:
        hit = (n & p) != 0
        @pl.when(hit if enable is None else hit & enable)
        def _(p=p):
            off = n & (-2 * p)
            src = src_ref.at[pl.ds(pl.multiple_of(src0 + off, RUN_ALIGN), p), :]
            dst = dst_ref.at[pl.ds(pl.multiple_of(dst0 + off, RUN_ALIGN), p), :]
            pltpu.make_async_copy(src, dst, sem).start()
        p //= 2


def _run_membership(tab, rows, first_row=0):
    j = lax.broadcasted_iota(I32, (rows, LANES), 0) + first_row
    start = tab[0:1, :]
    end = start + tab[1:2, :]
    return jnp.where(j >= start, jnp.where(j < end, 1.0, 0.0), 0.0)


def _dispatch_kernel(prev_runs_ref, runs_ref, seg_ref, tab_ref, rank_ref, x_ref, xs_ref, buf_ref, zero_ref, sems,
                     zsem, *, n_exp):
    step, n_steps = pl.program_id(0), pl.num_programs(0)
    tm = x_ref.shape[0]
    n_slots, rows = buf_ref.shape[:2]
    pad_rows = rows - TOP_K * tm
    cur, prev, old = step % n_slots, (step + n_slots - 1) % n_slots, (step + n_slots - 2) % n_slots

    def start_runs(tab_ref, s, experts, enable=None):
        for e in experts:
            _start_run(buf_ref.at[s], xs_ref, sems.at[s], tab_ref[0, e], tab_ref[2, e], tab_ref[1, e], tm, enable)

    def start_filler(tab_ref, s, enable=None):
        used = tab_ref[0, n_exp - 1] + tab_ref[1, n_exp - 1]
        spare = xs_ref.shape[0] - (n_slots - s) * pad_rows
        _start_run(buf_ref.at[s], xs_ref, sems.at[s], used, spare, rows - used, pad_rows, enable)

    def drain(s):
        pltpu.make_async_copy(buf_ref.at[s], xs_ref.at[pl.ds(0, rows), :], sems.at[s]).wait()

    @pl.when(step == 0)
    def _():
        zero_ref[...] = jnp.zeros_like(zero_ref)
        for wait in (False, True):
            def per_expert(e, c, wait=wait):
                first = seg_ref[e, 0]
                _copy_rows(zero_ref, xs_ref, zsem, 0, first, seg_ref[e, 1] - first, wait)
                return c
            lax.fori_loop(0, n_exp, per_expert, 0)

    lo, hi = _unpack_pair(x_ref[...])
    x = jnp.concatenate([lo.astype(BF16), hi.astype(BF16)], axis=1)
    tab = tab_ref[...]
    bits = lambda v: lax.bitcast_convert_type(v, U32)
    chunk, per = rows // DISPATCH_CHUNKS, n_exp // DISPATCH_CHUNKS
    for c in range(DISPATCH_CHUNKS):
        member = _run_membership(tab, chunk, c * chunk)
        run_start = jnp.sum(member * tab[0:1, :].astype(F32), axis=1, keepdims=True)
        in_run = (lax.broadcasted_iota(I32, (chunk, 1), 0) + c * chunk).astype(F32) - run_start
        onehot = jnp.where(_dot(member.astype(BF16), rank_ref[...]) == in_run, 1.0, 0.0)
        srt = _dot(onehot.astype(BF16), x)
        dh = srt.shape[1] // 2
        buf_ref[cur, c * chunk:(c + 1) * chunk, :] = (bits(srt[:, :dh]) >> 16) | bits(srt[:, dh:])
        start_runs(prev_runs_ref, prev, range(c * per, (c + 1) * per), step > 0)
    start_filler(prev_runs_ref, prev, step > 0)

    @pl.when(step >= 2)
    def _():
        drain(old)

    @pl.when(step == n_steps - 1)
    def _():
        start_runs(runs_ref, cur, range(n_exp))
        start_filler(runs_ref, cur)

        @pl.when(step >= 1)
        def _():
            drain(prev)
        drain(cur)


def _sorted_rows(tm, n_exp):
    return TOP_K * tm + n_exp * RUN_ALIGN


def _dispatch(x2, rank_tab, runs, seg, n_rows, n_exp):
    n, d = x2.shape
    tm = SORT_TILE
    rows = _sorted_rows(tm, n_exp)
    assert rows % DISPATCH_CHUNKS == 0 and n_exp % DISPATCH_CHUNKS == 0
    tab = (None,) + runs.shape[1:]
    return pl.pallas_call(
        functools.partial(_dispatch_kernel, n_exp=n_exp),
        grid=(n // tm,),
        in_specs=[pl.BlockSpec(tab, lambda i: (jnp.maximum(i - 1, 0), 0, 0), memory_space=pltpu.SMEM),
                  pl.BlockSpec(tab, lambda i: (i, 0, 0), memory_space=pltpu.SMEM),
                  pl.BlockSpec(seg.shape, lambda i: (0, 0), memory_space=pltpu.SMEM),
                  pl.BlockSpec(tab, lambda i: (i, 0, 0)),
                  pl.BlockSpec((LANES, tm), lambda i: (0, i)),
                  pl.BlockSpec((tm, d), lambda i: (i, 0))],
        out_specs=pl.BlockSpec(memory_space=pl.ANY),
        out_shape=jax.ShapeDtypeStruct((n_rows + DISPATCH_SLOTS * (rows - TOP_K * tm), d), x2.dtype),
        scratch_shapes=[pltpu.VMEM((DISPATCH_SLOTS, rows, d), x2.dtype), pltpu.VMEM((EXPERT_BLOCK, d), x2.dtype),
                        pltpu.SemaphoreType.DMA((DISPATCH_SLOTS,)), pltpu.SemaphoreType.DMA],
        compiler_params=_cparams(("arbitrary",)),
        name="dispatch",
    )(runs, runs, seg, runs, rank_tab, x2)


def _expert_kernel(plan_ref, xs_ref, wg_ref, wu_ref, wd_ref, ys_ref, xbuf_ref, xsems, wg_f32, wu_f32, wd_f32,
                   wsems, wslot_ref, wg_bf, wu_bf, wd_bf, *, n_blocks):
    i = pl.program_id(0)
    used = plan_ref[0, n_blocks]
    depth, blk = xbuf_ref.shape[:2]

    def fetch(j):
        rows = pl.ds(pl.multiple_of(j * blk, blk), blk)
        return pltpu.make_async_copy(xs_ref.at[rows, :], xbuf_ref.at[j % depth], xsems.at[j % depth])

    def weights(e, s):
        return [pltpu.make_async_copy(src.at[e], dst.at[s], wsems.at[s])
                for src, dst in ((wg_ref, wg_f32), (wu_ref, wu_f32), (wd_ref, wd_f32))]

    @pl.when(i == 0)
    def _():
        for j in range(depth - 1):
            @pl.when(j < used)
            def _(j=j):
                fetch(j).start()

    @pl.when(i + depth - 1 < used)
    def _():
        fetch(i + depth - 1).start()

    @pl.when((i == 0) & (used > 0))
    def _():
        wslot_ref[0] = 0
        for cp in weights(plan_ref[0, 0], 0):
            cp.start()

    @pl.when(i < used)
    def _():
        @pl.when((i == 0) | (plan_ref[0, i] != plan_ref[0, jnp.maximum(i - 1, 0)]))
        def _():
            s = wslot_ref[0]
            for cp in weights(plan_ref[0, i], s):
                cp.wait()

            @pl.when(plan_ref[2, i] < used)
            def _():
                for cp in weights(plan_ref[1, i], 1 - s):
                    cp.start()
            wslot_ref[0] = 1 - s
            wg_bf[...] = wg_f32[s].astype(BF16)
            wu_bf[...] = wu_f32[s].astype(BF16)
            wd_bf[...] = wd_f32[s].astype(BF16)

        fetch(i).wait()
        lo, hi = _unpack_pair(xbuf_ref[i % depth])
        x = jnp.concatenate([lo.astype(BF16), hi.astype(BF16)], axis=1)
        gate = _dot(x, wg_bf[...])
        up = _dot(x, wu_bf[...])
        hid = gate / (1.0 + jnp.exp(-gate)) * up
        y = _dot(hid.astype(BF16), wd_bf[...])
        dh = y.shape[1] // 2
        ys_ref[...] = _pack_pair(y[:, :dh], y[:, dh:])


def _experts(bexp, xs, exp_gate, exp_up, exp_down, n_blocks):
    n_rows = xs.shape[0]
    d, hidden = exp_gate.shape[1:]
    blk = EXPERT_BLOCK

    def rows(i, b):
        return (jnp.minimum(i, jnp.maximum(b[0, n_blocks] - 1, 0)), 0)

    hbm = pl.BlockSpec(memory_space=pl.ANY)
    grid_spec = pltpu.PrefetchScalarGridSpec(
        num_scalar_prefetch=1,
        grid=(n_blocks,),
        in_specs=[hbm, hbm, hbm, hbm],
        out_specs=pl.BlockSpec((blk, d // 2), rows),
        scratch_shapes=[pltpu.VMEM((EXPERT_RING, blk, d // 2), xs.dtype), pltpu.SemaphoreType.DMA((EXPERT_RING,)),
                        pltpu.VMEM((2, d, hidden), F32), pltpu.VMEM((2, d, hidden), F32),
                        pltpu.VMEM((2, hidden, d), F32), pltpu.SemaphoreType.DMA((2,)), pltpu.SMEM((1,), I32),
                        pltpu.VMEM((d, hidden), BF16), pltpu.VMEM((d, hidden), BF16), pltpu.VMEM((hidden, d), BF16)],
    )
    return pl.pallas_call(
        functools.partial(_expert_kernel, n_blocks=n_blocks),
        grid_spec=grid_spec,
        out_shape=jax.ShapeDtypeStruct((n_rows, d // 2), U32),
        compiler_params=_cparams(("arbitrary",)),
        name="experts",
    )(bexp, xs, exp_gate, exp_up, exp_down)


def _combine_kernel(runs_ref, next_runs_ref, ahead_runs_ref, tab_ref, rankt_ref, gatet_ref, base_ref, g3_ref, b3_ref,
                    ys_ref, out_ref, buf_ref, sems, *, n_exp):
    step, n_steps = pl.program_id(0), pl.num_programs(0)
    tm = base_ref.shape[0]
    n_slots, rows = buf_ref.shape[:2]
    slot, ahead = step % n_slots, (step + 2) % n_slots

    def fetch(tab_ref, s, experts, enable=None):
        for e in experts:
            _start_run(ys_ref, buf_ref.at[s], sems.at[s], tab_ref[2, e], tab_ref[0, e], tab_ref[1, e], tm, enable)

    def fetch_filler(tab_ref, s, enable=None):
        used = tab_ref[0, n_exp - 1] + tab_ref[1, n_exp - 1]
        _start_run(ys_ref, buf_ref.at[s], sems.at[s], 0, used, rows - used, rows - TOP_K * tm, enable)

    @pl.when(step == 0)
    def _():
        fetch(runs_ref, 0, range(n_exp))
        fetch_filler(runs_ref, 0)
        fetch(next_runs_ref, 1, range(n_exp), n_steps > 1)
        fetch_filler(next_runs_ref, 1, n_steps > 1)

    pltpu.make_async_copy(ys_ref.at[pl.ds(0, rows), :], buf_ref.at[slot], sems.at[slot]).wait()

    tab = tab_ref[...]
    start = tab[0:1, :].astype(F32)
    start_hi = jnp.floor(start / float(POS_BASE))
    sub = lax.broadcasted_iota(I32, (SUBLANES, LANES), 0)
    digits = jnp.where(sub == 0, start_hi, jnp.where(sub == 1, start - start_hi * float(POS_BASE), 0.0)).astype(BF16)
    have_ahead = step + 2 < n_steps
    chunk, per = rows // DISPATCH_CHUNKS, n_exp // DISPATCH_CHUNKS
    routed = jnp.zeros(base_ref.shape, F32)
    for c in range(DISPATCH_CHUNKS):
        member = _run_membership(tab, chunk, c * chunk).astype(BF16)
        picked = _dot_nt(digits, member)
        in_run = ((lax.broadcasted_iota(I32, (1, chunk), 1) + c * chunk).astype(F32)
                  - (picked[0:1, :] * float(POS_BASE) + picked[1:2, :]))
        w = jnp.where(_dot_nt(rankt_ref[...], member) == in_run, _dot_nt(gatet_ref[...], member), 0.0)
        lo, hi = _unpack_pair(buf_ref[slot, c * chunk:(c + 1) * chunk, :])
        y = jnp.concatenate([lo.astype(BF16), hi.astype(BF16)], axis=1)
        routed = routed + _dot(w.astype(BF16), y)
        fetch(ahead_runs_ref, ahead, range(c * per, (c + 1) * per), have_ahead)
    fetch_filler(ahead_runs_ref, ahead, have_ahead)
    out_ref[...] = _layer_norm(base_ref[...] + routed, g3_ref[...], b3_ref[...])


def _combine(runs, rank_t, gate_t, base, g3, b3, ys, n_exp):
    n, d = base.shape
    tm = SORT_TILE
    last = n // tm - 1
    tab = lambda f: pl.BlockSpec((None,) + runs.shape[1:], f, memory_space=pltpu.SMEM)
    tmaj = pl.BlockSpec((tm, LANES), lambda i: (i, 0))
    return pl.pallas_call(
        functools.partial(_combine_kernel, n_exp=n_exp),
        grid=(n // tm,),
        in_specs=[tab(lambda i: (i, 0, 0)), tab(lambda i: (jnp.minimum(i + 1, last), 0, 0)),
                  tab(lambda i: (jnp.minimum(i + 2, last), 0, 0)),
                  pl.BlockSpec((None,) + runs.shape[1:], lambda i: (i, 0, 0)),
                  tmaj, tmaj,
                  pl.BlockSpec((tm, d), lambda i: (i, 0)),
                  pl.BlockSpec(g3.shape, lambda i: (0, 0)),
                  pl.BlockSpec(b3.shape, lambda i: (0, 0)),
                  pl.BlockSpec(memory_space=pl.ANY)],
        out_specs=pl.BlockSpec((tm, d), lambda i: (i, 0)),
        out_shape=jax.ShapeDtypeStruct((n, d), F32),
        scratch_shapes=[pltpu.VMEM((DISPATCH_SLOTS, _sorted_rows(tm, n_exp), ys.shape[1]), ys.dtype),
                        pltpu.SemaphoreType.DMA((DISPATCH_SLOTS,))],
        compiler_params=_cparams(("arbitrary",)),
        name="combine",
    )(runs, runs, runs, runs, rank_t, gate_t, base, g3, b3, ys)


def _layer(x2d, mem2d, w_in, pool_w, pool_scale, w_out, ln1_g, ln1_b, w_cq, w_ck, w_cv, w_co, ln2_g, ln2_b,
           w_router, router_bias, exp_gate, exp_up, exp_down, sh_gate, sh_up, sh_down, ln3_g, ln3_b,
           *, alpha, batch, seq, mem_len):
    n, d = x2d.shape
    n_exp = w_router.shape[1]
    row = lambda a: a.reshape(1, -1)
    bf = lambda a: a.astype(BF16)

    dils = tuple(dil for _, dil in DILATED_PATTERNS)
    *qkv, p = _in_proj(x2d, bf(w_in), dils)
    branches = []
    for di, (window, dil) in enumerate(DILATED_PATTERNS):
        assert window // dil == ATTN_BLOCK and seq % (dil * ATTN_BLOCK) == 0
        assert IN_TILE % (dil * 2 * SUBLANES) == 0
        branches.append(_dil_attn(*qkv[3 * di:3 * di + 3], batch, seq, dil))
    (o1, l1), (o4, l4), (o16, l16) = branches
    kmem, vmem = _mem_kv(mem2d, bf(w_ck), bf(w_cv))

    wr_t = w_router.T
    wr_hi = bf(wr_t)
    wr_lo = bf(wr_t - wr_hi.astype(F32))
    x2, base, logits_t = _row_block(
        x2d, o1, o4, o16, l1, l4, l16, p, bf(pool_w), row(pool_scale), bf(w_out), row(ln1_g), row(ln1_b),
        bf(w_cq), kmem, vmem, bf(w_co), row(ln2_g), row(ln2_b), wr_hi, wr_lo, bf(sh_gate), bf(sh_up), bf(sh_down),
        alpha=alpha, batch=batch, seq=seq, mem_len=mem_len)

    bias_col = jnp.broadcast_to(router_bias.reshape(n_exp, 1), (n_exp, LANES))
    rank_tab, rank_t, gate_t, tab, counts = _route(logits_t, bias_col)
    max_rows = n * TOP_K + (n // SORT_TILE) * n_exp * (RUN_ALIGN - 1)
    n_blocks = -(-max_rows // EXPERT_BLOCK) + n_exp
    runs, bexp, seg = _place(counts, tab, n_blocks)
    xs = _dispatch(x2, rank_tab, runs, seg, n_blocks * EXPERT_BLOCK, n_exp)
    ys = _experts(bexp, xs, exp_gate, exp_up, exp_down, n_blocks)
    return _combine(runs, rank_t, gate_t, base, row(ln3_g), row(ln3_b), ys, n_exp)


def kernel(x, mem, w_in, pool_w, pool_scale, w_out, ln1_g, ln1_b, w_cq, w_ck, w_cv, w_co, ln2_g, ln2_b, w_router, router_bias, exp_gate, exp_up, exp_down, sh_gate, sh_up, sh_down, ln3_g, ln3_b):
    batch, seq, d = x.shape
    mem_len = mem.shape[1]
    depth = w_in.shape[0]
    alpha = (2.0 * depth) ** 0.25
    x2d = x.reshape(batch * seq, d)
    mem2d = mem.reshape(batch * mem_len, d)
    for l in range(depth):
        x2d = _layer(x2d, mem2d, w_in[l], pool_w[l], pool_scale[l], w_out[l], ln1_g[l], ln1_b[l], w_cq[l], w_ck[l],
                     w_cv[l], w_co[l], ln2_g[l], ln2_b[l], w_router[l], router_bias[l], exp_gate[l], exp_up[l],
                     exp_down[l], sh_gate[l], sh_up[l], sh_down[l], ln3_g[l], ln3_b[l],
                     alpha=alpha, batch=batch, seq=seq, mem_len=mem_len)
    return x2d.reshape(batch, seq, d)
```

```python
import functools

import jax
import jax.numpy as jnp
from jax import lax
from jax.experimental import pallas as pl
from jax.experimental.pallas import tpu as pltpu

F32 = jnp.float32
BF16 = jnp.bfloat16
I32 = jnp.int32
U32 = jnp.uint32

ATTN_HEADS = 8
HEAD_DIM = 64
ATTN_WIDTH = ATTN_HEADS * HEAD_DIM
DILATED_PATTERNS = ((128, 1), (512, 4), (2048, 16))
POOL_WINDOWS = (2, 4, 8, 16)
MEM_HEADS = 4
N_EXPERT_GROUPS = 8
TOPK_GROUPS = 4
TOP_K = 8
ROUTED_SCALE = 2.5
LN_EPS = 1e-5
NEG_INF = -1e30

LANES = 128
SUBLANES = 8
VMEM_LIMIT = 56 * 1024 * 1024

ATTN_BLOCK = 128
ATTN_STEP_BLOCKS = 8
ROW_TILE = 512
IN_TILE = 512
SORT_TILE = 256
EXPERT_BLOCK = 512
EXPERT_RING = 3
RUN_ALIGN = SUBLANES
RUN_CHUNK = 64
POS_BASE = 64
DISPATCH_SLOTS = 3
DISPATCH_CHUNKS = 4


def _cparams(sem, vmem=VMEM_LIMIT):
    return pltpu.CompilerParams(dimension_semantics=sem, vmem_limit_bytes=vmem)


def _layer_norm(y, g, b):
    mu = jnp.mean(y, axis=-1, keepdims=True)
    d = y - mu
    var = jnp.mean(d * d, axis=-1, keepdims=True)
    return d * lax.rsqrt(var + LN_EPS) * g + b


def _dot(a, b):
    return jnp.dot(a, b, preferred_element_type=F32)


def _dot_nt(a, b):
    return lax.dot_general(a, b, (((1,), (1,)), ((), ())), preferred_element_type=F32)


def _split_bf16(x):
    hi = x.astype(BF16)
    lo = (x - hi.astype(F32)).astype(BF16)
    return hi, lo


def _bf16_bits(x):
    return lax.bitcast_convert_type(x.astype(BF16).astype(F32), U32)


def _pack_pair(lo, hi):
    return (_bf16_bits(lo) >> 16) | (_bf16_bits(hi) & jnp.uint32(0xFFFF0000))


def _unpack_pair(w):
    lo = lax.bitcast_convert_type(w << 16, F32)
    hi = lax.bitcast_convert_type(w & jnp.uint32(0xFFFF0000), F32)
    return lo, hi


def _in_proj_kernel(x_ref, w_ref, *refs, dils):
    n_qkv = 3 * len(dils)
    out_refs, p_ref, slab_ref = refs[:n_qkv], refs[n_qkv], refs[n_qkv + 1]
    tm = x_ref.shape[0]
    x = x_ref[...].astype(BF16)
    aw = ATTN_WIDTH
    n_slabs = aw // LANES
    for a in range(3):
        val = _dot(x, w_ref[:, a * aw:(a + 1) * aw])
        if a == 0:
            val = val * (HEAD_DIM ** -0.5)
        for s in range(n_slabs):
            slab_ref[s] = val[:, s * LANES:(s + 1) * LANES]
        for di, dil in enumerate(dils):
            dst = out_refs[3 * di + a]
            if dil == 1:
                dst[...] = val.astype(BF16)
                continue
            for r in range(dil):
                for s in range(n_slabs):
                    rows = slab_ref[s, pl.ds(r, tm // dil, stride=dil), :]
                    dst[:, r * aw + s * LANES:r * aw + (s + 1) * LANES] = rows.astype(BF16)
    p_ref[...] = _dot(x, w_ref[:, 3 * aw:])


def _in_proj(x2d, w_in_bf, dils):
    n, d = x2d.shape
    aw = ATTN_WIDTH
    pw = w_in_bf.shape[1] - 3 * aw
    tm = IN_TILE
    row = lambda rows, w: pl.BlockSpec((rows, w), lambda i: (i, 0))
    qkv_specs, qkv_shapes = [], []
    for dil in dils:
        qkv_specs += [row(tm // dil, dil * aw)] * 3
        qkv_shapes += [jax.ShapeDtypeStruct((n // dil, dil * aw), BF16)] * 3
    return pl.pallas_call(
        functools.partial(_in_proj_kernel, dils=dils),
        grid=(n // tm,),
        in_specs=[row(tm, d), pl.BlockSpec(w_in_bf.shape, lambda i: (0, 0))],
        out_specs=qkv_specs + [row(tm, pw)],
        out_shape=qkv_shapes + [jax.ShapeDtypeStruct((n, pw), F32)],
        scratch_shapes=[pltpu.VMEM((aw // LANES, tm, LANES), F32)],
        compiler_params=_cparams(("arbitrary",)),
        name="in_proj",
    )(x2d, w_in_bf)


def _dil_attn_kernel(q_ref, kc_ref, kp_ref, vc_ref, vp_ref, o_ref, lse_ref, bias_ref, *, dil):
    blk = ATTN_BLOCK
    first_step = (pl.program_id(0) == 0) & (pl.program_id(1) == 0) & (pl.program_id(2) == 0)

    @pl.when(first_step)
    def _():
        qi = lax.broadcasted_iota(I32, (blk, 2 * blk), 0)
        kj = lax.broadcasted_iota(I32, (blk, 2 * blk), 1)
        dist = qi + blk - kj
        inside = (dist >= 0) & (dist <= blk)
        penalty = (dist * dil).astype(F32)
        for h in range(ATTN_HEADS):
            slope = 2.0 ** (-8.0 * (h + 1) / ATTN_HEADS)
            bias_ref[0, h] = jnp.where(inside, -slope * penalty, NEG_INF)
            bias_ref[1, h] = jnp.where(inside & (kj >= blk), -slope * penalty, NEG_INF)

    i = pl.program_id(1)
    r = pl.program_id(2)
    lane = lax.broadcasted_iota(I32, (blk, LANES), 1)
    low_half = lane < HEAD_DIM
    n_pairs = ATTN_WIDTH // LANES
    half = n_pairs // 2
    for j in range(q_ref.shape[0] // blk):
        table = jnp.where(i == 0, 1, 0) if j == 0 else 0
        lse_tile = jnp.zeros((blk, LANES), F32)
        pair_out = []
        for hp in range(n_pairs):
            sl = slice(hp * LANES, (hp + 1) * LANES)
            q2 = q_ref[j * blk:(j + 1) * blk, sl]
            if j == 0:
                kcat = jnp.concatenate([kp_ref[:, sl], kc_ref[0:blk, sl]], axis=0)
                vcat = jnp.concatenate([vp_ref[:, sl], vc_ref[0:blk, sl]], axis=0)
            else:
                kcat = kc_ref[(j - 1) * blk:(j + 1) * blk, sl]
                vcat = vc_ref[(j - 1) * blk:(j + 1) * blk, sl]
            outs = []
            for e in range(LANES // HEAD_DIM):
                h = hp * (LANES // HEAD_DIM) + e
                keep = low_half if e == 0 else jnp.logical_not(low_half)
                qe = jnp.where(keep, q2, jnp.zeros_like(q2))
                s = _dot_nt(qe, kcat) + bias_ref[table, h]
                m = jnp.max(s, axis=1, keepdims=True)
                p = jnp.exp(s - m)
                l = jnp.sum(p, axis=1, keepdims=True)
                outs.append(_dot(p.astype(BF16), vcat) / l)
                lse_tile = jnp.where(lane == h, m + jnp.log(l), lse_tile)
            pair_out.append(jnp.where(low_half, outs[0], outs[1]))
        rows = pl.ds(j * blk * dil + r, blk, stride=dil) if dil > 1 else slice(j * blk, (j + 1) * blk)
        for w in range(half):
            o_ref[w, rows, :] = _pack_pair(pair_out[w], pair_out[w + half])
        lse_ref[rows, :] = lse_tile


def _dil_attn(q, k, v, batch, seq, dil):
    blk = ATTN_BLOCK
    sub = seq // dil
    step_blocks = min(ATTN_STEP_BLOCKS, sub // blk)
    step_rows = step_blocks * blk
    assert sub % step_rows == 0
    view = lambda a: a.reshape(batch, sub, dil * ATTN_WIDTH)
    cur = pl.BlockSpec((None, step_rows, ATTN_WIDTH), lambda b, i, r: (b, i, r))
    prev = pl.BlockSpec((None, blk, ATTN_WIDTH), lambda b, i, r: (b, jnp.maximum(i * step_blocks - 1, 0), r))
    half = ATTN_WIDTH // LANES // 2
    o, lse = pl.pallas_call(
        functools.partial(_dil_attn_kernel, dil=dil),
        grid=(batch, sub // step_rows, dil),
        in_specs=[cur, cur, prev, cur, prev],
        out_specs=[pl.BlockSpec((None, half, step_rows * dil, LANES), lambda b, i, r: (b, 0, i, 0)),
                   pl.BlockSpec((None, step_rows * dil, LANES), lambda b, i, r: (b, i, 0))],
        out_shape=[jax.ShapeDtypeStruct((batch, half, seq, LANES), U32),
                   jax.ShapeDtypeStruct((batch, seq, LANES), F32)],
        scratch_shapes=[pltpu.VMEM((2, ATTN_HEADS, blk, 2 * blk), F32)],
        compiler_params=_cparams(("arbitrary",) * 3),
        name=f"dil_attn_d{dil}",
    )(view(q), view(k), view(k), view(v), view(v))
    return o, lse.reshape(batch * seq, LANES)


def _mem_kv_kernel(mem_ref, wk_ref, wv_ref, k_ref, v_ref):
    m = mem_ref[...].astype(BF16)
    k_ref[...] = _dot(m, wk_ref[...]).astype(BF16)
    v_ref[...] = _dot(m, wv_ref[...]).astype(BF16)


def _mem_kv(mem2d, w_ck_bf, w_cv_bf):
    rows = mem2d.shape[0]
    width = w_ck_bf.shape[1]
    return pl.pallas_call(
        _mem_kv_kernel,
        out_shape=[jax.ShapeDtypeStruct((rows, width), BF16)] * 2,
        compiler_params=_cparams(None),
        name="mem_kv",
    )(mem2d, w_ck_bf, w_cv_bf)


def _row_kernel(x_ref, o1_ref, o4_ref, o16_ref, l1_ref, l4_ref, l16_ref, p_ref, halo_ref,
                poolw_ref, pscale_ref, wout_ref, g1_ref, b1_ref,
                wcq_ref, km_ref, vm_ref, wco_ref, g2_ref, b2_ref,
                wrh_ref, wrl_ref, shg_ref, shu_ref, shd_ref,
                x2_ref, base_ref, logit_ref, *, alpha, seq):
    tm = x_ref.shape[0]
    i = pl.program_id(0)
    tile_pos = (i * tm) % seq

    l1, l4, l16 = l1_ref[...], l4_ref[...], l16_ref[...]
    mx = jnp.maximum(jnp.maximum(l1, l4), l16)
    e1, e4, e16 = jnp.exp(l1 - mx), jnp.exp(l4 - mx), jnp.exp(l16 - mx)
    inv = 1.0 / (e1 + e4 + e16)
    hrow = lax.broadcasted_iota(I32, (LANES, ATTN_WIDTH), 0)
    hcol = lax.broadcasted_iota(I32, (LANES, ATTN_WIDTH), 1) // HEAD_DIM
    spread = jnp.where(hrow == hcol, 1.0, 0.0).astype(BF16)

    def widen(w):
        hi, lo = _split_bf16(w)
        return _dot(hi, spread) + _dot(lo, spread)

    def branch(o_ref):
        lo, hi = zip(*[_unpack_pair(o_ref[w]) for w in range(o_ref.shape[0])])
        return jnp.concatenate(lo + hi, axis=1)

    last = branch(o16_ref)
    attn = last + widen(e1 * inv) * (branch(o1_ref) - last) + widen(e4 * inv) * (branch(o4_ref) - last)

    halo_rows = halo_ref.shape[0]
    halo = jnp.where(tile_pos > 0, halo_ref[...], 0.0)
    ext = jnp.concatenate([halo, p_ref[...]], axis=0)
    pos = (tile_pos + lax.broadcasted_iota(I32, (tm, 1), 0)).astype(F32)
    gd = ext.shape[1] // len(POOL_WINDOWS)
    mixed = []
    for g, w in enumerate(POOL_WINDOWS):
        eg = ext[:, g * gd:(g + 1) * gd]
        acc, span = eg, 1
        while span < w:
            acc = acc + pltpu.roll(acc, span, 0)
            span *= 2
        count = jnp.minimum(pos + 1.0, float(w))
        pooled = acc[halo_rows:, :] / count - eg[halo_rows:, :]
        mixed.append(_dot(pooled.astype(BF16), poolw_ref[g]) * pscale_ref[:, g * gd:(g + 1) * gd])
    cat = jnp.concatenate([attn.astype(BF16)] + [m.astype(BF16) for m in mixed], axis=1)
    x1 = _layer_norm(alpha * x_ref[...] + _dot(cat, wout_ref[...]), g1_ref[...], b1_ref[...])

    qc = _dot(x1.astype(BF16), wcq_ref[...])
    mhd = qc.shape[1] // MEM_HEADS
    heads = []
    for h in range(MEM_HEADS):
        sl = slice(h * mhd, (h + 1) * mhd)
        s = _dot_nt(qc[:, sl].astype(BF16), km_ref[:, sl]) * (mhd ** -0.5)
        m = jnp.max(s, axis=1, keepdims=True)
        p = jnp.exp(s - m)
        l = jnp.sum(p, axis=1, keepdims=True)
        heads.append((_dot(p.astype(BF16), vm_ref[:, sl]) / l).astype(BF16))
    oc = jnp.concatenate(heads, axis=1)
    x2 = _layer_norm(alpha * x1 + _dot(oc, wco_ref[...]), g2_ref[...], b2_ref[...])
    dh = x2.shape[1] // 2
    x2_ref[...] = _pack_pair(x2[:, :dh], x2[:, dh:])

    xh, xl = _split_bf16(x2)
    logit_ref[...] = _dot_nt(wrh_ref[...], xh) + _dot_nt(wrh_ref[...], xl) + _dot_nt(wrl_ref[...], xh)

    gate = _dot(xh, shg_ref[...])
    up = _dot(xh, shu_ref[...])
    hid = gate / (1.0 + jnp.exp(-gate)) * up
    base_ref[...] = alpha * x2 + _dot(hid.astype(BF16), shd_ref[...])


def _row_block(x2d, o1, o4, o16, l1, l4, l16, p, pool_w_bf, pool_scale, w_out_bf, g1, b1,
               w_cq_bf, kmem, vmem, w_co_bf, g2, b2, wr_hi, wr_lo, sh_g, sh_u, sh_d,
               *, alpha, batch, seq, mem_len):
    n, d = x2d.shape
    tm = ROW_TILE
    halo_rows = max(POOL_WINDOWS)
    steps_per_seq = seq // tm
    row = lambda w: pl.BlockSpec((tm, w), lambda i: (i, 0))
    full = lambda a: pl.BlockSpec(a.shape, lambda i: (0,) * a.ndim)
    halo = pl.BlockSpec((halo_rows, p.shape[1]), lambda i: (jnp.maximum(i * (tm // halo_rows) - 1, 0), 0))
    memspec = pl.BlockSpec((mem_len, kmem.shape[1]), lambda i: (i // steps_per_seq, 0))
    n_exp = wr_hi.shape[0]
    branch = pl.BlockSpec((None, o1.shape[1], tm, LANES), lambda i: (i // steps_per_seq, 0, i % steps_per_seq, 0))
    return pl.pallas_call(
        functools.partial(_row_kernel, alpha=alpha, seq=seq),
        grid=(n // tm,),
        in_specs=[row(d), branch, branch, branch, row(LANES), row(LANES), row(LANES),
                  row(p.shape[1]), halo,
                  full(pool_w_bf), full(pool_scale), full(w_out_bf), full(g1), full(b1),
                  full(w_cq_bf), memspec, memspec, full(w_co_bf), full(g2), full(b2),
                  full(wr_hi), full(wr_lo), full(sh_g), full(sh_u), full(sh_d)],
        out_specs=[row(d // 2), row(d), pl.BlockSpec((n_exp, tm), lambda i: (0, i))],
        out_shape=[jax.ShapeDtypeStruct((n, d // 2), U32), jax.ShapeDtypeStruct((n, d), F32),
                   jax.ShapeDtypeStruct((n_exp, n), F32)],
        compiler_params=_cparams(("arbitrary",)),
        name="row_block",
    )(x2d, o1, o4, o16, l1, l4, l16, p, p, pool_w_bf, pool_scale, w_out_bf, g1, b1,
      w_cq_bf, kmem, vmem, w_co_bf, g2, b2, wr_hi, wr_lo, sh_g, sh_u, sh_d)


def _beat_counts(vals, n_rows):
    tn = vals.shape[1]
    n_tiles = n_rows // SUBLANES
    tiles = [vals[t * SUBLANES:(t + 1) * SUBLANES, :] for t in range(n_tiles)]
    sub = lax.broadcasted_iota(I32, (SUBLANES, tn), 0)
    counts = [jnp.zeros((SUBLANES, tn), F32) for _ in range(n_tiles)]
    for e in range(n_rows):
        te, je = divmod(e, SUBLANES)
        row = jnp.broadcast_to(vals[e:e + 1, :], (SUBLANES, tn))
        for t in range(n_tiles):
            strict = jnp.where(row > tiles[t], 1.0, 0.0)
            loose = jnp.where(row >= tiles[t], 1.0, 0.0)
            if t < te:
                beat = strict
            elif t > te:
                beat = loose
            else:
                beat = jnp.where(sub > je, loose, strict)
            counts[t] = counts[t] + beat
    return jnp.concatenate(counts, axis=0)


def _top_k_mask(vals, k):
    ridx = lax.broadcasted_iota(I32, vals.shape, 0).astype(F32)
    mask = jnp.zeros(vals.shape, F32)
    work = vals
    for _ in range(k):
        top = jnp.max(work, axis=0, keepdims=True)
        first = jnp.min(jnp.where(work == top, ridx, float(vals.shape[0])), axis=0, keepdims=True)
        hit = ridx == first
        mask = jnp.where(hit, 1.0, mask)
        work = jnp.where(hit, -jnp.inf, work)
    return mask


def _route_kernel(logit_ref, bias_ref, rank_ref, rankt_ref, gatet_ref, tab_ref, count_ref, carry_ref, rcarry_ref,
                  *, per_group):
    n_exp, tn = logit_ref.shape
    step = pl.program_id(0)

    @pl.when(step == 0)
    def _():
        carry_ref[...] = jnp.zeros_like(carry_ref)
        rcarry_ref[...] = jnp.zeros_like(rcarry_ref)

    scores = 1.0 / (1.0 + jnp.exp(-logit_ref[...]))
    biased = scores + bias_ref[:, 0:1]
    n_groups = n_exp // per_group

    gscore = []
    for g in range(n_groups):
        tile = biased[g * per_group:(g + 1) * per_group, :]
        inner = _beat_counts(tile, per_group)
        gscore.append(jnp.sum(jnp.where(inner < 2.0, tile, 0.0), axis=0, keepdims=True))
    gscore = jnp.concatenate(gscore, axis=0)
    grank = _beat_counts(gscore, n_groups)
    grank = jnp.concatenate(
        [jnp.broadcast_to(grank[g:g + 1, :], (per_group, tn)) for g in range(n_groups)], axis=0)
    masked = jnp.where(grank < float(TOPK_GROUPS), biased, NEG_INF)
    self32 = _top_k_mask(masked, TOP_K)
    selbf = self32.astype(BF16)

    tr = lax.broadcasted_iota(I32, (tn, tn), 0)
    tc = lax.broadcasted_iota(I32, (tn, tn), 1)
    before = jnp.where(tr < tc, 1.0, 0.0).astype(BF16)
    lrank = _dot(selbf, before)
    align = float(RUN_ALIGN)
    lcount_col = jnp.broadcast_to(jnp.sum(self32, axis=1, keepdims=True), (n_exp, LANES))
    sel_wide = jnp.concatenate([selbf, jnp.zeros((LANES - n_exp, tn), BF16)], axis=0)
    lcount_row = _dot_nt(jnp.ones((SUBLANES, tn), BF16), sel_wide)
    lcount_col = jnp.ceil(lcount_col / align) * align
    lcount_row = jnp.ceil(lcount_row / align) * align
    wr = lax.broadcasted_iota(I32, (LANES, LANES), 0)
    wc = lax.broadcasted_iota(I32, (LANES, LANES), 1)
    lstart_row = _dot(lcount_row.astype(BF16), jnp.where(wr < wc, 1.0, 0.0).astype(BF16))

    sub = lax.broadcasted_iota(I32, (SUBLANES, LANES), 0)
    tab_ref[...] = jnp.where(sub == 0, lstart_row, jnp.where(sub == 1, lcount_row, jnp.where(
        sub == 2, rcarry_ref[...], 0.0))).astype(I32)
    rcarry_ref[...] = rcarry_ref[...] + lcount_row
    carry_ref[...] = carry_ref[...] + lcount_col
    count_ref[...] = carry_ref[...]

    denom = jnp.sum(self32 * scores, axis=0, keepdims=True)
    gate = self32 * scores / denom * ROUTED_SCALE
    pad = jnp.zeros((LANES - n_exp, tn), F32)
    widen = lambda v: jnp.concatenate([v, pad], axis=0)
    rank_tab = widen(jnp.where(self32 > 0.0, lrank, -1.0))
    rank_ref[...] = rank_tab.astype(BF16)
    rankt_ref[...] = rank_tab.T.astype(BF16)
    gatet_ref[...] = widen(gate).T.astype(BF16)


def _route(logits_t, bias_col):
    n_exp, n = logits_t.shape
    tn = SORT_TILE
    emaj = jax.ShapeDtypeStruct((LANES, n), BF16)
    tmaj = jax.ShapeDtypeStruct((n, LANES), BF16)
    emaj_spec = pl.BlockSpec((LANES, tn), lambda i: (0, i))
    tmaj_spec = pl.BlockSpec((tn, LANES), lambda i: (i, 0))
    return pl.pallas_call(
        functools.partial(_route_kernel, per_group=n_exp // N_EXPERT_GROUPS),
        grid=(n // tn,),
        in_specs=[pl.BlockSpec((n_exp, tn), lambda i: (0, i)), pl.BlockSpec(bias_col.shape, lambda i: (0, 0))],
        out_specs=[emaj_spec, tmaj_spec, tmaj_spec,
                   pl.BlockSpec((None, SUBLANES, LANES), lambda i: (i, 0, 0)),
                   pl.BlockSpec((n_exp, LANES), lambda i: (0, 0))],
        out_shape=[emaj, tmaj, tmaj, jax.ShapeDtypeStruct((n // tn, SUBLANES, LANES), I32),
                   jax.ShapeDtypeStruct((n_exp, LANES), F32)],
        scratch_shapes=[pltpu.VMEM((n_exp, LANES), F32), pltpu.VMEM((SUBLANES, LANES), F32)],
        compiler_params=_cparams(("arbitrary",)),
        name="route",
    )(logits_t, bias_col)


def _place_kernel(count_ref, tab_ref, runs_ref, bexp_ref, seg_ref, *, n_blocks):
    n_exp = count_ref.shape[0]
    blk = float(EXPERT_BLOCK)
    counts = count_ref[...]
    padded = jnp.ceil(counts / blk) * blk
    ridx = lax.broadcasted_iota(I32, (n_exp, LANES), 0)
    ends = padded
    shift = 1
    while shift < n_exp:
        ends = ends + jnp.where(ridx >= shift, pltpu.roll(ends, shift, 0), 0.0)
        shift *= 2
    starts = ends - padded

    tab = tab_ref[...]
    last = tab[tab.shape[0] - 1]
    total_row = (last[2:3, :] + last[1:2, :]).astype(F32)
    padded_row = jnp.ceil(total_row / blk) * blk
    lidx = lax.broadcasted_iota(I32, (1, LANES), 1)
    ends_row = padded_row
    shift = 1
    while shift < LANES:
        ends_row = ends_row + jnp.where(lidx >= shift, pltpu.roll(ends_row, shift, 1), 0.0)
        shift *= 2
    starts_row = (ends_row - padded_row).astype(I32)
    sub = lax.broadcasted_iota(I32, tab.shape, 1)
    runs_ref[...] = tab + jnp.where(sub == 2, starts_row[None], 0)

    nbp = bexp_ref.shape[1]
    blk_start = (lax.broadcasted_iota(I32, (n_exp, nbp), 1) * EXPERT_BLOCK).astype(F32)
    ends_wide = jnp.broadcast_to(ends[:, 0:1], (n_exp, nbp))
    count_le = lambda bound: jnp.minimum(
        jnp.sum(jnp.where(ends_wide <= bound, 1.0, 0.0), axis=0, keepdims=True), float(n_exp - 1))
    bexp = count_le(blk_start)
    used = ends[n_exp - 1:n_exp, 0:1] / blk
    seg_end = jnp.min(jnp.where(ends_wide > blk_start, ends_wide, 2.0 * float(nbp * EXPERT_BLOCK)),
                      axis=0, keepdims=True)
    sub8 = lax.broadcasted_iota(I32, (SUBLANES, nbp), 0)
    lane = lax.broadcasted_iota(I32, (SUBLANES, nbp), 1)
    plan = jnp.where(sub8 == 0, jnp.where(lane == n_blocks, used, bexp),
                     jnp.where(sub8 == 1, count_le(seg_end), jnp.where(sub8 == 2, seg_end / blk, 0.0)))
    bexp_ref[...] = plan.astype(I32)
    lane2 = lax.broadcasted_iota(I32, (n_exp, LANES), 1)
    seg_ref[...] = jnp.where(lane2 == 0, starts + counts, ends).astype(I32)


def _place(counts, tab, n_blocks):
    n_exp = counts.shape[0]
    nbp = -(-(n_blocks + 1) // LANES) * LANES
    return pl.pallas_call(
        functools.partial(_place_kernel, n_blocks=n_blocks),
        out_shape=[jax.ShapeDtypeStruct(tab.shape, I32), jax.ShapeDtypeStruct((SUBLANES, nbp), I32),
                   jax.ShapeDtypeStruct((n_exp, LANES), I32)],
        compiler_params=_cparams(None),
        name="route_place",
    )(counts, tab)


def _copy_rows(src_ref, dst_ref, sem, src0, dst0, n, wait=False):
    def piece(off, p):
        src = src_ref.at[pl.ds(pl.multiple_of(src0 + off, RUN_ALIGN), p), :]
        dst = dst_ref.at[pl.ds(pl.multiple_of(dst0 + off, RUN_ALIGN), p), :]
        cp = pltpu.make_async_copy(src, dst, sem)
        if wait:
            cp.wait()
        else:
            cp.start()

    def chunk(c, carry):
        piece(c * RUN_CHUNK, RUN_CHUNK)
        return carry
    lax.fori_loop(0, lax.shift_right_logical(n, RUN_CHUNK.bit_length() - 1), chunk, 0)
    p = RUN_CHUNK // 2
    while p >= RUN_ALIGN:
        @pl.when((n & p) != 0)
        def _(p=p):
            piece(n & (-2 * p), p)
        p //= 2


def _start_run(src_ref, dst_ref, sem, src0, dst0, n, limit, enable=None):
    p = RUN_ALIGN
    while p * 2 <= limit:
        p *= 2
    while p >= RUN_ALIGN:
        hit = (n & p) != 0
        @pl.when(hit if enable is None else hit & enable)
        def _(p=p):
            off = n & (-2 * p)
            src = src_ref.at[pl.ds(pl.multiple_of(src0 + off, RUN_ALIGN), p), :]
            dst = dst_ref.at[pl.ds(pl.multiple_of(dst0 + off, RUN_ALIGN), p), :]
            pltpu.make_async_copy(src, dst, sem).start()
        p //= 2


def _run_membership(tab, rows, first_row=0):
    j = lax.broadcasted_iota(I32, (rows, LANES), 0) + first_row
    start = tab[0:1, :]
    end = start + tab[1:2, :]
    return jnp.where(j >= start, jnp.where(j < end, 1.0, 0.0), 0.0)


def _dispatch_kernel(prev_runs_ref, runs_ref, seg_ref, tab_ref, rank_ref, x_ref, xs_ref, buf_ref, zero_ref, sems,
                     zsem, *, n_exp):
    step, n_steps = pl.program_id(0), pl.num_programs(0)
    tm = x_ref.shape[0]
    n_slots, rows = buf_ref.shape[:2]
    pad_rows = rows - TOP_K * tm
    cur, prev, old = step % n_slots, (step + n_slots - 1) % n_slots, (step + n_slots - 2) % n_slots

    def start_runs(tab_ref, s, experts, enable=None):
        for e in experts:
            _start_run(buf_ref.at[s], xs_ref, sems.at[s], tab_ref[0, e], tab_ref[2, e], tab_ref[1, e], tm, enable)

    def start_filler(tab_ref, s, enable=None):
        used = tab_ref[0, n_exp - 1] + tab_ref[1, n_exp - 1]
        spare = xs_ref.shape[0] - (n_slots - s) * pad_rows
        _start_run(buf_ref.at[s], xs_ref, sems.at[s], used, spare, rows - used, pad_rows, enable)

    def drain(s):
        pltpu.make_async_copy(buf_ref.at[s], xs_ref.at[pl.ds(0, rows), :], sems.at[s]).wait()

    @pl.when(step == 0)
    def _():
        zero_ref[...] = jnp.zeros_like(zero_ref)
        for wait in (False, True):
            def per_expert(e, c, wait=wait):
                first = seg_ref[e, 0]
                _copy_rows(zero_ref, xs_ref, zsem, 0, first, seg_ref[e, 1] - first, wait)
                return c
            lax.fori_loop(0, n_exp, per_expert, 0)

    lo, hi = _unpack_pair(x_ref[...])
    x = jnp.concatenate([lo.astype(BF16), hi.astype(BF16)], axis=1)
    tab = tab_ref[...]
    bits = lambda v: lax.bitcast_convert_type(v, U32)
    chunk, per = rows // DISPATCH_CHUNKS, n_exp // DISPATCH_CHUNKS
    for c in range(DISPATCH_CHUNKS):
        member = _run_membership(tab, chunk, c * chunk)
        run_start = jnp.sum(member * tab[0:1, :].astype(F32), axis=1, keepdims=True)
        in_run = (lax.broadcasted_iota(I32, (chunk, 1), 0) + c * chunk).astype(F32) - run_start
        onehot = jnp.where(_dot(member.astype(BF16), rank_ref[...]) == in_run, 1.0, 0.0)
        srt = _dot(onehot.astype(BF16), x)
        dh = srt.shape[1] // 2
        buf_ref[cur, c * chunk:(c + 1) * chunk, :] = (bits(srt[:, :dh]) >> 16) | bits(srt[:, dh:])
        start_runs(prev_runs_ref, prev, range(c * per, (c + 1) * per), step > 0)
    start_filler(prev_runs_ref, prev, step > 0)

    @pl.when(step >= 2)
    def _():
        drain(old)

    @pl.when(step == n_steps - 1)
    def _():
        start_runs(runs_ref, cur, range(n_exp))
        start_filler(runs_ref, cur)

        @pl.when(step >= 1)
        def _():
            drain(prev)
        drain(cur)


def _sorted_rows(tm, n_exp):
    return TOP_K * tm + n_exp * RUN_ALIGN


def _dispatch(x2, rank_tab, runs, seg, n_rows, n_exp):
    n, d = x2.shape
    tm = SORT_TILE
    rows = _sorted_rows(tm, n_exp)
    assert rows % DISPATCH_CHUNKS == 0 and n_exp % DISPATCH_CHUNKS == 0
    tab = (None,) + runs.shape[1:]
    return pl.pallas_call(
        functools.partial(_dispatch_kernel, n_exp=n_exp),
        grid=(n // tm,),
        in_specs=[pl.BlockSpec(tab, lambda i: (jnp.maximum(i - 1, 0), 0, 0), memory_space=pltpu.SMEM),
                  pl.BlockSpec(tab, lambda i: (i, 0, 0), memory_space=pltpu.SMEM),
                  pl.BlockSpec(seg.shape, lambda i: (0, 0), memory_space=pltpu.SMEM),
                  pl.BlockSpec(tab, lambda i: (i, 0, 0)),
                  pl.BlockSpec((LANES, tm), lambda i: (0, i)),
                  pl.BlockSpec((tm, d), lambda i: (i, 0))],
        out_specs=pl.BlockSpec(memory_space=pl.ANY),
        out_shape=jax.ShapeDtypeStruct((n_rows + DISPATCH_SLOTS * (rows - TOP_K * tm), d), x2.dtype),
        scratch_shapes=[pltpu.VMEM((DISPATCH_SLOTS, rows, d), x2.dtype), pltpu.VMEM((EXPERT_BLOCK, d), x2.dtype),
                        pltpu.SemaphoreType.DMA((DISPATCH_SLOTS,)), pltpu.SemaphoreType.DMA],
        compiler_params=_cparams(("arbitrary",)),
        name="dispatch",
    )(runs, runs, seg, runs, rank_tab, x2)


def _expert_kernel(plan_ref, xs_ref, wg_ref, wu_ref, wd_ref, ys_ref, xbuf_ref, xsems, wg_f32, wu_f32, wd_f32,
                   wsems, wslot_ref, wg_bf, wu_bf, wd_bf, *, n_blocks):
    i = pl.program_id(0)
    used = plan_ref[0, n_blocks]
    depth, blk = xbuf_ref.shape[:2]

    def fetch(j):
        rows = pl.ds(pl.multiple_of(j * blk, blk), blk)
        return pltpu.make_async_copy(xs_ref.at[rows, :], xbuf_ref.at[j % depth], xsems.at[j % depth])

    def weights(e, s):
        return [pltpu.make_async_copy(src.at[e], dst.at[s], wsems.at[s])
                for src, dst in ((wg_ref, wg_f32), (wu_ref, wu_f32), (wd_ref, wd_f32))]

    @pl.when(i == 0)
    def _():
        for j in range(depth - 1):
            @pl.when(j < used)
            def _(j=j):
                fetch(j).start()

    @pl.when(i + depth - 1 < used)
    def _():
        fetch(i + depth - 1).start()

    @pl.when((i == 0) & (used > 0))
    def _():
        wslot_ref[0] = 0
        for cp in weights(plan_ref[0, 0], 0):
            cp.start()

    @pl.when(i < used)
    def _():
        @pl.when((i == 0) | (plan_ref[0, i] != plan_ref[0, jnp.maximum(i - 1, 0)]))
        def _():
            s = wslot_ref[0]
            for cp in weights(plan_ref[0, i], s):
                cp.wait()

            @pl.when(plan_ref[2, i] < used)
            def _():
                for cp in weights(plan_ref[1, i], 1 - s):
                    cp.start()
            wslot_ref[0] = 1 - s
            wg_bf[...] = wg_f32[s].astype(BF16)
            wu_bf[...] = wu_f32[s].astype(BF16)
            wd_bf[...] = wd_f32[s].astype(BF16)

        fetch(i).wait()
        lo, hi = _unpack_pair(xbuf_ref[i % depth])
        x = jnp.concatenate([lo.astype(BF16), hi.astype(BF16)], axis=1)
        gate = _dot(x, wg_bf[...])
        up = _dot(x, wu_bf[...])
        hid = gate / (1.0 + jnp.exp(-gate)) * up
        y = _dot(hid.astype(BF16), wd_bf[...])
        dh = y.shape[1] // 2
        ys_ref[...] = _pack_pair(y[:, :dh], y[:, dh:])


def _experts(bexp, xs, exp_gate, exp_up, exp_down, n_blocks):
    n_rows = xs.shape[0]
    d, hidden = exp_gate.shape[1:]
    blk = EXPERT_BLOCK

    def rows(i, b):
        return (jnp.minimum(i, jnp.maximum(b[0, n_blocks] - 1, 0)), 0)

    hbm = pl.BlockSpec(memory_space=pl.ANY)
    grid_spec = pltpu.PrefetchScalarGridSpec(
        num_scalar_prefetch=1,
        grid=(n_blocks,),
        in_specs=[hbm, hbm, hbm, hbm],
        out_specs=pl.BlockSpec((blk, d // 2), rows),
        scratch_shapes=[pltpu.VMEM((EXPERT_RING, blk, d // 2), xs.dtype), pltpu.SemaphoreType.DMA((EXPERT_RING,)),
                        pltpu.VMEM((2, d, hidden), F32), pltpu.VMEM((2, d, hidden), F32),
                        pltpu.VMEM((2, hidden, d), F32), pltpu.SemaphoreType.DMA((2,)), pltpu.SMEM((1,), I32),
                        pltpu.VMEM((d, hidden), BF16), pltpu.VMEM((d, hidden), BF16), pltpu.VMEM((hidden, d), BF16)],
    )
    return pl.pallas_call(
        functools.partial(_expert_kernel, n_blocks=n_blocks),
        grid_spec=grid_spec,
        out_shape=jax.ShapeDtypeStruct((n_rows, d // 2), U32),
        compiler_params=_cparams(("arbitrary",)),
        name="experts",
    )(bexp, xs, exp_gate, exp_up, exp_down)


def _combine_kernel(runs_ref, next_runs_ref, ahead_runs_ref, tab_ref, rankt_ref, gatet_ref, base_ref, g3_ref, b3_ref,
                    ys_ref, out_ref, buf_ref, sems, *, n_exp):
    step, n_steps = pl.program_id(0), pl.num_programs(0)
    tm = base_ref.shape[0]
    n_slots, rows = buf_ref.shape[:2]
    slot, ahead = step % n_slots, (step + 2) % n_slots

    def fetch(tab_ref, s, experts, enable=None):
        for e in experts:
            _start_run(ys_ref, buf_ref.at[s], sems.at[s], tab_ref[2, e], tab_ref[0, e], tab_ref[1, e], tm, enable)

    def fetch_filler(tab_ref, s, enable=None):
        used = tab_ref[0, n_exp - 1] + tab_ref[1, n_exp - 1]
        _start_run(ys_ref, buf_ref.at[s], sems.at[s], 0, used, rows - used, rows - TOP_K * tm, enable)

    @pl.when(step == 0)
    def _():
        fetch(runs_ref, 0, range(n_exp))
        fetch_filler(runs_ref, 0)
        fetch(next_runs_ref, 1, range(n_exp), n_steps > 1)
        fetch_filler(next_runs_ref, 1, n_steps > 1)

    pltpu.make_async_copy(ys_ref.at[pl.ds(0, rows), :], buf_ref.at[slot], sems.at[slot]).wait()

    tab = tab_ref[...]
    start = tab[0:1, :].astype(F32)
    start_hi = jnp.floor(start / float(POS_BASE))
    sub = lax.broadcasted_iota(I32, (SUBLANES, LANES), 0)
    digits = jnp.where(sub == 0, start_hi, jnp.where(sub == 1, start - start_hi * float(POS_BASE), 0.0)).astype(BF16)
    have_ahead = step + 2 < n_steps
    chunk, per = rows // DISPATCH_CHUNKS, n_exp // DISPATCH_CHUNKS
    routed = jnp.zeros(base_ref.shape, F32)
    for c in range(DISPATCH_CHUNKS):
        member = _run_membership(tab, chunk, c * chunk).astype(BF16)
        picked = _dot_nt(digits, member)
        in_run = ((lax.broadcasted_iota(I32, (1, chunk), 1) + c * chunk).astype(F32)
                  - (picked[0:1, :] * float(POS_BASE) + picked[1:2, :]))
        w = jnp.where(_dot_nt(rankt_ref[...], member) == in_run, _dot_nt(gatet_ref[...], member), 0.0)
        lo, hi = _unpack_pair(buf_ref[slot, c * chunk:(c + 1) * chunk, :])
        y = jnp.concatenate([lo.astype(BF16), hi.astype(BF16)], axis=1)
        routed = routed + _dot(w.astype(BF16), y)
        fetch(ahead_runs_ref, ahead, range(c * per, (c + 1) * per), have_ahead)
    fetch_filler(ahead_runs_ref, ahead, have_ahead)
    out_ref[...] = _layer_norm(base_ref[...] + routed, g3_ref[...], b3_ref[...])


def _combine(runs, rank_t, gate_t, base, g3, b3, ys, n_exp):
    n, d = base.shape
    tm = SORT_TILE
    last = n // tm - 1
    tab = lambda f: pl.BlockSpec((None,) + runs.shape[1:], f, memory_space=pltpu.SMEM)
    tmaj = pl.BlockSpec((tm, LANES), lambda i: (i, 0))
    return pl.pallas_call(
        functools.partial(_combine_kernel, n_exp=n_exp),
        grid=(n // tm,),
        in_specs=[tab(lambda i: (i, 0, 0)), tab(lambda i: (jnp.minimum(i + 1, last), 0, 0)),
                  tab(lambda i: (jnp.minimum(i + 2, last), 0, 0)),
                  pl.BlockSpec((None,) + runs.shape[1:], lambda i: (i, 0, 0)),
                  tmaj, tmaj,
                  pl.BlockSpec((tm, d), lambda i: (i, 0)),
                  pl.BlockSpec(g3.shape, lambda i: (0, 0)),
                  pl.BlockSpec(b3.shape, lambda i: (0, 0)),
                  pl.BlockSpec(memory_space=pl.ANY)],
        out_specs=pl.BlockSpec((tm, d), lambda i: (i, 0)),
        out_shape=jax.ShapeDtypeStruct((n, d), F32),
        scratch_shapes=[pltpu.VMEM((DISPATCH_SLOTS, _sorted_rows(tm, n_exp), ys.shape[1]), ys.dtype),
                        pltpu.SemaphoreType.DMA((DISPATCH_SLOTS,))],
        compiler_params=_cparams(("arbitrary",)),
        name="combine",
    )(runs, runs, runs, runs, rank_t, gate_t, base, g3, b3, ys)


def _layer(x2d, mem2d, w_in, pool_w, pool_scale, w_out, ln1_g, ln1_b, w_cq, w_ck, w_cv, w_co, ln2_g, ln2_b,
           w_router, router_bias, exp_gate, exp_up, exp_down, sh_gate, sh_up, sh_down, ln3_g, ln3_b,
           *, alpha, batch, seq, mem_len):
    n, d = x2d.shape
    n_exp = w_router.shape[1]
    row = lambda a: a.reshape(1, -1)
    bf = lambda a: a.astype(BF16)

    dils = tuple(dil for _, dil in DILATED_PATTERNS)
    *qkv, p = _in_proj(x2d, bf(w_in), dils)
    branches = []
    for di, (window, dil) in enumerate(DILATED_PATTERNS):
        assert window // dil == ATTN_BLOCK and seq % (dil * ATTN_BLOCK) == 0
        assert IN_TILE % (dil * 2 * SUBLANES) == 0
        branches.append(_dil_attn(*qkv[3 * di:3 * di + 3], batch, seq, dil))
    (o1, l1), (o4, l4), (o16, l16) = branches
    kmem, vmem = _mem_kv(mem2d, bf(w_ck), bf(w_cv))

    wr_t = w_router.T
    wr_hi = bf(wr_t)
    wr_lo = bf(wr_t - wr_hi.astype(F32))
    x2, base, logits_t = _row_block(
        x2d, o1, o4, o16, l1, l4, l16, p, bf(pool_w), row(pool_scale), bf(w_out), row(ln1_g), row(ln1_b),
        bf(w_cq), kmem, vmem, bf(w_co), row(ln2_g), row(ln2_b), wr_hi, wr_lo, bf(sh_gate), bf(sh_up), bf(sh_down),
        alpha=alpha, batch=batch, seq=seq, mem_len=mem_len)

    bias_col = jnp.broadcast_to(router_bias.reshape(n_exp, 1), (n_exp, LANES))
    rank_tab, rank_t, gate_t, tab, counts = _route(logits_t, bias_col)
    max_rows = n * TOP_K + (n // SORT_TILE) * n_exp * (RUN_ALIGN - 1)
    n_blocks = -(-max_rows // EXPERT_BLOCK) + n_exp
    runs, bexp, seg = _place(counts, tab, n_blocks)
    xs = _dispatch(x2, rank_tab, runs, seg, n_blocks * EXPERT_BLOCK, n_exp)
    ys = _experts(bexp, xs, exp_gate, exp_up, exp_down, n_blocks)
    return _combine(runs, rank_t, gate_t, base, row(ln3_g), row(ln3_b), ys, n_exp)


def kernel(x, mem, w_in, pool_w, pool_scale, w_out, ln1_g, ln1_b, w_cq, w_ck, w_cv, w_co, ln2_g, ln2_b, w_router, router_bias, exp_gate, exp_up, exp_down, sh_gate, sh_up, sh_down, ln3_g, ln3_b):
    batch, seq, d = x.shape
    mem_len = mem.shape[1]
    depth = w_in.shape[0]
    alpha = (2.0 * depth) ** 0.25
    x2d = x.reshape(batch * seq, d)
    mem2d = mem.reshape(batch * mem_len, d)
    for l in range(depth):
        x2d = _layer(x2d, mem2d, w_in[l], pool_w[l], pool_scale[l], w_out[l], ln1_g[l], ln1_b[l], w_cq[l], w_ck[l],
                     w_cv[l], w_co[l], ln2_g[l], ln2_b[l], w_router[l], router_bias[l], exp_gate[l], exp_up[l],
                     exp_down[l], sh_gate[l], sh_up[l], sh_down[l], ln3_g[l], ln3_b[l],
                     alpha=alpha, batch=batch, seq=seq, mem_len=mem_len)
    return x2d.reshape(batch, seq, d)
```

```python
import functools

import jax
import jax.numpy as jnp
from jax import lax
from jax.experimental import pallas as pl
from jax.experimental.pallas import tpu as pltpu

F32 = jnp.float32
BF16 = jnp.bfloat16
I32 = jnp.int32
U32 = jnp.uint32

ATTN_HEADS = 8
HEAD_DIM = 64
ATTN_WIDTH = ATTN_HEADS * HEAD_DIM
DILATED_PATTERNS = ((128, 1), (512, 4), (2048, 16))
POOL_WINDOWS = (2, 4, 8, 16)
MEM_HEADS = 4
N_EXPERT_GROUPS = 8
TOPK_GROUPS = 4
TOP_K = 8
ROUTED_SCALE = 2.5
LN_EPS = 1e-5
NEG_INF = -1e30

LANES = 128
SUBLANES = 8
VMEM_LIMIT = 56 * 1024 * 1024

ATTN_BLOCK = 128
ATTN_STEP_BLOCKS = 8
ROW_TILE = 512
IN_TILE = 1024
SORT_TILE = 256
EXPERT_BLOCK = 512
EXPERT_RING = 3
RUN_ALIGN = SUBLANES
RUN_CHUNK = 64
POS_BASE = 64
DISPATCH_SLOTS = 3
DISPATCH_CHUNKS = 4


def _cparams(sem, vmem=VMEM_LIMIT):
    return pltpu.CompilerParams(dimension_semantics=sem, vmem_limit_bytes=vmem)


def _layer_norm(y, g, b):
    mu = jnp.mean(y, axis=-1, keepdims=True)
    d = y - mu
    var = jnp.mean(d * d, axis=-1, keepdims=True)
    return d * lax.rsqrt(var + LN_EPS) * g + b


def _dot(a, b):
    return jnp.dot(a, b, preferred_element_type=F32)


def _dot_nt(a, b):
    return lax.dot_general(a, b, (((1,), (1,)), ((), ())), preferred_element_type=F32)


def _split_bf16(x):
    hi = x.astype(BF16)
    lo = (x - hi.astype(F32)).astype(BF16)
    return hi, lo


def _bf16_bits(x):
    return lax.bitcast_convert_type(x.astype(BF16).astype(F32), U32)


def _pack_pair(lo, hi):
    return (_bf16_bits(lo) >> 16) | (_bf16_bits(hi) & jnp.uint32(0xFFFF0000))


def _unpack_pair(w):
    lo = lax.bitcast_convert_type(w << 16, F32)
    hi = lax.bitcast_convert_type(w & jnp.uint32(0xFFFF0000), F32)
    return lo, hi


def _in_proj_kernel(x_ref, w_ref, *refs, dils):
    n_qkv = 3 * len(dils)
    out_refs, p_ref, slab_ref = refs[:n_qkv], refs[n_qkv], refs[n_qkv + 1]
    tm = x_ref.shape[0]
    x = x_ref[...].astype(BF16)
    aw = ATTN_WIDTH
    n_slabs = aw // LANES
    for a in range(3):
        val = _dot(x, w_ref[:, a * aw:(a + 1) * aw])
        if a == 0:
            val = val * (HEAD_DIM ** -0.5)
        for s in range(n_slabs):
            slab_ref[s] = val[:, s * LANES:(s + 1) * LANES]
        for di, dil in enumerate(dils):
            dst = out_refs[3 * di + a]
            if dil == 1:
                dst[...] = val.astype(BF16)
                continue
            for r in range(dil):
                for s in range(n_slabs):
                    rows = slab_ref[s, pl.ds(r, tm // dil, stride=dil), :]
                    dst[:, r * aw + s * LANES:r * aw + (s + 1) * LANES] = rows.astype(BF16)
    p_ref[...] = _dot(x, w_ref[:, 3 * aw:])


def _in_proj(x2d, w_in_bf, dils):
    n, d = x2d.shape
    aw = ATTN_WIDTH
    pw = w_in_bf.shape[1] - 3 * aw
    tm = IN_TILE
    row = lambda rows, w: pl.BlockSpec((rows, w), lambda i: (i, 0))
    qkv_specs, qkv_shapes = [], []
    for dil in dils:
        qkv_specs += [row(tm // dil, dil * aw)] * 3
        qkv_shapes += [jax.ShapeDtypeStruct((n // dil, dil * aw), BF16)] * 3
    return pl.pallas_call(
        functools.partial(_in_proj_kernel, dils=dils),
        grid=(n // tm,),
        in_specs=[row(tm, d), pl.BlockSpec(w_in_bf.shape, lambda i: (0, 0))],
        out_specs=qkv_specs + [row(tm, pw)],
        out_shape=qkv_shapes + [jax.ShapeDtypeStruct((n, pw), F32)],
        scratch_shapes=[pltpu.VMEM((aw // LANES, tm, LANES), F32)],
        compiler_params=_cparams(("arbitrary",)),
        name="in_proj",
    )(x2d, w_in_bf)


def _dil_attn_kernel(q_ref, kc_ref, kp_ref, vc_ref, vp_ref, o_ref, lse_ref, bias_ref, *, dil):
    blk = ATTN_BLOCK
    first_step = (pl.program_id(0) == 0) & (pl.program_id(1) == 0) & (pl.program_id(2) == 0)

    @pl.when(first_step)
    def _():
        qi = lax.broadcasted_iota(I32, (blk, 2 * blk), 0)
        kj = lax.broadcasted_iota(I32, (blk, 2 * blk), 1)
        dist = qi + blk - kj
        inside = (dist >= 0) & (dist <= blk)
        penalty = (dist * dil).astype(F32)
        for h in range(ATTN_HEADS):
            slope = 2.0 ** (-8.0 * (h + 1) / ATTN_HEADS)
            bias_ref[0, h] = jnp.where(inside, -slope * penalty, NEG_INF)
            bias_ref[1, h] = jnp.where(inside & (kj >= blk), -slope * penalty, NEG_INF)

    i = pl.program_id(1)
    r = pl.program_id(2)
    lane = lax.broadcasted_iota(I32, (blk, LANES), 1)
    low_half = lane < HEAD_DIM
    n_pairs = ATTN_WIDTH // LANES
    half = n_pairs // 2
    for j in range(q_ref.shape[0] // blk):
        table = jnp.where(i == 0, 1, 0) if j == 0 else 0
        lse_tile = jnp.zeros((blk, LANES), F32)
        pair_out = []
        for hp in range(n_pairs):
            sl = slice(hp * LANES, (hp + 1) * LANES)
            q2 = q_ref[j * blk:(j + 1) * blk, sl]
            if j == 0:
                kcat = jnp.concatenate([kp_ref[:, sl], kc_ref[0:blk, sl]], axis=0)
                vcat = jnp.concatenate([vp_ref[:, sl], vc_ref[0:blk, sl]], axis=0)
            else:
                kcat = kc_ref[(j - 1) * blk:(j + 1) * blk, sl]
                vcat = vc_ref[(j - 1) * blk:(j + 1) * blk, sl]
            outs = []
            for e in range(LANES // HEAD_DIM):
                h = hp * (LANES // HEAD_DIM) + e
                keep = low_half if e == 0 else jnp.logical_not(low_half)
                qe = jnp.where(keep, q2, jnp.zeros_like(q2))
                s = _dot_nt(qe, kcat) + bias_ref[table, h]
                m = jnp.max(s, axis=1, keepdims=True)
                p = jnp.exp(s - m)
                l = jnp.sum(p, axis=1, keepdims=True)
                outs.append(_dot(p.astype(BF16), vcat) / l)
                lse_tile = jnp.where(lane == h, m + jnp.log(l), lse_tile)
            pair_out.append(jnp.where(low_half, outs[0], outs[1]))
        rows = pl.ds(j * blk * dil + r, blk, stride=dil) if dil > 1 else slice(j * blk, (j + 1) * blk)
        for w in range(half):
            o_ref[w, rows, :] = _pack_pair(pair_out[w], pair_out[w + half])
        lse_ref[rows, :] = lse_tile


def _dil_attn(q, k, v, batch, seq, dil):
    blk = ATTN_BLOCK
    sub = seq // dil
    step_blocks = min(ATTN_STEP_BLOCKS, sub // blk)
    step_rows = step_blocks * blk
    assert sub % step_rows == 0
    view = lambda a: a.reshape(batch, sub, dil * ATTN_WIDTH)
    cur = pl.BlockSpec((None, step_rows, ATTN_WIDTH), lambda b, i, r: (b, i, r))
    prev = pl.BlockSpec((None, blk, ATTN_WIDTH), lambda b, i, r: (b, jnp.maximum(i * step_blocks - 1, 0), r))
    half = ATTN_WIDTH // LANES // 2
    o, lse = pl.pallas_call(
        functools.partial(_dil_attn_kernel, dil=dil),
        grid=(batch, sub // step_rows, dil),
        in_specs=[cur, cur, prev, cur, prev],
        out_specs=[pl.BlockSpec((None, half, step_rows * dil, LANES), lambda b, i, r: (b, 0, i, 0)),
                   pl.BlockSpec((None, step_rows * dil, LANES), lambda b, i, r: (b, i, 0))],
        out_shape=[jax.ShapeDtypeStruct((batch, half, seq, LANES), U32),
                   jax.ShapeDtypeStruct((batch, seq, LANES), F32)],
        scratch_shapes=[pltpu.VMEM((2, ATTN_HEADS, blk, 2 * blk), F32)],
        compiler_params=_cparams(("arbitrary",) * 3),
        name=f"dil_attn_d{dil}",
    )(view(q), view(k), view(k), view(v), view(v))
    return o, lse.reshape(batch * seq, LANES)


def _mem_kv_kernel(mem_ref, wk_ref, wv_ref, k_ref, v_ref):
    m = mem_ref[...].astype(BF16)
    k_ref[...] = _dot(m, wk_ref[...]).astype(BF16)
    v_ref[...] = _dot(m, wv_ref[...]).astype(BF16)


def _mem_kv(mem2d, w_ck_bf, w_cv_bf):
    rows = mem2d.shape[0]
    width = w_ck_bf.shape[1]
    return pl.pallas_call(
        _mem_kv_kernel,
        out_shape=[jax.ShapeDtypeStruct((rows, width), BF16)] * 2,
        compiler_params=_cparams(None),
        name="mem_kv",
    )(mem2d, w_ck_bf, w_cv_bf)


def _row_kernel(x_ref, o1_ref, o4_ref, o16_ref, l1_ref, l4_ref, l16_ref, p_ref, halo_ref,
                poolw_ref, pscale_ref, wout_ref, g1_ref, b1_ref,
                wcq_ref, km_ref, vm_ref, wco_ref, g2_ref, b2_ref,
                wrh_ref, wrl_ref, shg_ref, shu_ref, shd_ref,
                x2_ref, base_ref, logit_ref, *, alpha, seq):
    tm = x_ref.shape[0]
    i = pl.program_id(0)
    tile_pos = (i * tm) % seq

    l1, l4, l16 = l1_ref[...], l4_ref[...], l16_ref[...]
    mx = jnp.maximum(jnp.maximum(l1, l4), l16)
    e1, e4, e16 = jnp.exp(l1 - mx), jnp.exp(l4 - mx), jnp.exp(l16 - mx)
    inv = 1.0 / (e1 + e4 + e16)
    hrow = lax.broadcasted_iota(I32, (LANES, ATTN_WIDTH), 0)
    hcol = lax.broadcasted_iota(I32, (LANES, ATTN_WIDTH), 1) // HEAD_DIM
    spread = jnp.where(hrow == hcol, 1.0, 0.0).astype(BF16)

    def widen(w):
        hi, lo = _split_bf16(w)
        return _dot(hi, spread) + _dot(lo, spread)

    def branch(o_ref):
        lo, hi = zip(*[_unpack_pair(o_ref[w]) for w in range(o_ref.shape[0])])
        return jnp.concatenate(lo + hi, axis=1)

    last = branch(o16_ref)
    attn = last + widen(e1 * inv) * (branch(o1_ref) - last) + widen(e4 * inv) * (branch(o4_ref) - last)

    halo_rows = halo_ref.shape[0]
    halo = jnp.where(tile_pos > 0, halo_ref[...], 0.0)
    ext = jnp.concatenate([halo, p_ref[...]], axis=0)
    pos = (tile_pos + lax.broadcasted_iota(I32, (tm, 1), 0)).astype(F32)
    gd = ext.shape[1] // len(POOL_WINDOWS)
    mixed = []
    for g, w in enumerate(POOL_WINDOWS):
        eg = ext[:, g * gd:(g + 1) * gd]
        acc, span = eg, 1
        while span < w:
            acc = acc + pltpu.roll(acc, span, 0)
            span *= 2
        count = jnp.minimum(pos + 1.0, float(w))
        pooled = acc[halo_rows:, :] / count - eg[halo_rows:, :]
        mixed.append(_dot(pooled.astype(BF16), poolw_ref[g]) * pscale_ref[:, g * gd:(g + 1) * gd])
    cat = jnp.concatenate([attn.astype(BF16)] + [m.astype(BF16) for m in mixed], axis=1)
    x1 = _layer_norm(alpha * x_ref[...] + _dot(cat, wout_ref[...]), g1_ref[...], b1_ref[...])

    qc = _dot(x1.astype(BF16), wcq_ref[...])
    mhd = qc.shape[1] // MEM_HEADS
    heads = []
    for h in range(MEM_HEADS):
        sl = slice(h * mhd, (h + 1) * mhd)
        s = _dot_nt(qc[:, sl].astype(BF16), km_ref[:, sl]) * (mhd ** -0.5)
        m = jnp.max(s, axis=1, keepdims=True)
        p = jnp.exp(s - m)
        l = jnp.sum(p, axis=1, keepdims=True)
        heads.append((_dot(p.astype(BF16), vm_ref[:, sl]) / l).astype(BF16))
    oc = jnp.concatenate(heads, axis=1)
    x2 = _layer_norm(alpha * x1 + _dot(oc, wco_ref[...]), g2_ref[...], b2_ref[...])
    dh = x2.shape[1] // 2
    x2_ref[...] = _pack_pair(x2[:, :dh], x2[:, dh:])

    xh, xl = _split_bf16(x2)
    logit_ref[...] = _dot_nt(wrh_ref[...], xh) + _dot_nt(wrh_ref[...], xl) + _dot_nt(wrl_ref[...], xh)

    gate = _dot(xh, shg_ref[...])
    up = _dot(xh, shu_ref[...])
    hid = gate / (1.0 + jnp.exp(-gate)) * up
    base_ref[...] = alpha * x2 + _dot(hid.astype(BF16), shd_ref[...])


def _row_block(x2d, o1, o4, o16, l1, l4, l16, p, pool_w_bf, pool_scale, w_out_bf, g1, b1,
               w_cq_bf, kmem, vmem, w_co_bf, g2, b2, wr_hi, wr_lo, sh_g, sh_u, sh_d,
               *, alpha, batch, seq, mem_len):
    n, d = x2d.shape
    tm = ROW_TILE
    halo_rows = max(POOL_WINDOWS)
    steps_per_seq = seq // tm
    row = lambda w: pl.BlockSpec((tm, w), lambda i: (i, 0))
    full = lambda a: pl.BlockSpec(a.shape, lambda i: (0,) * a.ndim)
    halo = pl.BlockSpec((halo_rows, p.shape[1]), lambda i: (jnp.maximum(i * (tm // halo_rows) - 1, 0), 0))
    memspec = pl.BlockSpec((mem_len, kmem.shape[1]), lambda i: (i // steps_per_seq, 0))
    n_exp = wr_hi.shape[0]
    branch = pl.BlockSpec((None, o1.shape[1], tm, LANES), lambda i: (i // steps_per_seq, 0, i % steps_per_seq, 0))
    return pl.pallas_call(
        functools.partial(_row_kernel, alpha=alpha, seq=seq),
        grid=(n // tm,),
        in_specs=[row(d), branch, branch, branch, row(LANES), row(LANES), row(LANES),
                  row(p.shape[1]), halo,
                  full(pool_w_bf), full(pool_scale), full(w_out_bf), full(g1), full(b1),
                  full(w_cq_bf), memspec, memspec, full(w_co_bf), full(g2), full(b2),
                  full(wr_hi), full(wr_lo), full(sh_g), full(sh_u), full(sh_d)],
        out_specs=[row(d // 2), row(d), pl.BlockSpec((n_exp, tm), lambda i: (0, i))],
        out_shape=[jax.ShapeDtypeStruct((n, d // 2), U32), jax.ShapeDtypeStruct((n, d), F32),
                   jax.ShapeDtypeStruct((n_exp, n), F32)],
        compiler_params=_cparams(("arbitrary",)),
        name="row_block",
    )(x2d, o1, o4, o16, l1, l4, l16, p, p, pool_w_bf, pool_scale, w_out_bf, g1, b1,
      w_cq_bf, kmem, vmem, w_co_bf, g2, b2, wr_hi, wr_lo, sh_g, sh_u, sh_d)


def _beat_counts(vals, n_rows):
    tn = vals.shape[1]
    n_tiles = n_rows // SUBLANES
    tiles = [vals[t * SUBLANES:(t + 1) * SUBLANES, :] for t in range(n_tiles)]
    sub = lax.broadcasted_iota(I32, (SUBLANES, tn), 0)
    counts = [jnp.zeros((SUBLANES, tn), F32) for _ in range(n_tiles)]
    for e in range(n_rows):
        te, je = divmod(e, SUBLANES)
        row = jnp.broadcast_to(vals[e:e + 1, :], (SUBLANES, tn))
        for t in range(n_tiles):
            strict = jnp.where(row > tiles[t], 1.0, 0.0)
            loose = jnp.where(row >= tiles[t], 1.0, 0.0)
            if t < te:
                beat = strict
            elif t > te:
                beat = loose
            else:
                beat = jnp.where(sub > je, loose, strict)
            counts[t] = counts[t] + beat
    return jnp.concatenate(counts, axis=0)


def _top_k_mask(vals, k):
    ridx = lax.broadcasted_iota(I32, vals.shape, 0).astype(F32)
    mask = jnp.zeros(vals.shape, F32)
    work = vals
    for _ in range(k):
        top = jnp.max(work, axis=0, keepdims=True)
        first = jnp.min(jnp.where(work == top, ridx, float(vals.shape[0])), axis=0, keepdims=True)
        hit = ridx == first
        mask = jnp.where(hit, 1.0, mask)
        work = jnp.where(hit, -jnp.inf, work)
    return mask


def _route_kernel(logit_ref, bias_ref, rank_ref, rankt_ref, gatet_ref, tab_ref, count_ref, carry_ref, rcarry_ref,
                  *, per_group):
    n_exp, tn = logit_ref.shape
    step = pl.program_id(0)

    @pl.when(step == 0)
    def _():
        carry_ref[...] = jnp.zeros_like(carry_ref)
        rcarry_ref[...] = jnp.zeros_like(rcarry_ref)

    scores = 1.0 / (1.0 + jnp.exp(-logit_ref[...]))
    biased = scores + bias_ref[:, 0:1]
    n_groups = n_exp // per_group

    gscore = []
    for g in range(n_groups):
        tile = biased[g * per_group:(g + 1) * per_group, :]
        inner = _beat_counts(tile, per_group)
        gscore.append(jnp.sum(jnp.where(inner < 2.0, tile, 0.0), axis=0, keepdims=True))
    gscore = jnp.concatenate(gscore, axis=0)
    grank = _beat_counts(gscore, n_groups)
    grank = jnp.concatenate(
        [jnp.broadcast_to(grank[g:g + 1, :], (per_group, tn)) for g in range(n_groups)], axis=0)
    masked = jnp.where(grank < float(TOPK_GROUPS), biased, NEG_INF)
    self32 = _top_k_mask(masked, TOP_K)
    selbf = self32.astype(BF16)

    tr = lax.broadcasted_iota(I32, (tn, tn), 0)
    tc = lax.broadcasted_iota(I32, (tn, tn), 1)
    before = jnp.where(tr < tc, 1.0, 0.0).astype(BF16)
    lrank = _dot(selbf, before)
    align = float(RUN_ALIGN)
    lcount_col = jnp.broadcast_to(jnp.sum(self32, axis=1, keepdims=True), (n_exp, LANES))
    sel_wide = jnp.concatenate([selbf, jnp.zeros((LANES - n_exp, tn), BF16)], axis=0)
    lcount_row = _dot_nt(jnp.ones((SUBLANES, tn), BF16), sel_wide)
    lcount_col = jnp.ceil(lcount_col / align) * align
    lcount_row = jnp.ceil(lcount_row / align) * align
    wr = lax.broadcasted_iota(I32, (LANES, LANES), 0)
    wc = lax.broadcasted_iota(I32, (LANES, LANES), 1)
    lstart_row = _dot(lcount_row.astype(BF16), jnp.where(wr < wc, 1.0, 0.0).astype(BF16))

    sub = lax.broadcasted_iota(I32, (SUBLANES, LANES), 0)
    tab_ref[...] = jnp.where(sub == 0, lstart_row, jnp.where(sub == 1, lcount_row, jnp.where(
        sub == 2, rcarry_ref[...], 0.0))).astype(I32)
    rcarry_ref[...] = rcarry_ref[...] + lcount_row
    carry_ref[...] = carry_ref[...] + lcount_col
    count_ref[...] = carry_ref[...]

    denom = jnp.sum(self32 * scores, axis=0, keepdims=True)
    gate = self32 * scores / denom * ROUTED_SCALE
    pad = jnp.zeros((LANES - n_exp, tn), F32)
    widen = lambda v: jnp.concatenate([v, pad], axis=0)
    rank_tab = widen(jnp.where(self32 > 0.0, lrank, -1.0))
    rank_ref[...] = rank_tab.astype(BF16)
    rankt_ref[...] = rank_tab.T.astype(BF16)
    gatet_ref[...] = widen(gate).T.astype(BF16)


def _route(logits_t, bias_col):
    n_exp, n = logits_t.shape
    tn = SORT_TILE
    emaj = jax.ShapeDtypeStruct((LANES, n), BF16)
    tmaj = jax.ShapeDtypeStruct((n, LANES), BF16)
    emaj_spec = pl.BlockSpec((LANES, tn), lambda i: (0, i))
    tmaj_spec = pl.BlockSpec((tn, LANES), lambda i: (i, 0))
    return pl.pallas_call(
        functools.partial(_route_kernel, per_group=n_exp // N_EXPERT_GROUPS),
        grid=(n // tn,),
        in_specs=[pl.BlockSpec((n_exp, tn), lambda i: (0, i)), pl.BlockSpec(bias_col.shape, lambda i: (0, 0))],
        out_specs=[emaj_spec, tmaj_spec, tmaj_spec,
                   pl.BlockSpec((None, SUBLANES, LANES), lambda i: (i, 0, 0)),
                   pl.BlockSpec((n_exp, LANES), lambda i: (0, 0))],
        out_shape=[emaj, tmaj, tmaj, jax.ShapeDtypeStruct((n // tn, SUBLANES, LANES), I32),
                   jax.ShapeDtypeStruct((n_exp, LANES), F32)],
        scratch_shapes=[pltpu.VMEM((n_exp, LANES), F32), pltpu.VMEM((SUBLANES, LANES), F32)],
        compiler_params=_cparams(("arbitrary",)),
        name="route",
    )(logits_t, bias_col)


def _place_kernel(count_ref, tab_ref, runs_ref, bexp_ref, seg_ref, *, n_blocks):
    n_exp = count_ref.shape[0]
    blk = float(EXPERT_BLOCK)
    counts = count_ref[...]
    padded = jnp.ceil(counts / blk) * blk
    ridx = lax.broadcasted_iota(I32, (n_exp, LANES), 0)
    ends = padded
    shift = 1
    while shift < n_exp:
        ends = ends + jnp.where(ridx >= shift, pltpu.roll(ends, shift, 0), 0.0)
        shift *= 2
    starts = ends - padded

    tab = tab_ref[...]
    last = tab[tab.shape[0] - 1]
    total_row = (last[2:3, :] + last[1:2, :]).astype(F32)
    padded_row = jnp.ceil(total_row / blk) * blk
    lidx = lax.broadcasted_iota(I32, (1, LANES), 1)
    ends_row = padded_row
    shift = 1
    while shift < LANES:
        ends_row = ends_row + jnp.where(lidx >= shift, pltpu.roll(ends_row, shift, 1), 0.0)
        shift *= 2
    starts_row = (ends_row - padded_row).astype(I32)
    sub = lax.broadcasted_iota(I32, tab.shape, 1)
    runs_ref[...] = tab + jnp.where(sub == 2, starts_row[None], 0)

    nbp = bexp_ref.shape[1]
    blk_start = (lax.broadcasted_iota(I32, (n_exp, nbp), 1) * EXPERT_BLOCK).astype(F32)
    ends_wide = jnp.broadcast_to(ends[:, 0:1], (n_exp, nbp))
    count_le = lambda bound: jnp.minimum(
        jnp.sum(jnp.where(ends_wide <= bound, 1.0, 0.0), axis=0, keepdims=True), float(n_exp - 1))
    bexp = count_le(blk_start)
    used = ends[n_exp - 1:n_exp, 0:1] / blk
    seg_end = jnp.min(jnp.where(ends_wide > blk_start, ends_wide, 2.0 * float(nbp * EXPERT_BLOCK)),
                      axis=0, keepdims=True)
    pad_wide = jnp.broadcast_to((padded - counts)[:, 0:1], (n_exp, nbp))
    data_rows = (seg_end - jnp.sum(jnp.where(ends_wide == seg_end, pad_wide, 0.0), axis=0, keepdims=True)
                 - blk_start[0:1, :])
    short = jnp.where(data_rows <= blk / 2, 1.0, 0.0)
    sub8 = lax.broadcasted_iota(I32, (SUBLANES, nbp), 0)
    lane = lax.broadcasted_iota(I32, (SUBLANES, nbp), 1)
    plan = jnp.where(sub8 == 0, jnp.where(lane == n_blocks, used, bexp),
                     jnp.where(sub8 == 1, count_le(seg_end),
                               jnp.where(sub8 == 2, seg_end / blk, jnp.where(sub8 == 3, short, 0.0))))
    bexp_ref[...] = plan.astype(I32)
    lane2 = lax.broadcasted_iota(I32, (n_exp, LANES), 1)
    seg_ref[...] = jnp.where(lane2 == 0, starts + counts, ends).astype(I32)


def _place(counts, tab, n_blocks):
    n_exp = counts.shape[0]
    nbp = -(-(n_blocks + 1) // LANES) * LANES
    return pl.pallas_call(
        functools.partial(_place_kernel, n_blocks=n_blocks),
        out_shape=[jax.ShapeDtypeStruct(tab.shape, I32), jax.ShapeDtypeStruct((SUBLANES, nbp), I32),
                   jax.ShapeDtypeStruct((n_exp, LANES), I32)],
        compiler_params=_cparams(None),
        name="route_place",
    )(counts, tab)


def _copy_rows(src_ref, dst_ref, sem, src0, dst0, n, wait=False):
    def piece(off, p):
        src = src_ref.at[pl.ds(pl.multiple_of(src0 + off, RUN_ALIGN), p), :]
        dst = dst_ref.at[pl.ds(pl.multiple_of(dst0 + off, RUN_ALIGN), p), :]
        cp = pltpu.make_async_copy(src, dst, sem)
        if wait:
            cp.wait()
        else:
            cp.start()

    def chunk(c, carry):
        piece(c * RUN_CHUNK, RUN_CHUNK)
        return carry
    lax.fori_loop(0, lax.shift_right_logical(n, RUN_CHUNK.bit_length() - 1), chunk, 0)
    p = RUN_CHUNK // 2
    while p >= RUN_ALIGN:
        @pl.when((n & p) != 0)
        def _(p=p):
            piece(n & (-2 * p), p)
        p //= 2


def _start_run(src_ref, dst_ref, sem, src0, dst0, n, limit, enable=None):
    p = RUN_ALIGN
    while p * 2 <= limit:
        p *= 2
    while p >= RUN_ALIGN:
        hit = (n & p) != 0
        @pl.when(hit if enable is None else hit & enable)
        def _(p=p):
            off = n & (-2 * p)
            src = src_ref.at[pl.ds(pl.multiple_of(src0 + off, RUN_ALIGN), p), :]
            dst = dst_ref.at[pl.ds(pl.multiple_of(dst0 + off, RUN_ALIGN), p), :]
            pltpu.make_async_copy(src, dst, sem).start()
        p //= 2


def _run_membership(tab, rows, first_row=0):
    j = lax.broadcasted_iota(I32, (rows, LANES), 0) + first_row
    start = tab[0:1, :]
    end = start + tab[1:2, :]
    return jnp.where(j >= start, jnp.where(j < end, 1.0, 0.0), 0.0)


def _dispatch_kernel(prev_runs_ref, runs_ref, seg_ref, tab_ref, rank_ref, x_ref, xs_ref, buf_ref, zero_ref, sems,
                     zsem, *, n_exp):
    step, n_steps = pl.program_id(0), pl.num_programs(0)
    tm = x_ref.shape[0]
    n_slots, rows = buf_ref.shape[:2]
    pad_rows = rows - TOP_K * tm
    cur, prev, old = step % n_slots, (step + n_slots - 1) % n_slots, (step + n_slots - 2) % n_slots

    def start_runs(tab_ref, s, experts, enable=None):
        for e in experts:
            _start_run(buf_ref.at[s], xs_ref, sems.at[s], tab_ref[0, e], tab_ref[2, e], tab_ref[1, e], tm, enable)

    def start_filler(tab_ref, s, enable=None):
        used = tab_ref[0, n_exp - 1] + tab_ref[1, n_exp - 1]
        spare = xs_ref.shape[0] - (n_slots - s) * pad_rows
        _start_run(buf_ref.at[s], xs_ref, sems.at[s], used, spare, rows - used, pad_rows, enable)

    def drain(s):
        pltpu.make_async_copy(buf_ref.at[s], xs_ref.at[pl.ds(0, rows), :], sems.at[s]).wait()

    @pl.when(step == 0)
    def _():
        zero_ref[...] = jnp.zeros_like(zero_ref)
        for wait in (False, True):
            def per_expert(e, c, wait=wait):
                first = seg_ref[e, 0]
                _copy_rows(zero_ref, xs_ref, zsem, 0, first, seg_ref[e, 1] - first, wait)
                return c
            lax.fori_loop(0, n_exp, per_expert, 0)

    lo, hi = _unpack_pair(x_ref[...])
    x = jnp.concatenate([lo.astype(BF16), hi.astype(BF16)], axis=1)
    tab = tab_ref[...]
    bits = lambda v: lax.bitcast_convert_type(v, U32)
    chunk, per = rows // DISPATCH_CHUNKS, n_exp // DISPATCH_CHUNKS
    for c in range(DISPATCH_CHUNKS):
        member = _run_membership(tab, chunk, c * chunk)
        run_start = jnp.sum(member * tab[0:1, :].astype(F32), axis=1, keepdims=True)
        in_run = (lax.broadcasted_iota(I32, (chunk, 1), 0) + c * chunk).astype(F32) - run_start
        onehot = jnp.where(_dot(member.astype(BF16), rank_ref[...]) == in_run, 1.0, 0.0)
        srt = _dot(onehot.astype(BF16), x)
        dh = srt.shape[1] // 2
        buf_ref[cur, c * chunk:(c + 1) * chunk, :] = (bits(srt[:, :dh]) >> 16) | bits(srt[:, dh:])
        start_runs(prev_runs_ref, prev, range(c * per, (c + 1) * per), step > 0)
    start_filler(prev_runs_ref, prev, step > 0)

    @pl.when(step >= 2)
    def _():
        drain(old)

    @pl.when(step == n_steps - 1)
    def _():
        start_runs(runs_ref, cur, range(n_exp))
        start_filler(runs_ref, cur)

        @pl.when(step >= 1)
        def _():
            drain(prev)
        drain(cur)


def _sorted_rows(tm, n_exp):
    return TOP_K * tm + n_exp * RUN_ALIGN


def _dispatch(x2, rank_tab, runs, seg, n_rows, n_exp):
    n, d = x2.shape
    tm = SORT_TILE
    rows = _sorted_rows(tm, n_exp)
    assert rows % DISPATCH_CHUNKS == 0 and n_exp % DISPATCH_CHUNKS == 0
    tab = (None,) + runs.shape[1:]
    return pl.pallas_call(
        functools.partial(_dispatch_kernel, n_exp=n_exp),
        grid=(n // tm,),
        in_specs=[pl.BlockSpec(tab, lambda i: (jnp.maximum(i - 1, 0), 0, 0), memory_space=pltpu.SMEM),
                  pl.BlockSpec(tab, lambda i: (i, 0, 0), memory_space=pltpu.SMEM),
                  pl.BlockSpec(seg.shape, lambda i: (0, 0), memory_space=pltpu.SMEM),
                  pl.BlockSpec(tab, lambda i: (i, 0, 0)),
                  pl.BlockSpec((LANES, tm), lambda i: (0, i)),
                  pl.BlockSpec((tm, d), lambda i: (i, 0))],
        out_specs=pl.BlockSpec(memory_space=pl.ANY),
        out_shape=jax.ShapeDtypeStruct((n_rows + DISPATCH_SLOTS * (rows - TOP_K * tm), d), x2.dtype),
        scratch_shapes=[pltpu.VMEM((DISPATCH_SLOTS, rows, d), x2.dtype), pltpu.VMEM((EXPERT_BLOCK, d), x2.dtype),
                        pltpu.SemaphoreType.DMA((DISPATCH_SLOTS,)), pltpu.SemaphoreType.DMA],
        compiler_params=_cparams(("arbitrary",)),
        name="dispatch",
    )(runs, runs, seg, runs, rank_tab, x2)


def _expert_kernel(plan_ref, xs_ref, wg_ref, wu_ref, wd_ref, ys_ref, xbuf_ref, xsems, wg_f32, wu_f32, wd_f32,
                   wsems, wslot_ref, wg_bf, wu_bf, wd_bf, *, n_blocks):
    i = pl.program_id(0)
    used = plan_ref[0, n_blocks]
    depth, blk = xbuf_ref.shape[:2]

    def fetch(j):
        rows = pl.ds(pl.multiple_of(j * blk, blk), blk)
        return pltpu.make_async_copy(xs_ref.at[rows, :], xbuf_ref.at[j % depth], xsems.at[j % depth])

    def weights(e, s):
        return [pltpu.make_async_copy(src.at[e], dst.at[s], wsems.at[s])
                for src, dst in ((wg_ref, wg_f32), (wu_ref, wu_f32), (wd_ref, wd_f32))]

    @pl.when(i == 0)
    def _():
        for j in range(depth - 1):
            @pl.when(j < used)
            def _(j=j):
                fetch(j).start()

    @pl.when(i + depth - 1 < used)
    def _():
        fetch(i + depth - 1).start()

    @pl.when((i == 0) & (used > 0))
    def _():
        wslot_ref[0] = 0
        for cp in weights(plan_ref[0, 0], 0):
            cp.start()

    @pl.when(i < used)
    def _():
        @pl.when((i == 0) | (plan_ref[0, i] != plan_ref[0, jnp.maximum(i - 1, 0)]))
        def _():
            s = wslot_ref[0]
            for cp in weights(plan_ref[0, i], s):
                cp.wait()

            @pl.when(plan_ref[2, i] < used)
            def _():
                for cp in weights(plan_ref[1, i], 1 - s):
                    cp.start()
            wslot_ref[0] = 1 - s
            wg_bf[...] = wg_f32[s].astype(BF16)
            wu_bf[...] = wu_f32[s].astype(BF16)
            wd_bf[...] = wd_f32[s].astype(BF16)

        fetch(i).wait()

        def swiglu(n_rows):
            lo, hi = _unpack_pair(xbuf_ref[i % depth, 0:n_rows, :])
            x = jnp.concatenate([lo.astype(BF16), hi.astype(BF16)], axis=1)
            gate = _dot(x, wg_bf[...])
            up = _dot(x, wu_bf[...])
            hid = gate / (1.0 + jnp.exp(-gate)) * up
            y = _dot(hid.astype(BF16), wd_bf[...])
            dh = y.shape[1] // 2
            ys_ref[0:n_rows, :] = _pack_pair(y[:, :dh], y[:, dh:])

        short = plan_ref[3, i] == 1

        @pl.when(short)
        def _():
            swiglu(blk // 2)
            ys_ref[blk // 2:, :] = jnp.zeros((blk - blk // 2, ys_ref.shape[1]), ys_ref.dtype)

        @pl.when(jnp.logical_not(short))
        def _():
            swiglu(blk)


def _experts(bexp, xs, exp_gate, exp_up, exp_down, n_blocks):
    n_rows = xs.shape[0]
    d, hidden = exp_gate.shape[1:]
    blk = EXPERT_BLOCK

    def rows(i, b):
        return (jnp.minimum(i, jnp.maximum(b[0, n_blocks] - 1, 0)), 0)

    hbm = pl.BlockSpec(memory_space=pl.ANY)
    grid_spec = pltpu.PrefetchScalarGridSpec(
        num_scalar_prefetch=1,
        grid=(n_blocks,),
        in_specs=[hbm, hbm, hbm, hbm],
        out_specs=pl.BlockSpec((blk, d // 2), rows),
        scratch_shapes=[pltpu.VMEM((EXPERT_RING, blk, d // 2), xs.dtype), pltpu.SemaphoreType.DMA((EXPERT_RING,)),
                        pltpu.VMEM((2, d, hidden), F32), pltpu.VMEM((2, d, hidden), F32),
                        pltpu.VMEM((2, hidden, d), F32), pltpu.SemaphoreType.DMA((2,)), pltpu.SMEM((1,), I32),
                        pltpu.VMEM((d, hidden), BF16), pltpu.VMEM((d, hidden), BF16), pltpu.VMEM((hidden, d), BF16)],
    )
    return pl.pallas_call(
        functools.partial(_expert_kernel, n_blocks=n_blocks),
        grid_spec=grid_spec,
        out_shape=jax.ShapeDtypeStruct((n_rows, d // 2), U32),
        compiler_params=_cparams(("arbitrary",)),
        name="experts",
    )(bexp, xs, exp_gate, exp_up, exp_down)


def _combine_kernel(runs_ref, next_runs_ref, ahead_runs_ref, tab_ref, rankt_ref, gatet_ref, base_ref, g3_ref, b3_ref,
                    ys_ref, out_ref, buf_ref, sems, *, n_exp):
    step, n_steps = pl.program_id(0), pl.num_programs(0)
    tm = base_ref.shape[0]
    n_slots, rows = buf_ref.shape[:2]
    slot, ahead = step % n_slots, (step + 2) % n_slots

    def fetch(tab_ref, s, experts, enable=None):
        for e in experts:
            _start_run(ys_ref, buf_ref.at[s], sems.at[s], tab_ref[2, e], tab_ref[0, e], tab_ref[1, e], tm, enable)

    def fetch_filler(tab_ref, s, enable=None):
        used = tab_ref[0, n_exp - 1] + tab_ref[1, n_exp - 1]
        _start_run(ys_ref, buf_ref.at[s], sems.at[s], 0, used, rows - used, rows - TOP_K * tm, enable)

    @pl.when(step == 0)
    def _():
        fetch(runs_ref, 0, range(n_exp))
        fetch_filler(runs_ref, 0)
        fetch(next_runs_ref, 1, range(n_exp), n_steps > 1)
        fetch_filler(next_runs_ref, 1, n_steps > 1)

    pltpu.make_async_copy(ys_ref.at[pl.ds(0, rows), :], buf_ref.at[slot], sems.at[slot]).wait()

    tab = tab_ref[...]
    start = tab[0:1, :].astype(F32)
    start_hi = jnp.floor(start / float(POS_BASE))
    sub = lax.broadcasted_iota(I32, (SUBLANES, LANES), 0)
    digits = jnp.where(sub == 0, start_hi, jnp.where(sub == 1, start - start_hi * float(POS_BASE), 0.0)).astype(BF16)
    have_ahead = step + 2 < n_steps
    chunk, per = rows // DISPATCH_CHUNKS, n_exp // DISPATCH_CHUNKS
    routed = jnp.zeros(base_ref.shape, F32)
    for c in range(DISPATCH_CHUNKS):
        member = _run_membership(tab, chunk, c * chunk).astype(BF16)
        picked = _dot_nt(digits, member)
        in_run = ((lax.broadcasted_iota(I32, (1, chunk), 1) + c * chunk).astype(F32)
                  - (picked[0:1, :] * float(POS_BASE) + picked[1:2, :]))
        w = jnp.where(_dot_nt(rankt_ref[...], member) == in_run, _dot_nt(gatet_ref[...], member), 0.0)
        lo, hi = _unpack_pair(buf_ref[slot, c * chunk:(c + 1) * chunk, :])
        y = jnp.concatenate([lo.astype(BF16), hi.astype(BF16)], axis=1)
        routed = routed + _dot(w.astype(BF16), y)
        fetch(ahead_runs_ref, ahead, range(c * per, (c + 1) * per), have_ahead)
    fetch_filler(ahead_runs_ref, ahead, have_ahead)
    out_ref[...] = _layer_norm(base_ref[...] + routed, g3_ref[...], b3_ref[...])


def _combine(runs, rank_t, gate_t, base, g3, b3, ys, n_exp):
    n, d = base.shape
    tm = SORT_TILE
    last = n // tm - 1
    tab = lambda f: pl.BlockSpec((None,) + runs.shape[1:], f, memory_space=pltpu.SMEM)
    tmaj = pl.BlockSpec((tm, LANES), lambda i: (i, 0))
    return pl.pallas_call(
        functools.partial(_combine_kernel, n_exp=n_exp),
        grid=(n // tm,),
        in_specs=[tab(lambda i: (i, 0, 0)), tab(lambda i: (jnp.minimum(i + 1, last), 0, 0)),
                  tab(lambda i: (jnp.minimum(i + 2, last), 0, 0)),
                  pl.BlockSpec((None,) + runs.shape[1:], lambda i: (i, 0, 0)),
                  tmaj, tmaj,
                  pl.BlockSpec((tm, d), lambda i: (i, 0)),
                  pl.BlockSpec(g3.shape, lambda i: (0, 0)),
                  pl.BlockSpec(b3.shape, lambda i: (0, 0)),
                  pl.BlockSpec(memory_space=pl.ANY)],
        out_specs=pl.BlockSpec((tm, d), lambda i: (i, 0)),
        out_shape=jax.ShapeDtypeStruct((n, d), F32),
        scratch_shapes=[pltpu.VMEM((DISPATCH_SLOTS, _sorted_rows(tm, n_exp), ys.shape[1]), ys.dtype),
                        pltpu.SemaphoreType.DMA((DISPATCH_SLOTS,))],
        compiler_params=_cparams(("arbitrary",)),
        name="combine",
    )(runs, runs, runs, runs, rank_t, gate_t, base, g3, b3, ys)


def _layer(x2d, mem2d, w_in, pool_w, pool_scale, w_out, ln1_g, ln1_b, w_cq, w_ck, w_cv, w_co, ln2_g, ln2_b,
           w_router, router_bias, exp_gate, exp_up, exp_down, sh_gate, sh_up, sh_down, ln3_g, ln3_b,
           *, alpha, batch, seq, mem_len):
    n, d = x2d.shape
    n_exp = w_router.shape[1]
    row = lambda a: a.reshape(1, -1)
    bf = lambda a: a.astype(BF16)

    dils = tuple(dil for _, dil in DILATED_PATTERNS)
    *qkv, p = _in_proj(x2d, bf(w_in), dils)
    branches = []
    for di, (window, dil) in enumerate(DILATED_PATTERNS):
        assert window // dil == ATTN_BLOCK and seq % (dil * ATTN_BLOCK) == 0
        assert IN_TILE % (dil * 2 * SUBLANES) == 0
        branches.append(_dil_attn(*qkv[3 * di:3 * di + 3], batch, seq, dil))
    (o1, l1), (o4, l4), (o16, l16) = branches
    kmem, vmem = _mem_kv(mem2d, bf(w_ck), bf(w_cv))

    wr_t = w_router.T
    wr_hi = bf(wr_t)
    wr_lo = bf(wr_t - wr_hi.astype(F32))
    x2, base, logits_t = _row_block(
        x2d, o1, o4, o16, l1, l4, l16, p, bf(pool_w), row(pool_scale), bf(w_out), row(ln1_g), row(ln1_b),
        bf(w_cq), kmem, vmem, bf(w_co), row(ln2_g), row(ln2_b), wr_hi, wr_lo, bf(sh_gate), bf(sh_up), bf(sh_down),
        alpha=alpha, batch=batch, seq=seq, mem_len=mem_len)

    bias_col = jnp.broadcast_to(router_bias.reshape(n_exp, 1), (n_exp, LANES))
    rank_tab, rank_t, gate_t, tab, counts = _route(logits_t, bias_col)
    max_rows = n * TOP_K + (n // SORT_TILE) * n_exp * (RUN_ALIGN - 1)
    n_blocks = -(-max_rows // EXPERT_BLOCK) + n_exp
    runs, bexp, seg = _place(counts, tab, n_blocks)
    xs = _dispatch(x2, rank_tab, runs, seg, n_blocks * EXPERT_BLOCK, n_exp)
    ys = _experts(bexp, xs, exp_gate, exp_up, exp_down, n_blocks)
    return _combine(runs, rank_t, gate_t, base, row(ln3_g), row(ln3_b), ys, n_exp)


def kernel(x, mem, w_in, pool_w, pool_scale, w_out, ln1_g, ln1_b, w_cq, w_ck, w_cv, w_co, ln2_g, ln2_b, w_router, router_bias, exp_gate, exp_up, exp_down, sh_gate, sh_up, sh_down, ln3_g, ln3_b):
    batch, seq, d = x.shape
    mem_len = mem.shape[1]
    depth = w_in.shape[0]
    alpha = (2.0 * depth) ** 0.25
    x2d = x.reshape(batch * seq, d)
    mem2d = mem.reshape(batch * mem_len, d)
    for l in range(depth):
        x2d = _layer(x2d, mem2d, w_in[l], pool_w[l], pool_scale[l], w_out[l], ln1_g[l], ln1_b[l], w_cq[l], w_ck[l],
                     w_cv[l], w_co[l], ln2_g[l], ln2_b[l], w_router[l], router_bias[l], exp_gate[l], exp_up[l],
                     exp_down[l], sh_gate[l], sh_up[l], sh_down[l], ln3_g[l], ln3_b[l],
                     alpha=alpha, batch=batch, seq=seq, mem_len=mem_len)
    return x2d.reshape(batch, seq, d)
```

```python
import functools

import jax
import jax.numpy as jnp
from jax import lax
from jax.experimental import pallas as pl
from jax.experimental.pallas import tpu as pltpu

F32 = jnp.float32
BF16 = jnp.bfloat16
I32 = jnp.int32
U32 = jnp.uint32

ATTN_HEADS = 8
HEAD_DIM = 64
ATTN_WIDTH = ATTN_HEADS * HEAD_DIM
DILATED_PATTERNS = ((128, 1), (512, 4), (2048, 16))
POOL_WINDOWS = (2, 4, 8, 16)
MEM_HEADS = 4
N_EXPERT_GROUPS = 8
TOPK_GROUPS = 4
TOP_K = 8
ROUTED_SCALE = 2.5
LN_EPS = 1e-5
NEG_INF = -1e30

LANES = 128
SUBLANES = 8
VMEM_LIMIT = 56 * 1024 * 1024

ATTN_BLOCK = 128
ATTN_STEP_BLOCKS = 8
ROW_TILE = 512
IN_TILE = 512
SORT_TILE = 256
EXPERT_BLOCK = 512
EXPERT_RING = 3
RUN_ALIGN = SUBLANES
RUN_CHUNK = 64
POS_BASE = 64
DISPATCH_SLOTS = 3
DISPATCH_CHUNKS = 4


def _cparams(sem, vmem=VMEM_LIMIT):
    return pltpu.CompilerParams(dimension_semantics=sem, vmem_limit_bytes=vmem)


def _layer_norm(y, g, b):
    mu = jnp.mean(y, axis=-1, keepdims=True)
    d = y - mu
    var = jnp.mean(d * d, axis=-1, keepdims=True)
    return d * lax.rsqrt(var + LN_EPS) * g + b


def _dot(a, b):
    return jnp.dot(a, b, preferred_element_type=F32)


def _dot_nt(a, b):
    return lax.dot_general(a, b, (((1,), (1,)), ((), ())), preferred_element_type=F32)


def _split_bf16(x):
    hi = x.astype(BF16)
    lo = (x - hi.astype(F32)).astype(BF16)
    return hi, lo


def _bf16_bits(x):
    return lax.bitcast_convert_type(x.astype(BF16).astype(F32), U32)


def _pack_pair(lo, hi):
    return (_bf16_bits(lo) >> 16) | (_bf16_bits(hi) & jnp.uint32(0xFFFF0000))


def _unpack_pair(w):
    lo = lax.bitcast_convert_type(w << 16, F32)
    hi = lax.bitcast_convert_type(w & jnp.uint32(0xFFFF0000), F32)
    return lo, hi


def _in_proj_kernel(x_ref, w_ref, *refs, dils):
    n_qkv = 3 * len(dils)
    out_refs, p_ref, slab_ref, wbf_ref = refs[:n_qkv], refs[n_qkv], refs[n_qkv + 1], refs[n_qkv + 2]

    @pl.when(pl.program_id(0) == 0)
    def _():
        wbf_ref[...] = w_ref[...].astype(BF16)

    w_ref = wbf_ref
    tm = x_ref.shape[0]
    x = x_ref[...].astype(BF16)
    aw = ATTN_WIDTH
    n_slabs = aw // LANES
    for a in range(3):
        val = _dot(x, w_ref[:, a * aw:(a + 1) * aw])
        if a == 0:
            val = val * (HEAD_DIM ** -0.5)
        for s in range(n_slabs):
            slab_ref[s] = val[:, s * LANES:(s + 1) * LANES]
        for di, dil in enumerate(dils):
            dst = out_refs[3 * di + a]
            if dil == 1:
                dst[...] = val.astype(BF16)
                continue
            for r in range(dil):
                for s in range(n_slabs):
                    rows = slab_ref[s, pl.ds(r, tm // dil, stride=dil), :]
                    dst[:, r * aw + s * LANES:r * aw + (s + 1) * LANES] = rows.astype(BF16)
    p_ref[...] = _dot(x, w_ref[:, 3 * aw:])


def _in_proj(x2d, w_in, dils):
    n, d = x2d.shape
    aw = ATTN_WIDTH
    pw = w_in.shape[1] - 3 * aw
    tm = IN_TILE
    row = lambda rows, w: pl.BlockSpec((rows, w), lambda i: (i, 0))
    qkv_specs, qkv_shapes = [], []
    for dil in dils:
        qkv_specs += [row(tm // dil, dil * aw)] * 3
        qkv_shapes += [jax.ShapeDtypeStruct((n // dil, dil * aw), BF16)] * 3
    return pl.pallas_call(
        functools.partial(_in_proj_kernel, dils=dils),
        grid=(n // tm,),
        in_specs=[row(tm, d), pl.BlockSpec(w_in.shape, lambda i: (0, 0))],
        out_specs=qkv_specs + [row(tm, pw)],
        out_shape=qkv_shapes + [jax.ShapeDtypeStruct((n, pw), F32)],
        scratch_shapes=[pltpu.VMEM((aw // LANES, tm, LANES), F32), pltpu.VMEM(w_in.shape, BF16)],
        compiler_params=_cparams(("arbitrary",)),
        name="in_proj",
    )(x2d, w_in)


def _dil_attn_kernel(q_ref, kc_ref, kp_ref, vc_ref, vp_ref, o_ref, lse_ref, bias_ref, *, dil):
    blk = ATTN_BLOCK
    first_step = (pl.program_id(0) == 0) & (pl.program_id(1) == 0) & (pl.program_id(2) == 0)

    @pl.when(first_step)
    def _():
        qi = lax.broadcasted_iota(I32, (blk, 2 * blk), 0)
        kj = lax.broadcasted_iota(I32, (blk, 2 * blk), 1)
        dist = qi + blk - kj
        inside = (dist >= 0) & (dist <= blk)
        penalty = (dist * dil).astype(F32)
        for h in range(ATTN_HEADS):
            slope = 2.0 ** (-8.0 * (h + 1) / ATTN_HEADS)
            bias_ref[0, h] = jnp.where(inside, -slope * penalty, NEG_INF)
            bias_ref[1, h] = jnp.where(inside & (kj >= blk), -slope * penalty, NEG_INF)

    i = pl.program_id(1)
    r = pl.program_id(2)
    lane = lax.broadcasted_iota(I32, (blk, LANES), 1)
    low_half = lane < HEAD_DIM
    n_pairs = ATTN_WIDTH // LANES
    half = n_pairs // 2
    for j in range(q_ref.shape[0] // blk):
        table = jnp.where(i == 0, 1, 0) if j == 0 else 0
        lse_tile = jnp.zeros((blk, LANES), F32)
        pair_out = []
        for hp in range(n_pairs):
            sl = slice(hp * LANES, (hp + 1) * LANES)
            q2 = q_ref[j * blk:(j + 1) * blk, sl]
            if j == 0:
                kcat = jnp.concatenate([kp_ref[:, sl], kc_ref[0:blk, sl]], axis=0)
                vcat = jnp.concatenate([vp_ref[:, sl], vc_ref[0:blk, sl]], axis=0)
            else:
                kcat = kc_ref[(j - 1) * blk:(j + 1) * blk, sl]
                vcat = vc_ref[(j - 1) * blk:(j + 1) * blk, sl]
            outs = []
            for e in range(LANES // HEAD_DIM):
                h = hp * (LANES // HEAD_DIM) + e
                keep = low_half if e == 0 else jnp.logical_not(low_half)
                qe = jnp.where(keep, q2, jnp.zeros_like(q2))
                s = _dot_nt(qe, kcat) + bias_ref[table, h]
                m = jnp.max(s, axis=1, keepdims=True)
                p = jnp.exp(s - m)
                l = jnp.sum(p, axis=1, keepdims=True)
                outs.append(_dot(p.astype(BF16), vcat) / l)
                lse_tile = jnp.where(lane == h, m + jnp.log(l), lse_tile)
            pair_out.append(jnp.where(low_half, outs[0], outs[1]))
        rows = pl.ds(j * blk * dil + r, blk, stride=dil) if dil > 1 else slice(j * blk, (j + 1) * blk)
        for w in range(half):
            o_ref[w, rows, :] = _pack_pair(pair_out[w], pair_out[w + half])
        lse_ref[rows, :] = lse_tile


def _dil_attn(q, k, v, batch, seq, dil):
    blk = ATTN_BLOCK
    sub = seq // dil
    step_blocks = min(ATTN_STEP_BLOCKS, sub // blk)
    step_rows = step_blocks * blk
    assert sub % step_rows == 0
    view = lambda a: a.reshape(batch, sub, dil * ATTN_WIDTH)
    cur = pl.BlockSpec((None, step_rows, ATTN_WIDTH), lambda b, i, r: (b, i, r))
    prev = pl.BlockSpec((None, blk, ATTN_WIDTH), lambda b, i, r: (b, jnp.maximum(i * step_blocks - 1, 0), r))
    half = ATTN_WIDTH // LANES // 2
    o, lse = pl.pallas_call(
        functools.partial(_dil_attn_kernel, dil=dil),
        grid=(batch, sub // step_rows, dil),
        in_specs=[cur, cur, prev, cur, prev],
        out_specs=[pl.BlockSpec((None, half, step_rows * dil, LANES), lambda b, i, r: (b, 0, i, 0)),
                   pl.BlockSpec((None, step_rows * dil, LANES), lambda b, i, r: (b, i, 0))],
        out_shape=[jax.ShapeDtypeStruct((batch, half, seq, LANES), U32),
                   jax.ShapeDtypeStruct((batch, seq, LANES), F32)],
        scratch_shapes=[pltpu.VMEM((2, ATTN_HEADS, blk, 2 * blk), F32)],
        compiler_params=_cparams(("arbitrary",) * 3),
        name=f"dil_attn_d{dil}",
    )(view(q), view(k), view(k), view(v), view(v))
    return o, lse.reshape(batch * seq, LANES)


def _mem_kv_kernel(mem_ref, wk_ref, wv_ref, k_ref, v_ref):
    m = mem_ref[...].astype(BF16)
    k_ref[...] = _dot(m, wk_ref[...].astype(BF16)).astype(BF16)
    v_ref[...] = _dot(m, wv_ref[...].astype(BF16)).astype(BF16)


def _mem_kv(mem2d, w_ck_bf, w_cv_bf):
    rows = mem2d.shape[0]
    width = w_ck_bf.shape[1]
    return pl.pallas_call(
        _mem_kv_kernel,
        out_shape=[jax.ShapeDtypeStruct((rows, width), BF16)] * 2,
        compiler_params=_cparams(None),
        name="mem_kv",
    )(mem2d, w_ck_bf, w_cv_bf)


def _row_kernel(x_ref, o1_ref, o4_ref, o16_ref, l1_ref, l4_ref, l16_ref, p_ref, halo_ref,
                poolw_f32, pscale_ref, wout_f32, g1_ref, b1_ref,
                wcq_f32, km_ref, vm_ref, wco_f32, g2_ref, b2_ref,
                wr_f32, shg_f32, shu_f32, shd_f32,
                x2_ref, base_ref, logit_ref,
                poolw_ref, wout_ref, wcq_ref, wco_ref, wrh_ref, wrl_ref, shg_ref, shu_ref, shd_ref, *, alpha, seq):
    tm = x_ref.shape[0]
    i = pl.program_id(0)

    @pl.when(i == 0)
    def _():
        for src, dst in ((poolw_f32, poolw_ref), (wout_f32, wout_ref), (wcq_f32, wcq_ref), (wco_f32, wco_ref),
                         (shg_f32, shg_ref), (shu_f32, shu_ref), (shd_f32, shd_ref)):
            dst[...] = src[...].astype(BF16)
        hi, lo = _split_bf16(wr_f32[...])
        wrh_ref[...] = hi
        wrl_ref[...] = lo

    tile_pos = (i * tm) % seq

    l1, l4, l16 = l1_ref[...], l4_ref[...], l16_ref[...]
    mx = jnp.maximum(jnp.maximum(l1, l4), l16)
    e1, e4, e16 = jnp.exp(l1 - mx), jnp.exp(l4 - mx), jnp.exp(l16 - mx)
    inv = 1.0 / (e1 + e4 + e16)
    hrow = lax.broadcasted_iota(I32, (LANES, ATTN_WIDTH), 0)
    hcol = lax.broadcasted_iota(I32, (LANES, ATTN_WIDTH), 1) // HEAD_DIM
    spread = jnp.where(hrow == hcol, 1.0, 0.0).astype(BF16)

    def widen(w):
        hi, lo = _split_bf16(w)
        return _dot(hi, spread) + _dot(lo, spread)

    def branch(o_ref):
        lo, hi = zip(*[_unpack_pair(o_ref[w]) for w in range(o_ref.shape[0])])
        return jnp.concatenate(lo + hi, axis=1)

    last = branch(o16_ref)
    attn = last + widen(e1 * inv) * (branch(o1_ref) - last) + widen(e4 * inv) * (branch(o4_ref) - last)

    halo_rows = halo_ref.shape[0]
    halo = jnp.where(tile_pos > 0, halo_ref[...], 0.0)
    ext = jnp.concatenate([halo, p_ref[...]], axis=0)
    pos = (tile_pos + lax.broadcasted_iota(I32, (tm, 1), 0)).astype(F32)
    gd = ext.shape[1] // len(POOL_WINDOWS)
    mixed = []
    for g, w in enumerate(POOL_WINDOWS):
        eg = ext[:, g * gd:(g + 1) * gd]
        acc, span = eg, 1
        while span < w:
            acc = acc + pltpu.roll(acc, span, 0)
            span *= 2
        count = jnp.minimum(pos + 1.0, float(w))
        pooled = acc[halo_rows:, :] / count - eg[halo_rows:, :]
        mixed.append(_dot(pooled.astype(BF16), poolw_ref[g]) * pscale_ref[:, g * gd:(g + 1) * gd])
    cat = jnp.concatenate([attn.astype(BF16)] + [m.astype(BF16) for m in mixed], axis=1)
    x1 = _layer_norm(alpha * x_ref[...] + _dot(cat, wout_ref[...]), g1_ref[...], b1_ref[...])

    qc = _dot(x1.astype(BF16), wcq_ref[...])
    mhd = qc.shape[1] // MEM_HEADS
    heads = []
    for h in range(MEM_HEADS):
        sl = slice(h * mhd, (h + 1) * mhd)
        s = _dot_nt(qc[:, sl].astype(BF16), km_ref[:, sl]) * (mhd ** -0.5)
        m = jnp.max(s, axis=1, keepdims=True)
        p = jnp.exp(s - m)
        l = jnp.sum(p, axis=1, keepdims=True)
        heads.append((_dot(p.astype(BF16), vm_ref[:, sl]) / l).astype(BF16))
    oc = jnp.concatenate(heads, axis=1)
    x2 = _layer_norm(alpha * x1 + _dot(oc, wco_ref[...]), g2_ref[...], b2_ref[...])
    dh = x2.shape[1] // 2
    x2_ref[...] = _pack_pair(x2[:, :dh], x2[:, dh:])

    xh, xl = _split_bf16(x2)
    logit_ref[...] = _dot_nt(wrh_ref[...], xh) + _dot_nt(wrh_ref[...], xl) + _dot_nt(wrl_ref[...], xh)

    gate = _dot(xh, shg_ref[...])
    up = _dot(xh, shu_ref[...])
    hid = gate / (1.0 + jnp.exp(-gate)) * up
    base_ref[...] = alpha * x2 + _dot(hid.astype(BF16), shd_ref[...])


def _row_block(x2d, o1, o4, o16, l1, l4, l16, p, pool_w, pool_scale, w_out, g1, b1,
               w_cq, kmem, vmem, w_co, g2, b2, wr_t, sh_g, sh_u, sh_d,
               *, alpha, batch, seq, mem_len):
    n, d = x2d.shape
    tm = ROW_TILE
    halo_rows = max(POOL_WINDOWS)
    steps_per_seq = seq // tm
    row = lambda w: pl.BlockSpec((tm, w), lambda i: (i, 0))
    full = lambda a: pl.BlockSpec(a.shape, lambda i: (0,) * a.ndim)
    halo = pl.BlockSpec((halo_rows, p.shape[1]), lambda i: (jnp.maximum(i * (tm // halo_rows) - 1, 0), 0))
    memspec = pl.BlockSpec((mem_len, kmem.shape[1]), lambda i: (i // steps_per_seq, 0))
    n_exp = wr_t.shape[0]
    branch = pl.BlockSpec((None, o1.shape[1], tm, LANES), lambda i: (i // steps_per_seq, 0, i % steps_per_seq, 0))
    bf_copy = lambda a: pltpu.VMEM(a.shape, BF16)
    return pl.pallas_call(
        functools.partial(_row_kernel, alpha=alpha, seq=seq),
        grid=(n // tm,),
        in_specs=[row(d), branch, branch, branch, row(LANES), row(LANES), row(LANES),
                  row(p.shape[1]), halo,
                  full(pool_w), full(pool_scale), full(w_out), full(g1), full(b1),
                  full(w_cq), memspec, memspec, full(w_co), full(g2), full(b2),
                  full(wr_t), full(sh_g), full(sh_u), full(sh_d)],
        out_specs=[row(d // 2), row(d), pl.BlockSpec((n_exp, tm), lambda i: (0, i))],
        out_shape=[jax.ShapeDtypeStruct((n, d // 2), U32), jax.ShapeDtypeStruct((n, d), F32),
                   jax.ShapeDtypeStruct((n_exp, n), F32)],
        scratch_shapes=[bf_copy(pool_w), bf_copy(w_out), bf_copy(w_cq), bf_copy(w_co), bf_copy(wr_t), bf_copy(wr_t),
                        bf_copy(sh_g), bf_copy(sh_u), bf_copy(sh_d)],
        compiler_params=_cparams(("arbitrary",)),
        name="row_block",
    )(x2d, o1, o4, o16, l1, l4, l16, p, p, pool_w, pool_scale, w_out, g1, b1,
      w_cq, kmem, vmem, w_co, g2, b2, wr_t, sh_g, sh_u, sh_d)


def _beat_counts(vals, n_rows):
    tn = vals.shape[1]
    n_tiles = n_rows // SUBLANES
    tiles = [vals[t * SUBLANES:(t + 1) * SUBLANES, :] for t in range(n_tiles)]
    sub = lax.broadcasted_iota(I32, (SUBLANES, tn), 0)
    counts = [jnp.zeros((SUBLANES, tn), F32) for _ in range(n_tiles)]
    for e in range(n_rows):
        te, je = divmod(e, SUBLANES)
        row = jnp.broadcast_to(vals[e:e + 1, :], (SUBLANES, tn))
        for t in range(n_tiles):
            strict = jnp.where(row > tiles[t], 1.0, 0.0)
            loose = jnp.where(row >= tiles[t], 1.0, 0.0)
            if t < te:
                beat = strict
            elif t > te:
                beat = loose
            else:
                beat = jnp.where(sub > je, loose, strict)
            counts[t] = counts[t] + beat
    return jnp.concatenate(counts, axis=0)


def _top_k_mask(vals, k):
    ridx = lax.broadcasted_iota(I32, vals.shape, 0).astype(F32)
    mask = jnp.zeros(vals.shape, F32)
    work = vals
    for _ in range(k):
        top = jnp.max(work, axis=0, keepdims=True)
        first = jnp.min(jnp.where(work == top, ridx, float(vals.shape[0])), axis=0, keepdims=True)
        hit = ridx == first
        mask = jnp.where(hit, 1.0, mask)
        work = jnp.where(hit, -jnp.inf, work)
    return mask


def _route_kernel(logit_ref, bias_ref, rank_ref, rankt_ref, gatet_ref, tab_ref, count_ref, carry_ref, rcarry_ref,
                  *, per_group):
    n_exp, tn = logit_ref.shape
    step = pl.program_id(0)

    @pl.when(step == 0)
    def _():
        carry_ref[...] = jnp.zeros_like(carry_ref)
        rcarry_ref[...] = jnp.zeros_like(rcarry_ref)

    scores = 1.0 / (1.0 + jnp.exp(-logit_ref[...]))
    biased = scores + bias_ref[:, 0:1]
    n_groups = n_exp // per_group

    gscore = []
    for g in range(n_groups):
        tile = biased[g * per_group:(g + 1) * per_group, :]
        inner = _beat_counts(tile, per_group)
        gscore.append(jnp.sum(jnp.where(inner < 2.0, tile, 0.0), axis=0, keepdims=True))
    gscore = jnp.concatenate(gscore, axis=0)
    grank = _beat_counts(gscore, n_groups)
    grank = jnp.concatenate(
        [jnp.broadcast_to(grank[g:g + 1, :], (per_group, tn)) for g in range(n_groups)], axis=0)
    masked = jnp.where(grank < float(TOPK_GROUPS), biased, NEG_INF)
    self32 = _top_k_mask(masked, TOP_K)
    selbf = self32.astype(BF16)

    tr = lax.broadcasted_iota(I32, (tn, tn), 0)
    tc = lax.broadcasted_iota(I32, (tn, tn), 1)
    before = jnp.where(tr < tc, 1.0, 0.0).astype(BF16)
    lrank = _dot(selbf, before)
    align = float(RUN_ALIGN)
    lcount_col = jnp.broadcast_to(jnp.sum(self32, axis=1, keepdims=True), (n_exp, LANES))
    sel_wide = jnp.concatenate([selbf, jnp.zeros((LANES - n_exp, tn), BF16)], axis=0)
    lcount_row = _dot_nt(jnp.ones((SUBLANES, tn), BF16), sel_wide)
    lcount_col = jnp.ceil(lcount_col / align) * align
    lcount_row = jnp.ceil(lcount_row / align) * align
    wr = lax.broadcasted_iota(I32, (LANES, LANES), 0)
    wc = lax.broadcasted_iota(I32, (LANES, LANES), 1)
    lstart_row = _dot(lcount_row.astype(BF16), jnp.where(wr < wc, 1.0, 0.0).astype(BF16))

    sub = lax.broadcasted_iota(I32, (SUBLANES, LANES), 0)
    tab_ref[...] = jnp.where(sub == 0, lstart_row, jnp.where(sub == 1, lcount_row, jnp.where(
        sub == 2, rcarry_ref[...], 0.0))).astype(I32)
    rcarry_ref[...] = rcarry_ref[...] + lcount_row
    carry_ref[...] = carry_ref[...] + lcount_col
    count_ref[...] = carry_ref[...]

    denom = jnp.sum(self32 * scores, axis=0, keepdims=True)
    gate = self32 * scores / denom * ROUTED_SCALE
    pad = jnp.zeros((LANES - n_exp, tn), F32)
    widen = lambda v: jnp.concatenate([v, pad], axis=0)
    rank_tab = widen(jnp.where(self32 > 0.0, lrank, -1.0))
    rank_ref[...] = rank_tab.astype(BF16)
    rankt_ref[...] = rank_tab.T.astype(BF16)
    gatet_ref[...] = widen(gate).T.astype(BF16)


def _route(logits_t, bias_col):
    n_exp, n = logits_t.shape
    tn = SORT_TILE
    emaj = jax.ShapeDtypeStruct((LANES, n), BF16)
    tmaj = jax.ShapeDtypeStruct((n, LANES), BF16)
    emaj_spec = pl.BlockSpec((LANES, tn), lambda i: (0, i))
    tmaj_spec = pl.BlockSpec((tn, LANES), lambda i: (i, 0))
    return pl.pallas_call(
        functools.partial(_route_kernel, per_group=n_exp // N_EXPERT_GROUPS),
        grid=(n // tn,),
        in_specs=[pl.BlockSpec((n_exp, tn), lambda i: (0, i)), pl.BlockSpec(bias_col.shape, lambda i: (0, 0))],
        out_specs=[emaj_spec, tmaj_spec, tmaj_spec,
                   pl.BlockSpec((None, SUBLANES, LANES), lambda i: (i, 0, 0)),
                   pl.BlockSpec((n_exp, LANES), lambda i: (0, 0))],
        out_shape=[emaj, tmaj, tmaj, jax.ShapeDtypeStruct((n // tn, SUBLANES, LANES), I32),
                   jax.ShapeDtypeStruct((n_exp, LANES), F32)],
        scratch_shapes=[pltpu.VMEM((n_exp, LANES), F32), pltpu.VMEM((SUBLANES, LANES), F32)],
        compiler_params=_cparams(("arbitrary",)),
        name="route",
    )(logits_t, bias_col)


def _place_kernel(count_ref, tab_ref, runs_ref, bexp_ref, seg_ref, *, n_blocks):
    n_exp = count_ref.shape[0]
    blk = float(EXPERT_BLOCK)
    counts = count_ref[...]
    padded = jnp.ceil(counts / blk) * blk
    ridx = lax.broadcasted_iota(I32, (n_exp, LANES), 0)
    ends = padded
    shift = 1
    while shift < n_exp:
        ends = ends + jnp.where(ridx >= shift, pltpu.roll(ends, shift, 0), 0.0)
        shift *= 2
    starts = ends - padded

    tab = tab_ref[...]
    last = tab[tab.shape[0] - 1]
    total_row = (last[2:3, :] + last[1:2, :]).astype(F32)
    padded_row = jnp.ceil(total_row / blk) * blk
    lidx = lax.broadcasted_iota(I32, (1, LANES), 1)
    ends_row = padded_row
    shift = 1
    while shift < LANES:
        ends_row = ends_row + jnp.where(lidx >= shift, pltpu.roll(ends_row, shift, 1), 0.0)
        shift *= 2
    starts_row = (ends_row - padded_row).astype(I32)
    sub = lax.broadcasted_iota(I32, tab.shape, 1)
    runs_ref[...] = tab + jnp.where(sub == 2, starts_row[None], 0)

    nbp = bexp_ref.shape[1]
    blk_start = (lax.broadcasted_iota(I32, (n_exp, nbp), 1) * EXPERT_BLOCK).astype(F32)
    ends_wide = jnp.broadcast_to(ends[:, 0:1], (n_exp, nbp))
    count_le = lambda bound: jnp.minimum(
        jnp.sum(jnp.where(ends_wide <= bound, 1.0, 0.0), axis=0, keepdims=True), float(n_exp - 1))
    bexp = count_le(blk_start)
    used = ends[n_exp - 1:n_exp, 0:1] / blk
    seg_end = jnp.min(jnp.where(ends_wide > blk_start, ends_wide, 2.0 * float(nbp * EXPERT_BLOCK)),
                      axis=0, keepdims=True)
    sub8 = lax.broadcasted_iota(I32, (SUBLANES, nbp), 0)
    lane = lax.broadcasted_iota(I32, (SUBLANES, nbp), 1)
    plan = jnp.where(sub8 == 0, jnp.where(lane == n_blocks, used, bexp),
                     jnp.where(sub8 == 1, count_le(seg_end), jnp.where(sub8 == 2, seg_end / blk, 0.0)))
    bexp_ref[...] = plan.astype(I32)
    lane2 = lax.broadcasted_iota(I32, (n_exp, LANES), 1)
    seg_ref[...] = jnp.where(lane2 == 0, starts + counts, ends).astype(I32)


def _place(counts, tab, n_blocks):
    n_exp = counts.shape[0]
    nbp = -(-(n_blocks + 1) // LANES) * LANES
    return pl.pallas_call(
        functools.partial(_place_kernel, n_blocks=n_blocks),
        out_shape=[jax.ShapeDtypeStruct(tab.shape, I32), jax.ShapeDtypeStruct((SUBLANES, nbp), I32),
                   jax.ShapeDtypeStruct((n_exp, LANES), I32)],
        compiler_params=_cparams(None),
        name="route_place",
    )(counts, tab)


def _copy_rows(src_ref, dst_ref, sem, src0, dst0, n, wait=False):
    def piece(off, p):
        src = src_ref.at[pl.ds(pl.multiple_of(src0 + off, RUN_ALIGN), p), :]
        dst = dst_ref.at[pl.ds(pl.multiple_of(dst0 + off, RUN_ALIGN), p), :]
        cp = pltpu.make_async_copy(src, dst, sem)
        if wait:
            cp.wait()
        else:
            cp.start()

    def chunk(c, carry):
        piece(c * RUN_CHUNK, RUN_CHUNK)
        return carry
    lax.fori_loop(0, lax.shift_right_logical(n, RUN_CHUNK.bit_length() - 1), chunk, 0)
    p = RUN_CHUNK // 2
    while p >= RUN_ALIGN:
        @pl.when((n & p) != 0)
        def _(p=p):
            piece(n & (-2 * p), p)
        p //= 2


def _start_run(src_ref, dst_ref, sem, src0, dst0, n, limit, enable=None):
    p = RUN_ALIGN
    while p * 2 <= limit:
        p *= 2
    while p >= RUN_ALIGN:
        hit = (n & p) != 0
        @pl.when(hit if enable is None else hit & enable)
        def _(p=p):
            off = n & (-2 * p)
            src = src_ref.at[pl.ds(pl.multiple_of(src0 + off, RUN_ALIGN), p), :]
            dst = dst_ref.at[pl.ds(pl.multiple_of(dst0 + off, RUN_ALIGN), p), :]
            pltpu.make_async_copy(src, dst, sem).start()
        p //= 2


def _run_membership(tab, rows, first_row=0):
    j = lax.broadcasted_iota(I32, (rows, LANES), 0) + first_row
    start = tab[0:1, :]
    end = start + tab[1:2, :]
    return jnp.where(j >= start, jnp.where(j < end, 1.0, 0.0), 0.0)


def _dispatch_kernel(prev_runs_ref, runs_ref, seg_ref, tab_ref, rank_ref, x_ref, xs_ref, buf_ref, zero_ref, sems,
                     zsem, *, n_exp):
    step, n_steps = pl.program_id(0), pl.num_programs(0)
    tm = x_ref.shape[0]
    n_slots, rows = buf_ref.shape[:2]
    pad_rows = rows - TOP_K * tm
    cur, prev, old = step % n_slots, (step + n_slots - 1) % n_slots, (step + n_slots - 2) % n_slots

    def start_runs(tab_ref, s, experts, enable=None):
        for e in experts:
            _start_run(buf_ref.at[s], xs_ref, sems.at[s], tab_ref[0, e], tab_ref[2, e], tab_ref[1, e], tm, enable)

    def start_filler(tab_ref, s, enable=None):
        used = tab_ref[0, n_exp - 1] + tab_ref[1, n_exp - 1]
        spare = xs_ref.shape[0] - (n_slots - s) * pad_rows
        _start_run(buf_ref.at[s], xs_ref, sems.at[s], used, spare, rows - used, pad_rows, enable)

    def drain(s):
        pltpu.make_async_copy(buf_ref.at[s], xs_ref.at[pl.ds(0, rows), :], sems.at[s]).wait()

    @pl.when(step == 0)
    def _():
        zero_ref[...] = jnp.zeros_like(zero_ref)
        for wait in (False, True):
            def per_expert(e, c, wait=wait):
                first = seg_ref[e, 0]
                _copy_rows(zero_ref, xs_ref, zsem, 0, first, seg_ref[e, 1] - first, wait)
                return c
            lax.fori_loop(0, n_exp, per_expert, 0)

    lo, hi = _unpack_pair(x_ref[...])
    x = jnp.concatenate([lo.astype(BF16), hi.astype(BF16)], axis=1)
    tab = tab_ref[...]
    bits = lambda v: lax.bitcast_convert_type(v, U32)
    chunk, per = rows // DISPATCH_CHUNKS, n_exp // DISPATCH_CHUNKS
    for c in range(DISPATCH_CHUNKS):
        member = _run_membership(tab, chunk, c * chunk)
        run_start = jnp.sum(member * tab[0:1, :].astype(F32), axis=1, keepdims=True)
        in_run = (lax.broadcasted_iota(I32, (chunk, 1), 0) + c * chunk).astype(F32) - run_start
        onehot = jnp.where(_dot(member.astype(BF16), rank_ref[...]) == in_run, 1.0, 0.0)
        srt = _dot(onehot.astype(BF16), x)
        dh = srt.shape[1] // 2
        buf_ref[cur, c * chunk:(c + 1) * chunk, :] = (bits(srt[:, :dh]) >> 16) | bits(srt[:, dh:])
        start_runs(prev_runs_ref, prev, range(c * per, (c + 1) * per), step > 0)
    start_filler(prev_runs_ref, prev, step > 0)

    @pl.when(step >= 2)
    def _():
        drain(old)

    @pl.when(step == n_steps - 1)
    def _():
        start_runs(runs_ref, cur, range(n_exp))
        start_filler(runs_ref, cur)

        @pl.when(step >= 1)
        def _():
            drain(prev)
        drain(cur)


def _sorted_rows(tm, n_exp):
    return TOP_K * tm + n_exp * RUN_ALIGN


def _dispatch(x2, rank_tab, runs, seg, n_rows, n_exp):
    n, d = x2.shape
    tm = SORT_TILE
    rows = _sorted_rows(tm, n_exp)
    assert rows % DISPATCH_CHUNKS == 0 and n_exp % DISPATCH_CHUNKS == 0
    tab = (None,) + runs.shape[1:]
    return pl.pallas_call(
        functools.partial(_dispatch_kernel, n_exp=n_exp),
        grid=(n // tm,),
        in_specs=[pl.BlockSpec(tab, lambda i: (jnp.maximum(i - 1, 0), 0, 0), memory_space=pltpu.SMEM),
                  pl.BlockSpec(tab, lambda i: (i, 0, 0), memory_space=pltpu.SMEM),
                  pl.BlockSpec(seg.shape, lambda i: (0, 0), memory_space=pltpu.SMEM),
                  pl.BlockSpec(tab, lambda i: (i, 0, 0)),
                  pl.BlockSpec((LANES, tm), lambda i: (0, i)),
                  pl.BlockSpec((tm, d), lambda i: (i, 0))],
        out_specs=pl.BlockSpec(memory_space=pl.ANY),
        out_shape=jax.ShapeDtypeStruct((n_rows + DISPATCH_SLOTS * (rows - TOP_K * tm), d), x2.dtype),
        scratch_shapes=[pltpu.VMEM((DISPATCH_SLOTS, rows, d), x2.dtype), pltpu.VMEM((EXPERT_BLOCK, d), x2.dtype),
                        pltpu.SemaphoreType.DMA((DISPATCH_SLOTS,)), pltpu.SemaphoreType.DMA],
        compiler_params=_cparams(("arbitrary",)),
        name="dispatch",
    )(runs, runs, seg, runs, rank_tab, x2)


def _expert_kernel(plan_ref, xs_ref, wg_ref, wu_ref, wd_ref, ys_ref, xbuf_ref, xsems, wg_f32, wu_f32, wd_f32,
                   wsems, wslot_ref, wg_bf, wu_bf, wd_bf, *, n_blocks):
    i = pl.program_id(0)
    used = plan_ref[0, n_blocks]
    depth, blk = xbuf_ref.shape[:2]

    def fetch(j):
        rows = pl.ds(pl.multiple_of(j * blk, blk), blk)
        return pltpu.make_async_copy(xs_ref.at[rows, :], xbuf_ref.at[j % depth], xsems.at[j % depth])

    def weights(e, s):
        return [pltpu.make_async_copy(src.at[e], dst.at[s], wsems.at[s])
                for src, dst in ((wg_ref, wg_f32), (wu_ref, wu_f32), (wd_ref, wd_f32))]

    @pl.when(i == 0)
    def _():
        for j in range(depth - 1):
            @pl.when(j < used)
            def _(j=j):
                fetch(j).start()

    @pl.when(i + depth - 1 < used)
    def _():
        fetch(i + depth - 1).start()

    @pl.when((i == 0) & (used > 0))
    def _():
        wslot_ref[0] = 0
        for cp in weights(plan_ref[0, 0], 0):
            cp.start()

    @pl.when(i < used)
    def _():
        @pl.when((i == 0) | (plan_ref[0, i] != plan_ref[0, jnp.maximum(i - 1, 0)]))
        def _():
            s = wslot_ref[0]
            for cp in weights(plan_ref[0, i], s):
                cp.wait()

            @pl.when(plan_ref[2, i] < used)
            def _():
                for cp in weights(plan_ref[1, i], 1 - s):
                    cp.start()
            wslot_ref[0] = 1 - s
            wg_bf[...] = wg_f32[s].astype(BF16)
            wu_bf[...] = wu_f32[s].astype(BF16)
            wd_bf[...] = wd_f32[s].astype(BF16)

        fetch(i).wait()
        lo, hi = _unpack_pair(xbuf_ref[i % depth])
        x = jnp.concatenate([lo.astype(BF16), hi.astype(BF16)], axis=1)
        gate = _dot(x, wg_bf[...])
        up = _dot(x, wu_bf[...])
        hid = gate / (1.0 + jnp.exp(-gate)) * up
        y = _dot(hid.astype(BF16), wd_bf[...])
        dh = y.shape[1] // 2
        ys_ref[...] = _pack_pair(y[:, :dh], y[:, dh:])


def _experts(bexp, xs, exp_gate, exp_up, exp_down, n_blocks):
    n_rows = xs.shape[0]
    d, hidden = exp_gate.shape[1:]
    blk = EXPERT_BLOCK

    def rows(i, b):
        return (jnp.minimum(i, jnp.maximum(b[0, n_blocks] - 1, 0)), 0)

    hbm = pl.BlockSpec(memory_space=pl.ANY)
    grid_spec = pltpu.PrefetchScalarGridSpec(
        num_scalar_prefetch=1,
        grid=(n_blocks,),
        in_specs=[hbm, hbm, hbm, hbm],
        out_specs=pl.BlockSpec((blk, d // 2), rows),
        scratch_shapes=[pltpu.VMEM((EXPERT_RING, blk, d // 2), xs.dtype), pltpu.SemaphoreType.DMA((EXPERT_RING,)),
                        pltpu.VMEM((2, d, hidden), F32), pltpu.VMEM((2, d, hidden), F32),
                        pltpu.VMEM((2, hidden, d), F32), pltpu.SemaphoreType.DMA((2,)), pltpu.SMEM((1,), I32),
                        pltpu.VMEM((d, hidden), BF16), pltpu.VMEM((d, hidden), BF16), pltpu.VMEM((hidden, d), BF16)],
    )
    return pl.pallas_call(
        functools.partial(_expert_kernel, n_blocks=n_blocks),
        grid_spec=grid_spec,
        out_shape=jax.ShapeDtypeStruct((n_rows, d // 2), U32),
        compiler_params=_cparams(("arbitrary",)),
        name="experts",
    )(bexp, xs, exp_gate, exp_up, exp_down)


def _combine_kernel(runs_ref, next_runs_ref, ahead_runs_ref, tab_ref, rankt_ref, gatet_ref, base_ref, g3_ref, b3_ref,
                    ys_ref, out_ref, buf_ref, sems, *, n_exp):
    step, n_steps = pl.program_id(0), pl.num_programs(0)
    tm = base_ref.shape[0]
    n_slots, rows = buf_ref.shape[:2]
    slot, ahead = step % n_slots, (step + 2) % n_slots

    def fetch(tab_ref, s, experts, enable=None):
        for e in experts:
            _start_run(ys_ref, buf_ref.at[s], sems.at[s], tab_ref[2, e], tab_ref[0, e], tab_ref[1, e], tm, enable)

    def fetch_filler(tab_ref, s, enable=None):
        used = tab_ref[0, n_exp - 1] + tab_ref[1, n_exp - 1]
        _start_run(ys_ref, buf_ref.at[s], sems.at[s], 0, used, rows - used, rows - TOP_K * tm, enable)

    @pl.when(step == 0)
    def _():
        fetch(runs_ref, 0, range(n_exp))
        fetch_filler(runs_ref, 0)
        fetch(next_runs_ref, 1, range(n_exp), n_steps > 1)
        fetch_filler(next_runs_ref, 1, n_steps > 1)

    pltpu.make_async_copy(ys_ref.at[pl.ds(0, rows), :], buf_ref.at[slot], sems.at[slot]).wait()

    tab = tab_ref[...]
    start = tab[0:1, :].astype(F32)
    start_hi = jnp.floor(start / float(POS_BASE))
    sub = lax.broadcasted_iota(I32, (SUBLANES, LANES), 0)
    digits = jnp.where(sub == 0, start_hi, jnp.where(sub == 1, start - start_hi * float(POS_BASE), 0.0)).astype(BF16)
    have_ahead = step + 2 < n_steps
    chunk, per = rows // DISPATCH_CHUNKS, n_exp // DISPATCH_CHUNKS
    routed = jnp.zeros(base_ref.shape, F32)
    for c in range(DISPATCH_CHUNKS):
        member = _run_membership(tab, chunk, c * chunk).astype(BF16)
        picked = _dot_nt(digits, member)
        in_run = ((lax.broadcasted_iota(I32, (1, chunk), 1) + c * chunk).astype(F32)
                  - (picked[0:1, :] * float(POS_BASE) + picked[1:2, :]))
        w = jnp.where(_dot_nt(rankt_ref[...], member) == in_run, _dot_nt(gatet_ref[...], member), 0.0)
        lo, hi = _unpack_pair(buf_ref[slot, c * chunk:(c + 1) * chunk, :])
        y = jnp.concatenate([lo.astype(BF16), hi.astype(BF16)], axis=1)
        routed = routed + _dot(w.astype(BF16), y)
        fetch(ahead_runs_ref, ahead, range(c * per, (c + 1) * per), have_ahead)
    fetch_filler(ahead_runs_ref, ahead, have_ahead)
    out_ref[...] = _layer_norm(base_ref[...] + routed, g3_ref[...], b3_ref[...])


def _combine(runs, rank_t, gate_t, base, g3, b3, ys, n_exp):
    n, d = base.shape
    tm = SORT_TILE
    last = n // tm - 1
    tab = lambda f: pl.BlockSpec((None,) + runs.shape[1:], f, memory_space=pltpu.SMEM)
    tmaj = pl.BlockSpec((tm, LANES), lambda i: (i, 0))
    return pl.pallas_call(
        functools.partial(_combine_kernel, n_exp=n_exp),
        grid=(n // tm,),
        in_specs=[tab(lambda i: (i, 0, 0)), tab(lambda i: (jnp.minimum(i + 1, last), 0, 0)),
                  tab(lambda i: (jnp.minimum(i + 2, last), 0, 0)),
                  pl.BlockSpec((None,) + runs.shape[1:], lambda i: (i, 0, 0)),
                  tmaj, tmaj,
                  pl.BlockSpec((tm, d), lambda i: (i, 0)),
                  pl.BlockSpec(g3.shape, lambda i: (0, 0)),
                  pl.BlockSpec(b3.shape, lambda i: (0, 0)),
                  pl.BlockSpec(memory_space=pl.ANY)],
        out_specs=pl.BlockSpec((tm, d), lambda i: (i, 0)),
        out_shape=jax.ShapeDtypeStruct((n, d), F32),
        scratch_shapes=[pltpu.VMEM((DISPATCH_SLOTS, _sorted_rows(tm, n_exp), ys.shape[1]), ys.dtype),
                        pltpu.SemaphoreType.DMA((DISPATCH_SLOTS,))],
        compiler_params=_cparams(("arbitrary",)),
        name="combine",
    )(runs, runs, runs, runs, rank_t, gate_t, base, g3, b3, ys)


def _layer(x2d, mem2d, w_in, pool_w, pool_scale, w_out, ln1_g, ln1_b, w_cq, w_ck, w_cv, w_co, ln2_g, ln2_b,
           w_router, router_bias, exp_gate, exp_up, exp_down, sh_gate, sh_up, sh_down, ln3_g, ln3_b,
           *, alpha, batch, seq, mem_len):
    n, d = x2d.shape
    n_exp = w_router.shape[1]
    row = lambda a: a.reshape(1, -1)

    dils = tuple(dil for _, dil in DILATED_PATTERNS)
    *qkv, p = _in_proj(x2d, w_in, dils)
    branches = []
    for di, (window, dil) in enumerate(DILATED_PATTERNS):
        assert window // dil == ATTN_BLOCK and seq % (dil * ATTN_BLOCK) == 0
        assert IN_TILE % (dil * 2 * SUBLANES) == 0
        branches.append(_dil_attn(*qkv[3 * di:3 * di + 3], batch, seq, dil))
    (o1, l1), (o4, l4), (o16, l16) = branches
    kmem, vmem = _mem_kv(mem2d, w_ck, w_cv)

    x2, base, logits_t = _row_block(
        x2d, o1, o4, o16, l1, l4, l16, p, pool_w, row(pool_scale), w_out, row(ln1_g), row(ln1_b),
        w_cq, kmem, vmem, w_co, row(ln2_g), row(ln2_b), w_router.T, sh_gate, sh_up, sh_down,
        alpha=alpha, batch=batch, seq=seq, mem_len=mem_len)

    bias_col = jnp.broadcast_to(router_bias.reshape(n_exp, 1), (n_exp, LANES))
    rank_tab, rank_t, gate_t, tab, counts = _route(logits_t, bias_col)
    max_rows = n * TOP_K + (n // SORT_TILE) * n_exp * (RUN_ALIGN - 1)
    n_blocks = -(-max_rows // EXPERT_BLOCK) + n_exp
    runs, bexp, seg = _place(counts, tab, n_blocks)
    xs = _dispatch(x2, rank_tab, runs, seg, n_blocks * EXPERT_BLOCK, n_exp)
    ys = _experts(bexp, xs, exp_gate, exp_up, exp_down, n_blocks)
    return _combine(runs, rank_t, gate_t, base, row(ln3_g), row(ln3_b), ys, n_exp)


def kernel(x, mem, w_in, pool_w, pool_scale, w_out, ln1_g, ln1_b, w_cq, w_ck, w_cv, w_co, ln2_g, ln2_b, w_router, router_bias, exp_gate, exp_up, exp_down, sh_gate, sh_up, sh_down, ln3_g, ln3_b):
    batch, seq, d = x.shape
    mem_len = mem.shape[1]
    depth = w_in.shape[0]
    alpha = (2.0 * depth) ** 0.25
    x2d = x.reshape(batch * seq, d)
    mem2d = mem.reshape(batch * mem_len, d)
    for l in range(depth):
        x2d = _layer(x2d, mem2d, w_in[l], pool_w[l], pool_scale[l], w_out[l], ln1_g[l], ln1_b[l], w_cq[l], w_ck[l],
                     w_cv[l], w_co[l], ln2_g[l], ln2_b[l], w_router[l], router_bias[l], exp_gate[l], exp_up[l],
                     exp_down[l], sh_gate[l], sh_up[l], sh_down[l], ln3_g[l], ln3_b[l],
                     alpha=alpha, batch=batch, seq=seq, mem_len=mem_len)
    return x2d.reshape(batch, seq, d)
```

```python
import functools

import jax
import jax.numpy as jnp
from jax import lax
from jax.experimental import pallas as pl
from jax.experimental.pallas import tpu as pltpu

F32 = jnp.float32
BF16 = jnp.bfloat16
I32 = jnp.int32
U32 = jnp.uint32

ATTN_HEADS = 8
HEAD_DIM = 64
ATTN_WIDTH = ATTN_HEADS * HEAD_DIM
DILATED_PATTERNS = ((128, 1), (512, 4), (2048, 16))
POOL_WINDOWS = (2, 4, 8, 16)
MEM_HEADS = 4
N_EXPERT_GROUPS = 8
TOPK_GROUPS = 4
TOP_K = 8
ROUTED_SCALE = 2.5
LN_EPS = 1e-5
NEG_INF = -1e30

LANES = 128
SUBLANES = 8
VMEM_LIMIT = 56 * 1024 * 1024

ATTN_BLOCK = 128
ATTN_STEP_BLOCKS = 8
ROW_TILE = 512
IN_TILE = 512
SORT_TILE = 256
EXPERT_BLOCK = 512
EXPERT_RING = 3
RUN_ALIGN = SUBLANES
RUN_CHUNK = 64
POS_BASE = 64
DISPATCH_SLOTS = 3
DISPATCH_CHUNKS = 4


def _cparams(sem, vmem=VMEM_LIMIT):
    return pltpu.CompilerParams(dimension_semantics=sem, vmem_limit_bytes=vmem)


def _layer_norm(y, g, b):
    mu = jnp.mean(y, axis=-1, keepdims=True)
    d = y - mu
    var = jnp.mean(d * d, axis=-1, keepdims=True)
    return d * lax.rsqrt(var + LN_EPS) * g + b


def _dot(a, b):
    return jnp.dot(a, b, preferred_element_type=F32)


def _dot_nt(a, b):
    return lax.dot_general(a, b, (((1,), (1,)), ((), ())), preferred_element_type=F32)


def _split_bf16(x):
    hi = x.astype(BF16)
    lo = (x - hi.astype(F32)).astype(BF16)
    return hi, lo


def _bf16_bits(x):
    return lax.bitcast_convert_type(x.astype(BF16).astype(F32), U32)


def _pack_pair(lo, hi):
    return (_bf16_bits(lo) >> 16) | (_bf16_bits(hi) & jnp.uint32(0xFFFF0000))


def _unpack_pair(w):
    lo = lax.bitcast_convert_type(w << 16, F32)
    hi = lax.bitcast_convert_type(w & jnp.uint32(0xFFFF0000), F32)
    return lo, hi


def _in_proj_kernel(x_ref, w_ref, *refs, dils):
    n_qkv = 3 * len(dils)
    out_refs, p_ref, slab_ref, wbf_ref = refs[:n_qkv], refs[n_qkv], refs[n_qkv + 1], refs[n_qkv + 2]

    @pl.when(pl.program_id(0) == 0)
    def _():
        wbf_ref[...] = w_ref[...].astype(BF16)

    w_ref = wbf_ref
    tm = x_ref.shape[0]
    x = x_ref[...].astype(BF16)
    aw = ATTN_WIDTH
    n_slabs = aw // LANES
    for a in range(3):
        val = _dot(x, w_ref[:, a * aw:(a + 1) * aw])
        if a == 0:
            val = val * (HEAD_DIM ** -0.5)
        for s in range(n_slabs):
            slab_ref[s] = val[:, s * LANES:(s + 1) * LANES]
        for di, dil in enumerate(dils):
            dst = out_refs[3 * di + a]
            if dil == 1:
                dst[...] = val.astype(BF16)
                continue
            for r in range(dil):
                for s in range(n_slabs):
                    rows = slab_ref[s, pl.ds(r, tm // dil, stride=dil), :]
                    dst[:, r * aw + s * LANES:r * aw + (s + 1) * LANES] = rows.astype(BF16)
    p_ref[...] = _dot(x, w_ref[:, 3 * aw:])


def _in_proj(x2d, w_in, dils):
    n, d = x2d.shape
    aw = ATTN_WIDTH
    pw = w_in.shape[1] - 3 * aw
    tm = IN_TILE
    row = lambda rows, w: pl.BlockSpec((rows, w), lambda i: (i, 0))
    qkv_specs, qkv_shapes = [], []
    for dil in dils:
        qkv_specs += [row(tm // dil, dil * aw)] * 3
        qkv_shapes += [jax.ShapeDtypeStruct((n // dil, dil * aw), BF16)] * 3
    return pl.pallas_call(
        functools.partial(_in_proj_kernel, dils=dils),
        grid=(n // tm,),
        in_specs=[row(tm, d), pl.BlockSpec(w_in.shape, lambda i: (0, 0))],
        out_specs=qkv_specs + [row(tm, pw)],
        out_shape=qkv_shapes + [jax.ShapeDtypeStruct((n, pw), F32)],
        scratch_shapes=[pltpu.VMEM((aw // LANES, tm, LANES), F32), pltpu.VMEM(w_in.shape, BF16)],
        compiler_params=_cparams(("arbitrary",)),
        name="in_proj",
    )(x2d, w_in)


def _dil_attn_kernel(q_ref, kc_ref, kp_ref, vc_ref, vp_ref, o_ref, lse_ref, bias_ref, *, dil):
    blk = ATTN_BLOCK
    first_step = (pl.program_id(0) == 0) & (pl.program_id(1) == 0) & (pl.program_id(2) == 0)

    @pl.when(first_step)
    def _():
        qi = lax.broadcasted_iota(I32, (blk, 2 * blk), 0)
        kj = lax.broadcasted_iota(I32, (blk, 2 * blk), 1)
        dist = qi + blk - kj
        inside = (dist >= 0) & (dist <= blk)
        penalty = (dist * dil).astype(F32)
        for h in range(ATTN_HEADS):
            slope = 2.0 ** (-8.0 * (h + 1) / ATTN_HEADS)
            bias_ref[0, h] = jnp.where(inside, -slope * penalty, NEG_INF)
            bias_ref[1, h] = jnp.where(inside & (kj >= blk), -slope * penalty, NEG_INF)

    i = pl.program_id(1)
    lane = lax.broadcasted_iota(I32, (blk, LANES), 1)
    low_half = lane < HEAD_DIM
    n_pairs = ATTN_WIDTH // LANES
    half = n_pairs // 2
    n_res = q_ref.shape[1] // ATTN_WIDTH
    for rr, j in [(rr, j) for rr in range(n_res) for j in range(q_ref.shape[0] // blk)]:
        r = pl.program_id(2) * n_res + rr
        table = jnp.where(i == 0, 1, 0) if j == 0 else 0
        lse_tile = jnp.zeros((blk, LANES), F32)
        pair_out = []
        for hp in range(n_pairs):
            sl = slice(rr * ATTN_WIDTH + hp * LANES, rr * ATTN_WIDTH + (hp + 1) * LANES)
            q2 = q_ref[j * blk:(j + 1) * blk, sl]
            if j == 0:
                kcat = jnp.concatenate([kp_ref[:, sl], kc_ref[0:blk, sl]], axis=0)
                vcat = jnp.concatenate([vp_ref[:, sl], vc_ref[0:blk, sl]], axis=0)
            else:
                kcat = kc_ref[(j - 1) * blk:(j + 1) * blk, sl]
                vcat = vc_ref[(j - 1) * blk:(j + 1) * blk, sl]
            outs = []
            for e in range(LANES // HEAD_DIM):
                h = hp * (LANES // HEAD_DIM) + e
                keep = low_half if e == 0 else jnp.logical_not(low_half)
                qe = jnp.where(keep, q2, jnp.zeros_like(q2))
                s = _dot_nt(qe, kcat) + bias_ref[table, h]
                m = jnp.max(s, axis=1, keepdims=True)
                p = jnp.exp(s - m)
                l = jnp.sum(p, axis=1, keepdims=True)
                outs.append(_dot(p.astype(BF16), vcat) / l)
                lse_tile = jnp.where(lane == h, m + jnp.log(l), lse_tile)
            pair_out.append(jnp.where(low_half, outs[0], outs[1]))
        rows = pl.ds(j * blk * dil + r, blk, stride=dil) if dil > 1 else slice(j * blk, (j + 1) * blk)
        for w in range(half):
            o_ref[w, rows, :] = _pack_pair(pair_out[w], pair_out[w + half])
        lse_ref[rows, :] = lse_tile


def _dil_attn(q, k, v, batch, seq, dil):
    blk = ATTN_BLOCK
    sub = seq // dil
    step_blocks = min(ATTN_STEP_BLOCKS, sub // blk)
    step_rows = step_blocks * blk
    n_res = min(dil, ATTN_STEP_BLOCKS // step_blocks)
    assert sub % step_rows == 0 and dil % n_res == 0
    view = lambda a: a.reshape(batch, sub, dil * ATTN_WIDTH)
    cur = pl.BlockSpec((None, step_rows, n_res * ATTN_WIDTH), lambda b, i, r: (b, i, r))
    prev = pl.BlockSpec((None, blk, n_res * ATTN_WIDTH), lambda b, i, r: (b, jnp.maximum(i * step_blocks - 1, 0), r))
    half = ATTN_WIDTH // LANES // 2
    o, lse = pl.pallas_call(
        functools.partial(_dil_attn_kernel, dil=dil),
        grid=(batch, sub // step_rows, dil // n_res),
        in_specs=[cur, cur, prev, cur, prev],
        out_specs=[pl.BlockSpec((None, half, step_rows * dil, LANES), lambda b, i, r: (b, 0, i, 0)),
                   pl.BlockSpec((None, step_rows * dil, LANES), lambda b, i, r: (b, i, 0))],
        out_shape=[jax.ShapeDtypeStruct((batch, half, seq, LANES), U32),
                   jax.ShapeDtypeStruct((batch, seq, LANES), F32)],
        scratch_shapes=[pltpu.VMEM((2, ATTN_HEADS, blk, 2 * blk), F32)],
        compiler_params=_cparams(("arbitrary",) * 3),
        name=f"dil_attn_d{dil}",
    )(view(q), view(k), view(k), view(v), view(v))
    return o, lse.reshape(batch * seq, LANES)


def _mem_kv_kernel(mem_ref, wk_ref, wv_ref, k_ref, v_ref):
    m = mem_ref[...].astype(BF16)
    k_ref[...] = _dot(m, wk_ref[...].astype(BF16)).astype(BF16)
    v_ref[...] = _dot(m, wv_ref[...].astype(BF16)).astype(BF16)


def _mem_kv(mem2d, w_ck_bf, w_cv_bf):
    rows = mem2d.shape[0]
    width = w_ck_bf.shape[1]
    return pl.pallas_call(
        _mem_kv_kernel,
        out_shape=[jax.ShapeDtypeStruct((rows, width), BF16)] * 2,
        compiler_params=_cparams(None),
        name="mem_kv",
    )(mem2d, w_ck_bf, w_cv_bf)


def _row_kernel(x_ref, o1_ref, o4_ref, o16_ref, l1_ref, l4_ref, l16_ref, p_ref, halo_ref,
                poolw_f32, pscale_ref, wout_f32, g1_ref, b1_ref,
                wcq_f32, km_ref, vm_ref, wco_f32, g2_ref, b2_ref,
                wr_f32, shg_f32, shu_f32, shd_f32,
                x2_ref, base_ref, logit_ref,
                poolw_ref, wout_ref, wcq_ref, wco_ref, wrh_ref, wrl_ref, shg_ref, shu_ref, shd_ref, *, alpha, seq):
    tm = x_ref.shape[0]
    i = pl.program_id(0)

    @pl.when(i == 0)
    def _():
        for src, dst in ((poolw_f32, poolw_ref), (wout_f32, wout_ref), (wcq_f32, wcq_ref), (wco_f32, wco_ref),
                         (shg_f32, shg_ref), (shu_f32, shu_ref), (shd_f32, shd_ref)):
            dst[...] = src[...].astype(BF16)
        hi, lo = _split_bf16(wr_f32[...])
        wrh_ref[...] = hi
        wrl_ref[...] = lo

    tile_pos = (i * tm) % seq

    l1, l4, l16 = l1_ref[...], l4_ref[...], l16_ref[...]
    mx = jnp.maximum(jnp.maximum(l1, l4), l16)
    e1, e4, e16 = jnp.exp(l1 - mx), jnp.exp(l4 - mx), jnp.exp(l16 - mx)
    inv = 1.0 / (e1 + e4 + e16)
    hrow = lax.broadcasted_iota(I32, (LANES, ATTN_WIDTH), 0)
    hcol = lax.broadcasted_iota(I32, (LANES, ATTN_WIDTH), 1) // HEAD_DIM
    spread = jnp.where(hrow == hcol, 1.0, 0.0).astype(BF16)

    def widen(w):
        hi, lo = _split_bf16(w)
        return _dot(hi, spread) + _dot(lo, spread)

    def branch(o_ref):
        lo, hi = zip(*[_unpack_pair(o_ref[w]) for w in range(o_ref.shape[0])])
        return jnp.concatenate(lo + hi, axis=1)

    last = branch(o16_ref)
    attn = last + widen(e1 * inv) * (branch(o1_ref) - last) + widen(e4 * inv) * (branch(o4_ref) - last)

    halo_rows = halo_ref.shape[0]
    halo = jnp.where(tile_pos > 0, halo_ref[...], 0.0)
    ext = jnp.concatenate([halo, p_ref[...]], axis=0)
    pos = (tile_pos + lax.broadcasted_iota(I32, (tm, 1), 0)).astype(F32)
    gd = ext.shape[1] // len(POOL_WINDOWS)
    mixed = []
    for g, w in enumerate(POOL_WINDOWS):
        eg = ext[:, g * gd:(g + 1) * gd]
        acc, span = eg, 1
        while span < w:
            acc = acc + pltpu.roll(acc, span, 0)
            span *= 2
        count = jnp.minimum(pos + 1.0, float(w))
        pooled = acc[halo_rows:, :] / count - eg[halo_rows:, :]
        mixed.append(_dot(pooled.astype(BF16), poolw_ref[g]) * pscale_ref[:, g * gd:(g + 1) * gd])
    cat = jnp.concatenate([attn.astype(BF16)] + [m.astype(BF16) for m in mixed], axis=1)
    x1 = _layer_norm(alpha * x_ref[...] + _dot(cat, wout_ref[...]), g1_ref[...], b1_ref[...])

    qc = _dot(x1.astype(BF16), wcq_ref[...])
    mhd = qc.shape[1] // MEM_HEADS
    heads = []
    for h in range(MEM_HEADS):
        sl = slice(h * mhd, (h + 1) * mhd)
        s = _dot_nt(qc[:, sl].astype(BF16), km_ref[:, sl]) * (mhd ** -0.5)
        m = jnp.max(s, axis=1, keepdims=True)
        p = jnp.exp(s - m)
        l = jnp.sum(p, axis=1, keepdims=True)
        heads.append((_dot(p.astype(BF16), vm_ref[:, sl]) / l).astype(BF16))
    oc = jnp.concatenate(heads, axis=1)
    x2 = _layer_norm(alpha * x1 + _dot(oc, wco_ref[...]), g2_ref[...], b2_ref[...])
    dh = x2.shape[1] // 2
    x2_ref[...] = _pack_pair(x2[:, :dh], x2[:, dh:])

    xh, xl = _split_bf16(x2)
    logit_ref[...] = _dot_nt(wrh_ref[...], xh) + _dot_nt(wrh_ref[...], xl) + _dot_nt(wrl_ref[...], xh)

    gate = _dot(xh, shg_ref[...])
    up = _dot(xh, shu_ref[...])
    hid = gate / (1.0 + jnp.exp(-gate)) * up
    base_ref[...] = alpha * x2 + _dot(hid.astype(BF16), shd_ref[...])


def _row_block(x2d, o1, o4, o16, l1, l4, l16, p, pool_w, pool_scale, w_out, g1, b1,
               w_cq, kmem, vmem, w_co, g2, b2, wr_t, sh_g, sh_u, sh_d,
               *, alpha, batch, seq, mem_len):
    n, d = x2d.shape
    tm = ROW_TILE
    halo_rows = max(POOL_WINDOWS)
    steps_per_seq = seq // tm
    row = lambda w: pl.BlockSpec((tm, w), lambda i: (i, 0))
    full = lambda a: pl.BlockSpec(a.shape, lambda i: (0,) * a.ndim)
    halo = pl.BlockSpec((halo_rows, p.shape[1]), lambda i: (jnp.maximum(i * (tm // halo_rows) - 1, 0), 0))
    memspec = pl.BlockSpec((mem_len, kmem.shape[1]), lambda i: (i // steps_per_seq, 0))
    n_exp = wr_t.shape[0]
    branch = pl.BlockSpec((None, o1.shape[1], tm, LANES), lambda i: (i // steps_per_seq, 0, i % steps_per_seq, 0))
    bf_copy = lambda a: pltpu.VMEM(a.shape, BF16)
    return pl.pallas_call(
        functools.partial(_row_kernel, alpha=alpha, seq=seq),
        grid=(n // tm,),
        in_specs=[row(d), branch, branch, branch, row(LANES), row(LANES), row(LANES),
                  row(p.shape[1]), halo,
                  full(pool_w), full(pool_scale), full(w_out), full(g1), full(b1),
                  full(w_cq), memspec, memspec, full(w_co), full(g2), full(b2),
                  full(wr_t), full(sh_g), full(sh_u), full(sh_d)],
        out_specs=[row(d // 2), row(d), pl.BlockSpec((n_exp, tm), lambda i: (0, i))],
        out_shape=[jax.ShapeDtypeStruct((n, d // 2), U32), jax.ShapeDtypeStruct((n, d), F32),
                   jax.ShapeDtypeStruct((n_exp, n), F32)],
        scratch_shapes=[bf_copy(pool_w), bf_copy(w_out), bf_copy(w_cq), bf_copy(w_co), bf_copy(wr_t), bf_copy(wr_t),
                        bf_copy(sh_g), bf_copy(sh_u), bf_copy(sh_d)],
        compiler_params=_cparams(("arbitrary",)),
        name="row_block",
    )(x2d, o1, o4, o16, l1, l4, l16, p, p, pool_w, pool_scale, w_out, g1, b1,
      w_cq, kmem, vmem, w_co, g2, b2, wr_t, sh_g, sh_u, sh_d)


def _beat_counts(vals, n_rows):
    tn = vals.shape[1]
    n_tiles = n_rows // SUBLANES
    tiles = [vals[t * SUBLANES:(t + 1) * SUBLANES, :] for t in range(n_tiles)]
    sub = lax.broadcasted_iota(I32, (SUBLANES, tn), 0)
    counts = [jnp.zeros((SUBLANES, tn), F32) for _ in range(n_tiles)]
    for e in range(n_rows):
        te, je = divmod(e, SUBLANES)
        row = jnp.broadcast_to(vals[e:e + 1, :], (SUBLANES, tn))
        for t in range(n_tiles):
            strict = jnp.where(row > tiles[t], 1.0, 0.0)
            loose = jnp.where(row >= tiles[t], 1.0, 0.0)
            if t < te:
                beat = strict
            elif t > te:
                beat = loose
            else:
                beat = jnp.where(sub > je, loose, strict)
            counts[t] = counts[t] + beat
    return jnp.concatenate(counts, axis=0)


def _top_k_mask(vals, k):
    ridx = lax.broadcasted_iota(I32, vals.shape, 0).astype(F32)
    mask = jnp.zeros(vals.shape, F32)
    work = vals
    for _ in range(k):
        top = jnp.max(work, axis=0, keepdims=True)
        first = jnp.min(jnp.where(work == top, ridx, float(vals.shape[0])), axis=0, keepdims=True)
        hit = ridx == first
        mask = jnp.where(hit, 1.0, mask)
        work = jnp.where(hit, -jnp.inf, work)
    return mask


def _route_kernel(logit_ref, bias_ref, rank_ref, rankt_ref, gatet_ref, tab_ref, count_ref, carry_ref, rcarry_ref,
                  *, per_group):
    n_exp, tn = logit_ref.shape
    step = pl.program_id(0)

    @pl.when(step == 0)
    def _():
        carry_ref[...] = jnp.zeros_like(carry_ref)
        rcarry_ref[...] = jnp.zeros_like(rcarry_ref)

    scores = 1.0 / (1.0 + jnp.exp(-logit_ref[...]))
    biased = scores + bias_ref[:, 0:1]
    n_groups = n_exp // per_group

    gscore = []
    for g in range(n_groups):
        tile = biased[g * per_group:(g + 1) * per_group, :]
        inner = _beat_counts(tile, per_group)
        gscore.append(jnp.sum(jnp.where(inner < 2.0, tile, 0.0), axis=0, keepdims=True))
    gscore = jnp.concatenate(gscore, axis=0)
    grank = _beat_counts(gscore, n_groups)
    grank = jnp.concatenate(
        [jnp.broadcast_to(grank[g:g + 1, :], (per_group, tn)) for g in range(n_groups)], axis=0)
    masked = jnp.where(grank < float(TOPK_GROUPS), biased, NEG_INF)
    self32 = _top_k_mask(masked, TOP_K)
    selbf = self32.astype(BF16)

    tr = lax.broadcasted_iota(I32, (tn, tn), 0)
    tc = lax.broadcasted_iota(I32, (tn, tn), 1)
    before = jnp.where(tr < tc, 1.0, 0.0).astype(BF16)
    lrank = _dot(selbf, before)
    align = float(RUN_ALIGN)
    lcount_col = jnp.broadcast_to(jnp.sum(self32, axis=1, keepdims=True), (n_exp, LANES))
    sel_wide = jnp.concatenate([selbf, jnp.zeros((LANES - n_exp, tn), BF16)], axis=0)
    lcount_row = _dot_nt(jnp.ones((SUBLANES, tn), BF16), sel_wide)
    lcount_col = jnp.ceil(lcount_col / align) * align
    lcount_row = jnp.ceil(lcount_row / align) * align
    wr = lax.broadcasted_iota(I32, (LANES, LANES), 0)
    wc = lax.broadcasted_iota(I32, (LANES, LANES), 1)
    lstart_row = _dot(lcount_row.astype(BF16), jnp.where(wr < wc, 1.0, 0.0).astype(BF16))

    sub = lax.broadcasted_iota(I32, (SUBLANES, LANES), 0)
    tab_ref[...] = jnp.where(sub == 0, lstart_row, jnp.where(sub == 1, lcount_row, jnp.where(
        sub == 2, rcarry_ref[...], 0.0))).astype(I32)
    rcarry_ref[...] = rcarry_ref[...] + lcount_row
    carry_ref[...] = carry_ref[...] + lcount_col
    count_ref[...] = carry_ref[...]

    denom = jnp.sum(self32 * scores, axis=0, keepdims=True)
    gate = self32 * scores / denom * ROUTED_SCALE
    pad = jnp.zeros((LANES - n_exp, tn), F32)
    widen = lambda v: jnp.concatenate([v, pad], axis=0)
    rank_tab = widen(jnp.where(self32 > 0.0, lrank, -1.0))
    rank_ref[...] = rank_tab.astype(BF16)
    rankt_ref[...] = rank_tab.T.astype(BF16)
    gatet_ref[...] = widen(gate).T.astype(BF16)


def _route(logits_t, bias_col):
    n_exp, n = logits_t.shape
    tn = SORT_TILE
    emaj = jax.ShapeDtypeStruct((LANES, n), BF16)
    tmaj = jax.ShapeDtypeStruct((n, LANES), BF16)
    emaj_spec = pl.BlockSpec((LANES, tn), lambda i: (0, i))
    tmaj_spec = pl.BlockSpec((tn, LANES), lambda i: (i, 0))
    return pl.pallas_call(
        functools.partial(_route_kernel, per_group=n_exp // N_EXPERT_GROUPS),
        grid=(n // tn,),
        in_specs=[pl.BlockSpec((n_exp, tn), lambda i: (0, i)), pl.BlockSpec(bias_col.shape, lambda i: (0, 0))],
        out_specs=[emaj_spec, tmaj_spec, tmaj_spec,
                   pl.BlockSpec((None, SUBLANES, LANES), lambda i: (i, 0, 0)),
                   pl.BlockSpec((n_exp, LANES), lambda i: (0, 0))],
        out_shape=[emaj, tmaj, tmaj, jax.ShapeDtypeStruct((n // tn, SUBLANES, LANES), I32),
                   jax.ShapeDtypeStruct((n_exp, LANES), F32)],
        scratch_shapes=[pltpu.VMEM((n_exp, LANES), F32), pltpu.VMEM((SUBLANES, LANES), F32)],
        compiler_params=_cparams(("arbitrary",)),
        name="route",
    )(logits_t, bias_col)


def _place_kernel(count_ref, tab_ref, runs_ref, bexp_ref, seg_ref, *, n_blocks):
    n_exp = count_ref.shape[0]
    blk = float(EXPERT_BLOCK)
    counts = count_ref[...]
    padded = jnp.ceil(counts / blk) * blk
    ridx = lax.broadcasted_iota(I32, (n_exp, LANES), 0)
    ends = padded
    shift = 1
    while shift < n_exp:
        ends = ends + jnp.where(ridx >= shift, pltpu.roll(ends, shift, 0), 0.0)
        shift *= 2
    starts = ends - padded

    tab = tab_ref[...]
    last = tab[tab.shape[0] - 1]
    total_row = (last[2:3, :] + last[1:2, :]).astype(F32)
    padded_row = jnp.ceil(total_row / blk) * blk
    lidx = lax.broadcasted_iota(I32, (1, LANES), 1)
    ends_row = padded_row
    shift = 1
    while shift < LANES:
        ends_row = ends_row + jnp.where(lidx >= shift, pltpu.roll(ends_row, shift, 1), 0.0)
        shift *= 2
    starts_row = (ends_row - padded_row).astype(I32)
    sub = lax.broadcasted_iota(I32, tab.shape, 1)
    runs_ref[...] = tab + jnp.where(sub == 2, starts_row[None], 0)

    nbp = bexp_ref.shape[1]
    blk_start = (lax.broadcasted_iota(I32, (n_exp, nbp), 1) * EXPERT_BLOCK).astype(F32)
    ends_wide = jnp.broadcast_to(ends[:, 0:1], (n_exp, nbp))
    count_le = lambda bound: jnp.minimum(
        jnp.sum(jnp.where(ends_wide <= bound, 1.0, 0.0), axis=0, keepdims=True), float(n_exp - 1))
    bexp = count_le(blk_start)
    used = ends[n_exp - 1:n_exp, 0:1] / blk
    seg_end = jnp.min(jnp.where(ends_wide > blk_start, ends_wide, 2.0 * float(nbp * EXPERT_BLOCK)),
                      axis=0, keepdims=True)
    sub8 = lax.broadcasted_iota(I32, (SUBLANES, nbp), 0)
    lane = lax.broadcasted_iota(I32, (SUBLANES, nbp), 1)
    plan = jnp.where(sub8 == 0, jnp.where(lane == n_blocks, used, bexp),
                     jnp.where(sub8 == 1, count_le(seg_end), jnp.where(sub8 == 2, seg_end / blk, 0.0)))
    bexp_ref[...] = plan.astype(I32)
    lane2 = lax.broadcasted_iota(I32, (n_exp, LANES), 1)
    seg_ref[...] = jnp.where(lane2 == 0, starts + counts, ends).astype(I32)


def _place(counts, tab, n_blocks):
    n_exp = counts.shape[0]
    nbp = -(-(n_blocks + 1) // LANES) * LANES
    return pl.pallas_call(
        functools.partial(_place_kernel, n_blocks=n_blocks),
        out_shape=[jax.ShapeDtypeStruct(tab.shape, I32), jax.ShapeDtypeStruct((SUBLANES, nbp), I32),
                   jax.ShapeDtypeStruct((n_exp, LANES), I32)],
        compiler_params=_cparams(None),
        name="route_place",
    )(counts, tab)


def _copy_rows(src_ref, dst_ref, sem, src0, dst0, n, wait=False):
    def piece(off, p):
        src = src_ref.at[pl.ds(pl.multiple_of(src0 + off, RUN_ALIGN), p), :]
        dst = dst_ref.at[pl.ds(pl.multiple_of(dst0 + off, RUN_ALIGN), p), :]
        cp = pltpu.make_async_copy(src, dst, sem)
        if wait:
            cp.wait()
        else:
            cp.start()

    def chunk(c, carry):
        piece(c * RUN_CHUNK, RUN_CHUNK)
        return carry
    lax.fori_loop(0, lax.shift_right_logical(n, RUN_CHUNK.bit_length() - 1), chunk, 0)
    p = RUN_CHUNK // 2
    while p >= RUN_ALIGN:
        @pl.when((n & p) != 0)
        def _(p=p):
            piece(n & (-2 * p), p)
        p //= 2


def _start_run(src_ref, dst_ref, sem, src0, dst0, n, limit, enable=None):
    p = RUN_ALIGN
    while p * 2 <= limit:
        p *= 2
    while p >= RUN_ALIGN:
        hit = (n & p) != 0
        @pl.when(hit if enable is None else hit & enable)
        def _(p=p):
            off = n & (-2 * p)
            src = src_ref.at[pl.ds(pl.multiple_of(src0 + off, RUN_ALIGN), p), :]
            dst = dst_ref.at[pl.ds(pl.multiple_of(dst0 + off, RUN_ALIGN), p), :]
            pltpu.make_async_copy(src, dst, sem).start()
        p //= 2


def _run_membership(tab, rows, first_row=0):
    j = lax.broadcasted_iota(I32, (rows, LANES), 0) + first_row
    start = tab[0:1, :]
    end = start + tab[1:2, :]
    return jnp.where(j >= start, jnp.where(j < end, 1.0, 0.0), 0.0)


def _dispatch_kernel(prev_runs_ref, runs_ref, seg_ref, tab_ref, rank_ref, x_ref, xs_ref, buf_ref, zero_ref, sems,
                     zsem, *, n_exp):
    step, n_steps = pl.program_id(0), pl.num_programs(0)
    tm = x_ref.shape[0]
    n_slots, rows = buf_ref.shape[:2]
    pad_rows = rows - TOP_K * tm
    cur, prev, old = step % n_slots, (step + n_slots - 1) % n_slots, (step + n_slots - 2) % n_slots

    def start_runs(tab_ref, s, experts, enable=None):
        for e in experts:
            _start_run(buf_ref.at[s], xs_ref, sems.at[s], tab_ref[0, e], tab_ref[2, e], tab_ref[1, e], tm, enable)

    def start_filler(tab_ref, s, enable=None):
        used = tab_ref[0, n_exp - 1] + tab_ref[1, n_exp - 1]
        spare = xs_ref.shape[0] - (n_slots - s) * pad_rows
        _start_run(buf_ref.at[s], xs_ref, sems.at[s], used, spare, rows - used, pad_rows, enable)

    def drain(s):
        pltpu.make_async_copy(buf_ref.at[s], xs_ref.at[pl.ds(0, rows), :], sems.at[s]).wait()

    @pl.when(step == 0)
    def _():
        zero_ref[...] = jnp.zeros_like(zero_ref)
        for wait in (False, True):
            def per_expert(e, c, wait=wait):
                first = seg_ref[e, 0]
                _copy_rows(zero_ref, xs_ref, zsem, 0, first, seg_ref[e, 1] - first, wait)
                return c
            lax.fori_loop(0, n_exp, per_expert, 0)

    lo, hi = _unpack_pair(x_ref[...])
    x = jnp.concatenate([lo.astype(BF16), hi.astype(BF16)], axis=1)
    tab = tab_ref[...]
    bits = lambda v: lax.bitcast_convert_type(v, U32)
    chunk, per = rows // DISPATCH_CHUNKS, n_exp // DISPATCH_CHUNKS
    for c in range(DISPATCH_CHUNKS):
        member = _run_membership(tab, chunk, c * chunk)
        run_start = jnp.sum(member * tab[0:1, :].astype(F32), axis=1, keepdims=True)
        in_run = (lax.broadcasted_iota(I32, (chunk, 1), 0) + c * chunk).astype(F32) - run_start
        onehot = jnp.where(_dot(member.astype(BF16), rank_ref[...]) == in_run, 1.0, 0.0)
        srt = _dot(onehot.astype(BF16), x)
        dh = srt.shape[1] // 2
        buf_ref[cur, c * chunk:(c + 1) * chunk, :] = (bits(srt[:, :dh]) >> 16) | bits(srt[:, dh:])
        start_runs(prev_runs_ref, prev, range(c * per, (c + 1) * per), step > 0)
    start_filler(prev_runs_ref, prev, step > 0)

    @pl.when(step >= 2)
    def _():
        drain(old)

    @pl.when(step == n_steps - 1)
    def _():
        start_runs(runs_ref, cur, range(n_exp))
        start_filler(runs_ref, cur)

        @pl.when(step >= 1)
        def _():
            drain(prev)
        drain(cur)


def _sorted_rows(tm, n_exp):
    return TOP_K * tm + n_exp * RUN_ALIGN


def _dispatch(x2, rank_tab, runs, seg, n_rows, n_exp):
    n, d = x2.shape
    tm = SORT_TILE
    rows = _sorted_rows(tm, n_exp)
    assert rows % DISPATCH_CHUNKS == 0 and n_exp % DISPATCH_CHUNKS == 0
    tab = (None,) + runs.shape[1:]
    return pl.pallas_call(
        functools.partial(_dispatch_kernel, n_exp=n_exp),
        grid=(n // tm,),
        in_specs=[pl.BlockSpec(tab, lambda i: (jnp.maximum(i - 1, 0), 0, 0), memory_space=pltpu.SMEM),
                  pl.BlockSpec(tab, lambda i: (i, 0, 0), memory_space=pltpu.SMEM),
                  pl.BlockSpec(seg.shape, lambda i: (0, 0), memory_space=pltpu.SMEM),
                  pl.BlockSpec(tab, lambda i: (i, 0, 0)),
                  pl.BlockSpec((LANES, tm), lambda i: (0, i)),
                  pl.BlockSpec((tm, d), lambda i: (i, 0))],
        out_specs=pl.BlockSpec(memory_space=pl.ANY),
        out_shape=jax.ShapeDtypeStruct((n_rows + DISPATCH_SLOTS * (rows - TOP_K * tm), d), x2.dtype),
        scratch_shapes=[pltpu.VMEM((DISPATCH_SLOTS, rows, d), x2.dtype), pltpu.VMEM((EXPERT_BLOCK, d), x2.dtype),
                        pltpu.SemaphoreType.DMA((DISPATCH_SLOTS,)), pltpu.SemaphoreType.DMA],
        compiler_params=_cparams(("arbitrary",)),
        name="dispatch",
    )(runs, runs, seg, runs, rank_tab, x2)


def _expert_kernel(plan_ref, xs_ref, wg_ref, wu_ref, wd_ref, ys_ref, xbuf_ref, xsems, wg_f32, wu_f32, wd_f32,
                   wsems, wslot_ref, wg_bf, wu_bf, wd_bf, *, n_blocks):
    i = pl.program_id(0)
    used = plan_ref[0, n_blocks]
    depth, blk = xbuf_ref.shape[:2]

    def fetch(j):
        rows = pl.ds(pl.multiple_of(j * blk, blk), blk)
        return pltpu.make_async_copy(xs_ref.at[rows, :], xbuf_ref.at[j % depth], xsems.at[j % depth])

    def weights(e, s):
        return [pltpu.make_async_copy(src.at[e], dst.at[s], wsems.at[s])
                for src, dst in ((wg_ref, wg_f32), (wu_ref, wu_f32), (wd_ref, wd_f32))]

    @pl.when(i == 0)
    def _():
        for j in range(depth - 1):
            @pl.when(j < used)
            def _(j=j):
                fetch(j).start()

    @pl.when(i + depth - 1 < used)
    def _():
        fetch(i + depth - 1).start()

    @pl.when((i == 0) & (used > 0))
    def _():
        wslot_ref[0] = 0
        for cp in weights(plan_ref[0, 0], 0):
            cp.start()

    @pl.when(i < used)
    def _():
        @pl.when((i == 0) | (plan_ref[0, i] != plan_ref[0, jnp.maximum(i - 1, 0)]))
        def _():
            s = wslot_ref[0]
            for cp in weights(plan_ref[0, i], s):
                cp.wait()

            @pl.when(plan_ref[2, i] < used)
            def _():
                for cp in weights(plan_ref[1, i], 1 - s):
                    cp.start()
            wslot_ref[0] = 1 - s
            wg_bf[...] = wg_f32[s].astype(BF16)
            wu_bf[...] = wu_f32[s].astype(BF16)
            wd_bf[...] = wd_f32[s].astype(BF16)

        fetch(i).wait()
        lo, hi = _unpack_pair(xbuf_ref[i % depth])
        x = jnp.concatenate([lo.astype(BF16), hi.astype(BF16)], axis=1)
        gate = _dot(x, wg_bf[...])
        up = _dot(x, wu_bf[...])
        hid = gate / (1.0 + jnp.exp(-gate)) * up
        y = _dot(hid.astype(BF16), wd_bf[...])
        dh = y.shape[1] // 2
        ys_ref[...] = _pack_pair(y[:, :dh], y[:, dh:])


def _experts(bexp, xs, exp_gate, exp_up, exp_down, n_blocks):
    n_rows = xs.shape[0]
    d, hidden = exp_gate.shape[1:]
    blk = EXPERT_BLOCK

    def rows(i, b):
        return (jnp.minimum(i, jnp.maximum(b[0, n_blocks] - 1, 0)), 0)

    hbm = pl.BlockSpec(memory_space=pl.ANY)
    grid_spec = pltpu.PrefetchScalarGridSpec(
        num_scalar_prefetch=1,
        grid=(n_blocks,),
        in_specs=[hbm, hbm, hbm, hbm],
        out_specs=pl.BlockSpec((blk, d // 2), rows),
        scratch_shapes=[pltpu.VMEM((EXPERT_RING, blk, d // 2), xs.dtype), pltpu.SemaphoreType.DMA((EXPERT_RING,)),
                        pltpu.VMEM((2, d, hidden), F32), pltpu.VMEM((2, d, hidden), F32),
                        pltpu.VMEM((2, hidden, d), F32), pltpu.SemaphoreType.DMA((2,)), pltpu.SMEM((1,), I32),
                        pltpu.VMEM((d, hidden), BF16), pltpu.VMEM((d, hidden), BF16), pltpu.VMEM((hidden, d), BF16)],
    )
    return pl.pallas_call(
        functools.partial(_expert_kernel, n_blocks=n_blocks),
        grid_spec=grid_spec,
        out_shape=jax.ShapeDtypeStruct((n_rows, d // 2), U32),
        compiler_params=_cparams(("arbitrary",)),
        name="experts",
    )(bexp, xs, exp_gate, exp_up, exp_down)


def _combine_kernel(runs_ref, next_runs_ref, ahead_runs_ref, tab_ref, rankt_ref, gatet_ref, base_ref, g3_ref, b3_ref,
                    ys_ref, out_ref, buf_ref, sems, *, n_exp):
    step, n_steps = pl.program_id(0), pl.num_programs(0)
    tm = base_ref.shape[0]
    n_slots, rows = buf_ref.shape[:2]
    slot, ahead = step % n_slots, (step + 2) % n_slots

    def fetch(tab_ref, s, experts, enable=None):
        for e in experts:
            _start_run(ys_ref, buf_ref.at[s], sems.at[s], tab_ref[2, e], tab_ref[0, e], tab_ref[1, e], tm, enable)

    def fetch_filler(tab_ref, s, enable=None):
        used = tab_ref[0, n_exp - 1] + tab_ref[1, n_exp - 1]
        _start_run(ys_ref, buf_ref.at[s], sems.at[s], 0, used, rows - used, rows - TOP_K * tm, enable)

    @pl.when(step == 0)
    def _():
        fetch(runs_ref, 0, range(n_exp))
        fetch_filler(runs_ref, 0)
        fetch(next_runs_ref, 1, range(n_exp), n_steps > 1)
        fetch_filler(next_runs_ref, 1, n_steps > 1)

    pltpu.make_async_copy(ys_ref.at[pl.ds(0, rows), :], buf_ref.at[slot], sems.at[slot]).wait()

    tab = tab_ref[...]
    start = tab[0:1, :].astype(F32)
    start_hi = jnp.floor(start / float(POS_BASE))
    sub = lax.broadcasted_iota(I32, (SUBLANES, LANES), 0)
    digits = jnp.where(sub == 0, start_hi, jnp.where(sub == 1, start - start_hi * float(POS_BASE), 0.0)).astype(BF16)
    have_ahead = step + 2 < n_steps
    chunk, per = rows // DISPATCH_CHUNKS, n_exp // DISPATCH_CHUNKS
    routed = jnp.zeros(base_ref.shape, F32)
    for c in range(DISPATCH_CHUNKS):
        member = _run_membership(tab, chunk, c * chunk).astype(BF16)
        picked = _dot_nt(digits, member)
        in_run = ((lax.broadcasted_iota(I32, (1, chunk), 1) + c * chunk).astype(F32)
                  - (picked[0:1, :] * float(POS_BASE) + picked[1:2, :]))
        w = jnp.where(_dot_nt(rankt_ref[...], member) == in_run, _dot_nt(gatet_ref[...], member), 0.0)
        lo, hi = _unpack_pair(buf_ref[slot, c * chunk:(c + 1) * chunk, :])
        y = jnp.concatenate([lo.astype(BF16), hi.astype(BF16)], axis=1)
        routed = routed + _dot(w.astype(BF16), y)
        fetch(ahead_runs_ref, ahead, range(c * per, (c + 1) * per), have_ahead)
    fetch_filler(ahead_runs_ref, ahead, have_ahead)
    out_ref[...] = _layer_norm(base_ref[...] + routed, g3_ref[...], b3_ref[...])


def _combine(runs, rank_t, gate_t, base, g3, b3, ys, n_exp):
    n, d = base.shape
    tm = SORT_TILE
    last = n // tm - 1
    tab = lambda f: pl.BlockSpec((None,) + runs.shape[1:], f, memory_space=pltpu.SMEM)
    tmaj = pl.BlockSpec((tm, LANES), lambda i: (i, 0))
    return pl.pallas_call(
        functools.partial(_combine_kernel, n_exp=n_exp),
        grid=(n // tm,),
        in_specs=[tab(lambda i: (i, 0, 0)), tab(lambda i: (jnp.minimum(i + 1, last), 0, 0)),
                  tab(lambda i: (jnp.minimum(i + 2, last), 0, 0)),
                  pl.BlockSpec((None,) + runs.shape[1:], lambda i: (i, 0, 0)),
                  tmaj, tmaj,
                  pl.BlockSpec((tm, d), lambda i: (i, 0)),
                  pl.BlockSpec(g3.shape, lambda i: (0, 0)),
                  pl.BlockSpec(b3.shape, lambda i: (0, 0)),
                  pl.BlockSpec(memory_space=pl.ANY)],
        out_specs=pl.BlockSpec((tm, d), lambda i: (i, 0)),
        out_shape=jax.ShapeDtypeStruct((n, d), F32),
        scratch_shapes=[pltpu.VMEM((DISPATCH_SLOTS, _sorted_rows(tm, n_exp), ys.shape[1]), ys.dtype),
                        pltpu.SemaphoreType.DMA((DISPATCH_SLOTS,))],
        compiler_params=_cparams(("arbitrary",)),
        name="combine",
    )(runs, runs, runs, runs, rank_t, gate_t, base, g3, b3, ys)


def _layer(x2d, mem2d, w_in, pool_w, pool_scale, w_out, ln1_g, ln1_b, w_cq, w_ck, w_cv, w_co, ln2_g, ln2_b,
           w_router, router_bias, exp_gate, exp_up, exp_down, sh_gate, sh_up, sh_down, ln3_g, ln3_b,
           *, alpha, batch, seq, mem_len):
    n, d = x2d.shape
    n_exp = w_router.shape[1]
    row = lambda a: a.reshape(1, -1)

    dils = tuple(dil for _, dil in DILATED_PATTERNS)
    *qkv, p = _in_proj(x2d, w_in, dils)
    branches = []
    for di, (window, dil) in enumerate(DILATED_PATTERNS):
        assert window // dil == ATTN_BLOCK and seq % (dil * ATTN_BLOCK) == 0
        assert IN_TILE % (dil * 2 * SUBLANES) == 0
        branches.append(_dil_attn(*qkv[3 * di:3 * di + 3], batch, seq, dil))
    (o1, l1), (o4, l4), (o16, l16) = branches
    kmem, vmem = _mem_kv(mem2d, w_ck, w_cv)

    x2, base, logits_t = _row_block(
        x2d, o1, o4, o16, l1, l4, l16, p, pool_w, row(pool_scale), w_out, row(ln1_g), row(ln1_b),
        w_cq, kmem, vmem, w_co, row(ln2_g), row(ln2_b), w_router.T, sh_gate, sh_up, sh_down,
        alpha=alpha, batch=batch, seq=seq, mem_len=mem_len)

    bias_col = router_bias.reshape(n_exp, 1)
    rank_tab, rank_t, gate_t, tab, counts = _route(logits_t, bias_col)
    max_rows = n * TOP_K + (n // SORT_TILE) * n_exp * (RUN_ALIGN - 1)
    n_blocks = -(-max_rows // EXPERT_BLOCK) + n_exp
    runs, bexp, seg = _place(counts, tab, n_blocks)
    xs = _dispatch(x2, rank_tab, runs, seg, n_blocks * EXPERT_BLOCK, n_exp)
    ys = _experts(bexp, xs, exp_gate, exp_up, exp_down, n_blocks)
    return _combine(runs, rank_t, gate_t, base, row(ln3_g), row(ln3_b), ys, n_exp)


def kernel(x, mem, w_in, pool_w, pool_scale, w_out, ln1_g, ln1_b, w_cq, w_ck, w_cv, w_co, ln2_g, ln2_b, w_router, router_bias, exp_gate, exp_up, exp_down, sh_gate, sh_up, sh_down, ln3_g, ln3_b):
    batch, seq, d = x.shape
    mem_len = mem.shape[1]
    depth = w_in.shape[0]
    alpha = (2.0 * depth) ** 0.25
    x2d = x.reshape(batch * seq, d)
    mem2d = mem.reshape(batch * mem_len, d)
    for l in range(depth):
        x2d = _layer(x2d, mem2d, w_in[l], pool_w[l], pool_scale[l], w_out[l], ln1_g[l], ln1_b[l], w_cq[l], w_ck[l],
                     w_cv[l], w_co[l], ln2_g[l], ln2_b[l], w_router[l], router_bias[l], exp_gate[l], exp_up[l],
                     exp_down[l], sh_gate[l], sh_up[l], sh_down[l], ln3_g[l], ln3_b[l],
                     alpha=alpha, batch=batch, seq=seq, mem_len=mem_len)
    return x2d.reshape(batch, seq, d)
```

```python
import functools

import jax
import jax.numpy as jnp
from jax import lax
from jax.experimental import pallas as pl
from jax.experimental.pallas import tpu as pltpu

F32 = jnp.float32
BF16 = jnp.bfloat16
I32 = jnp.int32
U32 = jnp.uint32

ATTN_HEADS = 8
HEAD_DIM = 64
ATTN_WIDTH = ATTN_HEADS * HEAD_DIM
DILATED_PATTERNS = ((128, 1), (512, 4), (2048, 16))
POOL_WINDOWS = (2, 4, 8, 16)
MEM_HEADS = 4
N_EXPERT_GROUPS = 8
TOPK_GROUPS = 4
TOP_K = 8
ROUTED_SCALE = 2.5
LN_EPS = 1e-5
NEG_INF = -1e30

LANES = 128
SUBLANES = 8
VMEM_LIMIT = 56 * 1024 * 1024

ATTN_BLOCK = 128
ATTN_STEP_BLOCKS = 8
ROW_TILE = 512
IN_TILE = 512
SORT_TILE = 256
EXPERT_BLOCK = 512
EXPERT_RING = 3
RUN_ALIGN = SUBLANES
RUN_CHUNK = 64
POS_BASE = 64
DISPATCH_SLOTS = 3
DISPATCH_CHUNKS = 4


def _cparams(sem, vmem=VMEM_LIMIT):
    return pltpu.CompilerParams(dimension_semantics=sem, vmem_limit_bytes=vmem)


def _layer_norm(y, g, b):
    mu = jnp.mean(y, axis=-1, keepdims=True)
    d = y - mu
    var = jnp.mean(d * d, axis=-1, keepdims=True)
    return d * lax.rsqrt(var + LN_EPS) * g + b


def _dot(a, b):
    return jnp.dot(a, b, preferred_element_type=F32)


def _dot_nt(a, b):
    return lax.dot_general(a, b, (((1,), (1,)), ((), ())), preferred_element_type=F32)


def _split_bf16(x):
    hi = x.astype(BF16)
    lo = (x - hi.astype(F32)).astype(BF16)
    return hi, lo


def _bf16_bits(x):
    return lax.bitcast_convert_type(x.astype(BF16).astype(F32), U32)


def _pack_pair(lo, hi):
    return (_bf16_bits(lo) >> 16) | _bf16_bits(hi)


def _unpack_pair(w):
    lo = lax.bitcast_convert_type(w << 16, F32)
    hi = lax.bitcast_convert_type(w & jnp.uint32(0xFFFF0000), F32)
    return lo, hi


def _in_proj_kernel(x_ref, w_ref, *refs, dils):
    n_qkv = 3 * len(dils)
    out_refs, p_ref, slab_ref, wbf_ref = refs[:n_qkv], refs[n_qkv], refs[n_qkv + 1], refs[n_qkv + 2]

    @pl.when(pl.program_id(0) == 0)
    def _():
        wbf_ref[...] = w_ref[...].astype(BF16)

    w_ref = wbf_ref
    tm = x_ref.shape[0]
    x = x_ref[...].astype(BF16)
    aw = ATTN_WIDTH
    n_slabs = aw // LANES
    for a in range(3):
        val = _dot(x, w_ref[:, a * aw:(a + 1) * aw])
        if a == 0:
            val = val * (HEAD_DIM ** -0.5)
        for s in range(n_slabs):
            slab_ref[s] = val[:, s * LANES:(s + 1) * LANES]
        for di, dil in enumerate(dils):
            dst = out_refs[3 * di + a]
            if dil == 1:
                dst[...] = val.astype(BF16)
                continue
            for r in range(dil):
                for s in range(n_slabs):
                    rows = slab_ref[s, pl.ds(r, tm // dil, stride=dil), :]
                    dst[:, r * aw + s * LANES:r * aw + (s + 1) * LANES] = rows.astype(BF16)
    p_ref[...] = _dot(x, w_ref[:, 3 * aw:])


def _in_proj(x2d, w_in, dils):
    n, d = x2d.shape
    aw = ATTN_WIDTH
    pw = w_in.shape[1] - 3 * aw
    tm = IN_TILE
    row = lambda rows, w: pl.BlockSpec((rows, w), lambda i: (i, 0))
    qkv_specs, qkv_shapes = [], []
    for dil in dils:
        qkv_specs += [row(tm // dil, dil * aw)] * 3
        qkv_shapes += [jax.ShapeDtypeStruct((n // dil, dil * aw), BF16)] * 3
    return pl.pallas_call(
        functools.partial(_in_proj_kernel, dils=dils),
        grid=(n // tm,),
        in_specs=[row(tm, d), pl.BlockSpec(w_in.shape, lambda i: (0, 0))],
        out_specs=qkv_specs + [row(tm, pw)],
        out_shape=qkv_shapes + [jax.ShapeDtypeStruct((n, pw), F32)],
        scratch_shapes=[pltpu.VMEM((aw // LANES, tm, LANES), F32), pltpu.VMEM(w_in.shape, BF16)],
        compiler_params=_cparams(("arbitrary",)),
        name="in_proj",
    )(x2d, w_in)


def _dil_attn_kernel(q_ref, kc_ref, kp_ref, vc_ref, vp_ref, o_ref, lse_ref, bias_ref, *, dil):
    blk = ATTN_BLOCK
    first_step = (pl.program_id(0) == 0) & (pl.program_id(1) == 0) & (pl.program_id(2) == 0)

    @pl.when(first_step)
    def _():
        qi = lax.broadcasted_iota(I32, (blk, 2 * blk), 0)
        kj = lax.broadcasted_iota(I32, (blk, 2 * blk), 1)
        dist = qi + blk - kj
        inside = (dist >= 0) & (dist <= blk)
        penalty = (dist * dil).astype(F32)
        for h in range(ATTN_HEADS):
            slope = 2.0 ** (-8.0 * (h + 1) / ATTN_HEADS)
            bias_ref[0, h] = jnp.where(inside, -slope * penalty, NEG_INF)
            bias_ref[1, h] = jnp.where(inside & (kj >= blk), -slope * penalty, NEG_INF)

    i = pl.program_id(1)
    lane = lax.broadcasted_iota(I32, (blk, LANES), 1)
    low_half = lane < HEAD_DIM
    n_pairs = ATTN_WIDTH // LANES
    half = n_pairs // 2
    n_res = q_ref.shape[1] // ATTN_WIDTH
    for rr, j in [(rr, j) for rr in range(n_res) for j in range(q_ref.shape[0] // blk)]:
        r = pl.program_id(2) * n_res + rr
        table = jnp.where(i == 0, 1, 0) if j == 0 else 0
        lse_tile = jnp.zeros((blk, LANES), F32)
        pair_out = []
        for hp in range(n_pairs):
            sl = slice(rr * ATTN_WIDTH + hp * LANES, rr * ATTN_WIDTH + (hp + 1) * LANES)
            q2 = q_ref[j * blk:(j + 1) * blk, sl]
            if j == 0:
                kcat = jnp.concatenate([kp_ref[:, sl], kc_ref[0:blk, sl]], axis=0)
                vcat = jnp.concatenate([vp_ref[:, sl], vc_ref[0:blk, sl]], axis=0)
            else:
                kcat = kc_ref[(j - 1) * blk:(j + 1) * blk, sl]
                vcat = vc_ref[(j - 1) * blk:(j + 1) * blk, sl]
            outs = []
            for e in range(LANES // HEAD_DIM):
                h = hp * (LANES // HEAD_DIM) + e
                keep = low_half if e == 0 else jnp.logical_not(low_half)
                qe = jnp.where(keep, q2, jnp.zeros_like(q2))
                s = _dot_nt(qe, kcat) + bias_ref[table, h]
                m = jnp.max(s, axis=1, keepdims=True)
                p = jnp.exp(s - m)
                l = jnp.sum(p, axis=1, keepdims=True)
                outs.append(_dot(p.astype(BF16), vcat) / l)
                lse_tile = jnp.where(lane == h, m + jnp.log(l), lse_tile)
            pair_out.append(jnp.where(low_half, outs[0], outs[1]))
        rows = pl.ds(j * blk * dil + r, blk, stride=dil) if dil > 1 else slice(j * blk, (j + 1) * blk)
        for w in range(half):
            o_ref[w, rows, :] = _pack_pair(pair_out[w], pair_out[w + half])
        lse_ref[rows, :] = lse_tile


def _dil_attn(q, k, v, batch, seq, dil):
    blk = ATTN_BLOCK
    sub = seq // dil
    step_blocks = min(ATTN_STEP_BLOCKS, sub // blk)
    step_rows = step_blocks * blk
    n_res = min(dil, ATTN_STEP_BLOCKS // step_blocks)
    assert sub % step_rows == 0 and dil % n_res == 0
    view = lambda a: a.reshape(batch, sub, dil * ATTN_WIDTH)
    cur = pl.BlockSpec((None, step_rows, n_res * ATTN_WIDTH), lambda b, i, r: (b, i, r))
    prev = pl.BlockSpec((None, blk, n_res * ATTN_WIDTH), lambda b, i, r: (b, jnp.maximum(i * step_blocks - 1, 0), r))
    half = ATTN_WIDTH // LANES // 2
    o, lse = pl.pallas_call(
        functools.partial(_dil_attn_kernel, dil=dil),
        grid=(batch, sub // step_rows, dil // n_res),
        in_specs=[cur, cur, prev, cur, prev],
        out_specs=[pl.BlockSpec((None, half, step_rows * dil, LANES), lambda b, i, r: (b, 0, i, 0)),
                   pl.BlockSpec((None, step_rows * dil, LANES), lambda b, i, r: (b, i, 0))],
        out_shape=[jax.ShapeDtypeStruct((batch, half, seq, LANES), U32),
                   jax.ShapeDtypeStruct((batch, seq, LANES), F32)],
        scratch_shapes=[pltpu.VMEM((2, ATTN_HEADS, blk, 2 * blk), F32)],
        compiler_params=_cparams(("arbitrary",) * 3),
        name=f"dil_attn_d{dil}",
    )(view(q), view(k), view(k), view(v), view(v))
    return o, lse.reshape(batch * seq, LANES)


def _mem_kv_kernel(mem_ref, wk_ref, wv_ref, k_ref, v_ref):
    m = mem_ref[...].astype(BF16)
    k_ref[...] = _dot(m, wk_ref[...].astype(BF16)).astype(BF16)
    v_ref[...] = _dot(m, wv_ref[...].astype(BF16)).astype(BF16)


def _mem_kv(mem2d, w_ck_bf, w_cv_bf):
    rows = mem2d.shape[0]
    width = w_ck_bf.shape[1]
    return pl.pallas_call(
        _mem_kv_kernel,
        out_shape=[jax.ShapeDtypeStruct((rows, width), BF16)] * 2,
        compiler_params=_cparams(None),
        name="mem_kv",
    )(mem2d, w_ck_bf, w_cv_bf)


def _row_kernel(x_ref, o1_ref, o4_ref, o16_ref, l1_ref, l4_ref, l16_ref, p_ref, halo_ref,
                poolw_f32, pscale_ref, wout_f32, g1_ref, b1_ref,
                wcq_f32, km_ref, vm_ref, wco_f32, g2_ref, b2_ref,
                wr_f32, shg_f32, shu_f32, shd_f32,
                x2_ref, base_ref, logit_ref,
                poolw_ref, wout_ref, wcq_ref, wco_ref, wrh_ref, wrl_ref, shg_ref, shu_ref, shd_ref, *, alpha, seq):
    tm = x_ref.shape[0]
    i = pl.program_id(0)

    @pl.when(i == 0)
    def _():
        for src, dst in ((poolw_f32, poolw_ref), (wout_f32, wout_ref), (wcq_f32, wcq_ref), (wco_f32, wco_ref),
                         (shg_f32, shg_ref), (shu_f32, shu_ref), (shd_f32, shd_ref)):
            dst[...] = src[...].astype(BF16)
        hi, lo = _split_bf16(wr_f32[...])
        wrh_ref[...] = hi
        wrl_ref[...] = lo

    tile_pos = (i * tm) % seq

    l1, l4, l16 = l1_ref[...], l4_ref[...], l16_ref[...]
    mx = jnp.maximum(jnp.maximum(l1, l4), l16)
    e1, e4, e16 = jnp.exp(l1 - mx), jnp.exp(l4 - mx), jnp.exp(l16 - mx)
    inv = 1.0 / (e1 + e4 + e16)
    hrow = lax.broadcasted_iota(I32, (LANES, ATTN_WIDTH), 0)
    hcol = lax.broadcasted_iota(I32, (LANES, ATTN_WIDTH), 1) // HEAD_DIM
    spread = jnp.where(hrow == hcol, 1.0, 0.0).astype(BF16)

    def widen(w):
        hi, lo = _split_bf16(w)
        return _dot(hi, spread) + _dot(lo, spread)

    def branch(o_ref):
        lo, hi = zip(*[_unpack_pair(o_ref[w]) for w in range(o_ref.shape[0])])
        return jnp.concatenate(lo + hi, axis=1)

    last = branch(o16_ref)
    attn = last + widen(e1 * inv) * (branch(o1_ref) - last) + widen(e4 * inv) * (branch(o4_ref) - last)

    halo_rows = halo_ref.shape[0]
    halo = jnp.where(tile_pos > 0, halo_ref[...], 0.0)
    ext = jnp.concatenate([halo, p_ref[...]], axis=0)
    pos = (tile_pos + lax.broadcasted_iota(I32, (tm, 1), 0)).astype(F32)
    gd = ext.shape[1] // len(POOL_WINDOWS)
    mixed = []
    for g, w in enumerate(POOL_WINDOWS):
        eg = ext[:, g * gd:(g + 1) * gd]
        acc, span = eg, 1
        while span < w:
            acc = acc + pltpu.roll(acc, span, 0)
            span *= 2
        count = jnp.minimum(pos + 1.0, float(w))
        pooled = acc[halo_rows:, :] / count - eg[halo_rows:, :]
        mixed.append(_dot(pooled.astype(BF16), poolw_ref[g]) * pscale_ref[:, g * gd:(g + 1) * gd])
    cat = jnp.concatenate([attn.astype(BF16)] + [m.astype(BF16) for m in mixed], axis=1)
    x1 = _layer_norm(alpha * x_ref[...] + _dot(cat, wout_ref[...]), g1_ref[...], b1_ref[...])

    qc = _dot(x1.astype(BF16), wcq_ref[...])
    mhd = qc.shape[1] // MEM_HEADS
    heads = []
    for h in range(MEM_HEADS):
        sl = slice(h * mhd, (h + 1) * mhd)
        s = _dot_nt(qc[:, sl].astype(BF16), km_ref[:, sl]) * (mhd ** -0.5)
        m = jnp.max(s, axis=1, keepdims=True)
        p = jnp.exp(s - m)
        l = jnp.sum(p, axis=1, keepdims=True)
        heads.append((_dot(p.astype(BF16), vm_ref[:, sl]) / l).astype(BF16))
    oc = jnp.concatenate(heads, axis=1)
    x2 = _layer_norm(alpha * x1 + _dot(oc, wco_ref[...]), g2_ref[...], b2_ref[...])
    dh = x2.shape[1] // 2
    x2_ref[...] = _pack_pair(x2[:, :dh], x2[:, dh:])

    xh, xl = _split_bf16(x2)
    logit_ref[...] = _dot_nt(wrh_ref[...], xh) + _dot_nt(wrh_ref[...], xl) + _dot_nt(wrl_ref[...], xh)

    gate = _dot(xh, shg_ref[...])
    up = _dot(xh, shu_ref[...])
    hid = gate / (1.0 + jnp.exp(-gate)) * up
    base_ref[...] = alpha * x2 + _dot(hid.astype(BF16), shd_ref[...])


def _row_block(x2d, o1, o4, o16, l1, l4, l16, p, pool_w, pool_scale, w_out, g1, b1,
               w_cq, kmem, vmem, w_co, g2, b2, wr_t, sh_g, sh_u, sh_d,
               *, alpha, batch, seq, mem_len):
    n, d = x2d.shape
    tm = ROW_TILE
    halo_rows = max(POOL_WINDOWS)
    steps_per_seq = seq // tm
    row = lambda w: pl.BlockSpec((tm, w), lambda i: (i, 0))
    full = lambda a: pl.BlockSpec(a.shape, lambda i: (0,) * a.ndim)
    halo = pl.BlockSpec((halo_rows, p.shape[1]), lambda i: (jnp.maximum(i * (tm // halo_rows) - 1, 0), 0))
    memspec = pl.BlockSpec((mem_len, kmem.shape[1]), lambda i: (i // steps_per_seq, 0))
    n_exp = wr_t.shape[0]
    branch = pl.BlockSpec((None, o1.shape[1], tm, LANES), lambda i: (i // steps_per_seq, 0, i % steps_per_seq, 0))
    bf_copy = lambda a: pltpu.VMEM(a.shape, BF16)
    return pl.pallas_call(
        functools.partial(_row_kernel, alpha=alpha, seq=seq),
        grid=(n // tm,),
        in_specs=[row(d), branch, branch, branch, row(LANES), row(LANES), row(LANES),
                  row(p.shape[1]), halo,
                  full(pool_w), full(pool_scale), full(w_out), full(g1), full(b1),
                  full(w_cq), memspec, memspec, full(w_co), full(g2), full(b2),
                  full(wr_t), full(sh_g), full(sh_u), full(sh_d)],
        out_specs=[row(d // 2), row(d), pl.BlockSpec((n_exp, tm), lambda i: (0, i))],
        out_shape=[jax.ShapeDtypeStruct((n, d // 2), U32), jax.ShapeDtypeStruct((n, d), F32),
                   jax.ShapeDtypeStruct((n_exp, n), F32)],
        scratch_shapes=[bf_copy(pool_w), bf_copy(w_out), bf_copy(w_cq), bf_copy(w_co), bf_copy(wr_t), bf_copy(wr_t),
                        bf_copy(sh_g), bf_copy(sh_u), bf_copy(sh_d)],
        compiler_params=_cparams(("arbitrary",)),
        name="row_block",
    )(x2d, o1, o4, o16, l1, l4, l16, p, p, pool_w, pool_scale, w_out, g1, b1,
      w_cq, kmem, vmem, w_co, g2, b2, wr_t, sh_g, sh_u, sh_d)


def _beat_counts(vals, n_rows):
    tn = vals.shape[1]
    n_tiles = n_rows // SUBLANES
    tiles = [vals[t * SUBLANES:(t + 1) * SUBLANES, :] for t in range(n_tiles)]
    sub = lax.broadcasted_iota(I32, (SUBLANES, tn), 0)
    counts = [jnp.zeros((SUBLANES, tn), F32) for _ in range(n_tiles)]
    for e in range(n_rows):
        te, je = divmod(e, SUBLANES)
        row = jnp.broadcast_to(vals[e:e + 1, :], (SUBLANES, tn))
        for t in range(n_tiles):
            strict = jnp.where(row > tiles[t], 1.0, 0.0)
            loose = jnp.where(row >= tiles[t], 1.0, 0.0)
            if t < te:
                beat = strict
            elif t > te:
                beat = loose
            else:
                beat = jnp.where(sub > je, loose, strict)
            counts[t] = counts[t] + beat
    return jnp.concatenate(counts, axis=0)


def _top_k_mask(vals, k):
    ridx = lax.broadcasted_iota(I32, vals.shape, 0).astype(F32)
    mask = jnp.zeros(vals.shape, F32)
    work = vals
    for _ in range(k):
        top = jnp.max(work, axis=0, keepdims=True)
        first = jnp.min(jnp.where(work == top, ridx, float(vals.shape[0])), axis=0, keepdims=True)
        hit = ridx == first
        mask = jnp.where(hit, 1.0, mask)
        work = jnp.where(hit, -jnp.inf, work)
    return mask


def _route_kernel(logit_ref, bias_ref, rank_ref, rankt_ref, gatet_ref, tab_ref, count_ref, carry_ref, rcarry_ref,
                  *, per_group):
    n_exp, tn = logit_ref.shape
    step = pl.program_id(0)

    @pl.when(step == 0)
    def _():
        carry_ref[...] = jnp.zeros_like(carry_ref)
        rcarry_ref[...] = jnp.zeros_like(rcarry_ref)

    scores = 1.0 / (1.0 + jnp.exp(-logit_ref[...]))
    biased = scores + bias_ref[:, 0:1]
    n_groups = n_exp // per_group

    gscore = []
    for g in range(n_groups):
        tile = biased[g * per_group:(g + 1) * per_group, :]
        inner = _beat_counts(tile, per_group)
        gscore.append(jnp.sum(jnp.where(inner < 2.0, tile, 0.0), axis=0, keepdims=True))
    gscore = jnp.concatenate(gscore, axis=0)
    grank = _beat_counts(gscore, n_groups)
    grank = jnp.concatenate(
        [jnp.broadcast_to(grank[g:g + 1, :], (per_group, tn)) for g in range(n_groups)], axis=0)
    masked = jnp.where(grank < float(TOPK_GROUPS), biased, NEG_INF)
    self32 = _top_k_mask(masked, TOP_K)
    selbf = self32.astype(BF16)

    tr = lax.broadcasted_iota(I32, (tn, tn), 0)
    tc = lax.broadcasted_iota(I32, (tn, tn), 1)
    before = jnp.where(tr < tc, 1.0, 0.0).astype(BF16)
    lrank = _dot(selbf, before)
    align = float(RUN_ALIGN)
    lcount_col = jnp.broadcast_to(jnp.sum(self32, axis=1, keepdims=True), (n_exp, LANES))
    sel_wide = jnp.concatenate([selbf, jnp.zeros((LANES - n_exp, tn), BF16)], axis=0)
    lcount_row = _dot_nt(jnp.ones((SUBLANES, tn), BF16), sel_wide)
    lcount_col = jnp.ceil(lcount_col / align) * align
    lcount_row = jnp.ceil(lcount_row / align) * align
    wr = lax.broadcasted_iota(I32, (LANES, LANES), 0)
    wc = lax.broadcasted_iota(I32, (LANES, LANES), 1)
    lstart_row = _dot(lcount_row.astype(BF16), jnp.where(wr < wc, 1.0, 0.0).astype(BF16))

    sub = lax.broadcasted_iota(I32, (SUBLANES, LANES), 0)
    tab_ref[...] = jnp.where(sub == 0, lstart_row, jnp.where(sub == 1, lcount_row, jnp.where(
        sub == 2, rcarry_ref[...], 0.0))).astype(I32)
    rcarry_ref[...] = rcarry_ref[...] + lcount_row
    carry_ref[...] = carry_ref[...] + lcount_col
    count_ref[...] = carry_ref[...]

    denom = jnp.sum(self32 * scores, axis=0, keepdims=True)
    gate = self32 * scores / denom * ROUTED_SCALE
    pad = jnp.zeros((LANES - n_exp, tn), F32)
    widen = lambda v: jnp.concatenate([v, pad], axis=0)
    rank_tab = widen(jnp.where(self32 > 0.0, lrank, -1.0))
    rank_ref[...] = rank_tab.astype(BF16)
    rankt_ref[...] = rank_tab.T.astype(BF16)
    gatet_ref[...] = widen(gate).T.astype(BF16)


def _route(logits_t, bias_col):
    n_exp, n = logits_t.shape
    tn = SORT_TILE
    emaj = jax.ShapeDtypeStruct((LANES, n), BF16)
    tmaj = jax.ShapeDtypeStruct((n, LANES), BF16)
    emaj_spec = pl.BlockSpec((LANES, tn), lambda i: (0, i))
    tmaj_spec = pl.BlockSpec((tn, LANES), lambda i: (i, 0))
    return pl.pallas_call(
        functools.partial(_route_kernel, per_group=n_exp // N_EXPERT_GROUPS),
        grid=(n // tn,),
        in_specs=[pl.BlockSpec((n_exp, tn), lambda i: (0, i)), pl.BlockSpec(bias_col.shape, lambda i: (0, 0))],
        out_specs=[emaj_spec, tmaj_spec, tmaj_spec,
                   pl.BlockSpec((None, SUBLANES, LANES), lambda i: (i, 0, 0)),
                   pl.BlockSpec((n_exp, LANES), lambda i: (0, 0))],
        out_shape=[emaj, tmaj, tmaj, jax.ShapeDtypeStruct((n // tn, SUBLANES, LANES), I32),
                   jax.ShapeDtypeStruct((n_exp, LANES), F32)],
        scratch_shapes=[pltpu.VMEM((n_exp, LANES), F32), pltpu.VMEM((SUBLANES, LANES), F32)],
        compiler_params=_cparams(("arbitrary",)),
        name="route",
    )(logits_t, bias_col)


def _place_kernel(count_ref, tab_ref, runs_ref, bexp_ref, seg_ref, *, n_blocks):
    n_exp = count_ref.shape[0]
    blk = float(EXPERT_BLOCK)
    counts = count_ref[...]
    padded = jnp.ceil(counts / blk) * blk
    ridx = lax.broadcasted_iota(I32, (n_exp, LANES), 0)
    ends = padded
    shift = 1
    while shift < n_exp:
        ends = ends + jnp.where(ridx >= shift, pltpu.roll(ends, shift, 0), 0.0)
        shift *= 2
    starts = ends - padded

    tab = tab_ref[...]
    last = tab[tab.shape[0] - 1]
    total_row = (last[2:3, :] + last[1:2, :]).astype(F32)
    padded_row = jnp.ceil(total_row / blk) * blk
    lidx = lax.broadcasted_iota(I32, (1, LANES), 1)
    ends_row = padded_row
    shift = 1
    while shift < LANES:
        ends_row = ends_row + jnp.where(lidx >= shift, pltpu.roll(ends_row, shift, 1), 0.0)
        shift *= 2
    starts_row = (ends_row - padded_row).astype(I32)
    sub = lax.broadcasted_iota(I32, tab.shape, 1)
    runs_ref[...] = tab + jnp.where(sub == 2, starts_row[None], 0)

    nbp = bexp_ref.shape[1]
    blk_start = (lax.broadcasted_iota(I32, (n_exp, nbp), 1) * EXPERT_BLOCK).astype(F32)
    ends_wide = jnp.broadcast_to(ends[:, 0:1], (n_exp, nbp))
    count_le = lambda bound: jnp.minimum(
        jnp.sum(jnp.where(ends_wide <= bound, 1.0, 0.0), axis=0, keepdims=True), float(n_exp - 1))
    bexp = count_le(blk_start)
    used = ends[n_exp - 1:n_exp, 0:1] / blk
    seg_end = jnp.min(jnp.where(ends_wide > blk_start, ends_wide, 2.0 * float(nbp * EXPERT_BLOCK)),
                      axis=0, keepdims=True)
    sub8 = lax.broadcasted_iota(I32, (SUBLANES, nbp), 0)
    lane = lax.broadcasted_iota(I32, (SUBLANES, nbp), 1)
    plan = jnp.where(sub8 == 0, jnp.where(lane == n_blocks, used, bexp),
                     jnp.where(sub8 == 1, count_le(seg_end), jnp.where(sub8 == 2, seg_end / blk, 0.0)))
    bexp_ref[...] = plan.astype(I32)
    lane2 = lax.broadcasted_iota(I32, (n_exp, LANES), 1)
    seg_ref[...] = jnp.where(lane2 == 0, starts + counts, ends).astype(I32)


def _place(counts, tab, n_blocks):
    n_exp = counts.shape[0]
    nbp = -(-(n_blocks + 1) // LANES) * LANES
    return pl.pallas_call(
        functools.partial(_place_kernel, n_blocks=n_blocks),
        out_shape=[jax.ShapeDtypeStruct(tab.shape, I32), jax.ShapeDtypeStruct((SUBLANES, nbp), I32),
                   jax.ShapeDtypeStruct((n_exp, LANES), I32)],
        compiler_params=_cparams(None),
        name="route_place",
    )(counts, tab)


def _copy_rows(src_ref, dst_ref, sem, src0, dst0, n, wait=False):
    def piece(off, p):
        src = src_ref.at[pl.ds(pl.multiple_of(src0 + off, RUN_ALIGN), p), :]
        dst = dst_ref.at[pl.ds(pl.multiple_of(dst0 + off, RUN_ALIGN), p), :]
        cp = pltpu.make_async_copy(src, dst, sem)
        if wait:
            cp.wait()
        else:
            cp.start()

    def chunk(c, carry):
        piece(c * RUN_CHUNK, RUN_CHUNK)
        return carry
    lax.fori_loop(0, lax.shift_right_logical(n, RUN_CHUNK.bit_length() - 1), chunk, 0)
    p = RUN_CHUNK // 2
    while p >= RUN_ALIGN:
        @pl.when((n & p) != 0)
        def _(p=p):
            piece(n & (-2 * p), p)
        p //= 2


def _start_run(src_ref, dst_ref, sem, src0, dst0, n, limit, enable=None):
    p = RUN_ALIGN
    while p * 2 <= limit:
        p *= 2
    while p >= RUN_ALIGN:
        hit = (n & p) != 0
        @pl.when(hit if enable is None else hit & enable)
        def _(p=p):
            off = n & (-2 * p)
            src = src_ref.at[pl.ds(pl.multiple_of(src0 + off, RUN_ALIGN), p), :]
            dst = dst_ref.at[pl.ds(pl.multiple_of(dst0 + off, RUN_ALIGN), p), :]
            pltpu.make_async_copy(src, dst, sem).start()
        p //= 2


def _run_membership(tab, rows, first_row=0):
    j = lax.broadcasted_iota(I32, (rows, LANES), 0) + first_row
    start = tab[0:1, :]
    end = start + tab[1:2, :]
    return jnp.where(j >= start, jnp.where(j < end, 1.0, 0.0), 0.0)


def _dispatch_kernel(prev_runs_ref, runs_ref, seg_ref, tab_ref, rank_ref, x_ref, xs_ref, buf_ref, zero_ref, sems,
                     zsem, *, n_exp):
    step, n_steps = pl.program_id(0), pl.num_programs(0)
    tm = x_ref.shape[0]
    n_slots, rows = buf_ref.shape[:2]
    pad_rows = rows - TOP_K * tm
    cur, prev, old = step % n_slots, (step + n_slots - 1) % n_slots, (step + n_slots - 2) % n_slots

    def start_runs(tab_ref, s, experts, enable=None):
        for e in experts:
            _start_run(buf_ref.at[s], xs_ref, sems.at[s], tab_ref[0, e], tab_ref[2, e], tab_ref[1, e], tm, enable)

    def start_filler(tab_ref, s, enable=None):
        used = tab_ref[0, n_exp - 1] + tab_ref[1, n_exp - 1]
        spare = xs_ref.shape[0] - (n_slots - s) * pad_rows
        _start_run(buf_ref.at[s], xs_ref, sems.at[s], used, spare, rows - used, pad_rows, enable)

    def drain(s):
        pltpu.make_async_copy(buf_ref.at[s], xs_ref.at[pl.ds(0, rows), :], sems.at[s]).wait()

    @pl.when(step == 0)
    def _():
        zero_ref[...] = jnp.zeros_like(zero_ref)
        for wait in (False, True):
            def per_expert(e, c, wait=wait):
                first = seg_ref[e, 0]
                _copy_rows(zero_ref, xs_ref, zsem, 0, first, seg_ref[e, 1] - first, wait)
                return c
            lax.fori_loop(0, n_exp, per_expert, 0)

    lo, hi = _unpack_pair(x_ref[...])
    x = jnp.concatenate([lo.astype(BF16), hi.astype(BF16)], axis=1)
    tab = tab_ref[...]
    bits = lambda v: lax.bitcast_convert_type(v, U32)
    chunk, per = rows // DISPATCH_CHUNKS, n_exp // DISPATCH_CHUNKS
    for c in range(DISPATCH_CHUNKS):
        member = _run_membership(tab, chunk, c * chunk)
        run_start = jnp.sum(member * tab[0:1, :].astype(F32), axis=1, keepdims=True)
        in_run = (lax.broadcasted_iota(I32, (chunk, 1), 0) + c * chunk).astype(F32) - run_start
        onehot = jnp.where(_dot(member.astype(BF16), rank_ref[...]) == in_run, 1.0, 0.0)
        srt = _dot(onehot.astype(BF16), x)
        dh = srt.shape[1] // 2
        buf_ref[cur, c * chunk:(c + 1) * chunk, :] = (bits(srt[:, :dh]) >> 16) | bits(srt[:, dh:])
        start_runs(prev_runs_ref, prev, range(c * per, (c + 1) * per), step > 0)
    start_filler(prev_runs_ref, prev, step > 0)

    @pl.when(step >= 2)
    def _():
        drain(old)

    @pl.when(step == n_steps - 1)
    def _():
        start_runs(runs_ref, cur, range(n_exp))
        start_filler(runs_ref, cur)

        @pl.when(step >= 1)
        def _():
            drain(prev)
        drain(cur)


def _sorted_rows(tm, n_exp):
    return TOP_K * tm + n_exp * RUN_ALIGN


def _dispatch(x2, rank_tab, runs, seg, n_rows, n_exp):
    n, d = x2.shape
    tm = SORT_TILE
    rows = _sorted_rows(tm, n_exp)
    assert rows % DISPATCH_CHUNKS == 0 and n_exp % DISPATCH_CHUNKS == 0
    tab = (None,) + runs.shape[1:]
    return pl.pallas_call(
        functools.partial(_dispatch_kernel, n_exp=n_exp),
        grid=(n // tm,),
        in_specs=[pl.BlockSpec(tab, lambda i: (jnp.maximum(i - 1, 0), 0, 0), memory_space=pltpu.SMEM),
                  pl.BlockSpec(tab, lambda i: (i, 0, 0), memory_space=pltpu.SMEM),
                  pl.BlockSpec(seg.shape, lambda i: (0, 0), memory_space=pltpu.SMEM),
                  pl.BlockSpec(tab, lambda i: (i, 0, 0)),
                  pl.BlockSpec((LANES, tm), lambda i: (0, i)),
                  pl.BlockSpec((tm, d), lambda i: (i, 0))],
        out_specs=pl.BlockSpec(memory_space=pl.ANY),
        out_shape=jax.ShapeDtypeStruct((n_rows + DISPATCH_SLOTS * (rows - TOP_K * tm), d), x2.dtype),
        scratch_shapes=[pltpu.VMEM((DISPATCH_SLOTS, rows, d), x2.dtype), pltpu.VMEM((EXPERT_BLOCK, d), x2.dtype),
                        pltpu.SemaphoreType.DMA((DISPATCH_SLOTS,)), pltpu.SemaphoreType.DMA],
        compiler_params=_cparams(("arbitrary",)),
        name="dispatch",
    )(runs, runs, seg, runs, rank_tab, x2)


def _expert_kernel(plan_ref, xs_ref, wg_ref, wu_ref, wd_ref, ys_ref, xbuf_ref, xsems, wg_f32, wu_f32, wd_f32,
                   wsems, wslot_ref, wg_bf, wu_bf, wd_bf, *, n_blocks):
    i = pl.program_id(0)
    used = plan_ref[0, n_blocks]
    depth, blk = xbuf_ref.shape[:2]

    def fetch(j):
        rows = pl.ds(pl.multiple_of(j * blk, blk), blk)
        return pltpu.make_async_copy(xs_ref.at[rows, :], xbuf_ref.at[j % depth], xsems.at[j % depth])

    def weights(e, s):
        return [pltpu.make_async_copy(src.at[e], dst.at[s], wsems.at[s])
                for src, dst in ((wg_ref, wg_f32), (wu_ref, wu_f32), (wd_ref, wd_f32))]

    @pl.when(i == 0)
    def _():
        for j in range(depth - 1):
            @pl.when(j < used)
            def _(j=j):
                fetch(j).start()

    @pl.when(i + depth - 1 < used)
    def _():
        fetch(i + depth - 1).start()

    @pl.when((i == 0) & (used > 0))
    def _():
        wslot_ref[0] = 0
        for cp in weights(plan_ref[0, 0], 0):
            cp.start()

    @pl.when(i < used)
    def _():
        @pl.when((i == 0) | (plan_ref[0, i] != plan_ref[0, jnp.maximum(i - 1, 0)]))
        def _():
            s = wslot_ref[0]
            for cp in weights(plan_ref[0, i], s):
                cp.wait()

            @pl.when(plan_ref[2, i] < used)
            def _():
                for cp in weights(plan_ref[1, i], 1 - s):
                    cp.start()
            wslot_ref[0] = 1 - s
            wg_bf[...] = wg_f32[s].astype(BF16)
            wu_bf[...] = wu_f32[s].astype(BF16)
            wd_bf[...] = wd_f32[s].astype(BF16)

        fetch(i).wait()
        lo, hi = _unpack_pair(xbuf_ref[i % depth])
        x = jnp.concatenate([lo.astype(BF16), hi.astype(BF16)], axis=1)
        gate = _dot(x, wg_bf[...])
        up = _dot(x, wu_bf[...])
        hid = gate / (1.0 + jnp.exp(-gate)) * up
        y = _dot(hid.astype(BF16), wd_bf[...])
        dh = y.shape[1] // 2
        ys_ref[...] = _pack_pair(y[:, :dh], y[:, dh:])


def _experts(bexp, xs, exp_gate, exp_up, exp_down, n_blocks):
    n_rows = xs.shape[0]
    d, hidden = exp_gate.shape[1:]
    blk = EXPERT_BLOCK

    def rows(i, b):
        return (jnp.minimum(i, jnp.maximum(b[0, n_blocks] - 1, 0)), 0)

    hbm = pl.BlockSpec(memory_space=pl.ANY)
    grid_spec = pltpu.PrefetchScalarGridSpec(
        num_scalar_prefetch=1,
        grid=(n_blocks,),
        in_specs=[hbm, hbm, hbm, hbm],
        out_specs=pl.BlockSpec((blk, d // 2), rows),
        scratch_shapes=[pltpu.VMEM((EXPERT_RING, blk, d // 2), xs.dtype), pltpu.SemaphoreType.DMA((EXPERT_RING,)),
                        pltpu.VMEM((2, d, hidden), F32), pltpu.VMEM((2, d, hidden), F32),
                        pltpu.VMEM((2, hidden, d), F32), pltpu.SemaphoreType.DMA((2,)), pltpu.SMEM((1,), I32),
                        pltpu.VMEM((d, hidden), BF16), pltpu.VMEM((d, hidden), BF16), pltpu.VMEM((hidden, d), BF16)],
    )
    return pl.pallas_call(
        functools.partial(_expert_kernel, n_blocks=n_blocks),
        grid_spec=grid_spec,
        out_shape=jax.ShapeDtypeStruct((n_rows, d // 2), U32),
        compiler_params=_cparams(("arbitrary",)),
        name="experts",
    )(bexp, xs, exp_gate, exp_up, exp_down)


def _combine_kernel(runs_ref, next_runs_ref, ahead_runs_ref, tab_ref, rankt_ref, gatet_ref, base_ref, g3_ref, b3_ref,
                    ys_ref, out_ref, buf_ref, sems, *, n_exp):
    step, n_steps = pl.program_id(0), pl.num_programs(0)
    tm = base_ref.shape[0]
    n_slots, rows = buf_ref.shape[:2]
    slot, ahead = step % n_slots, (step + 2) % n_slots

    def fetch(tab_ref, s, experts, enable=None):
        for e in experts:
            _start_run(ys_ref, buf_ref.at[s], sems.at[s], tab_ref[2, e], tab_ref[0, e], tab_ref[1, e], tm, enable)

    def fetch_filler(tab_ref, s, enable=None):
        used = tab_ref[0, n_exp - 1] + tab_ref[1, n_exp - 1]
        _start_run(ys_ref, buf_ref.at[s], sems.at[s], 0, used, rows - used, rows - TOP_K * tm, enable)

    @pl.when(step == 0)
    def _():
        fetch(runs_ref, 0, range(n_exp))
        fetch_filler(runs_ref, 0)
        fetch(next_runs_ref, 1, range(n_exp), n_steps > 1)
        fetch_filler(next_runs_ref, 1, n_steps > 1)

    pltpu.make_async_copy(ys_ref.at[pl.ds(0, rows), :], buf_ref.at[slot], sems.at[slot]).wait()

    tab = tab_ref[...]
    start = tab[0:1, :].astype(F32)
    start_hi = jnp.floor(start / float(POS_BASE))
    sub = lax.broadcasted_iota(I32, (SUBLANES, LANES), 0)
    digits = jnp.where(sub == 0, start_hi, jnp.where(sub == 1, start - start_hi * float(POS_BASE), 0.0)).astype(BF16)
    have_ahead = step + 2 < n_steps
    chunk, per = rows // DISPATCH_CHUNKS, n_exp // DISPATCH_CHUNKS
    routed = jnp.zeros(base_ref.shape, F32)
    for c in range(DISPATCH_CHUNKS):
        member = _run_membership(tab, chunk, c * chunk).astype(BF16)
        picked = _dot_nt(digits, member)
        in_run = ((lax.broadcasted_iota(I32, (1, chunk), 1) + c * chunk).astype(F32)
                  - (picked[0:1, :] * float(POS_BASE) + picked[1:2, :]))
        w = jnp.where(_dot_nt(rankt_ref[...], member) == in_run, _dot_nt(gatet_ref[...], member), 0.0)
        lo, hi = _unpack_pair(buf_ref[slot, c * chunk:(c + 1) * chunk, :])
        y = jnp.concatenate([lo.astype(BF16), hi.astype(BF16)], axis=1)
        routed = routed + _dot(w.astype(BF16), y)
        fetch(ahead_runs_ref, ahead, range(c * per, (c + 1) * per), have_ahead)
    fetch_filler(ahead_runs_ref, ahead, have_ahead)
    out_ref[...] = _layer_norm(base_ref[...] + routed, g3_ref[...], b3_ref[...])


def _combine(runs, rank_t, gate_t, base, g3, b3, ys, n_exp):
    n, d = base.shape
    tm = SORT_TILE
    last = n // tm - 1
    tab = lambda f: pl.BlockSpec((None,) + runs.shape[1:], f, memory_space=pltpu.SMEM)
    tmaj = pl.BlockSpec((tm, LANES), lambda i: (i, 0))
    return pl.pallas_call(
        functools.partial(_combine_kernel, n_exp=n_exp),
        grid=(n // tm,),
        in_specs=[tab(lambda i: (i, 0, 0)), tab(lambda i: (jnp.minimum(i + 1, last), 0, 0)),
                  tab(lambda i: (jnp.minimum(i + 2, last), 0, 0)),
                  pl.BlockSpec((None,) + runs.shape[1:], lambda i: (i, 0, 0)),
                  tmaj, tmaj,
                  pl.BlockSpec((tm, d), lambda i: (i, 0)),
                  pl.BlockSpec(g3.shape, lambda i: (0, 0)),
                  pl.BlockSpec(b3.shape, lambda i: (0, 0)),
                  pl.BlockSpec(memory_space=pl.ANY)],
        out_specs=pl.BlockSpec((tm, d), lambda i: (i, 0)),
        out_shape=jax.ShapeDtypeStruct((n, d), F32),
        scratch_shapes=[pltpu.VMEM((DISPATCH_SLOTS, _sorted_rows(tm, n_exp), ys.shape[1]), ys.dtype),
                        pltpu.SemaphoreType.DMA((DISPATCH_SLOTS,))],
        compiler_params=_cparams(("arbitrary",)),
        name="combine",
    )(runs, runs, runs, runs, rank_t, gate_t, base, g3, b3, ys)


def _layer(x2d, mem2d, w_in, pool_w, pool_scale, w_out, ln1_g, ln1_b, w_cq, w_ck, w_cv, w_co, ln2_g, ln2_b,
           w_router, router_bias, exp_gate, exp_up, exp_down, sh_gate, sh_up, sh_down, ln3_g, ln3_b,
           *, alpha, batch, seq, mem_len):
    n, d = x2d.shape
    n_exp = w_router.shape[1]
    row = lambda a: a.reshape(1, -1)

    dils = tuple(dil for _, dil in DILATED_PATTERNS)
    *qkv, p = _in_proj(x2d, w_in, dils)
    branches = []
    for di, (window, dil) in enumerate(DILATED_PATTERNS):
        assert window // dil == ATTN_BLOCK and seq % (dil * ATTN_BLOCK) == 0
        assert IN_TILE % (dil * 2 * SUBLANES) == 0
        branches.append(_dil_attn(*qkv[3 * di:3 * di + 3], batch, seq, dil))
    (o1, l1), (o4, l4), (o16, l16) = branches
    kmem, vmem = _mem_kv(mem2d, w_ck, w_cv)

    x2, base, logits_t = _row_block(
        x2d, o1, o4, o16, l1, l4, l16, p, pool_w, row(pool_scale), w_out, row(ln1_g), row(ln1_b),
        w_cq, kmem, vmem, w_co, row(ln2_g), row(ln2_b), w_router.T, sh_gate, sh_up, sh_down,
        alpha=alpha, batch=batch, seq=seq, mem_len=mem_len)

    bias_col = router_bias.reshape(n_exp, 1)
    rank_tab, rank_t, gate_t, tab, counts = _route(logits_t, bias_col)
    max_rows = n * TOP_K + (n // SORT_TILE) * n_exp * (RUN_ALIGN - 1)
    n_blocks = -(-max_rows // EXPERT_BLOCK) + n_exp
    runs, bexp, seg = _place(counts, tab, n_blocks)
    xs = _dispatch(x2, rank_tab, runs, seg, n_blocks * EXPERT_BLOCK, n_exp)
    ys = _experts(bexp, xs, exp_gate, exp_up, exp_down, n_blocks)
    return _combine(runs, rank_t, gate_t, base, row(ln3_g), row(ln3_b), ys, n_exp)


def kernel(x, mem, w_in, pool_w, pool_scale, w_out, ln1_g, ln1_b, w_cq, w_ck, w_cv, w_co, ln2_g, ln2_b, w_router, router_bias, exp_gate, exp_up, exp_down, sh_gate, sh_up, sh_down, ln3_g, ln3_b):
    batch, seq, d = x.shape
    mem_len = mem.shape[1]
    depth = w_in.shape[0]
    alpha = (2.0 * depth) ** 0.25
    x2d = x.reshape(batch * seq, d)
    mem2d = mem.reshape(batch * mem_len, d)
    for l in range(depth):
        x2d = _layer(x2d, mem2d, w_in[l], pool_w[l], pool_scale[l], w_out[l], ln1_g[l], ln1_b[l], w_cq[l], w_ck[l],
                     w_cv[l], w_co[l], ln2_g[l], ln2_b[l], w_router[l], router_bias[l], exp_gate[l], exp_up[l],
                     exp_down[l], sh_gate[l], sh_up[l], sh_down[l], ln3_g[l], ln3_b[l],
                     alpha=alpha, batch=batch, seq=seq, mem_len=mem_len)
    return x2d.reshape(batch, seq, d)
```

```python
import functools

import jax
import jax.numpy as jnp
from jax import lax
from jax.experimental import pallas as pl
from jax.experimental.pallas import tpu as pltpu

F32 = jnp.float32
BF16 = jnp.bfloat16
I32 = jnp.int32
U32 = jnp.uint32

ATTN_HEADS = 8
HEAD_DIM = 64
ATTN_WIDTH = ATTN_HEADS * HEAD_DIM
DILATED_PATTERNS = ((128, 1), (512, 4), (2048, 16))
POOL_WINDOWS = (2, 4, 8, 16)
MEM_HEADS = 4
N_EXPERT_GROUPS = 8
TOPK_GROUPS = 4
TOP_K = 8
ROUTED_SCALE = 2.5
LN_EPS = 1e-5
NEG_INF = -1e30

LANES = 128
SUBLANES = 8
VMEM_LIMIT = 56 * 1024 * 1024

ATTN_BLOCK = 128
ATTN_STEP_BLOCKS = 8
ROW_TILE = 512
IN_TILE = 512
SORT_TILE = 256
EXPERT_BLOCK = 512
EXPERT_RING = 3
RUN_ALIGN = SUBLANES
RUN_CHUNK = 64
POS_BASE = 64
DISPATCH_SLOTS = 3
DISPATCH_CHUNKS = 4


def _cparams(sem, vmem=VMEM_LIMIT):
    return pltpu.CompilerParams(dimension_semantics=sem, vmem_limit_bytes=vmem)


def _layer_norm(y, g, b):
    mu = jnp.mean(y, axis=-1, keepdims=True)
    d = y - mu
    var = jnp.mean(d * d, axis=-1, keepdims=True)
    return d * lax.rsqrt(var + LN_EPS) * g + b


def _dot(a, b):
    return jnp.dot(a, b, preferred_element_type=F32)


def _dot_nt(a, b):
    return lax.dot_general(a, b, (((1,), (1,)), ((), ())), preferred_element_type=F32)


def _split_bf16(x):
    hi = x.astype(BF16)
    lo = (x - hi.astype(F32)).astype(BF16)
    return hi, lo


def _bf16_bits(x):
    return lax.bitcast_convert_type(x.astype(BF16).astype(F32), U32)


def _pack_pair(lo, hi):
    return (_bf16_bits(lo) >> 16) | (_bf16_bits(hi) & jnp.uint32(0xFFFF0000))


def _unpack_pair(w):
    lo = lax.bitcast_convert_type(w << 16, F32)
    hi = lax.bitcast_convert_type(w & jnp.uint32(0xFFFF0000), F32)
    return lo, hi


def _in_proj_kernel(x_ref, w_ref, *refs, dils):
    n_qkv = 3 * len(dils)
    out_refs, p_ref, slab_ref, wbf_ref = refs[:n_qkv], refs[n_qkv], refs[n_qkv + 1], refs[n_qkv + 2]

    @pl.when(pl.program_id(0) == 0)
    def _():
        wbf_ref[...] = w_ref[...].astype(BF16)

    w_ref = wbf_ref
    tm = x_ref.shape[0]
    x = x_ref[...].astype(BF16)
    aw = ATTN_WIDTH
    n_slabs = aw // LANES
    for a in range(3):
        val = _dot(x, w_ref[:, a * aw:(a + 1) * aw])
        if a == 0:
            val = val * (HEAD_DIM ** -0.5)
        for s in range(n_slabs):
            slab_ref[s] = val[:, s * LANES:(s + 1) * LANES]
        for di, dil in enumerate(dils):
            dst = out_refs[3 * di + a]
            if dil == 1:
                dst[...] = val.astype(BF16)
                continue
            for r in range(dil):
                for s in range(n_slabs):
                    rows = slab_ref[s, pl.ds(r, tm // dil, stride=dil), :]
                    dst[:, r * aw + s * LANES:r * aw + (s + 1) * LANES] = rows.astype(BF16)
    p_ref[...] = _dot(x, w_ref[:, 3 * aw:])


def _in_proj(x2d, w_in, dils):
    n, d = x2d.shape
    aw = ATTN_WIDTH
    pw = w_in.shape[1] - 3 * aw
    tm = IN_TILE
    row = lambda rows, w: pl.BlockSpec((rows, w), lambda i: (i, 0))
    qkv_specs, qkv_shapes = [], []
    for dil in dils:
        qkv_specs += [row(tm // dil, dil * aw)] * 3
        qkv_shapes += [jax.ShapeDtypeStruct((n // dil, dil * aw), BF16)] * 3
    return pl.pallas_call(
        functools.partial(_in_proj_kernel, dils=dils),
        grid=(n // tm,),
        in_specs=[row(tm, d), pl.BlockSpec(w_in.shape, lambda i: (0, 0))],
        out_specs=qkv_specs + [row(tm, pw)],
        out_shape=qkv_shapes + [jax.ShapeDtypeStruct((n, pw), F32)],
        scratch_shapes=[pltpu.VMEM((aw // LANES, tm, LANES), F32), pltpu.VMEM(w_in.shape, BF16)],
        compiler_params=_cparams(("arbitrary",)),
        name="in_proj",
    )(x2d, w_in)


def _dil_attn_kernel(q_ref, kc_ref, kp_ref, vc_ref, vp_ref, o_ref, lse_ref, bias_ref, *, dil):
    blk = ATTN_BLOCK
    first_step = (pl.program_id(0) == 0) & (pl.program_id(1) == 0) & (pl.program_id(2) == 0)

    @pl.when(first_step)
    def _():
        qi = lax.broadcasted_iota(I32, (blk, 2 * blk), 0)
        kj = lax.broadcasted_iota(I32, (blk, 2 * blk), 1)
        dist = qi + blk - kj
        inside = (dist >= 0) & (dist <= blk)
        penalty = (dist * dil).astype(F32)
        for h in range(ATTN_HEADS):
            slope = 2.0 ** (-8.0 * (h + 1) / ATTN_HEADS)
            bias_ref[0, h] = jnp.where(inside, -slope * penalty, NEG_INF)
            bias_ref[1, h] = jnp.where(inside & (kj >= blk), -slope * penalty, NEG_INF)

    i = pl.program_id(1)
    lane = lax.broadcasted_iota(I32, (blk, LANES), 1)
    low_half = lane < HEAD_DIM
    n_pairs = ATTN_WIDTH // LANES
    half = n_pairs // 2
    n_res = q_ref.shape[1] // ATTN_WIDTH
    for rr, j in [(rr, j) for rr in range(n_res) for j in range(q_ref.shape[0] // blk)]:
        r = pl.program_id(2) * n_res + rr
        table = jnp.where(i == 0, 1, 0) if j == 0 else 0
        lse_tile = jnp.zeros((blk, LANES), F32)
        pair_out = []
        for hp in range(n_pairs):
            sl = slice(rr * ATTN_WIDTH + hp * LANES, rr * ATTN_WIDTH + (hp + 1) * LANES)
            q2 = q_ref[j * blk:(j + 1) * blk, sl]
            if j == 0:
                kcat = jnp.concatenate([kp_ref[:, sl], kc_ref[0:blk, sl]], axis=0)
                vcat = jnp.concatenate([vp_ref[:, sl], vc_ref[0:blk, sl]], axis=0)
            else:
                kcat = kc_ref[(j - 1) * blk:(j + 1) * blk, sl]
                vcat = vc_ref[(j - 1) * blk:(j + 1) * blk, sl]
            outs = []
            for e in range(LANES // HEAD_DIM):
                h = hp * (LANES // HEAD_DIM) + e
                keep = low_half if e == 0 else jnp.logical_not(low_half)
                qe = jnp.where(keep, q2, jnp.zeros_like(q2))
                s = _dot_nt(qe, kcat) + bias_ref[table, h]
                m = jnp.max(s, axis=1, keepdims=True)
                p = jnp.exp(s - m)
                l = jnp.sum(p, axis=1, keepdims=True)
                outs.append(_dot(p.astype(BF16), vcat) / l)
                lse_tile = jnp.where(lane == h, m + jnp.log(l), lse_tile)
            pair_out.append(jnp.where(low_half, outs[0], outs[1]))
        rows = pl.ds(j * blk * dil + r, blk, stride=dil) if dil > 1 else slice(j * blk, (j + 1) * blk)
        for w in range(half):
            o_ref[w, rows, :] = _pack_pair(pair_out[w], pair_out[w + half])
        lse_ref[rows, :] = lse_tile


def _dil_attn(q, k, v, batch, seq, dil):
    blk = ATTN_BLOCK
    sub = seq // dil
    step_blocks = min(ATTN_STEP_BLOCKS, sub // blk)
    step_rows = step_blocks * blk
    n_res = min(dil, ATTN_STEP_BLOCKS // step_blocks)
    assert sub % step_rows == 0 and dil % n_res == 0
    view = lambda a: a.reshape(batch, sub, dil * ATTN_WIDTH)
    cur = pl.BlockSpec((None, step_rows, n_res * ATTN_WIDTH), lambda b, i, r: (b, i, r))
    prev = pl.BlockSpec((None, blk, n_res * ATTN_WIDTH), lambda b, i, r: (b, jnp.maximum(i * step_blocks - 1, 0), r))
    half = ATTN_WIDTH // LANES // 2
    o, lse = pl.pallas_call(
        functools.partial(_dil_attn_kernel, dil=dil),
        grid=(batch, sub // step_rows, dil // n_res),
        in_specs=[cur, cur, prev, cur, prev],
        out_specs=[pl.BlockSpec((None, half, step_rows * dil, LANES), lambda b, i, r: (b, 0, i, 0)),
                   pl.BlockSpec((None, step_rows * dil, LANES), lambda b, i, r: (b, i, 0))],
        out_shape=[jax.ShapeDtypeStruct((batch, half, seq, LANES), U32),
                   jax.ShapeDtypeStruct((batch, seq, LANES), F32)],
        scratch_shapes=[pltpu.VMEM((2, ATTN_HEADS, blk, 2 * blk), F32)],
        compiler_params=_cparams(("arbitrary",) * 3),
        name=f"dil_attn_d{dil}",
    )(view(q), view(k), view(k), view(v), view(v))
    return o, lse.reshape(batch * seq, LANES)


def _mem_kv_kernel(mem_ref, wk_ref, wv_ref, k_ref, v_ref):
    m = mem_ref[...].astype(BF16)
    k_ref[...] = _dot(m, wk_ref[...].astype(BF16)).astype(BF16)
    v_ref[...] = _dot(m, wv_ref[...].astype(BF16)).astype(BF16)


def _mem_kv(mem2d, w_ck_bf, w_cv_bf):
    rows = mem2d.shape[0]
    width = w_ck_bf.shape[1]
    return pl.pallas_call(
        _mem_kv_kernel,
        out_shape=[jax.ShapeDtypeStruct((rows, width), BF16)] * 2,
        compiler_params=_cparams(None),
        name="mem_kv",
    )(mem2d, w_ck_bf, w_cv_bf)


def _row_kernel(x_ref, o1_ref, o4_ref, o16_ref, l1_ref, l4_ref, l16_ref, p_ref, halo_ref,
                poolw_f32, pscale_ref, wout_f32, g1_ref, b1_ref,
                wcq_f32, km_ref, vm_ref, wco_f32, g2_ref, b2_ref,
                wr_f32, shg_f32, shu_f32, shd_f32,
                x2_ref, base_ref, logit_ref,
                poolw_ref, wout_ref, wcq_ref, wco_ref, wrh_ref, wrl_ref, shg_ref, shu_ref, shd_ref, *, alpha, seq):
    tm = x_ref.shape[0]
    i = pl.program_id(0)

    @pl.when(i == 0)
    def _():
        for src, dst in ((poolw_f32, poolw_ref), (wout_f32, wout_ref), (wcq_f32, wcq_ref), (wco_f32, wco_ref),
                         (shg_f32, shg_ref), (shu_f32, shu_ref), (shd_f32, shd_ref)):
            dst[...] = src[...].astype(BF16)
        hi, lo = _split_bf16(wr_f32[...])
        wrh_ref[...] = hi
        wrl_ref[...] = lo

    tile_pos = (i * tm) % seq

    l1, l4, l16 = l1_ref[...], l4_ref[...], l16_ref[...]
    mx = jnp.maximum(jnp.maximum(l1, l4), l16)
    e1, e4, e16 = jnp.exp(l1 - mx), jnp.exp(l4 - mx), jnp.exp(l16 - mx)
    inv = 1.0 / (e1 + e4 + e16)
    hrow = lax.broadcasted_iota(I32, (LANES, ATTN_WIDTH), 0)
    hcol = lax.broadcasted_iota(I32, (LANES, ATTN_WIDTH), 1) // HEAD_DIM
    spread = jnp.where(hrow == hcol, 1.0, 0.0).astype(BF16)

    def widen(w):
        hi, lo = _split_bf16(w)
        return _dot(hi, spread) + _dot(lo, spread)

    def branch(o_ref):
        lo, hi = zip(*[_unpack_pair(o_ref[w]) for w in range(o_ref.shape[0])])
        return jnp.concatenate(lo + hi, axis=1)

    last = branch(o16_ref)
    attn = last + widen(e1 * inv) * (branch(o1_ref) - last) + widen(e4 * inv) * (branch(o4_ref) - last)

    halo_rows = halo_ref.shape[0]
    halo = jnp.where(tile_pos > 0, halo_ref[...], 0.0)
    ext = jnp.concatenate([halo, p_ref[...]], axis=0)
    pos = (tile_pos + lax.broadcasted_iota(I32, (tm, 1), 0)).astype(F32)
    gd = ext.shape[1] // len(POOL_WINDOWS)
    mixed = []
    for g, w in enumerate(POOL_WINDOWS):
        eg = ext[:, g * gd:(g + 1) * gd]
        acc, span = eg, 1
        while span < w:
            acc = acc + pltpu.roll(acc, span, 0)
            span *= 2
        count = jnp.minimum(pos + 1.0, float(w))
        pooled = acc[halo_rows:, :] / count - eg[halo_rows:, :]
        mixed.append(_dot(pooled.astype(BF16), poolw_ref[g]) * pscale_ref[:, g * gd:(g + 1) * gd])
    cat = jnp.concatenate([attn.astype(BF16)] + [m.astype(BF16) for m in mixed], axis=1)
    x1 = _layer_norm(alpha * x_ref[...] + _dot(cat, wout_ref[...]), g1_ref[...], b1_ref[...])

    qc = _dot(x1.astype(BF16), wcq_ref[...])
    mhd = qc.shape[1] // MEM_HEADS
    heads = []
    for h in range(MEM_HEADS):
        sl = slice(h * mhd, (h + 1) * mhd)
        s = _dot_nt(qc[:, sl].astype(BF16), km_ref[:, sl]) * (mhd ** -0.5)
        m = jnp.max(s, axis=1, keepdims=True)
        p = jnp.exp(s - m)
        l = jnp.sum(p, axis=1, keepdims=True)
        heads.append((_dot(p.astype(BF16), vm_ref[:, sl]) / l).astype(BF16))
    oc = jnp.concatenate(heads, axis=1)
    x2 = _layer_norm(alpha * x1 + _dot(oc, wco_ref[...]), g2_ref[...], b2_ref[...])
    dh = x2.shape[1] // 2
    x2_ref[...] = _pack_pair(x2[:, :dh], x2[:, dh:])

    xh, xl = _split_bf16(x2)
    logit_ref[...] = _dot_nt(wrh_ref[...], xh) + _dot_nt(wrh_ref[...], xl) + _dot_nt(wrl_ref[...], xh)

    gate = _dot(xh, shg_ref[...])
    up = _dot(xh, shu_ref[...])
    hid = gate / (1.0 + jnp.exp(-gate)) * up
    base_ref[...] = alpha * x2 + _dot(hid.astype(BF16), shd_ref[...])


def _row_block(x2d, o1, o4, o16, l1, l4, l16, p, pool_w, pool_scale, w_out, g1, b1,
               w_cq, kmem, vmem, w_co, g2, b2, wr_t, sh_g, sh_u, sh_d,
               *, alpha, batch, seq, mem_len):
    n, d = x2d.shape
    tm = ROW_TILE
    halo_rows = max(POOL_WINDOWS)
    steps_per_seq = seq // tm
    row = lambda w: pl.BlockSpec((tm, w), lambda i: (i, 0))
    full = lambda a: pl.BlockSpec(a.shape, lambda i: (0,) * a.ndim)
    halo = pl.BlockSpec((halo_rows, p.shape[1]), lambda i: (jnp.maximum(i * (tm // halo_rows) - 1, 0), 0))
    memspec = pl.BlockSpec((mem_len, kmem.shape[1]), lambda i: (i // steps_per_seq, 0))
    n_exp = wr_t.shape[0]
    branch = pl.BlockSpec((None, o1.shape[1], tm, LANES), lambda i: (i // steps_per_seq, 0, i % steps_per_seq, 0))
    bf_copy = lambda a: pltpu.VMEM(a.shape, BF16)
    return pl.pallas_call(
        functools.partial(_row_kernel, alpha=alpha, seq=seq),
        grid=(n // tm,),
        in_specs=[row(d), branch, branch, branch, row(LANES), row(LANES), row(LANES),
                  row(p.shape[1]), halo,
                  full(pool_w), full(pool_scale), full(w_out), full(g1), full(b1),
                  full(w_cq), memspec, memspec, full(w_co), full(g2), full(b2),
                  full(wr_t), full(sh_g), full(sh_u), full(sh_d)],
        out_specs=[row(d // 2), row(d), pl.BlockSpec((n_exp, tm), lambda i: (0, i))],
        out_shape=[jax.ShapeDtypeStruct((n, d // 2), U32), jax.ShapeDtypeStruct((n, d), F32),
                   jax.ShapeDtypeStruct((n_exp, n), F32)],
        scratch_shapes=[bf_copy(pool_w), bf_copy(w_out), bf_copy(w_cq), bf_copy(w_co), bf_copy(wr_t), bf_copy(wr_t),
                        bf_copy(sh_g), bf_copy(sh_u), bf_copy(sh_d)],
        compiler_params=_cparams(("arbitrary",)),
        name="row_block",
    )(x2d, o1, o4, o16, l1, l4, l16, p, p, pool_w, pool_scale, w_out, g1, b1,
      w_cq, kmem, vmem, w_co, g2, b2, wr_t, sh_g, sh_u, sh_d)


def _beat_counts(vals, n_rows):
    tn = vals.shape[1]
    n_tiles = n_rows // SUBLANES
    tiles = [vals[t * SUBLANES:(t + 1) * SUBLANES, :] for t in range(n_tiles)]
    sub = lax.broadcasted_iota(I32, (SUBLANES, tn), 0)
    counts = [jnp.zeros((SUBLANES, tn), F32) for _ in range(n_tiles)]
    for e in range(n_rows):
        te, je = divmod(e, SUBLANES)
        row = jnp.broadcast_to(vals[e:e + 1, :], (SUBLANES, tn))
        for t in range(n_tiles):
            strict = jnp.where(row > tiles[t], 1.0, 0.0)
            loose = jnp.where(row >= tiles[t], 1.0, 0.0)
            if t < te:
                beat = strict
            elif t > te:
                beat = loose
            else:
                beat = jnp.where(sub > je, loose, strict)
            counts[t] = counts[t] + beat
    return jnp.concatenate(counts, axis=0)


def _top_k_mask(vals, k):
    ridx = lax.broadcasted_iota(I32, vals.shape, 0).astype(F32)
    mask = jnp.zeros(vals.shape, F32)
    work = vals
    for _ in range(k):
        top = jnp.max(work, axis=0, keepdims=True)
        first = jnp.min(jnp.where(work == top, ridx, float(vals.shape[0])), axis=0, keepdims=True)
        hit = ridx == first
        mask = jnp.where(hit, 1.0, mask)
        work = jnp.where(hit, -jnp.inf, work)
    return mask


def _route_kernel(logit_ref, bias_ref, rank_ref, rankt_ref, gatet_ref, tab_ref, count_ref, carry_ref, rcarry_ref,
                  *, per_group):
    n_exp, tn = logit_ref.shape
    step = pl.program_id(0)

    @pl.when(step == 0)
    def _():
        carry_ref[...] = jnp.zeros_like(carry_ref)
        rcarry_ref[...] = jnp.zeros_like(rcarry_ref)

    scores = 1.0 / (1.0 + jnp.exp(-logit_ref[...]))
    biased = scores + bias_ref[:, 0:1]
    n_groups = n_exp // per_group

    gscore = []
    for g in range(n_groups):
        tile = biased[g * per_group:(g + 1) * per_group, :]
        inner = _beat_counts(tile, per_group)
        gscore.append(jnp.sum(jnp.where(inner < 2.0, tile, 0.0), axis=0, keepdims=True))
    gscore = jnp.concatenate(gscore, axis=0)
    grank = _beat_counts(gscore, n_groups)
    grank = jnp.concatenate(
        [jnp.broadcast_to(grank[g:g + 1, :], (per_group, tn)) for g in range(n_groups)], axis=0)
    masked = jnp.where(grank < float(TOPK_GROUPS), biased, NEG_INF)
    self32 = _top_k_mask(masked, TOP_K)
    selbf = self32.astype(BF16)

    tr = lax.broadcasted_iota(I32, (tn, tn), 0)
    tc = lax.broadcasted_iota(I32, (tn, tn), 1)
    before = jnp.where(tr < tc, 1.0, 0.0).astype(BF16)
    lrank = _dot(selbf, before)
    align = float(RUN_ALIGN)
    lcount_col = jnp.broadcast_to(jnp.sum(self32, axis=1, keepdims=True), (n_exp, LANES))
    sel_wide = jnp.concatenate([selbf, jnp.zeros((LANES - n_exp, tn), BF16)], axis=0)
    lcount_row = _dot_nt(jnp.ones((SUBLANES, tn), BF16), sel_wide)
    lcount_col = jnp.ceil(lcount_col / align) * align
    lcount_row = jnp.ceil(lcount_row / align) * align
    wr = lax.broadcasted_iota(I32, (LANES, LANES), 0)
    wc = lax.broadcasted_iota(I32, (LANES, LANES), 1)
    lstart_row = _dot(lcount_row.astype(BF16), jnp.where(wr < wc, 1.0, 0.0).astype(BF16))

    sub = lax.broadcasted_iota(I32, (SUBLANES, LANES), 0)
    tab_ref[...] = jnp.where(sub == 0, lstart_row, jnp.where(sub == 1, lcount_row, jnp.where(
        sub == 2, rcarry_ref[...], 0.0))).astype(I32)
    rcarry_ref[...] = rcarry_ref[...] + lcount_row
    carry_ref[...] = carry_ref[...] + lcount_col
    count_ref[...] = carry_ref[...]

    denom = jnp.sum(self32 * scores, axis=0, keepdims=True)
    gate = self32 * scores / denom * ROUTED_SCALE
    pad = jnp.zeros((LANES - n_exp, tn), F32)
    widen = lambda v: jnp.concatenate([v, pad], axis=0)
    rank_tab = widen(jnp.where(self32 > 0.0, lrank, -1.0))
    rank_ref[...] = rank_tab.astype(BF16)
    rankt_ref[...] = rank_tab.T.astype(BF16)
    gatet_ref[...] = widen(gate).T.astype(BF16)


def _route(logits_t, bias_col):
    n_exp, n = logits_t.shape
    tn = SORT_TILE
    emaj = jax.ShapeDtypeStruct((LANES, n), BF16)
    tmaj = jax.ShapeDtypeStruct((n, LANES), BF16)
    emaj_spec = pl.BlockSpec((LANES, tn), lambda i: (0, i))
    tmaj_spec = pl.BlockSpec((tn, LANES), lambda i: (i, 0))
    return pl.pallas_call(
        functools.partial(_route_kernel, per_group=n_exp // N_EXPERT_GROUPS),
        grid=(n // tn,),
        in_specs=[pl.BlockSpec((n_exp, tn), lambda i: (0, i)), pl.BlockSpec(bias_col.shape, lambda i: (0, 0))],
        out_specs=[emaj_spec, tmaj_spec, tmaj_spec,
                   pl.BlockSpec((None, SUBLANES, LANES), lambda i: (i, 0, 0)),
                   pl.BlockSpec((n_exp, LANES), lambda i: (0, 0))],
        out_shape=[emaj, tmaj, tmaj, jax.ShapeDtypeStruct((n // tn, SUBLANES, LANES), I32),
                   jax.ShapeDtypeStruct((n_exp, LANES), F32)],
        scratch_shapes=[pltpu.VMEM((n_exp, LANES), F32), pltpu.VMEM((SUBLANES, LANES), F32)],
        compiler_params=_cparams(("arbitrary",)),
        name="route",
    )(logits_t, bias_col)


def _place_kernel(count_ref, tab_ref, runs_ref, bexp_ref, seg_ref, *, n_blocks):
    n_exp = count_ref.shape[0]
    blk = float(EXPERT_BLOCK)
    counts = count_ref[...]
    padded = jnp.ceil(counts / blk) * blk
    ridx = lax.broadcasted_iota(I32, (n_exp, LANES), 0)
    ends = padded
    shift = 1
    while shift < n_exp:
        ends = ends + jnp.where(ridx >= shift, pltpu.roll(ends, shift, 0), 0.0)
        shift *= 2
    starts = ends - padded

    tab = tab_ref[...]
    last = tab[tab.shape[0] - 1]
    total_row = (last[2:3, :] + last[1:2, :]).astype(F32)
    padded_row = jnp.ceil(total_row / blk) * blk
    lidx = lax.broadcasted_iota(I32, (1, LANES), 1)
    ends_row = padded_row
    shift = 1
    while shift < LANES:
        ends_row = ends_row + jnp.where(lidx >= shift, pltpu.roll(ends_row, shift, 1), 0.0)
        shift *= 2
    starts_row = (ends_row - padded_row).astype(I32)
    sub = lax.broadcasted_iota(I32, tab.shape, 1)
    runs_ref[...] = tab + jnp.where(sub == 2, starts_row[None], 0)

    nbp = bexp_ref.shape[1]
    blk_start = (lax.broadcasted_iota(I32, (n_exp, nbp), 1) * EXPERT_BLOCK).astype(F32)
    ends_wide = jnp.broadcast_to(ends[:, 0:1], (n_exp, nbp))
    count_le = lambda bound: jnp.minimum(
        jnp.sum(jnp.where(ends_wide <= bound, 1.0, 0.0), axis=0, keepdims=True), float(n_exp - 1))
    bexp = count_le(blk_start)
    used = ends[n_exp - 1:n_exp, 0:1] / blk
    seg_end = jnp.min(jnp.where(ends_wide > blk_start, ends_wide, 2.0 * float(nbp * EXPERT_BLOCK)),
                      axis=0, keepdims=True)
    sub8 = lax.broadcasted_iota(I32, (SUBLANES, nbp), 0)
    lane = lax.broadcasted_iota(I32, (SUBLANES, nbp), 1)
    plan = jnp.where(sub8 == 0, jnp.where(lane == n_blocks, used, bexp),
                     jnp.where(sub8 == 1, count_le(seg_end), jnp.where(sub8 == 2, seg_end / blk, 0.0)))
    bexp_ref[...] = plan.astype(I32)
    lane2 = lax.broadcasted_iota(I32, (n_exp, LANES), 1)
    seg_ref[...] = jnp.where(lane2 == 0, starts + counts, ends).astype(I32)


def _place(counts, tab, n_blocks):
    n_exp = counts.shape[0]
    nbp = -(-(n_blocks + 1) // LANES) * LANES
    return pl.pallas_call(
        functools.partial(_place_kernel, n_blocks=n_blocks),
        out_shape=[jax.ShapeDtypeStruct(tab.shape, I32), jax.ShapeDtypeStruct((SUBLANES, nbp), I32),
                   jax.ShapeDtypeStruct((n_exp, LANES), I32)],
        compiler_params=_cparams(None),
        name="route_place",
    )(counts, tab)


def _copy_rows(src_ref, dst_ref, sem, src0, dst0, n, wait=False):
    def piece(off, p):
        src = src_ref.at[pl.ds(pl.multiple_of(src0 + off, RUN_ALIGN), p), :]
        dst = dst_ref.at[pl.ds(pl.multiple_of(dst0 + off, RUN_ALIGN), p), :]
        cp = pltpu.make_async_copy(src, dst, sem)
        if wait:
            cp.wait()
        else:
            cp.start()

    def chunk(c, carry):
        piece(c * RUN_CHUNK, RUN_CHUNK)
        return carry
    lax.fori_loop(0, lax.shift_right_logical(n, RUN_CHUNK.bit_length() - 1), chunk, 0)
    p = RUN_CHUNK // 2
    while p >= RUN_ALIGN:
        @pl.when((n & p) != 0)
        def _(p=p):
            piece(n & (-2 * p), p)
        p //= 2


def _start_run(src_ref, dst_ref, sem, src0, dst0, n, limit, enable=None):
    p = RUN_ALIGN
    while p * 2 <= limit:
        p *= 2
    while p >= RUN_ALIGN:
        hit = (n & p) != 0
        @pl.when(hit if enable is None else hit & enable)
        def _(p=p):
            off = n & (-2 * p)
            src = src_ref.at[pl.ds(pl.multiple_of(src0 + off, RUN_ALIGN), p), :]
            dst = dst_ref.at[pl.ds(pl.multiple_of(dst0 + off, RUN_ALIGN), p), :]
            pltpu.make_async_copy(src, dst, sem).start()
        p //= 2


def _run_membership(tab, rows, first_row=0):
    j = lax.broadcasted_iota(I32, (rows, LANES), 0) + first_row
    start = tab[0:1, :]
    end = start + tab[1:2, :]
    return jnp.where(j >= start, jnp.where(j < end, 1.0, 0.0), 0.0)


def _dispatch_kernel(prev_runs_ref, runs_ref, seg_ref, tab_ref, rank_ref, x_ref, xs_ref, buf_ref, zero_ref, sems,
                     zsem, *, n_exp):
    step, n_steps = pl.program_id(0), pl.num_programs(0)
    tm = x_ref.shape[0]
    n_slots, rows = buf_ref.shape[:2]
    pad_rows = rows - TOP_K * tm
    cur, prev, old = step % n_slots, (step + n_slots - 1) % n_slots, (step + n_slots - 2) % n_slots

    def start_runs(tab_ref, s, experts, enable=None):
        for e in experts:
            _start_run(buf_ref.at[s], xs_ref, sems.at[s], tab_ref[0, e], tab_ref[2, e], tab_ref[1, e], tm, enable)

    def start_filler(tab_ref, s, enable=None):
        used = tab_ref[0, n_exp - 1] + tab_ref[1, n_exp - 1]
        spare = xs_ref.shape[0] - (n_slots - s) * pad_rows
        _start_run(buf_ref.at[s], xs_ref, sems.at[s], used, spare, rows - used, pad_rows, enable)

    def drain(s):
        pltpu.make_async_copy(buf_ref.at[s], xs_ref.at[pl.ds(0, rows), :], sems.at[s]).wait()

    @pl.when(step == 0)
    def _():
        zero_ref[...] = jnp.zeros_like(zero_ref)
        for wait in (False, True):
            def per_expert(e, c, wait=wait):
                first = seg_ref[e, 0]
                _copy_rows(zero_ref, xs_ref, zsem, 0, first, seg_ref[e, 1] - first, wait)
                return c
            lax.fori_loop(0, n_exp, per_expert, 0)

    lo, hi = _unpack_pair(x_ref[...])
    x = jnp.concatenate([lo.astype(BF16), hi.astype(BF16)], axis=1)
    tab = tab_ref[...]
    bits = lambda v: lax.bitcast_convert_type(v, U32)
    chunk, per = rows // DISPATCH_CHUNKS, n_exp // DISPATCH_CHUNKS
    for c in range(DISPATCH_CHUNKS):
        member = _run_membership(tab, chunk, c * chunk)
        run_start = jnp.sum(member * tab[0:1, :].astype(F32), axis=1, keepdims=True)
        in_run = (lax.broadcasted_iota(I32, (chunk, 1), 0) + c * chunk).astype(F32) - run_start
        onehot = jnp.where(_dot(member.astype(BF16), rank_ref[...]) == in_run, 1.0, 0.0)
        srt = _dot(onehot.astype(BF16), x)
        dh = srt.shape[1] // 2
        buf_ref[cur, c * chunk:(c + 1) * chunk, :] = (bits(srt[:, :dh]) >> 16) | bits(srt[:, dh:])
        start_runs(prev_runs_ref, prev, range(c * per, (c + 1) * per), step > 0)
    start_filler(prev_runs_ref, prev, step > 0)

    @pl.when(step >= 2)
    def _():
        drain(old)

    @pl.when(step == n_steps - 1)
    def _():
        start_runs(runs_ref, cur, range(n_exp))
        start_filler(runs_ref, cur)

        @pl.when(step >= 1)
        def _():
            drain(prev)
        drain(cur)


def _sorted_rows(tm, n_exp):
    return TOP_K * tm + n_exp * RUN_ALIGN


def _dispatch(x2, rank_tab, runs, seg, n_rows, n_exp):
    n, d = x2.shape
    tm = SORT_TILE
    rows = _sorted_rows(tm, n_exp)
    assert rows % DISPATCH_CHUNKS == 0 and n_exp % DISPATCH_CHUNKS == 0
    tab = (None,) + runs.shape[1:]
    return pl.pallas_call(
        functools.partial(_dispatch_kernel, n_exp=n_exp),
        grid=(n // tm,),
        in_specs=[pl.BlockSpec(tab, lambda i: (jnp.maximum(i - 1, 0), 0, 0), memory_space=pltpu.SMEM),
                  pl.BlockSpec(tab, lambda i: (i, 0, 0), memory_space=pltpu.SMEM),
                  pl.BlockSpec(seg.shape, lambda i: (0, 0), memory_space=pltpu.SMEM),
                  pl.BlockSpec(tab, lambda i: (i, 0, 0)),
                  pl.BlockSpec((LANES, tm), lambda i: (0, i)),
                  pl.BlockSpec((tm, d), lambda i: (i, 0))],
        out_specs=pl.BlockSpec(memory_space=pl.ANY),
        out_shape=jax.ShapeDtypeStruct((n_rows + DISPATCH_SLOTS * (rows - TOP_K * tm), d), x2.dtype),
        scratch_shapes=[pltpu.VMEM((DISPATCH_SLOTS, rows, d), x2.dtype), pltpu.VMEM((EXPERT_BLOCK, d), x2.dtype),
                        pltpu.SemaphoreType.DMA((DISPATCH_SLOTS,)), pltpu.SemaphoreType.DMA],
        compiler_params=_cparams(("arbitrary",)),
        name="dispatch",
    )(runs, runs, seg, runs, rank_tab, x2)


def _expert_kernel(plan_ref, xs_ref, wg_ref, wu_ref, wd_ref, ys_ref, xbuf_ref, xsems, wg_f32, wu_f32, wd_f32,
                   wsems, wslot_ref, wg_bf, wu_bf, wd_bf, *, n_blocks):
    i = pl.program_id(0)
    used = plan_ref[0, n_blocks]
    depth, blk = xbuf_ref.shape[:2]

    def fetch(j):
        rows = pl.ds(pl.multiple_of(j * blk, blk), blk)
        return pltpu.make_async_copy(xs_ref.at[rows, :], xbuf_ref.at[j % depth], xsems.at[j % depth])

    def weights(e, s):
        return [pltpu.make_async_copy(src.at[e], dst.at[s], wsems.at[s])
                for src, dst in ((wg_ref, wg_f32), (wu_ref, wu_f32), (wd_ref, wd_f32))]

    @pl.when(i == 0)
    def _():
        for j in range(depth - 1):
            @pl.when(j < used)
            def _(j=j):
                fetch(j).start()

    @pl.when(i + depth - 1 < used)
    def _():
        fetch(i + depth - 1).start()

    @pl.when((i == 0) & (used > 0))
    def _():
        wslot_ref[0] = 0
        for cp in weights(plan_ref[0, 0], 0):
            cp.start()

    @pl.when(i < used)
    def _():
        @pl.when((i == 0) | (plan_ref[0, i] != plan_ref[0, jnp.maximum(i - 1, 0)]))
        def _():
            s = wslot_ref[0]
            for cp in weights(plan_ref[0, i], s):
                cp.wait()

            @pl.when(plan_ref[2, i] < used)
            def _():
                for cp in weights(plan_ref[1, i], 1 - s):
                    cp.start()
            wslot_ref[0] = 1 - s
            wg_bf[...] = wg_f32[s].astype(BF16)
            wu_bf[...] = wu_f32[s].astype(BF16)
            wd_bf[...] = wd_f32[s].astype(BF16)

        fetch(i).wait()
        lo, hi = _unpack_pair(xbuf_ref[i % depth])
        x = jnp.concatenate([lo.astype(BF16), hi.astype(BF16)], axis=1)
        gate = _dot(x, wg_bf[...])
        up = _dot(x, wu_bf[...])
        hid = gate / (1.0 + jnp.exp(-gate)) * up
        y = _dot(hid.astype(BF16), wd_bf[...])
        dh = y.shape[1] // 2
        ys_ref[...] = _pack_pair(y[:, :dh], y[:, dh:])


def _experts(bexp, xs, exp_gate, exp_up, exp_down, n_blocks):
    n_rows = xs.shape[0]
    d, hidden = exp_gate.shape[1:]
    blk = EXPERT_BLOCK

    def rows(i, b):
        return (jnp.minimum(i, jnp.maximum(b[0, n_blocks] - 1, 0)), 0)

    hbm = pl.BlockSpec(memory_space=pl.ANY)
    grid_spec = pltpu.PrefetchScalarGridSpec(
        num_scalar_prefetch=1,
        grid=(n_blocks,),
        in_specs=[hbm, hbm, hbm, hbm],
        out_specs=pl.BlockSpec((blk, d // 2), rows),
        scratch_shapes=[pltpu.VMEM((EXPERT_RING, blk, d // 2), xs.dtype), pltpu.SemaphoreType.DMA((EXPERT_RING,)),
                        pltpu.VMEM((2, d, hidden), F32), pltpu.VMEM((2, d, hidden), F32),
                        pltpu.VMEM((2, hidden, d), F32), pltpu.SemaphoreType.DMA((2,)), pltpu.SMEM((1,), I32),
                        pltpu.VMEM((d, hidden), BF16), pltpu.VMEM((d, hidden), BF16), pltpu.VMEM((hidden, d), BF16)],
    )
    return pl.pallas_call(
        functools.partial(_expert_kernel, n_blocks=n_blocks),
        grid_spec=grid_spec,
        out_shape=jax.ShapeDtypeStruct((n_rows, d // 2), U32),
        compiler_params=_cparams(("arbitrary",)),
        name="experts",
    )(bexp, xs, exp_gate, exp_up, exp_down)


def _combine_kernel(runs_ref, next_runs_ref, ahead_runs_ref, tab_ref, rankt_ref, gatet_ref, base_ref, g3_ref, b3_ref,
                    ys_ref, out_ref, buf_ref, sems, *, n_exp):
    step, n_steps = pl.program_id(0), pl.num_programs(0)
    tm = base_ref.shape[0]
    n_slots, rows = buf_ref.shape[:2]
    slot, ahead = step % n_slots, (step + 2) % n_slots

    def fetch(tab_ref, s, experts, enable=None):
        for e in experts:
            _start_run(ys_ref, buf_ref.at[s], sems.at[s], tab_ref[2, e], tab_ref[0, e], tab_ref[1, e], tm, enable)

    def fetch_filler(tab_ref, s, enable=None):
        used = tab_ref[0, n_exp - 1] + tab_ref[1, n_exp - 1]
        _start_run(ys_ref, buf_ref.at[s], sems.at[s], 0, used, rows - used, rows - TOP_K * tm, enable)

    @pl.when(step == 0)
    def _():
        fetch(runs_ref, 0, range(n_exp))
        fetch_filler(runs_ref, 0)
        fetch(next_runs_ref, 1, range(n_exp), n_steps > 1)
        fetch_filler(next_runs_ref, 1, n_steps > 1)

    pltpu.make_async_copy(ys_ref.at[pl.ds(0, rows), :], buf_ref.at[slot], sems.at[slot]).wait()

    tab = tab_ref[...]
    start = tab[0:1, :].astype(F32)
    start_hi = jnp.floor(start / float(POS_BASE))
    sub = lax.broadcasted_iota(I32, (SUBLANES, LANES), 0)
    digits = jnp.where(sub == 0, start_hi, jnp.where(sub == 1, start - start_hi * float(POS_BASE), 0.0)).astype(BF16)
    have_ahead = step + 2 < n_steps
    chunk, per = rows // DISPATCH_CHUNKS, n_exp // DISPATCH_CHUNKS
    routed = jnp.zeros(base_ref.shape, F32)
    for c in range(DISPATCH_CHUNKS):
        member = _run_membership(tab, chunk, c * chunk).astype(BF16)
        picked = _dot_nt(digits, member)
        in_run = ((lax.broadcasted_iota(I32, (1, chunk), 1) + c * chunk).astype(F32)
                  - (picked[0:1, :] * float(POS_BASE) + picked[1:2, :]))
        w = jnp.where(_dot_nt(rankt_ref[...], member) == in_run, _dot_nt(gatet_ref[...], member), 0.0)
        lo, hi = _unpack_pair(buf_ref[slot, c * chunk:(c + 1) * chunk, :])
        y = jnp.concatenate([lo.astype(BF16), hi.astype(BF16)], axis=1)
        routed = routed + _dot(w.astype(BF16), y)
        fetch(ahead_runs_ref, ahead, range(c * per, (c + 1) * per), have_ahead)
    fetch_filler(ahead_runs_ref, ahead, have_ahead)
    out_ref[...] = _layer_norm(base_ref[...] + routed, g3_ref[...], b3_ref[...])


def _combine(runs, rank_t, gate_t, base, g3, b3, ys, n_exp):
    n, d = base.shape
    tm = SORT_TILE
    last = n // tm - 1
    tab = lambda f: pl.BlockSpec((None,) + runs.shape[1:], f, memory_space=pltpu.SMEM)
    tmaj = pl.BlockSpec((tm, LANES), lambda i: (i, 0))
    return pl.pallas_call(
        functools.partial(_combine_kernel, n_exp=n_exp),
        grid=(n // tm,),
        in_specs=[tab(lambda i: (i, 0, 0)), tab(lambda i: (jnp.minimum(i + 1, last), 0, 0)),
                  tab(lambda i: (jnp.minimum(i + 2, last), 0, 0)),
                  pl.BlockSpec((None,) + runs.shape[1:], lambda i: (i, 0, 0)),
                  tmaj, tmaj,
                  pl.BlockSpec((tm, d), lambda i: (i, 0)),
                  pl.BlockSpec(g3.shape, lambda i: (0, 0)),
                  pl.BlockSpec(b3.shape, lambda i: (0, 0)),
                  pl.BlockSpec(memory_space=pl.ANY)],
        out_specs=pl.BlockSpec((tm, d), lambda i: (i, 0)),
        out_shape=jax.ShapeDtypeStruct((n, d), F32),
        scratch_shapes=[pltpu.VMEM((DISPATCH_SLOTS, _sorted_rows(tm, n_exp), ys.shape[1]), ys.dtype),
                        pltpu.SemaphoreType.DMA((DISPATCH_SLOTS,))],
        compiler_params=_cparams(("arbitrary",)),
        name="combine",
    )(runs, runs, runs, runs, rank_t, gate_t, base, g3, b3, ys)


def _layer(x2d, mem2d, w_in, pool_w, pool_scale, w_out, ln1_g, ln1_b, w_cq, w_ck, w_cv, w_co, ln2_g, ln2_b,
           w_router, router_bias, exp_gate, exp_up, exp_down, sh_gate, sh_up, sh_down, ln3_g, ln3_b,
           *, alpha, batch, seq, mem_len):
    n, d = x2d.shape
    n_exp = w_router.shape[1]
    row = lambda a: a.reshape(1, -1)

    dils = tuple(dil for _, dil in DILATED_PATTERNS)
    *qkv, p = _in_proj(x2d, w_in, dils)
    branches = []
    for di, (window, dil) in enumerate(DILATED_PATTERNS):
        assert window // dil == ATTN_BLOCK and seq % (dil * ATTN_BLOCK) == 0
        assert IN_TILE % (dil * 2 * SUBLANES) == 0
        branches.append(_dil_attn(*qkv[3 * di:3 * di + 3], batch, seq, dil))
    (o1, l1), (o4, l4), (o16, l16) = branches
    kmem, vmem = _mem_kv(mem2d, w_ck, w_cv)

    x2, base, logits_t = _row_block(
        x2d, o1, o4, o16, l1, l4, l16, p, pool_w, row(pool_scale), w_out, row(ln1_g), row(ln1_b),
        w_cq, kmem, vmem, w_co, row(ln2_g), row(ln2_b), w_router.T, sh_gate, sh_up, sh_down,
        alpha=alpha, batch=batch, seq=seq, mem_len=mem_len)

    bias_col = router_bias.reshape(n_exp, 1)
    rank_tab, rank_t, gate_t, tab, counts = _route(logits_t, bias_col)
    max_rows = n * TOP_K + (n // SORT_TILE) * n_exp * (RUN_ALIGN - 1)
    n_blocks = -(-max_rows // EXPERT_BLOCK) + n_exp
    runs, bexp, seg = _place(counts, tab, n_blocks)
    xs = _dispatch(x2, rank_tab, runs, seg, n_blocks * EXPERT_BLOCK, n_exp)
    ys = _experts(bexp, xs, exp_gate, exp_up, exp_down, n_blocks)
    return _combine(runs, rank_t, gate_t, base, row(ln3_g), row(ln3_b), ys, n_exp)


def kernel(x, mem, w_in, pool_w, pool_scale, w_out, ln1_g, ln1_b, w_cq, w_ck, w_cv, w_co, ln2_g, ln2_b, w_router, router_bias, exp_gate, exp_up, exp_down, sh_gate, sh_up, sh_down, ln3_g, ln3_b):
    batch, seq, d = x.shape
    mem_len = mem.shape[1]
    depth = w_in.shape[0]
    alpha = (2.0 * depth) ** 0.25
    x2d = x.reshape(batch * seq, d)
    mem2d = mem.reshape(batch * mem_len, d)
    for l in range(depth):
        x2d = _layer(x2d, mem2d, w_in[l], pool_w[l], pool_scale[l], w_out[l], ln1_g[l], ln1_b[l], w_cq[l], w_ck[l],
                     w_cv[l], w_co[l], ln2_g[l], ln2_b[l], w_router[l], router_bias[l], exp_gate[l], exp_up[l],
                     exp_down[l], sh_gate[l], sh_up[l], sh_down[l], ln3_g[l], ln3_b[l],
                     alpha=alpha, batch=batch, seq=seq, mem_len=mem_len)
    return x2d.reshape(batch, seq, d)
```
